```python
import math
import jax, jax.numpy as jnp
from jax import lax
import numpy as np

D_MODEL = 1024
BATCH = 2
SEQ = 8192
DEPTH = 1

MEM_LEN = 256
NORM_EPS = 1e-6
SSD_HEADS = 8
SSD_HEAD_DIM = 64
SSD_WIDTH = SSD_HEADS * SSD_HEAD_DIM
SSD_GROUPS = 2
SSD_STATE = 128
SSD_CONV = 4
SSD_CHUNK = 128
SSD_CONV_CH = SSD_WIDTH + 2 * SSD_GROUPS * SSD_STATE
MOBA_HEADS = 4
MOBA_HEAD_DIM = 64
MOBA_WIDTH = MOBA_HEADS * MOBA_HEAD_DIM
MOBA_BLOCK = 256
MOBA_TOPK = 3
MOBA_QBLOCK = 128
XATTN_HEADS = 4
XATTN_HEAD_DIM = 64
XATTN_WIDTH = XATTN_HEADS * XATTN_HEAD_DIM
MIX_WIDTH = SSD_WIDTH + MOBA_WIDTH + XATTN_WIDTH
IN_COLS = SSD_WIDTH + SSD_CONV_CH + SSD_HEADS + 3 * MOBA_WIDTH + XATTN_WIDTH
IN_SPLITS = (SSD_WIDTH,
             SSD_WIDTH + SSD_CONV_CH,
             SSD_WIDTH + SSD_CONV_CH + SSD_HEADS,
             SSD_WIDTH + SSD_CONV_CH + SSD_HEADS + MOBA_WIDTH,
             SSD_WIDTH + SSD_CONV_CH + SSD_HEADS + 2 * MOBA_WIDTH,
             SSD_WIDTH + SSD_CONV_CH + SSD_HEADS + 3 * MOBA_WIDTH)
PEER_HEADS = 8
PEER_NKEYS = 128
PEER_EXPERTS = PEER_NKEYS * PEER_NKEYS
PEER_TOPK = 16
PEER_QDIM = 128
PEER_HALF = PEER_QDIM // 2
PEER_CHUNK = 128

kernel_name = 'hybrid_ssd_moba_peer_block'


def rms_norm(x, w):
    xf = x.astype(jnp.float32)
    y = xf * lax.rsqrt(jnp.mean(xf * xf, axis=-1, keepdims=True) + NORM_EPS)
    return (y * w.astype(jnp.float32)).astype(x.dtype)


def alibi_slopes(n):
    return jnp.asarray([2.0 ** (-8.0 * (i + 1) / n) for i in range(n)], dtype=jnp.float32)


def causal_depthwise_conv(u, w, b):
    out = lax.conv_general_dilated(u, w[:, None, :], window_strides=(1,),
                                   padding=[(SSD_CONV - 1, 0)],
                                   dimension_numbers=('NWC', 'WIO', 'NWC'),
                                   feature_group_count=u.shape[-1])
    return out + b


def segsum(a):
    T = a.shape[-1]
    a_rep = jnp.broadcast_to(a[..., :, None], a.shape + (T,))
    a_rep = jnp.where(jnp.tril(jnp.ones((T, T), bool), -1), a_rep, 0.0)
    cs = jnp.cumsum(a_rep, axis=-2)
    return jnp.where(jnp.tril(jnp.ones((T, T), bool), 0), cs, -jnp.inf)


def ssd_chunked(xh, dt, a_coef, bm, cm):
    b, L, h, p = xh.shape
    n = bm.shape[-1]
    nc = L // SSD_CHUNK
    q = SSD_CHUNK
    xdt = (xh * dt[..., None].astype(xh.dtype)).reshape(b, nc, q, h, p)
    bc = bm.reshape(b, nc, q, h, n)
    cc = cm.reshape(b, nc, q, h, n)
    a = (dt * a_coef).reshape(b, nc, q, h).transpose(0, 1, 3, 2)
    a_cs = jnp.cumsum(a, axis=-1)
    decay_in = jnp.exp(segsum(a))
    cb = jnp.einsum('bclhn,bcshn->bchls', cc, bc)
    y_diag = jnp.einsum('bchls,bcshp->bclhp', cb * decay_in, xdt)
    decay_to_end = jnp.exp(a_cs[..., -1:] - a_cs)
    chunk_states = jnp.einsum('bclhn,bchl,bclhp->bchpn', bc, decay_to_end, xdt)
    chunk_decay = jnp.exp(a_cs[..., -1])

    def carry_state(state, inp):
        s_c, d_c = inp
        return state * d_c[..., None, None] + s_c, state

    init = jnp.zeros((b, h, p, n), chunk_states.dtype)
    _, states_in = lax.scan(carry_state, init,
                            (jnp.moveaxis(chunk_states, 1, 0), jnp.moveaxis(chunk_decay, 1, 0)))
    states_in = jnp.moveaxis(states_in, 0, 1)
    y_off = jnp.einsum('bclhn,bchpn,bchl->bclhp', cc, states_in, jnp.exp(a_cs))
    return (y_diag + y_off).reshape(b, L, h, p)


def ssd_mixer(z, xbc, dt_raw, conv_w, conv_b, dt_bias, a_log, d_skip, norm_w):
    b, L, _ = z.shape
    xbc = jax.nn.silu(causal_depthwise_conv(xbc, conv_w, conv_b))
    xs, bm, cm = jnp.split(xbc, [SSD_WIDTH, SSD_WIDTH + SSD_GROUPS * SSD_STATE], axis=-1)
    xh = xs.reshape(b, L, SSD_HEADS, SSD_HEAD_DIM)
    hpg = SSD_HEADS // SSD_GROUPS
    bm = jnp.repeat(bm.reshape(b, L, SSD_GROUPS, SSD_STATE), hpg, axis=2)
    cm = jnp.repeat(cm.reshape(b, L, SSD_GROUPS, SSD_STATE), hpg, axis=2)
    dt = jax.nn.softplus((dt_raw + dt_bias).astype(jnp.float32))
    a_coef = -jnp.exp(a_log.astype(jnp.float32))
    y = ssd_chunked(xh, dt, a_coef, bm, cm) + xh * d_skip[:, None]
    y = y.reshape(b, L, SSD_WIDTH).astype(z.dtype) * jax.nn.silu(z)
    y = rms_norm(y.reshape(b, L, SSD_GROUPS, SSD_WIDTH // SSD_GROUPS),
                 norm_w.reshape(SSD_GROUPS, SSD_WIDTH // SSD_GROUPS))
    return y.reshape(b, L, SSD_WIDTH)


def moba_attention(q, k, v, slopes):
    b, L, h, d = q.shape
    nb = -(-L // MOBA_BLOCK)
    pad = nb * MOBA_BLOCK - L
    scale = 1.0 / math.sqrt(d)
    qt = q.transpose(0, 2, 1, 3)
    kt = jnp.pad(k.transpose(0, 2, 1, 3), ((0, 0), (0, 0), (0, pad), (0, 0)))
    vt = jnp.pad(v.transpose(0, 2, 1, 3), ((0, 0), (0, 0), (0, pad), (0, 0)))
    k_blocks = kt.reshape(b, h, nb, MOBA_BLOCK, d)
    v_blocks = vt.reshape(b, h, nb, MOBA_BLOCK, d)
    k_mean = jnp.mean(k_blocks.astype(jnp.float32), axis=3)
    gate = jnp.einsum('bhtd,bhnd->bhtn', qt.astype(jnp.float32), k_mean)
    q_blk = jnp.arange(L) // MOBA_BLOCK
    past = jnp.arange(nb)[None, :] < q_blk[:, None]
    gate = jnp.where(past, gate, -jnp.inf)
    n_sel = min(MOBA_TOPK, nb)
    _, sel = lax.top_k(gate, n_sel)
    ncq = L // MOBA_QBLOCK
    q_chunks = jnp.moveaxis(qt.reshape(b, h, ncq, MOBA_QBLOCK, d), 2, 0)
    sel_chunks = jnp.moveaxis(sel.reshape(b, h, ncq, MOBA_QBLOCK, n_sel), 2, 0)
    bi = jnp.arange(b)[:, None, None, None]
    hi = jnp.arange(h)[None, :, None, None]

    def attend(args):
        qc, selc, ci = args
        t = ci * MOBA_QBLOCK + jnp.arange(MOBA_QBLOCK)
        own = (ci * MOBA_QBLOCK) // MOBA_BLOCK
        ks = k_blocks[bi, hi, selc]
        vs = v_blocks[bi, hi, selc]
        s_pos = selc[..., None] * MOBA_BLOCK + jnp.arange(MOBA_BLOCK)
        s_sel = jnp.einsum('bhqd,bhqjkd->bhqjk', qc, ks) * scale
        dist_sel = jnp.abs(t[:, None, None] - s_pos).astype(jnp.float32)
        s_sel = s_sel - slopes[:, None, None, None] * dist_sel
        valid = jnp.arange(n_sel) < own
        s_sel = jnp.where(valid[:, None], s_sel, -jnp.inf)
        k_own = lax.dynamic_index_in_dim(k_blocks, own, axis=2, keepdims=False)
        v_own = lax.dynamic_index_in_dim(v_blocks, own, axis=2, keepdims=False)
        own_pos = own * MOBA_BLOCK + jnp.arange(MOBA_BLOCK)
        s_own = jnp.einsum('bhqd,bhkd->bhqk', qc, k_own) * scale
        dist_own = jnp.abs(t[:, None] - own_pos[None, :]).astype(jnp.float32)
        s_own = s_own - slopes[:, None, None] * dist_own
        s_own = jnp.where(own_pos[None, :] <= t[:, None], s_own, -jnp.inf)
        scores = jnp.concatenate(
            [s_sel.reshape(b, h, MOBA_QBLOCK, n_sel * MOBA_BLOCK), s_own], axis=-1).astype(jnp.float32)
        p = jax.nn.softmax(scores, axis=-1).astype(v.dtype)
        p_sel = p[..., :n_sel * MOBA_BLOCK].reshape(b, h, MOBA_QBLOCK, n_sel, MOBA_BLOCK)
        p_own = p[..., n_sel * MOBA_BLOCK:]
        return (jnp.einsum('bhqjk,bhqjkd->bhqd', p_sel, vs)
                + jnp.einsum('bhqk,bhkd->bhqd', p_own, v_own))

    out = lax.map(attend, (q_chunks, sel_chunks, jnp.arange(ncq)))
    out = jnp.moveaxis(out, 0, 2).reshape(b, h, L, d)
    return out.transpose(0, 2, 1, 3)


def memory_cross_attention(q, mem_k, mem_v):
    scale = 1.0 / math.sqrt(q.shape[-1])
    s = jnp.einsum('blhd,bmhd->bhlm', q, mem_k).astype(jnp.float32) * scale
    p = jax.nn.softmax(s, axis=-1).astype(mem_v.dtype)
    return jnp.einsum('bhlm,bmhd->blhd', p, mem_v)


def peer_ffn(xn, w_query, sub_keys_1, sub_keys_2, expert_down, expert_up):
    b, L, _ = xn.shape
    q = jnp.einsum('bld,dk->blk', xn, w_query).reshape(b, L, PEER_HEADS, 2, PEER_HALF)
    s1 = jnp.einsum('blhd,nd->blhn', q[..., 0, :], sub_keys_1)
    s2 = jnp.einsum('blhd,nd->blhn', q[..., 1, :], sub_keys_2)
    v1, i1 = lax.top_k(s1, PEER_TOPK)
    v2, i2 = lax.top_k(s2, PEER_TOPK)
    cand_s = (v1[..., :, None] + v2[..., None, :]).reshape(b, L, PEER_HEADS, PEER_TOPK * PEER_TOPK)
    cand_i = (i1[..., :, None] * PEER_NKEYS + i2[..., None, :]).reshape(b, L, PEER_HEADS, PEER_TOPK * PEER_TOPK)
    top_s, top_pos = lax.top_k(cand_s, PEER_TOPK)
    e_idx = jnp.take_along_axis(cand_i, top_pos, axis=-1)
    gates = jax.nn.softmax(top_s.astype(jnp.float32), axis=-1).astype(xn.dtype)
    nchunk = L // PEER_CHUNK

    def to_chunks(t):
        return jnp.moveaxis(t.reshape((b, nchunk, PEER_CHUNK) + t.shape[2:]), 1, 0)

    def expert_block(args):
        xc, ic, gc = args
        u = expert_down[ic]
        act = jax.nn.gelu(jnp.einsum('bthkd,btd->bthk', u, xc), approximate=False)
        w_up = expert_up[ic]
        return jnp.einsum('bthk,bthkd->btd', gc * act, w_up)

    out = lax.map(expert_block, (to_chunks(xn), to_chunks(e_idx), to_chunks(gates)))
    return jnp.moveaxis(out, 0, 1).reshape(b, L, D_MODEL)


def setup_inputs(seed: int = 0) -> dict:
    key = jax.random.key(seed)
    ks = jax.random.split(key, 24)
    f32 = jnp.float32

    def nrm(k, shape, scale):
        return jax.random.normal(k, shape, f32) * scale

    def gain(k, shape):
        return 1.0 + 0.05 * jax.random.normal(k, shape, f32)

    dt0 = jnp.exp(jax.random.uniform(ks[5], (DEPTH, SSD_HEADS), f32, math.log(1e-3), math.log(1e-1)))
    return {
        'x': nrm(ks[0], (BATCH, SEQ, D_MODEL), 1.0),
        'mem': nrm(ks[1], (BATCH, MEM_LEN, D_MODEL), 1.0),
        'mix_norm_w': gain(ks[2], (DEPTH, D_MODEL)),
        'w_in': nrm(ks[3], (DEPTH, D_MODEL, IN_COLS), D_MODEL ** -0.5),
        'ssd_conv_w': nrm(ks[4], (DEPTH, SSD_CONV, SSD_CONV_CH), 0.5),
        'ssd_conv_b': nrm(ks[6], (DEPTH, SSD_CONV_CH), 0.02),
        'ssd_dt_bias': dt0 + jnp.log(-jnp.expm1(-dt0)),
        'ssd_a_log': jnp.log(jax.random.uniform(ks[7], (DEPTH, SSD_HEADS), f32, 1.0, 16.0)),
        'ssd_d': gain(ks[8], (DEPTH, SSD_HEADS)),
        'ssd_norm_w': gain(ks[9], (DEPTH, SSD_WIDTH)),
        'moba_q_norm_w': gain(ks[10], (DEPTH, MOBA_HEAD_DIM)),
        'moba_k_norm_w': gain(ks[11], (DEPTH, MOBA_HEAD_DIM)),
        'mem_norm_w': gain(ks[12], (DEPTH, D_MODEL)),
        'w_mem_kv': nrm(ks[13], (DEPTH, D_MODEL, 2 * XATTN_WIDTH), D_MODEL ** -0.5),
        'xattn_q_norm_w': gain(ks[14], (DEPTH, XATTN_HEAD_DIM)),
        'xattn_k_norm_w': gain(ks[15], (DEPTH, XATTN_HEAD_DIM)),
        'w_out': nrm(ks[16], (DEPTH, MIX_WIDTH, D_MODEL), MIX_WIDTH ** -0.5),
        'ffn_norm_w': gain(ks[17], (DEPTH, D_MODEL)),
        'peer_w_query': nrm(ks[18], (DEPTH, D_MODEL, PEER_HEADS * PEER_QDIM), D_MODEL ** -0.5),
        'peer_sub_keys_1': nrm(ks[19], (DEPTH, PEER_NKEYS, PEER_HALF), PEER_HALF ** -0.5),
        'peer_sub_keys_2': nrm(ks[20], (DEPTH, PEER_NKEYS, PEER_HALF), PEER_HALF ** -0.5),
        'peer_expert_down': nrm(ks[21], (DEPTH, PEER_EXPERTS, D_MODEL), D_MODEL ** -0.5),
        'peer_expert_up': nrm(ks[22], (DEPTH, PEER_EXPERTS, D_MODEL), PEER_HEADS ** -0.5),
    }


def reference(x, mem, mix_norm_w, w_in, ssd_conv_w, ssd_conv_b, ssd_dt_bias, ssd_a_log, ssd_d,
              ssd_norm_w, moba_q_norm_w, moba_k_norm_w, mem_norm_w, w_mem_kv, xattn_q_norm_w,
              xattn_k_norm_w, w_out, ffn_norm_w, peer_w_query, peer_sub_keys_1, peer_sub_keys_2,
              peer_expert_down, peer_expert_up):
    b, L, _ = x.shape
    slopes = alibi_slopes(MOBA_HEADS)
    h = x
    for l in range(DEPTH):
        xn = rms_norm(h, mix_norm_w[l])
        proj = jnp.einsum('bld,dk->blk', xn, w_in[l])
        z, xbc, dt_raw, mq, mk, mv, xq = jnp.split(proj, list(IN_SPLITS), axis=-1)
        y_ssd = ssd_mixer(z, xbc, dt_raw, ssd_conv_w[l], ssd_conv_b[l], ssd_dt_bias[l],
                          ssd_a_log[l], ssd_d[l], ssd_norm_w[l])
        mq = rms_norm(mq.reshape(b, L, MOBA_HEADS, MOBA_HEAD_DIM), moba_q_norm_w[l])
        mk = rms_norm(mk.reshape(b, L, MOBA_HEADS, MOBA_HEAD_DIM), moba_k_norm_w[l])
        mv = mv.reshape(b, L, MOBA_HEADS, MOBA_HEAD_DIM)
        y_moba = moba_attention(mq, mk, mv, slopes).reshape(b, L, MOBA_WIDTH)
        mem_n = rms_norm(mem, mem_norm_w[l])
        mem_kv = jnp.einsum('bmd,dk->bmk', mem_n, w_mem_kv[l])
        mem_k, mem_v = jnp.split(mem_kv, 2, axis=-1)
        M = mem.shape[1]
        mem_k = rms_norm(mem_k.reshape(b, M, XATTN_HEADS, XATTN_HEAD_DIM), xattn_k_norm_w[l])
        mem_v = mem_v.reshape(b, M, XATTN_HEADS, XATTN_HEAD_DIM)
        xq = rms_norm(xq.reshape(b, L, XATTN_HEADS, XATTN_HEAD_DIM), xattn_q_norm_w[l])
        y_mem = memory_cross_attention(xq, mem_k, mem_v).reshape(b, L, XATTN_WIDTH)
        mixed = jnp.concatenate([y_ssd, y_moba, y_mem], axis=-1)
        h = h + jnp.einsum('blk,kd->bld', mixed, w_out[l])
        hn = rms_norm(h, ffn_norm_w[l])
        h = h + peer_ffn(hn, peer_w_query[l], peer_sub_keys_1[l], peer_sub_keys_2[l],
                         peer_expert_down[l], peer_expert_up[l])
    return h
```

```python
import functools
import math

import numpy as np
import jax
import jax.numpy as jnp
from jax import lax
from jax.experimental import pallas as pl
from jax.experimental.pallas import tpu as pltpu

F32 = jnp.float32
BF16 = jnp.bfloat16
HIGHEST = lax.Precision.HIGHEST

NORM_EPS = 1e-6
D_MODEL = 1024
SSD_HEADS = 8
SSD_HEAD_DIM = 64
SSD_WIDTH = 512
SSD_GROUPS = 2
SSD_STATE = 128
SSD_CONV = 4
SSD_CONV_CH = 1024
ATT_HEADS = 4
ATT_DIM = 64
ATT_WIDTH = 256
MOBA_BLOCK = 256
MOBA_TOPK = 3
MEM_LEN = 256
PEER_HEADS = 8
PEER_NKEYS = 128
PEER_TOPK = 16
PEER_HALF = 64
PEER_EXPERTS = PEER_NKEYS * PEER_NKEYS

LANE = 128
NEG = -1e30
VMEM_LIMIT = 56 * 1024 * 1024

TM_IN = 512
SSD_CHUNK = 256
TQ = MOBA_BLOCK
TM_PP = 256
TT_PEER = 512
ET_PEER = 512

_SLOPES = [2.0 ** (-8.0 * (i + 1) / ATT_HEADS) for i in range(ATT_HEADS)]


def _bf16_split(v):
    hi = float(np.float32(v).astype(BF16).astype(np.float32))
    lo = float(np.float32(v - hi).astype(BF16).astype(np.float32))
    return hi, lo


def _cparams(sem):
    return pltpu.CompilerParams(dimension_semantics=sem, vmem_limit_bytes=VMEM_LIMIT)


def _sigmoid(x):
    return 1.0 / (1.0 + jnp.exp(-x))


def _full(shape):
    n = len(shape)
    return pl.BlockSpec(shape, lambda *_: (0,) * n)


def _in_proj_body(x_ref, g_ref, *refs):
    n = len(refs) // 2
    w_refs, o_refs = refs[:n], refs[n:]
    x = x_ref[...]
    ms = jnp.mean(x * x, axis=-1, keepdims=True)
    xn = ((x * lax.rsqrt(ms + NORM_EPS)) * g_ref[...]).astype(BF16)
    for w_ref, o_ref in zip(w_refs, o_refs):
        o_ref[...] = jnp.dot(xn, w_ref[...], preferred_element_type=F32).astype(o_ref.dtype)


def _in_proj(x2d, gain, weights):
    t = x2d.shape[0]
    in_specs = [pl.BlockSpec((TM_IN, D_MODEL), lambda i: (i, 0)), _full((1, D_MODEL))]
    in_specs += [_full(w.shape) for w in weights]
    out_specs = [pl.BlockSpec((TM_IN, w.shape[1]), lambda i: (i, 0)) for w in weights]
    out_shape = [jax.ShapeDtypeStruct((t, w.shape[1]), F32) for w in weights]
    return pl.pallas_call(
        _in_proj_body, grid=(t // TM_IN,), in_specs=in_specs, out_specs=out_specs,
        out_shape=out_shape, compiler_params=_cparams(("parallel",)), name="in_proj",
    )(x2d, gain, *weights)


def _ssd_body(z_ref, xbc_ref, dt_ref, cw_ref, cb_ref, dtb_ref, alog_ref, dskip_ref, nw_ref, e_ref,
              y_ref, ext_scr, state_scr):
    q = SSD_CHUNK
    c = pl.program_id(1)

    @pl.when(c == 0)
    def _():
        ext_scr[0:8, :] = jnp.zeros((8, SSD_CONV_CH), F32)
        state_scr[...] = jnp.zeros_like(state_scr)

    u = xbc_ref[0]
    ext_scr[8:8 + q, :] = u
    acc = cb_ref[...] + cw_ref[3:4, :] * u
    acc = acc + cw_ref[2:3, :] * ext_scr[7:7 + q, :]
    acc = acc + cw_ref[1:2, :] * ext_scr[6:6 + q, :]
    acc = acc + cw_ref[0:1, :] * ext_scr[5:5 + q, :]
    ext_scr[0:8, :] = u[q - 8:q, :]
    act = acc * _sigmoid(acc)
    xs = act[:, 0:SSD_WIDTH]
    bm = act[:, SSD_WIDTH:SSD_WIDTH + SSD_GROUPS * SSD_STATE]
    cm = act[:, SSD_WIDTH + SSD_GROUPS * SSD_STATE:]

    dtr = dt_ref[0] + dtb_ref[...]
    dt = jnp.maximum(dtr, 0.0) + jnp.log(1.0 + jnp.exp(-jnp.abs(dtr)))
    a = dt * (-jnp.exp(alog_ref[...]))
    row = lax.broadcasted_iota(jnp.int32, (q, q), 0)
    col = lax.broadcasted_iota(jnp.int32, (q, q), 1)
    causal = row >= col
    a_cs = jnp.dot(causal.astype(F32), a, precision=HIGHEST, preferred_element_type=F32)
    a_cs_t = a_cs.T
    a_cs_w = jnp.dot(a_cs, e_ref[...], precision=HIGHEST, preferred_element_type=F32)
    dt_w = jnp.dot(dt, e_ref[...], precision=HIGHEST, preferred_element_type=F32)
    total_w = a_cs_w[q - 1:q, :]
    exp_cs_w = jnp.exp(a_cs_w)
    dte_w = jnp.exp(total_w - a_cs_w)
    cd_w = jnp.exp(total_w)
    xdt_w = xs * dt_w
    lane = lax.broadcasted_iota(jnp.int32, (1, LANE), 1)
    first = lane < SSD_HEAD_DIM

    z = z_ref[0]
    gated = []
    for g in range(SSD_GROUPS):
        bg = bm[:, g * SSD_STATE:(g + 1) * SSD_STATE]
        cg = cm[:, g * SSD_STATE:(g + 1) * SSD_STATE].astype(BF16)
        cb = lax.dot_general(cg, bg.astype(BF16), (((1,), (1,)), ((), ())), preferred_element_type=F32)
        bg_t = bg.T.astype(BF16)
        for kk in range(2):
            k = 2 * g + kk
            sl = slice(k * LANE, (k + 1) * LANE)
            xdt = xdt_w[:, sl]
            xdt_b = xdt.astype(BF16)
            yd = []
            for hh in range(2):
                h = 2 * k + hh
                seg = a_cs[:, h:h + 1] - a_cs_t[h:h + 1, :]
                lm = jnp.exp(jnp.where(causal, seg, -jnp.inf))
                yd.append(jnp.dot((cb * lm).astype(BF16), xdt_b, preferred_element_type=F32))
            y = jnp.where(first, yd[0], yd[1])
            s_old = state_scr[k]
            y = y + jnp.dot(cg, s_old.astype(BF16), preferred_element_type=F32) * exp_cs_w[:, sl]
            y = y + xs[:, sl] * dskip_ref[:, sl]
            state_scr[k] = s_old * cd_w[:, sl] + jnp.dot(
                bg_t, (xdt * dte_w[:, sl]).astype(BF16), preferred_element_type=F32)
            zz = z[:, sl]
            gated.append(y * (zz * _sigmoid(zz)))
    for g in range(SSD_GROUPS):
        y0, y1 = gated[2 * g], gated[2 * g + 1]
        ms = (jnp.sum(y0 * y0, axis=-1, keepdims=True)
              + jnp.sum(y1 * y1, axis=-1, keepdims=True)) * (1.0 / (SSD_WIDTH // SSD_GROUPS))
        r = lax.rsqrt(ms + NORM_EPS)
        lo = 2 * g * LANE
        y_ref[0, :, lo:lo + LANE] = (y0 * r * nw_ref[:, lo:lo + LANE]).astype(y_ref.dtype)
        y_ref[0, :, lo + LANE:lo + 2 * LANE] = (y1 * r * nw_ref[:, lo + LANE:lo + 2 * LANE]).astype(y_ref.dtype)


def _ssd(z, xbc, dt, conv_w, conv_b, dt_bias, a_log, d_skip, norm_w):
    b, l, _ = z.shape
    q = SSD_CHUNK
    pad = LANE - SSD_HEADS
    dtb = jnp.pad(dt_bias, (0, pad)).reshape(1, LANE)
    alog = jnp.pad(a_log, (0, pad)).reshape(1, LANE)
    dsk = jnp.repeat(d_skip, SSD_HEAD_DIM).reshape(1, SSD_WIDTH)
    expand = (jnp.arange(LANE)[:, None] == (jnp.arange(SSD_WIDTH)[None, :] // SSD_HEAD_DIM)).astype(F32)
    tok = lambda w: pl.BlockSpec((1, q, w), lambda bi, ci: (bi, ci, 0))
    return pl.pallas_call(
        _ssd_body, grid=(b, l // q),
        in_specs=[tok(SSD_WIDTH), tok(SSD_CONV_CH), tok(LANE), _full((SSD_CONV, SSD_CONV_CH)),
                  _full((1, SSD_CONV_CH)), _full((1, LANE)), _full((1, LANE)), _full((1, SSD_WIDTH)),
                  _full((1, SSD_WIDTH)), _full((LANE, SSD_WIDTH))],
        out_specs=tok(SSD_WIDTH),
        out_shape=jax.ShapeDtypeStruct((b, l, SSD_WIDTH), BF16),
        scratch_shapes=[pltpu.VMEM((q + 8, SSD_CONV_CH), F32),
                        pltpu.VMEM((SSD_HEADS // 2, SSD_STATE, LANE), F32)],
        compiler_params=_cparams(("parallel", "arbitrary")), name="ssd",
    )(z, xbc, dt, conv_w, conv_b.reshape(1, -1), dtb, alog, dsk, norm_w.reshape(1, -1), expand)


def _head_rms(xh, w_row):
    ms = jnp.sum(xh * xh, axis=-1, keepdims=True) * (1.0 / ATT_DIM)
    return xh * lax.rsqrt(ms + NORM_EPS) * w_row


def _mem_kv_body(mem_ref, g_ref, wk_ref, wv_ref, kw_ref, k_ref, vt_ref):
    x = mem_ref[0]
    ms = jnp.mean(x * x, axis=-1, keepdims=True)
    xn = ((x * lax.rsqrt(ms + NORM_EPS)) * g_ref[...]).astype(BF16)
    kp = jnp.dot(xn, wk_ref[...], preferred_element_type=F32)
    vp = jnp.dot(xn, wv_ref[...], preferred_element_type=F32)
    for h in range(ATT_HEADS):
        sl = slice(h * LANE, (h + 1) * LANE)
        k_ref[0, h] = _head_rms(kp[:, sl], kw_ref[...]).astype(BF16)
        vt_ref[0, h] = vp[:, sl].T[0:ATT_DIM, :].astype(BF16)


def _mem_kv(mem, gain, wk, wv, k_norm_w):
    b = mem.shape[0]
    return pl.pallas_call(
        _mem_kv_body, grid=(b,),
        in_specs=[pl.BlockSpec((1, MEM_LEN, D_MODEL), lambda i: (i, 0, 0)), _full((1, D_MODEL)),
                  _full(wk.shape), _full(wv.shape), _full((1, LANE))],
        out_specs=[pl.BlockSpec((1, ATT_HEADS, MEM_LEN, LANE), lambda i: (i, 0, 0, 0)),
                   pl.BlockSpec((1, ATT_HEADS, ATT_DIM, MEM_LEN), lambda i: (i, 0, 0, 0))],
        out_shape=[jax.ShapeDtypeStruct((b, ATT_HEADS, MEM_LEN, LANE), BF16),
                   jax.ShapeDtypeStruct((b, ATT_HEADS, ATT_DIM, MEM_LEN), BF16)],
        compiler_params=_cparams(("parallel",)), name="mem_kv",
    )(mem, gain, wk, wv, k_norm_w)


MAX_BLOCKS = 32
_AUG_ONEHOT = ATT_DIM
_AUG_EXTRA = ATT_DIM + MAX_BLOCKS


def _attn_prep_body(slopes, q_ref, k_ref, v_ref, xq_ref, qw_ref, kw_ref, xqw_ref,
                    qat_ref, ka_ref, vt_ref, xqt_ref, kmean_scr):
    i = pl.program_id(1)
    nb = kmean_scr.shape[1]

    @pl.when(i == 0)
    def _():
        kmean_scr[...] = jnp.zeros_like(kmean_scr)

    n_iota = lax.broadcasted_iota(jnp.int32, (nb, TQ), 0)
    past = n_iota < i
    t_loc = lax.broadcasted_iota(jnp.int32, (32, TQ), 1).astype(F32)
    r32 = lax.broadcasted_iota(jnp.int32, (32, TQ), 0)
    k_lane = lax.broadcasted_iota(jnp.int32, (TQ, LANE), 1)
    s_loc = lax.broadcasted_iota(jnp.int32, (TQ, LANE), 0).astype(F32)
    for h in range(ATT_HEADS):
        hi, lo = slopes[h]
        sl = slice(h * LANE, (h + 1) * LANE)
        qn = _head_rms(q_ref[0, :, sl], qw_ref[...])
        kn = _head_rms(k_ref[0, :, sl], kw_ref[...])
        qn_t = qn.T
        gate = jnp.dot(kmean_scr[h], qn_t, precision=HIGHEST, preferred_element_type=F32)
        gate = jnp.where(past, gate, -jnp.inf)
        cnt = jnp.zeros((nb, TQ), F32)
        for n2 in range(nb):
            gn = gate[n2:n2 + 1, :]
            ahead = jnp.where(gn > gate, 1.0, jnp.where(gn == gate, jnp.where(n_iota > n2, 1.0, 0.0), 0.0))
            cnt = cnt + ahead
        allowed = jnp.where(past, jnp.where(cnt < float(MOBA_TOPK), 1.0, 0.0), 0.0)
        allowed = jnp.where(n_iota == i, 1.0, allowed)
        bias = jnp.where(allowed > 0.5, 0.0, NEG)
        extra = jnp.where(r32 < 2, t_loc, jnp.where(r32 == 2, hi, jnp.where(r32 == 3, lo, 0.0)))
        qat = jnp.concatenate([qn_t[0:ATT_DIM, :] * (1.0 / math.sqrt(ATT_DIM)), bias, extra], axis=0)
        qat_ref[0, h] = qat.astype(BF16)
        kx = jnp.where(k_lane - _AUG_ONEHOT == i, 1.0, 0.0)
        kx = jnp.where(k_lane < _AUG_ONEHOT, 0.0, kx)
        kx = jnp.where(k_lane >= _AUG_EXTRA, 0.0, kx)
        kx = jnp.where(k_lane == _AUG_EXTRA, -hi, kx)
        kx = jnp.where(k_lane == _AUG_EXTRA + 1, -lo, kx)
        kx = jnp.where((k_lane == _AUG_EXTRA + 2) | (k_lane == _AUG_EXTRA + 3), s_loc, kx)
        ka_ref[0, h] = (kn + kx).astype(BF16)
        vt_ref[0, h] = v_ref[0, :, sl].T[0:ATT_DIM, :].astype(BF16)
        xqn = _head_rms(xq_ref[0, :, sl], xqw_ref[...])
        xqt_ref[0, h] = (xqn.T * (1.0 / math.sqrt(ATT_DIM))).astype(BF16)
        kmean_scr[h, pl.ds(i, 1), :] = jnp.sum(kn, axis=0, keepdims=True) * (1.0 / MOBA_BLOCK)


def _attn_prep(q, k, v, xq, qw, kw, xqw):
    b, l, _ = q.shape
    nb = l // MOBA_BLOCK
    assert nb <= MAX_BLOCKS
    slopes = tuple(_bf16_split(s) for s in _SLOPES)
    tok = pl.BlockSpec((1, TQ, ATT_HEADS * LANE), lambda bi, i: (bi, i, 0))
    return pl.pallas_call(
        functools.partial(_attn_prep_body, slopes), grid=(b, nb),
        in_specs=[tok, tok, tok, tok, _full((1, LANE)), _full((1, LANE)), _full((1, LANE))],
        out_specs=[pl.BlockSpec((1, ATT_HEADS, LANE, TQ), lambda bi, i: (bi, 0, 0, i)),
                   pl.BlockSpec((1, ATT_HEADS, TQ, LANE), lambda bi, i: (bi, 0, i, 0)),
                   pl.BlockSpec((1, ATT_HEADS, ATT_DIM, TQ), lambda bi, i: (bi, 0, 0, i)),
                   pl.BlockSpec((1, ATT_HEADS, LANE, TQ), lambda bi, i: (bi, 0, 0, i))],
        out_shape=[jax.ShapeDtypeStruct((b, ATT_HEADS, LANE, l), BF16),
                   jax.ShapeDtypeStruct((b, ATT_HEADS, l, LANE), BF16),
                   jax.ShapeDtypeStruct((b, ATT_HEADS, ATT_DIM, l), BF16),
                   jax.ShapeDtypeStruct((b, ATT_HEADS, LANE, l), BF16)],
        scratch_shapes=[pltpu.VMEM((ATT_HEADS, MAX_BLOCKS, LANE), F32)],
        compiler_params=_cparams(("parallel", "arbitrary")), name="attn_prep",
    )(q, k, v, xq, qw, kw, xqw)


def _attn_body(slope_ref, qat_ref, ka_ref, vt_ref, xqt_ref, mk_ref, mvt_ref, o_ref, om_ref):
    h = pl.program_id(1)
    i = pl.program_id(2)
    qat = qat_ref[0, 0]
    step = slope_ref[h] * float(MOBA_BLOCK)

    base = pl.multiple_of(i * MOBA_BLOCK, MOBA_BLOCK)
    s = jnp.dot(ka_ref[0, 0, pl.ds(base, MOBA_BLOCK), :], qat, preferred_element_type=F32)
    key = lax.broadcasted_iota(jnp.int32, (MOBA_BLOCK, TQ), 0)
    qry = lax.broadcasted_iota(jnp.int32, (MOBA_BLOCK, TQ), 1)
    s = jnp.where(key <= qry, s, NEG)
    m = jnp.max(s, axis=0, keepdims=True)
    p = jnp.exp(s - m)
    l = jnp.sum(p, axis=0, keepdims=True)
    acc = jnp.dot(vt_ref[0, 0, :, pl.ds(base, MOBA_BLOCK)], p.astype(BF16), preferred_element_type=F32)

    def body(j, carry):
        m, l, acc = carry
        off = pl.multiple_of(j * MOBA_BLOCK, MOBA_BLOCK)
        s = jnp.dot(ka_ref[0, 0, pl.ds(off, MOBA_BLOCK), :], qat, preferred_element_type=F32)
        shift = step * (i - j).astype(F32)
        m_new = jnp.maximum(m, jnp.max(s, axis=0, keepdims=True) - shift)
        alpha = jnp.exp(m - m_new)
        p = jnp.exp(s - (m_new + shift))
        l = l * alpha + jnp.sum(p, axis=0, keepdims=True)
        acc = acc * alpha + jnp.dot(vt_ref[0, 0, :, pl.ds(off, MOBA_BLOCK)], p.astype(BF16),
                                    preferred_element_type=F32)
        return m_new, l, acc

    m, l, acc = lax.fori_loop(0, i, body, (m, l, acc))
    o_ref[0, 0] = (acc * (1.0 / l)).T

    sm = jnp.dot(mk_ref[0, 0], xqt_ref[0, 0], preferred_element_type=F32)
    mm = jnp.max(sm, axis=0, keepdims=True)
    pm = jnp.exp(sm - mm)
    lm = jnp.sum(pm, axis=0, keepdims=True)
    am = jnp.dot(mvt_ref[0, 0], pm.astype(BF16), preferred_element_type=F32)
    om_ref[0, 0] = (am * (1.0 / lm)).T


def _attn(qat, ka, vt, xqt, mem_k, mem_vt):
    b, _, _, l = qat.shape
    slopes = jnp.asarray(_SLOPES, F32)
    per_q = lambda r: pl.BlockSpec((1, 1, r, TQ), lambda bi, h, i: (bi, h, 0, i))
    out = pl.BlockSpec((1, 1, TQ, ATT_DIM), lambda bi, h, i: (bi, h, i, 0))
    return pl.pallas_call(
        _attn_body, grid=(b, ATT_HEADS, l // TQ),
        in_specs=[pl.BlockSpec(memory_space=pltpu.SMEM), per_q(LANE),
                  pl.BlockSpec((1, 1, l, LANE), lambda bi, h, i: (bi, h, 0, 0)),
                  pl.BlockSpec((1, 1, ATT_DIM, l), lambda bi, h, i: (bi, h, 0, 0)),
                  per_q(LANE),
                  pl.BlockSpec((1, 1, MEM_LEN, LANE), lambda bi, h, i: (bi, h, 0, 0)),
                  pl.BlockSpec((1, 1, ATT_DIM, MEM_LEN), lambda bi, h, i: (bi, h, 0, 0))],
        out_specs=[out, out],
        out_shape=[jax.ShapeDtypeStruct((b, ATT_HEADS, l, ATT_DIM), F32)] * 2,
        compiler_params=_cparams(("parallel", "parallel", "arbitrary")), name="attn",
    )(slopes, qat, ka, vt, xqt, mem_k, mem_vt)


def _cand_layout():
    slabs = [("r1", 0, 16), ("r1", 1, 8), ("r1", 2, 8), ("r1", 3, 8), ("r2", 0, 16), ("r2", 1, 8), ("r2", 2, 8)]
    pos, valid = [], []
    for kind, fixed, n in slabs:
        for j in range(n):
            r1, r2 = (fixed, j) if kind == "r1" else (j, fixed)
            ok = (r1 + 1) * (r2 + 1) <= PEER_TOPK and ((kind == "r1") or r1 >= 4)
            pos.append(r1 * PEER_TOPK + r2)
            valid.append(ok)
    assert sum(valid) == 50
    return slabs, np.asarray(pos, np.int32), np.asarray(valid)


_CAND_SLABS, _CAND_POS, _CAND_VALID = _cand_layout()
_NCAND = len(_CAND_POS)


def _top16(s, v_scr):
    row = lax.broadcasted_iota(jnp.int32, s.shape, 0)

    def body(r, carry):
        s, rank = carry
        m = jnp.max(s, axis=0, keepdims=True)
        idx = jnp.min(jnp.where(s == m, row, PEER_NKEYS), axis=0, keepdims=True)
        hit = row == idx
        v_scr[pl.ds(r, 1), :] = m
        return jnp.where(hit, -jnp.inf, s), jnp.where(hit, r.astype(F32), rank)

    _, rank = lax.fori_loop(0, PEER_TOPK, body, (s, jnp.full(s.shape, float(PEER_TOPK), F32)))
    return rank


def _peer_prep_body(x_ref, ys_ref, ym_ref, yx_ref, wos_ref, wom_ref, wox_ref, g_ref, wqt_ref,
                    k1_ref, k2_ref, pos_ref, cbias_ref,
                    h1_ref, hnt_ref, r2_ref, f_ref, n_ref, c_ref, q_scr, v1_scr, v2_scr):
    hres = x_ref[...] + jnp.dot(ys_ref[...], wos_ref[...], preferred_element_type=F32)
    for h in range(ATT_HEADS):
        hres = hres + jnp.dot(ym_ref[0, h].astype(BF16), wom_ref[h], preferred_element_type=F32)
        hres = hres + jnp.dot(yx_ref[0, h].astype(BF16), wox_ref[h], preferred_element_type=F32)
    h1_ref[...] = hres
    ms = jnp.mean(hres * hres, axis=-1, keepdims=True)
    hn = (hres * lax.rsqrt(ms + NORM_EPS)) * g_ref[...]
    hn_t = hn.T.astype(BF16)
    hnt_ref[...] = hn_t
    q_scr[...] = jnp.dot(wqt_ref[...], hn_t, preferred_element_type=F32)

    pos = pos_ref[...]
    cbias = cbias_ref[...]
    r16 = lax.broadcasted_iota(jnp.int32, (PEER_TOPK, LANE), 0)
    for half in range(TM_PP // LANE):
        ls = slice(half * LANE, (half + 1) * LANE)

        def head(h, _):
            base = pl.multiple_of(h * (2 * PEER_HALF), 2 * PEER_HALF)
            s1 = jnp.dot(k1_ref[...], q_scr[pl.ds(base, PEER_HALF), ls], precision=HIGHEST,
                         preferred_element_type=F32)
            s2 = jnp.dot(k2_ref[...], q_scr[pl.ds(base + PEER_HALF, PEER_HALF), ls], precision=HIGHEST,
                         preferred_element_type=F32)
            rank1 = _top16(s1, v1_scr)
            rank2 = _top16(s2, v2_scr)
            v1 = v1_scr[...]
            v2 = v2_scr[...]
            parts = []
            for kind, fixed, n in _CAND_SLABS:
                if kind == "r1":
                    parts.append(v1[fixed:fixed + 1, :] + v2[0:n, :])
                else:
                    parts.append(v1[0:n, :] + v2[fixed:fixed + 1, :])
            cand = jnp.concatenate(parts, axis=0) + cbias
            top = v1[0:1, :] + v2[0:1, :]

            def pick(_, carry):
                cand, cnt, zsum = carry
                m = jnp.max(cand, axis=0, keepdims=True)
                p = jnp.min(jnp.where(cand == m, pos, 4 * PEER_TOPK * PEER_TOPK), axis=0, keepdims=True)
                cand = jnp.where(pos == p, -jnp.inf, cand)
                cnt = cnt + jnp.where(r16 == (p >> 4), 1.0, 0.0)
                return cand, cnt, zsum + jnp.exp(m - top)

            _, cnt, zsum = lax.fori_loop(
                0, PEER_TOPK, pick, (cand, jnp.zeros((PEER_TOPK, LANE), F32), jnp.zeros((1, LANE), F32)))
            nsel = jnp.zeros((PEER_NKEYS, LANE), F32)
            for r in range(PEER_TOPK):
                nsel = jnp.where(rank1 == float(r), cnt[r:r + 1, :], nsel)
            r2_ref[h, :, ls] = rank2
            f_ref[h, :, ls] = jnp.exp(s2 - v2[0:1, :])
            n_ref[h, :, ls] = nsel
            c_ref[h, :, ls] = jnp.exp(s1 - v1[0:1, :]) * (1.0 / zsum)
            return 0

        lax.fori_loop(0, PEER_HEADS, head, 0)


def _peer_prep(x2d, y_ssd, y_moba, y_mem, wo_ssd, wo_moba, wo_mem, gain, wq_t, k1, k2):
    t = x2d.shape[0]
    b, _, l, _ = y_moba.shape
    per_b = l // TM_PP
    pos = jnp.asarray(np.broadcast_to(_CAND_POS[:, None], (_NCAND, LANE)))
    cbias = jnp.asarray(np.broadcast_to(np.where(_CAND_VALID, 0.0, -np.inf).astype(np.float32)[:, None],
                                        (_NCAND, LANE)))
    tok = lambda w: pl.BlockSpec((TM_PP, w), lambda i: (i, 0))
    att = pl.BlockSpec((1, ATT_HEADS, TM_PP, ATT_DIM), lambda i: (i // per_b, 0, i % per_b, 0))
    meta = pl.BlockSpec((PEER_HEADS, PEER_NKEYS, TM_PP), lambda i: (0, 0, i))
    meta_shape = jax.ShapeDtypeStruct((PEER_HEADS, PEER_NKEYS, t), F32)
    return pl.pallas_call(
        _peer_prep_body, grid=(t // TM_PP,),
        in_specs=[tok(D_MODEL), tok(SSD_WIDTH), att, att, _full(wo_ssd.shape), _full(wo_moba.shape),
                  _full(wo_mem.shape), _full((1, D_MODEL)), _full(wq_t.shape), _full(k1.shape), _full(k2.shape),
                  _full(pos.shape), _full(cbias.shape)],
        out_specs=[tok(D_MODEL), pl.BlockSpec((D_MODEL, TM_PP), lambda i: (0, i)), meta, meta, meta, meta],
        out_shape=[jax.ShapeDtypeStruct((t, D_MODEL), F32), jax.ShapeDtypeStruct((D_MODEL, t), BF16),
                   meta_shape, meta_shape, meta_shape, meta_shape],
        scratch_shapes=[pltpu.VMEM((PEER_HEADS * 2 * PEER_HALF, TM_PP), F32),
                        pltpu.VMEM((PEER_TOPK, LANE), F32), pltpu.VMEM((PEER_TOPK, LANE), F32)],
        compiler_params=_cparams(("parallel",)), name="peer_prep",
    )(x2d, y_ssd, y_moba, y_mem, wo_ssd, wo_moba, wo_mem, gain, wq_t, k1, k2, pos, cbias)


def _peer_body(wd_ref, wut_ref, hnt_ref, r2_ref, f_ref, n_ref, c_ref, h1_ref, o_ref, acc_scr):
    e = pl.program_id(1)

    @pl.when(e == 0)
    def _():
        acc_scr[...] = jnp.zeros_like(acc_scr)

    s = jnp.dot(wd_ref[...], hnt_ref[...], preferred_element_type=F32)
    act = 0.5 * s * (1.0 + lax.erf(s * math.sqrt(0.5)))
    pieces = []
    for k in range(ET_PEER // PEER_NKEYS):
        a = e * (ET_PEER // PEER_NKEYS) + k
        g = jnp.zeros((PEER_NKEYS, TT_PEER), F32)
        for h in range(PEER_HEADS):
            nrow = n_ref[h, pl.ds(a, 1), :]
            crow = c_ref[h, pl.ds(a, 1), :]
            g = g + jnp.where(r2_ref[h] < nrow, f_ref[h] * crow, 0.0)
        pieces.append((act[k * PEER_NKEYS:(k + 1) * PEER_NKEYS, :] * g).astype(BF16))
    a_t = jnp.concatenate(pieces, axis=0)
    acc_scr[...] += jnp.dot(wut_ref[...], a_t, preferred_element_type=F32)

    @pl.when(e == pl.num_programs(1) - 1)
    def _():
        o_ref[...] = h1_ref[...] + acc_scr[...].T


def _peer(wd, wu_t, hn_t, rank2, f, nsel, c, h1):
    t = h1.shape[0]
    meta = pl.BlockSpec((PEER_HEADS, PEER_NKEYS, TT_PEER), lambda i, e: (0, 0, i))
    return pl.pallas_call(
        _peer_body, grid=(t // TT_PEER, PEER_EXPERTS // ET_PEER),
        in_specs=[pl.BlockSpec((ET_PEER, D_MODEL), lambda i, e: (e, 0)),
                  pl.BlockSpec((D_MODEL, ET_PEER), lambda i, e: (0, e)),
                  pl.BlockSpec((D_MODEL, TT_PEER), lambda i, e: (0, i)),
                  meta, meta, meta, meta,
                  pl.BlockSpec((TT_PEER, D_MODEL), lambda i, e: (i, 0))],
        out_specs=pl.BlockSpec((TT_PEER, D_MODEL), lambda i, e: (i, 0)),
        out_shape=jax.ShapeDtypeStruct((t, D_MODEL), F32),
        scratch_shapes=[pltpu.VMEM((D_MODEL, TT_PEER), F32)],
        compiler_params=_cparams(("parallel", "arbitrary")), name="peer",
    )(wd, wu_t, hn_t, rank2, f, nsel, c, h1)


def _pad_heads(w):
    r = w.shape[0]
    w = w.reshape(r, -1, ATT_DIM)
    return jnp.pad(w, ((0, 0), (0, 0), (0, LANE - ATT_DIM))).reshape(r, -1)


def _pad_row(w):
    return jnp.pad(w, (0, LANE - w.shape[0])).reshape(1, LANE)


def kernel(x, mem, mix_norm_w, w_in, ssd_conv_w, ssd_conv_b, ssd_dt_bias, ssd_a_log, ssd_d, ssd_norm_w,
           moba_q_norm_w, moba_k_norm_w, mem_norm_w, w_mem_kv, xattn_q_norm_w, xattn_k_norm_w, w_out,
           ffn_norm_w, peer_w_query, peer_sub_keys_1, peer_sub_keys_2, peer_expert_down, peer_expert_up):
    b, l, d = x.shape
    depth = w_in.shape[0]
    h = x.reshape(b * l, d)
    for li in range(depth):
        wi = w_in[li]
        o = 0
        cols = {}
        for name, width in (("z", SSD_WIDTH), ("xbc", SSD_CONV_CH), ("dt", SSD_HEADS), ("mq", ATT_WIDTH),
                            ("mk", ATT_WIDTH), ("mv", ATT_WIDTH), ("xq", ATT_WIDTH)):
            cols[name] = wi[:, o:o + width]
            o += width
        w_list = [cols["z"], cols["xbc"], jnp.pad(cols["dt"], ((0, 0), (0, LANE - SSD_HEADS))),
                  _pad_heads(cols["mq"]), _pad_heads(cols["mk"]), _pad_heads(cols["mv"]), _pad_heads(cols["xq"])]
        w_list = [w.astype(BF16) for w in w_list]
        z, xbc, dt, mq, mk, mv, xq = _in_proj(h, mix_norm_w[li].reshape(1, d), w_list)

        y_ssd = _ssd(z.reshape(b, l, -1), xbc.reshape(b, l, -1), dt.reshape(b, l, -1), ssd_conv_w[li],
                     ssd_conv_b[li], ssd_dt_bias[li], ssd_a_log[li], ssd_d[li], ssd_norm_w[li])

        wkv = w_mem_kv[li]
        mem_k, mem_vt = _mem_kv(mem, mem_norm_w[li].reshape(1, d), _pad_heads(wkv[:, :ATT_WIDTH]).astype(BF16),
                                _pad_heads(wkv[:, ATT_WIDTH:]).astype(BF16), _pad_row(xattn_k_norm_w[li]))
        r3 = lambda a: a.reshape(b, l, -1)
        qat, ka, vt, xqt = _attn_prep(r3(mq), r3(mk), r3(mv), r3(xq), _pad_row(moba_q_norm_w[li]),
                                      _pad_row(moba_k_norm_w[li]), _pad_row(xattn_q_norm_w[li]))
        y_moba, y_mem = _attn(qat, ka, vt, xqt, mem_k, mem_vt)

        wo = w_out[li].astype(BF16)
        wo_moba = wo[SSD_WIDTH:SSD_WIDTH + ATT_WIDTH].reshape(ATT_HEADS, ATT_DIM, d)
        wo_mem = wo[SSD_WIDTH + ATT_WIDTH:].reshape(ATT_HEADS, ATT_DIM, d)
        h1, hn_t, rank2, f, nsel, c = _peer_prep(
            h, y_ssd.reshape(b * l, -1), y_moba, y_mem, wo[:SSD_WIDTH], wo_moba, wo_mem,
            ffn_norm_w[li].reshape(1, d), peer_w_query[li].T.astype(BF16), peer_sub_keys_1[li], peer_sub_keys_2[li])
        h = _peer(peer_expert_down[li].astype(BF16), peer_expert_up[li].T.astype(BF16), hn_t, rank2, f, nsel, c, h1)
    return h.reshape(b, l, d)
```

```python
import functools
import math

import numpy as np
import jax
import jax.numpy as jnp
from jax import lax
from jax.experimental import pallas as pl
from jax.experimental.pallas import tpu as pltpu

F32 = jnp.float32
BF16 = jnp.bfloat16
HIGHEST = lax.Precision.HIGHEST

NORM_EPS = 1e-6
D_MODEL = 1024
SSD_HEADS = 8
SSD_HEAD_DIM = 64
SSD_WIDTH = 512
SSD_GROUPS = 2
SSD_STATE = 128
SSD_CONV = 4
SSD_CONV_CH = 1024
ATT_HEADS = 4
ATT_DIM = 64
ATT_WIDTH = 256
MOBA_BLOCK = 256
MOBA_TOPK = 3
MEM_LEN = 256
PEER_HEADS = 8
PEER_NKEYS = 128
PEER_TOPK = 16
PEER_HALF = 64
PEER_EXPERTS = PEER_NKEYS * PEER_NKEYS

LANE = 128
NEG = -1e30
VMEM_LIMIT = 56 * 1024 * 1024

TM_IN = 512
SSD_CHUNK = 256
TQ = MOBA_BLOCK
ATT_GROUP = 4
TM_PP = 256
TT_PEER = 512
ET_PEER = 1024
ES_PEER = 512

_SLOPES = [2.0 ** (-8.0 * (i + 1) / ATT_HEADS) for i in range(ATT_HEADS)]


def _bf16_split(v):
    hi = float(np.float32(v).astype(BF16).astype(np.float32))
    lo = float(np.float32(v - hi).astype(BF16).astype(np.float32))
    return hi, lo


def _cparams(sem):
    return pltpu.CompilerParams(dimension_semantics=sem, vmem_limit_bytes=VMEM_LIMIT)


def _sigmoid(x):
    return 1.0 / (1.0 + jnp.exp(-x))


def _full(shape):
    n = len(shape)
    return pl.BlockSpec(shape, lambda *_: (0,) * n)


def _in_proj_body(x_ref, g_ref, *refs):
    n = len(refs) // 2
    w_refs, o_refs = refs[:n], refs[n:]
    x = x_ref[...]
    ms = jnp.mean(x * x, axis=-1, keepdims=True)
    xn = ((x * lax.rsqrt(ms + NORM_EPS)) * g_ref[...]).astype(BF16)
    for w_ref, o_ref in zip(w_refs, o_refs):
        o_ref[...] = jnp.dot(xn, w_ref[...], preferred_element_type=F32).astype(o_ref.dtype)


def _in_proj(x2d, gain, weights):
    t = x2d.shape[0]
    in_specs = [pl.BlockSpec((TM_IN, D_MODEL), lambda i: (i, 0)), _full((1, D_MODEL))]
    in_specs += [_full(w.shape) for w in weights]
    out_specs = [pl.BlockSpec((TM_IN, w.shape[1]), lambda i: (i, 0)) for w in weights]
    out_shape = [jax.ShapeDtypeStruct((t, w.shape[1]), F32) for w in weights]
    return pl.pallas_call(
        _in_proj_body, grid=(t // TM_IN,), in_specs=in_specs, out_specs=out_specs,
        out_shape=out_shape, compiler_params=_cparams(("parallel",)), name="in_proj",
    )(x2d, gain, *weights)


def _ssd_body(z_ref, xbc_ref, dt_ref, cw_ref, cb_ref, dtb_ref, alog_ref, dskip_ref, nw_ref, e_ref,
              y_ref, ext_scr, state_scr):
    q = SSD_CHUNK
    c = pl.program_id(1)

    @pl.when(c == 0)
    def _():
        ext_scr[0:8, :] = jnp.zeros((8, SSD_CONV_CH), F32)
        state_scr[...] = jnp.zeros_like(state_scr)

    u = xbc_ref[0]
    ext_scr[8:8 + q, :] = u
    acc = cb_ref[...] + cw_ref[3:4, :] * u
    acc = acc + cw_ref[2:3, :] * ext_scr[7:7 + q, :]
    acc = acc + cw_ref[1:2, :] * ext_scr[6:6 + q, :]
    acc = acc + cw_ref[0:1, :] * ext_scr[5:5 + q, :]
    ext_scr[0:8, :] = u[q - 8:q, :]
    act = acc * _sigmoid(acc)
    xs = act[:, 0:SSD_WIDTH]
    bm = act[:, SSD_WIDTH:SSD_WIDTH + SSD_GROUPS * SSD_STATE]
    cm = act[:, SSD_WIDTH + SSD_GROUPS * SSD_STATE:]

    dtr = dt_ref[0] + dtb_ref[...]
    dt = jnp.maximum(dtr, 0.0) + jnp.log(1.0 + jnp.exp(-jnp.abs(dtr)))
    a = dt * (-jnp.exp(alog_ref[...]))
    row = lax.broadcasted_iota(jnp.int32, (q, q), 0)
    col = lax.broadcasted_iota(jnp.int32, (q, q), 1)
    causal = row >= col
    a_cs = jnp.dot(causal.astype(F32), a, precision=HIGHEST, preferred_element_type=F32)
    a_cs_t = a_cs.T
    a_cs_w = jnp.dot(a_cs, e_ref[...], precision=HIGHEST, preferred_element_type=F32)
    dt_w = jnp.dot(dt, e_ref[...], precision=HIGHEST, preferred_element_type=F32)
    total_w = a_cs_w[q - 1:q, :]
    exp_cs_w = jnp.exp(a_cs_w)
    dte_w = jnp.exp(total_w - a_cs_w)
    cd_w = jnp.exp(total_w)
    xdt_w = xs * dt_w
    lane = lax.broadcasted_iota(jnp.int32, (1, LANE), 1)
    first = lane < SSD_HEAD_DIM

    z = z_ref[0]
    gated = []
    for g in range(SSD_GROUPS):
        bg = bm[:, g * SSD_STATE:(g + 1) * SSD_STATE]
        cg = cm[:, g * SSD_STATE:(g + 1) * SSD_STATE].astype(BF16)
        cb = lax.dot_general(cg, bg.astype(BF16), (((1,), (1,)), ((), ())), preferred_element_type=F32)
        bg_t = bg.T.astype(BF16)
        for kk in range(2):
            k = 2 * g + kk
            sl = slice(k * LANE, (k + 1) * LANE)
            xdt = xdt_w[:, sl]
            xdt_b = xdt.astype(BF16)
            yd = []
            for hh in range(2):
                h = 2 * k + hh
                seg = a_cs[:, h:h + 1] - a_cs_t[h:h + 1, :]
                lm = jnp.exp(jnp.where(causal, seg, -jnp.inf))
                yd.append(jnp.dot((cb * lm).astype(BF16), xdt_b, preferred_element_type=F32))
            y = jnp.where(first, yd[0], yd[1])
            s_old = state_scr[k]
            y = y + jnp.dot(cg, s_old.astype(BF16), preferred_element_type=F32) * exp_cs_w[:, sl]
            y = y + xs[:, sl] * dskip_ref[:, sl]
            state_scr[k] = s_old * cd_w[:, sl] + jnp.dot(
                bg_t, (xdt * dte_w[:, sl]).astype(BF16), preferred_element_type=F32)
            zz = z[:, sl]
            gated.append(y * (zz * _sigmoid(zz)))
    for g in range(SSD_GROUPS):
        y0, y1 = gated[2 * g], gated[2 * g + 1]
        ms = (jnp.sum(y0 * y0, axis=-1, keepdims=True)
              + jnp.sum(y1 * y1, axis=-1, keepdims=True)) * (1.0 / (SSD_WIDTH // SSD_GROUPS))
        r = lax.rsqrt(ms + NORM_EPS)
        lo = 2 * g * LANE
        y_ref[0, :, lo:lo + LANE] = (y0 * r * nw_ref[:, lo:lo + LANE]).astype(y_ref.dtype)
        y_ref[0, :, lo + LANE:lo + 2 * LANE] = (y1 * r * nw_ref[:, lo + LANE:lo + 2 * LANE]).astype(y_ref.dtype)


def _ssd(z, xbc, dt, conv_w, conv_b, dt_bias, a_log, d_skip, norm_w):
    b, l, _ = z.shape
    q = SSD_CHUNK
    pad = LANE - SSD_HEADS
    dtb = jnp.pad(dt_bias, (0, pad)).reshape(1, LANE)
    alog = jnp.pad(a_log, (0, pad)).reshape(1, LANE)
    dsk = jnp.repeat(d_skip, SSD_HEAD_DIM).reshape(1, SSD_WIDTH)
    expand = (jnp.arange(LANE)[:, None] == (jnp.arange(SSD_WIDTH)[None, :] // SSD_HEAD_DIM)).astype(F32)
    tok = lambda w: pl.BlockSpec((1, q, w), lambda bi, ci: (bi, ci, 0))
    return pl.pallas_call(
        _ssd_body, grid=(b, l // q),
        in_specs=[tok(SSD_WIDTH), tok(SSD_CONV_CH), tok(LANE), _full((SSD_CONV, SSD_CONV_CH)),
                  _full((1, SSD_CONV_CH)), _full((1, LANE)), _full((1, LANE)), _full((1, SSD_WIDTH)),
                  _full((1, SSD_WIDTH)), _full((LANE, SSD_WIDTH))],
        out_specs=tok(SSD_WIDTH),
        out_shape=jax.ShapeDtypeStruct((b, l, SSD_WIDTH), BF16),
        scratch_shapes=[pltpu.VMEM((q + 8, SSD_CONV_CH), F32),
                        pltpu.VMEM((SSD_HEADS // 2, SSD_STATE, LANE), F32)],
        compiler_params=_cparams(("parallel", "arbitrary")), name="ssd",
    )(z, xbc, dt, conv_w, conv_b.reshape(1, -1), dtb, alog, dsk, norm_w.reshape(1, -1), expand)


def _head_rms(xh, w_row):
    ms = jnp.sum(xh * xh, axis=-1, keepdims=True) * (1.0 / ATT_DIM)
    return xh * lax.rsqrt(ms + NORM_EPS) * w_row


def _mem_kv_body(mem_ref, g_ref, wk_ref, wv_ref, kw_ref, k_ref, vt_ref):
    x = mem_ref[0]
    ms = jnp.mean(x * x, axis=-1, keepdims=True)
    xn = ((x * lax.rsqrt(ms + NORM_EPS)) * g_ref[...]).astype(BF16)
    kp = jnp.dot(xn, wk_ref[...], preferred_element_type=F32)
    vp = jnp.dot(xn, wv_ref[...], preferred_element_type=F32)
    for h in range(ATT_HEADS):
        sl = slice(h * LANE, (h + 1) * LANE)
        k_ref[0, h] = _head_rms(kp[:, sl], kw_ref[...]).astype(BF16)
        vt_ref[0, h] = vp[:, sl].T[0:ATT_DIM, :].astype(BF16)


def _mem_kv(mem, gain, wk, wv, k_norm_w):
    b = mem.shape[0]
    return pl.pallas_call(
        _mem_kv_body, grid=(b,),
        in_specs=[pl.BlockSpec((1, MEM_LEN, D_MODEL), lambda i: (i, 0, 0)), _full((1, D_MODEL)),
                  _full(wk.shape), _full(wv.shape), _full((1, LANE))],
        out_specs=[pl.BlockSpec((1, ATT_HEADS, MEM_LEN, LANE), lambda i: (i, 0, 0, 0)),
                   pl.BlockSpec((1, ATT_HEADS, ATT_DIM, MEM_LEN), lambda i: (i, 0, 0, 0))],
        out_shape=[jax.ShapeDtypeStruct((b, ATT_HEADS, MEM_LEN, LANE), BF16),
                   jax.ShapeDtypeStruct((b, ATT_HEADS, ATT_DIM, MEM_LEN), BF16)],
        compiler_params=_cparams(("parallel",)), name="mem_kv",
    )(mem, gain, wk, wv, k_norm_w)


MAX_BLOCKS = 32
_AUG_ONEHOT = ATT_DIM
_AUG_EXTRA = ATT_DIM + MAX_BLOCKS


def _attn_prep_body(slopes, q_ref, k_ref, v_ref, xq_ref, qw_ref, kw_ref, xqw_ref,
                    qat_ref, ka_ref, kad_ref, vt_ref, xqt_ref, kmean_scr):
    i = pl.program_id(1)
    nb = kmean_scr.shape[1]

    @pl.when(i == 0)
    def _():
        kmean_scr[...] = jnp.zeros_like(kmean_scr)

    n_iota = lax.broadcasted_iota(jnp.int32, (nb, TQ), 0)
    past = n_iota < i
    t_loc = lax.broadcasted_iota(jnp.int32, (32, TQ), 1).astype(F32)
    r32 = lax.broadcasted_iota(jnp.int32, (32, TQ), 0)
    k_lane = lax.broadcasted_iota(jnp.int32, (TQ, LANE), 1)
    s_loc = lax.broadcasted_iota(jnp.int32, (TQ, LANE), 0).astype(F32)
    for h in range(ATT_HEADS):
        hi, lo = slopes[h]
        sl = slice(h * LANE, (h + 1) * LANE)
        qn = _head_rms(q_ref[0, :, sl], qw_ref[...])
        kn = _head_rms(k_ref[0, :, sl], kw_ref[...])
        qn_t = qn.T
        gate = jnp.dot(kmean_scr[h], qn_t, precision=HIGHEST, preferred_element_type=F32)
        gate = jnp.where(past, gate, -jnp.inf)
        cnt = jnp.zeros((nb, TQ), F32)
        for n2 in range(nb):
            gn = gate[n2:n2 + 1, :]
            ahead = jnp.where(gn > gate, 1.0, jnp.where(gn == gate, jnp.where(n_iota > n2, 1.0, 0.0), 0.0))
            cnt = cnt + ahead
        allowed = jnp.where(past, jnp.where(cnt < float(MOBA_TOPK), 1.0, 0.0), 0.0)
        bias = jnp.where(allowed > 0.5, 0.0, NEG)
        extra = jnp.where(r32 < 2, t_loc, 0.0)
        for r, val in ((2, hi), (3, lo), (4, MOBA_BLOCK * hi), (5, MOBA_BLOCK * lo)):
            extra = jnp.where(r32 == r, val, extra)
        extra = jnp.where((r32 == 6) | (r32 == 7), i.astype(F32), extra)
        qat = jnp.concatenate([qn_t[0:ATT_DIM, :] * (1.0 / math.sqrt(ATT_DIM)), bias, extra], axis=0)
        qat_ref[0, h] = qat.astype(BF16)
        kx = jnp.where((k_lane == _AUG_EXTRA + 2) | (k_lane == _AUG_EXTRA + 3), s_loc, 0.0)
        kx = jnp.where((k_lane == _AUG_EXTRA + 4) | (k_lane == _AUG_EXTRA + 5), i.astype(F32), kx)
        for c, val in ((0, -hi), (1, -lo), (6, -MOBA_BLOCK * hi), (7, -MOBA_BLOCK * lo)):
            kx = jnp.where(k_lane == _AUG_EXTRA + c, val, kx)
        kad = kn + kx
        kad_ref[0, h] = kad.astype(BF16)
        ka_ref[0, h] = (kad + jnp.where(k_lane - _AUG_ONEHOT == i, 1.0, 0.0)).astype(BF16)
        vt_ref[0, h] = v_ref[0, :, sl].T[0:ATT_DIM, :].astype(BF16)
        xqn = _head_rms(xq_ref[0, :, sl], xqw_ref[...])
        xqt_ref[0, h] = (xqn.T * (1.0 / math.sqrt(ATT_DIM))).astype(BF16)
        kmean_scr[h, pl.ds(i, 1), :] = jnp.sum(kn, axis=0, keepdims=True) * (1.0 / MOBA_BLOCK)


def _attn_prep(q, k, v, xq, qw, kw, xqw):
    b, l, _ = q.shape
    nb = l // MOBA_BLOCK
    assert nb <= MAX_BLOCKS
    slopes = tuple(_bf16_split(s) for s in _SLOPES)
    tok = pl.BlockSpec((1, TQ, ATT_HEADS * LANE), lambda bi, i: (bi, i, 0))
    return pl.pallas_call(
        functools.partial(_attn_prep_body, slopes), grid=(b, nb),
        in_specs=[tok, tok, tok, tok, _full((1, LANE)), _full((1, LANE)), _full((1, LANE))],
        out_specs=[pl.BlockSpec((1, ATT_HEADS, LANE, TQ), lambda bi, i: (bi, 0, 0, i)),
                   pl.BlockSpec((1, ATT_HEADS, TQ, LANE), lambda bi, i: (bi, 0, i, 0)),
                   pl.BlockSpec((1, ATT_HEADS, TQ, LANE), lambda bi, i: (bi, 0, i, 0)),
                   pl.BlockSpec((1, ATT_HEADS, ATT_DIM, TQ), lambda bi, i: (bi, 0, 0, i)),
                   pl.BlockSpec((1, ATT_HEADS, LANE, TQ), lambda bi, i: (bi, 0, 0, i))],
        out_shape=[jax.ShapeDtypeStruct((b, ATT_HEADS, LANE, l), BF16),
                   jax.ShapeDtypeStruct((b, ATT_HEADS, l, LANE), BF16),
                   jax.ShapeDtypeStruct((b, ATT_HEADS, l, LANE), BF16),
                   jax.ShapeDtypeStruct((b, ATT_HEADS, ATT_DIM, l), BF16),
                   jax.ShapeDtypeStruct((b, ATT_HEADS, LANE, l), BF16)],
        scratch_shapes=[pltpu.VMEM((ATT_HEADS, MAX_BLOCKS, LANE), F32)],
        compiler_params=_cparams(("parallel", "arbitrary")), name="attn_prep",
    )(q, k, v, xq, qw, kw, xqw)


def _attn_body(qat_ref, ka_ref, kad_ref, vt_ref, xqt_ref, mk_ref, mvt_ref, o_ref, om_ref):
    i = pl.program_id(2)
    qat = qat_ref[0, 0]

    base = pl.multiple_of(i * MOBA_BLOCK, MOBA_BLOCK)
    s = jnp.dot(kad_ref[0, 0], qat, preferred_element_type=F32)
    key = lax.broadcasted_iota(jnp.int32, (MOBA_BLOCK, TQ), 0)
    qry = lax.broadcasted_iota(jnp.int32, (MOBA_BLOCK, TQ), 1)
    s = jnp.where(key <= qry, s, NEG)
    m = jnp.max(s, axis=0, keepdims=True)
    p = jnp.exp(s - m)
    l = jnp.sum(p, axis=0, keepdims=True)
    acc = jnp.dot(vt_ref[0, 0, :, pl.ds(base, MOBA_BLOCK)], p.astype(BF16), preferred_element_type=F32)

    span = ATT_GROUP * MOBA_BLOCK

    def body(g, carry):
        m, l, acc = carry
        off = pl.multiple_of(g * span, span)
        s = jnp.dot(ka_ref[0, 0, pl.ds(off, span), :], qat, preferred_element_type=F32)
        m_new = jnp.maximum(m, jnp.max(s, axis=0, keepdims=True))
        alpha = jnp.exp(m - m_new)
        p = jnp.exp(s - m_new)
        l = l * alpha + jnp.sum(p, axis=0, keepdims=True)
        acc = acc * alpha + jnp.dot(vt_ref[0, 0, :, pl.ds(off, span)], p.astype(BF16),
                                    preferred_element_type=F32)
        return m_new, l, acc

    m, l, acc = lax.fori_loop(0, (i + ATT_GROUP - 1) // ATT_GROUP, body, (m, l, acc))
    o_ref[0, 0] = (acc * (1.0 / l)).T

    sm = jnp.dot(mk_ref[0, 0], xqt_ref[0, 0], preferred_element_type=F32)
    mm = jnp.max(sm, axis=0, keepdims=True)
    pm = jnp.exp(sm - mm)
    lm = jnp.sum(pm, axis=0, keepdims=True)
    am = jnp.dot(mvt_ref[0, 0], pm.astype(BF16), preferred_element_type=F32)
    om_ref[0, 0] = (am * (1.0 / lm)).T


def _attn(qat, ka, kad, vt, xqt, mem_k, mem_vt):
    b, _, _, l = qat.shape
    assert l % (ATT_GROUP * MOBA_BLOCK) == 0
    per_q = lambda r: pl.BlockSpec((1, 1, r, TQ), lambda bi, h, i: (bi, h, 0, i))
    out = pl.BlockSpec((1, 1, TQ, ATT_DIM), lambda bi, h, i: (bi, h, i, 0))
    return pl.pallas_call(
        _attn_body, grid=(b, ATT_HEADS, l // TQ),
        in_specs=[per_q(LANE),
                  pl.BlockSpec((1, 1, l, LANE), lambda bi, h, i: (bi, h, 0, 0)),
                  pl.BlockSpec((1, 1, TQ, LANE), lambda bi, h, i: (bi, h, i, 0)),
                  pl.BlockSpec((1, 1, ATT_DIM, l), lambda bi, h, i: (bi, h, 0, 0)),
                  per_q(LANE),
                  pl.BlockSpec((1, 1, MEM_LEN, LANE), lambda bi, h, i: (bi, h, 0, 0)),
                  pl.BlockSpec((1, 1, ATT_DIM, MEM_LEN), lambda bi, h, i: (bi, h, 0, 0))],
        out_specs=[out, out],
        out_shape=[jax.ShapeDtypeStruct((b, ATT_HEADS, l, ATT_DIM), F32)] * 2,
        compiler_params=_cparams(("parallel", "parallel", "arbitrary")), name="attn",
    )(qat, ka, kad, vt, xqt, mem_k, mem_vt)


def _cand_layout():
    slabs = [("r1", 0, 16), ("r1", 1, 8), ("r1", 2, 8), ("r1", 3, 8), ("r2", 0, 16), ("r2", 1, 8), ("r2", 2, 8)]
    pos, valid = [], []
    for kind, fixed, n in slabs:
        for j in range(n):
            r1, r2 = (fixed, j) if kind == "r1" else (j, fixed)
            ok = (r1 + 1) * (r2 + 1) <= PEER_TOPK and ((kind == "r1") or r1 >= 4)
            pos.append(r1 * PEER_TOPK + r2)
            valid.append(ok)
    assert sum(valid) == 50
    return slabs, np.asarray(pos, np.int32), np.asarray(valid)


_CAND_SLABS, _CAND_POS, _CAND_VALID = _cand_layout()
_NCAND = len(_CAND_POS)


def _pack_rows(x):
    return pltpu.bitcast(x.astype(BF16), jnp.uint32)


def _unpack_rows(x):
    return pltpu.bitcast(x, BF16)


def _top16_pair(s1, s2, v1_scr, v2_scr, i1_scr):
    row = lax.broadcasted_iota(jnp.int32, s1.shape, 0)

    def argmax(s):
        m = jnp.max(s, axis=0, keepdims=True)
        return m, jnp.min(jnp.where(s == m, row, PEER_NKEYS), axis=0, keepdims=True)

    def body(r, carry):
        s1, s2, rank2 = carry
        m1, idx1 = argmax(s1)
        m2, idx2 = argmax(s2)
        v1_scr[pl.ds(r, 1), :] = m1
        v2_scr[pl.ds(r, 1), :] = m2
        i1_scr[pl.ds(r, 1), :] = idx1
        hit2 = row == idx2
        return (jnp.where(row == idx1, -jnp.inf, s1), jnp.where(hit2, -jnp.inf, s2),
                jnp.where(hit2, r.astype(F32), rank2))

    init = (s1, s2, jnp.full(s2.shape, float(PEER_TOPK), F32))
    return lax.fori_loop(0, PEER_TOPK, body, init)[2]


def _peer_prep_body(x_ref, ys_ref, ym_ref, yx_ref, wos_ref, wom_ref, wox_ref, g_ref, wqt_ref,
                    k1_ref, k2_ref, pos_ref, cbias_ref,
                    h1_ref, hnt_ref, r2_ref, f_ref, n_ref, c_ref, q_scr, v1_scr, v2_scr, i1_scr):
    hres = x_ref[...] + jnp.dot(ys_ref[...], wos_ref[...], preferred_element_type=F32)
    for h in range(ATT_HEADS):
        hres = hres + jnp.dot(ym_ref[0, h].astype(BF16), wom_ref[h], preferred_element_type=F32)
        hres = hres + jnp.dot(yx_ref[0, h].astype(BF16), wox_ref[h], preferred_element_type=F32)
    h1_ref[...] = hres
    ms = jnp.mean(hres * hres, axis=-1, keepdims=True)
    hn = (hres * lax.rsqrt(ms + NORM_EPS)) * g_ref[...]
    hn_t = hn.T.astype(BF16)
    hnt_ref[...] = hn_t
    q_scr[...] = jnp.dot(wqt_ref[...], hn_t, preferred_element_type=F32)

    pos = pos_ref[...]
    cbias = cbias_ref[...]
    r16 = lax.broadcasted_iota(jnp.int32, (PEER_TOPK, TM_PP), 0)
    key_row = lax.broadcasted_iota(jnp.int32, (PEER_NKEYS, TM_PP), 0)

    def head(h, _):
        base = pl.multiple_of(h * (2 * PEER_HALF), 2 * PEER_HALF)
        s1 = jnp.dot(k1_ref[...], q_scr[pl.ds(base, PEER_HALF), :], precision=HIGHEST,
                     preferred_element_type=F32)
        s2 = jnp.dot(k2_ref[...], q_scr[pl.ds(base + PEER_HALF, PEER_HALF), :], precision=HIGHEST,
                     preferred_element_type=F32)
        rank2 = _top16_pair(s1, s2, v1_scr, v2_scr, i1_scr)
        v1 = v1_scr[...]
        v2 = v2_scr[...]
        parts = []
        for kind, fixed, n in _CAND_SLABS:
            if kind == "r1":
                parts.append(v1[fixed:fixed + 1, :] + v2[0:n, :])
            else:
                parts.append(v1[0:n, :] + v2[fixed:fixed + 1, :])
        cand = jnp.concatenate(parts, axis=0) + cbias
        top = v1[0:1, :] + v2[0:1, :]

        def pick(_, carry):
            cand, cnt, zsum = carry
            m = jnp.max(cand, axis=0, keepdims=True)
            p = jnp.min(jnp.where(cand == m, pos, 4 * PEER_TOPK * PEER_TOPK), axis=0, keepdims=True)
            cand = jnp.where(pos == p, -jnp.inf, cand)
            cnt = cnt + jnp.where(r16 == (p >> 4), 1.0, 0.0)
            return cand, cnt, zsum + jnp.exp(m - top)

        _, cnt, zsum = lax.fori_loop(
            0, PEER_TOPK, pick, (cand, jnp.zeros((PEER_TOPK, TM_PP), F32), jnp.zeros((1, TM_PP), F32)))
        i1 = i1_scr[...]
        nsel = jnp.zeros((PEER_NKEYS, TM_PP), F32)
        for r in range(PEER_TOPK):
            nsel = jnp.where(key_row == i1[r:r + 1, :], cnt[r:r + 1, :], nsel)
        r2_ref[h] = _pack_rows(rank2)
        f_ref[h] = _pack_rows(jnp.exp(s2 - v2[0:1, :]))
        cw = jnp.exp(s1 - v1[0:1, :]) * (0.5 / zsum)
        for lc in range(TM_PP // LANE):
            n_ref[h, lc] = nsel[:, lc * LANE:(lc + 1) * LANE]
            c_ref[h, lc] = cw[:, lc * LANE:(lc + 1) * LANE]
        return 0

    lax.fori_loop(0, PEER_HEADS, head, 0)


def _peer_prep(x2d, y_ssd, y_moba, y_mem, wo_ssd, wo_moba, wo_mem, gain, wq_t, k1, k2):
    t = x2d.shape[0]
    b, _, l, _ = y_moba.shape
    per_b = l // TM_PP
    pos = jnp.asarray(np.broadcast_to(_CAND_POS[:, None], (_NCAND, TM_PP)))
    cbias = jnp.asarray(np.broadcast_to(np.where(_CAND_VALID, 0.0, -np.inf).astype(np.float32)[:, None],
                                        (_NCAND, TM_PP)))
    tok = lambda w: pl.BlockSpec((TM_PP, w), lambda i: (i, 0))
    att = pl.BlockSpec((1, ATT_HEADS, TM_PP, ATT_DIM), lambda i: (i // per_b, 0, i % per_b, 0))
    meta = pl.BlockSpec((PEER_HEADS, TM_PP // LANE, PEER_NKEYS, LANE), lambda i: (0, i, 0, 0))
    meta_f32 = jax.ShapeDtypeStruct((PEER_HEADS, t // LANE, PEER_NKEYS, LANE), F32)
    packed = pl.BlockSpec((PEER_HEADS, PEER_NKEYS // 2, TM_PP), lambda i: (0, 0, i))
    meta_pk = jax.ShapeDtypeStruct((PEER_HEADS, PEER_NKEYS // 2, t), jnp.uint32)
    return pl.pallas_call(
        _peer_prep_body, grid=(t // TM_PP,),
        in_specs=[tok(D_MODEL), tok(SSD_WIDTH), att, att, _full(wo_ssd.shape), _full(wo_moba.shape),
                  _full(wo_mem.shape), _full((1, D_MODEL)), _full(wq_t.shape), _full(k1.shape), _full(k2.shape),
                  _full(pos.shape), _full(cbias.shape)],
        out_specs=[tok(D_MODEL), pl.BlockSpec((D_MODEL, TM_PP), lambda i: (0, i)), packed, packed, meta, meta],
        out_shape=[jax.ShapeDtypeStruct((t, D_MODEL), F32), jax.ShapeDtypeStruct((D_MODEL, t), BF16),
                   meta_pk, meta_pk, meta_f32, meta_f32],
        scratch_shapes=[pltpu.VMEM((PEER_HEADS * 2 * PEER_HALF, TM_PP), F32),
                        pltpu.VMEM((PEER_TOPK, TM_PP), F32), pltpu.VMEM((PEER_TOPK, TM_PP), F32),
                        pltpu.VMEM((PEER_TOPK, TM_PP), jnp.int32)],
        compiler_params=_cparams(("parallel",)), name="peer_prep",
    )(x2d, y_ssd, y_moba, y_mem, wo_ssd, wo_moba, wo_mem, gain, wq_t, k1, k2, pos, cbias)


def _peer_body(wd_ref, wut_ref, hnt_ref, r2_ref, f_ref, n_ref, c_ref, h1_ref, o_ref, acc_scr, s_scr, a_scr):
    e = pl.program_id(1)

    @pl.when(e == 0)
    def _():
        acc_scr[...] = jnp.zeros_like(acc_scr)

    nslab = ET_PEER // ES_PEER
    hnt = hnt_ref[...]
    for u in range(nslab):
        rows = slice(u * ES_PEER, (u + 1) * ES_PEER)
        s_scr[rows, :] = jnp.dot(wd_ref[rows, :], hnt, preferred_element_type=F32)
    zero = jnp.zeros((PEER_NKEYS, LANE), BF16)

    def bcast(row):
        return pltpu.repeat(jnp.broadcast_to(row, (16, LANE)).astype(BF16), PEER_NKEYS // 16, axis=0)

    for k in range(ET_PEER // PEER_NKEYS):
        a = e * (ET_PEER // PEER_NKEYS) + k
        rows = slice(k * PEER_NKEYS, (k + 1) * PEER_NKEYS)
        for lc in range(TT_PEER // LANE):
            ls = slice(lc * LANE, (lc + 1) * LANE)
            s = s_scr[rows, ls]
            act = s * (1.0 + lax.erf(s * math.sqrt(0.5)))
            g = zero
            for h in range(PEER_HEADS):
                nrow = bcast(n_ref[h, lc, pl.ds(a, 1), :])
                crow = bcast(c_ref[h, lc, pl.ds(a, 1), :])
                g = g + jnp.where(_unpack_rows(r2_ref[h, :, ls]) < nrow, _unpack_rows(f_ref[h, :, ls]) * crow, zero)
            a_scr[k * (PEER_NKEYS // 2):(k + 1) * (PEER_NKEYS // 2), ls] = pltpu.bitcast(
                act.astype(BF16) * g, jnp.uint32)
    part = None
    for u in range(nslab):
        rows = slice(u * ES_PEER, (u + 1) * ES_PEER)
        a_t = _unpack_rows(a_scr[u * (ES_PEER // 2):(u + 1) * (ES_PEER // 2), :])
        d = jnp.dot(wut_ref[:, rows], a_t, preferred_element_type=F32)
        part = d if part is None else part + d
    acc_scr[...] += part

    @pl.when(e == pl.num_programs(1) - 1)
    def _():
        o_ref[...] = h1_ref[...] + acc_scr[...].T


def _peer(wd, wu_t, hn_t, rank2, f, nsel, c, h1):
    t = h1.shape[0]
    meta = pl.BlockSpec((PEER_HEADS, TT_PEER // LANE, PEER_NKEYS, LANE), lambda i, e: (0, i, 0, 0))
    packed = pl.BlockSpec((PEER_HEADS, PEER_NKEYS // 2, TT_PEER), lambda i, e: (0, 0, i))
    return pl.pallas_call(
        _peer_body, grid=(t // TT_PEER, PEER_EXPERTS // ET_PEER),
        in_specs=[pl.BlockSpec((ET_PEER, D_MODEL), lambda i, e: (e, 0)),
                  pl.BlockSpec((D_MODEL, ET_PEER), lambda i, e: (0, e)),
                  pl.BlockSpec((D_MODEL, TT_PEER), lambda i, e: (0, i)),
                  packed, packed, meta, meta,
                  pl.BlockSpec((TT_PEER, D_MODEL), lambda i, e: (i, 0))],
        out_specs=pl.BlockSpec((TT_PEER, D_MODEL), lambda i, e: (i, 0)),
        out_shape=jax.ShapeDtypeStruct((t, D_MODEL), F32),
        scratch_shapes=[pltpu.VMEM((D_MODEL, TT_PEER), F32), pltpu.VMEM((ET_PEER, TT_PEER), F32),
                        pltpu.VMEM((ET_PEER // 2, TT_PEER), jnp.uint32)],
        compiler_params=_cparams(("parallel", "arbitrary")), name="peer",
    )(wd, wu_t, hn_t, rank2, f, nsel, c, h1)


def _pad_heads(w):
    r = w.shape[0]
    w = w.reshape(r, -1, ATT_DIM)
    return jnp.pad(w, ((0, 0), (0, 0), (0, LANE - ATT_DIM))).reshape(r, -1)


def _pad_row(w):
    return jnp.pad(w, (0, LANE - w.shape[0])).reshape(1, LANE)


def kernel(x, mem, mix_norm_w, w_in, ssd_conv_w, ssd_conv_b, ssd_dt_bias, ssd_a_log, ssd_d, ssd_norm_w,
           moba_q_norm_w, moba_k_norm_w, mem_norm_w, w_mem_kv, xattn_q_norm_w, xattn_k_norm_w, w_out,
           ffn_norm_w, peer_w_query, peer_sub_keys_1, peer_sub_keys_2, peer_expert_down, peer_expert_up):
    b, l, d = x.shape
    depth = w_in.shape[0]
    h = x.reshape(b * l, d)
    for li in range(depth):
        wi = w_in[li]
        o = 0
        cols = {}
        for name, width in (("z", SSD_WIDTH), ("xbc", SSD_CONV_CH), ("dt", SSD_HEADS), ("mq", ATT_WIDTH),
                            ("mk", ATT_WIDTH), ("mv", ATT_WIDTH), ("xq", ATT_WIDTH)):
            cols[name] = wi[:, o:o + width]
            o += width
        w_list = [cols["z"], cols["xbc"], jnp.pad(cols["dt"], ((0, 0), (0, LANE - SSD_HEADS))),
                  _pad_heads(cols["mq"]), _pad_heads(cols["mk"]), _pad_heads(cols["mv"]), _pad_heads(cols["xq"])]
        w_list = [w.astype(BF16) for w in w_list]
        z, xbc, dt, mq, mk, mv, xq = _in_proj(h, mix_norm_w[li].reshape(1, d), w_list)

        y_ssd = _ssd(z.reshape(b, l, -1), xbc.reshape(b, l, -1), dt.reshape(b, l, -1), ssd_conv_w[li],
                     ssd_conv_b[li], ssd_dt_bias[li], ssd_a_log[li], ssd_d[li], ssd_norm_w[li])

        wkv = w_mem_kv[li]
        mem_k, mem_vt = _mem_kv(mem, mem_norm_w[li].reshape(1, d), _pad_heads(wkv[:, :ATT_WIDTH]).astype(BF16),
                                _pad_heads(wkv[:, ATT_WIDTH:]).astype(BF16), _pad_row(xattn_k_norm_w[li]))
        r3 = lambda a: a.reshape(b, l, -1)
        qat, ka, kad, vt, xqt = _attn_prep(r3(mq), r3(mk), r3(mv), r3(xq), _pad_row(moba_q_norm_w[li]),
                                           _pad_row(moba_k_norm_w[li]), _pad_row(xattn_q_norm_w[li]))
        y_moba, y_mem = _attn(qat, ka, kad, vt, xqt, mem_k, mem_vt)

        wo = w_out[li].astype(BF16)
        wo_moba = wo[SSD_WIDTH:SSD_WIDTH + ATT_WIDTH].reshape(ATT_HEADS, ATT_DIM, d)
        wo_mem = wo[SSD_WIDTH + ATT_WIDTH:].reshape(ATT_HEADS, ATT_DIM, d)
        h1, hn_t, rank2, f, nsel, c = _peer_prep(
            h, y_ssd.reshape(b * l, -1), y_moba, y_mem, wo[:SSD_WIDTH], wo_moba, wo_mem,
            ffn_norm_w[li].reshape(1, d), peer_w_query[li].T.astype(BF16), peer_sub_keys_1[li], peer_sub_keys_2[li])
        h = _peer(peer_expert_down[li].astype(BF16), peer_expert_up[li].T.astype(BF16), hn_t, rank2, f, nsel, c, h1)
    return h.reshape(b, l, d)
```

```python
import functools
import math

import numpy as np
import jax
import jax.numpy as jnp
from jax import lax
from jax.experimental import pallas as pl
from jax.experimental.pallas import tpu as pltpu

F32 = jnp.float32
BF16 = jnp.bfloat16
HIGHEST = lax.Precision.HIGHEST

NORM_EPS = 1e-6
D_MODEL = 1024
SSD_HEADS = 8
SSD_HEAD_DIM = 64
SSD_WIDTH = 512
SSD_GROUPS = 2
SSD_STATE = 128
SSD_CONV = 4
SSD_CONV_CH = 1024
ATT_HEADS = 4
ATT_DIM = 64
ATT_WIDTH = 256
MOBA_BLOCK = 256
MOBA_TOPK = 3
MEM_LEN = 256
PEER_HEADS = 8
PEER_NKEYS = 128
PEER_TOPK = 16
PEER_HALF = 64
PEER_EXPERTS = PEER_NKEYS * PEER_NKEYS

LANE = 128
NEG = -1e30
VMEM_LIMIT = 56 * 1024 * 1024

TM_IN = 512
SSD_CHUNK = 256
TQ = MOBA_BLOCK
ATT_GROUP = 4
TM_PP = 256
TT_PEER = 512
ES_PEER = 512

_SLOPES = [2.0 ** (-8.0 * (i + 1) / ATT_HEADS) for i in range(ATT_HEADS)]


def _bf16_split(v):
    hi = float(np.float32(v).astype(BF16).astype(np.float32))
    lo = float(np.float32(v - hi).astype(BF16).astype(np.float32))
    return hi, lo


def _cparams(sem):
    return pltpu.CompilerParams(dimension_semantics=sem, vmem_limit_bytes=VMEM_LIMIT)


def _sigmoid(x):
    return 1.0 / (1.0 + jnp.exp(-x))


def _full(shape):
    n = len(shape)
    return pl.BlockSpec(shape, lambda *_: (0,) * n)


def _in_proj_body(x_ref, g_ref, *refs):
    n = len(refs) // 2
    w_refs, o_refs = refs[:n], refs[n:]
    x = x_ref[...]
    ms = jnp.mean(x * x, axis=-1, keepdims=True)
    xn = ((x * lax.rsqrt(ms + NORM_EPS)) * g_ref[...]).astype(BF16)
    for w_ref, o_ref in zip(w_refs, o_refs):
        o_ref[...] = jnp.dot(xn, w_ref[...], preferred_element_type=F32).astype(o_ref.dtype)


def _in_proj(x2d, gain, weights):
    t = x2d.shape[0]
    in_specs = [pl.BlockSpec((TM_IN, D_MODEL), lambda i: (i, 0)), _full((1, D_MODEL))]
    in_specs += [_full(w.shape) for w in weights]
    out_specs = [pl.BlockSpec((TM_IN, w.shape[1]), lambda i: (i, 0)) for w in weights]
    out_shape = [jax.ShapeDtypeStruct((t, w.shape[1]), F32) for w in weights]
    return pl.pallas_call(
        _in_proj_body, grid=(t // TM_IN,), in_specs=in_specs, out_specs=out_specs,
        out_shape=out_shape, compiler_params=_cparams(("parallel",)), name="in_proj",
    )(x2d, gain, *weights)


def _ssd_body(z_ref, xbc_ref, dt_ref, cw_ref, cb_ref, dtb_ref, alog_ref, dskip_ref, nw_ref, e_ref,
              y_ref, ext_scr, state_scr):
    q = SSD_CHUNK
    c = pl.program_id(1)

    @pl.when(c == 0)
    def _():
        ext_scr[0:8, :] = jnp.zeros((8, SSD_CONV_CH), F32)
        state_scr[...] = jnp.zeros_like(state_scr)

    u = xbc_ref[0]
    ext_scr[8:8 + q, :] = u
    acc = cb_ref[...] + cw_ref[3:4, :] * u
    acc = acc + cw_ref[2:3, :] * ext_scr[7:7 + q, :]
    acc = acc + cw_ref[1:2, :] * ext_scr[6:6 + q, :]
    acc = acc + cw_ref[0:1, :] * ext_scr[5:5 + q, :]
    ext_scr[0:8, :] = u[q - 8:q, :]
    act = acc * _sigmoid(acc)
    xs = act[:, 0:SSD_WIDTH]
    bm = act[:, SSD_WIDTH:SSD_WIDTH + SSD_GROUPS * SSD_STATE]
    cm = act[:, SSD_WIDTH + SSD_GROUPS * SSD_STATE:]

    dtr = dt_ref[0] + dtb_ref[...]
    dt = jnp.maximum(dtr, 0.0) + jnp.log(1.0 + jnp.exp(-jnp.abs(dtr)))
    a = dt * (-jnp.exp(alog_ref[...]))
    row = lax.broadcasted_iota(jnp.int32, (q, q), 0)
    col = lax.broadcasted_iota(jnp.int32, (q, q), 1)
    causal = row >= col
    a_cs = jnp.dot(causal.astype(F32), a, precision=HIGHEST, preferred_element_type=F32)
    a_cs_t = a_cs.T
    a_cs_w = jnp.dot(a_cs, e_ref[...], precision=HIGHEST, preferred_element_type=F32)
    dt_w = jnp.dot(dt, e_ref[...], precision=HIGHEST, preferred_element_type=F32)
    total_w = a_cs_w[q - 1:q, :]
    exp_cs_w = jnp.exp(a_cs_w)
    dte_w = jnp.exp(total_w - a_cs_w)
    cd_w = jnp.exp(total_w)
    xdt_w = xs * dt_w
    lane = lax.broadcasted_iota(jnp.int32, (1, LANE), 1)
    first = lane < SSD_HEAD_DIM

    z = z_ref[0]
    gated = []
    for g in range(SSD_GROUPS):
        bg = bm[:, g * SSD_STATE:(g + 1) * SSD_STATE]
        cg = cm[:, g * SSD_STATE:(g + 1) * SSD_STATE].astype(BF16)
        cb = lax.dot_general(cg, bg.astype(BF16), (((1,), (1,)), ((), ())), preferred_element_type=F32)
        bg_t = bg.T.astype(BF16)
        for kk in range(2):
            k = 2 * g + kk
            sl = slice(k * LANE, (k + 1) * LANE)
            xdt = xdt_w[:, sl]
            xdt_b = xdt.astype(BF16)
            yd = []
            for hh in range(2):
                h = 2 * k + hh
                seg = a_cs[:, h:h + 1] - a_cs_t[h:h + 1, :]
                lm = jnp.exp(jnp.where(causal, seg, -jnp.inf))
                yd.append(jnp.dot((cb * lm).astype(BF16), xdt_b, preferred_element_type=F32))
            y = jnp.where(first, yd[0], yd[1])
            s_old = state_scr[k]
            y = y + jnp.dot(cg, s_old.astype(BF16), preferred_element_type=F32) * exp_cs_w[:, sl]
            y = y + xs[:, sl] * dskip_ref[:, sl]
            state_scr[k] = s_old * cd_w[:, sl] + jnp.dot(
                bg_t, (xdt * dte_w[:, sl]).astype(BF16), preferred_element_type=F32)
            zz = z[:, sl]
            gated.append(y * (zz * _sigmoid(zz)))
    for g in range(SSD_GROUPS):
        y0, y1 = gated[2 * g], gated[2 * g + 1]
        ms = (jnp.sum(y0 * y0, axis=-1, keepdims=True)
              + jnp.sum(y1 * y1, axis=-1, keepdims=True)) * (1.0 / (SSD_WIDTH // SSD_GROUPS))
        r = lax.rsqrt(ms + NORM_EPS)
        lo = 2 * g * LANE
        y_ref[0, :, lo:lo + LANE] = (y0 * r * nw_ref[:, lo:lo + LANE]).astype(y_ref.dtype)
        y_ref[0, :, lo + LANE:lo + 2 * LANE] = (y1 * r * nw_ref[:, lo + LANE:lo + 2 * LANE]).astype(y_ref.dtype)


def _ssd(z, xbc, dt, conv_w, conv_b, dt_bias, a_log, d_skip, norm_w):
    b, l, _ = z.shape
    q = SSD_CHUNK
    pad = LANE - SSD_HEADS
    dtb = jnp.pad(dt_bias, (0, pad)).reshape(1, LANE)
    alog = jnp.pad(a_log, (0, pad)).reshape(1, LANE)
    dsk = jnp.repeat(d_skip, SSD_HEAD_DIM).reshape(1, SSD_WIDTH)
    expand = (jnp.arange(LANE)[:, None] == (jnp.arange(SSD_WIDTH)[None, :] // SSD_HEAD_DIM)).astype(F32)
    tok = lambda w: pl.BlockSpec((1, q, w), lambda bi, ci: (bi, ci, 0))
    return pl.pallas_call(
        _ssd_body, grid=(b, l // q),
        in_specs=[tok(SSD_WIDTH), tok(SSD_CONV_CH), tok(LANE), _full((SSD_CONV, SSD_CONV_CH)),
                  _full((1, SSD_CONV_CH)), _full((1, LANE)), _full((1, LANE)), _full((1, SSD_WIDTH)),
                  _full((1, SSD_WIDTH)), _full((LANE, SSD_WIDTH))],
        out_specs=tok(SSD_WIDTH),
        out_shape=jax.ShapeDtypeStruct((b, l, SSD_WIDTH), BF16),
        scratch_shapes=[pltpu.VMEM((q + 8, SSD_CONV_CH), F32),
                        pltpu.VMEM((SSD_HEADS // 2, SSD_STATE, LANE), F32)],
        compiler_params=_cparams(("parallel", "arbitrary")), name="ssd",
    )(z, xbc, dt, conv_w, conv_b.reshape(1, -1), dtb, alog, dsk, norm_w.reshape(1, -1), expand)


V_ROWS = ATT_DIM + 16


def _with_ones_row(v_t):
    n = v_t.shape[1]
    r = lax.broadcasted_iota(jnp.int32, (V_ROWS - ATT_DIM, n), 0)
    return jnp.concatenate([v_t, jnp.where(r == 0, 1.0, 0.0)], axis=0)


def _head_rms(xh, w_row):
    ms = jnp.sum(xh * xh, axis=-1, keepdims=True) * (1.0 / ATT_DIM)
    return xh * lax.rsqrt(ms + NORM_EPS) * w_row


def _mem_kv_body(mem_ref, g_ref, wk_ref, wv_ref, kw_ref, k_ref, vt_ref):
    x = mem_ref[0]
    ms = jnp.mean(x * x, axis=-1, keepdims=True)
    xn = ((x * lax.rsqrt(ms + NORM_EPS)) * g_ref[...]).astype(BF16)
    kp = jnp.dot(xn, wk_ref[...], preferred_element_type=F32)
    vp = jnp.dot(xn, wv_ref[...], preferred_element_type=F32)
    for h in range(ATT_HEADS):
        sl = slice(h * LANE, (h + 1) * LANE)
        k_ref[0, h] = _head_rms(kp[:, sl], kw_ref[...]).astype(BF16)
        vt_ref[0, h] = _with_ones_row(vp[:, sl].T[0:ATT_DIM, :]).astype(BF16)


def _mem_kv(mem, gain, wk, wv, k_norm_w):
    b = mem.shape[0]
    return pl.pallas_call(
        _mem_kv_body, grid=(b,),
        in_specs=[pl.BlockSpec((1, MEM_LEN, D_MODEL), lambda i: (i, 0, 0)), _full((1, D_MODEL)),
                  _full(wk.shape), _full(wv.shape), _full((1, LANE))],
        out_specs=[pl.BlockSpec((1, ATT_HEADS, MEM_LEN, LANE), lambda i: (i, 0, 0, 0)),
                   pl.BlockSpec((1, ATT_HEADS, V_ROWS, MEM_LEN), lambda i: (i, 0, 0, 0))],
        out_shape=[jax.ShapeDtypeStruct((b, ATT_HEADS, MEM_LEN, LANE), BF16),
                   jax.ShapeDtypeStruct((b, ATT_HEADS, V_ROWS, MEM_LEN), BF16)],
        compiler_params=_cparams(("parallel",)), name="mem_kv",
    )(mem, gain, wk, wv, k_norm_w)


MAX_BLOCKS = 32
_AUG_ONEHOT = ATT_DIM
_AUG_EXTRA = ATT_DIM + MAX_BLOCKS


def _attn_prep_body(slopes, q_ref, k_ref, v_ref, xq_ref, qw_ref, kw_ref, xqw_ref,
                    qat_ref, ka_ref, kad_ref, vt_ref, xqt_ref, kmean_scr):
    i = pl.program_id(1)
    nb = kmean_scr.shape[1]

    @pl.when(i == 0)
    def _():
        kmean_scr[...] = jnp.zeros_like(kmean_scr)

    n_iota = lax.broadcasted_iota(jnp.int32, (nb, TQ), 0)
    past = n_iota < i
    t_loc = lax.broadcasted_iota(jnp.int32, (32, TQ), 1).astype(F32)
    r32 = lax.broadcasted_iota(jnp.int32, (32, TQ), 0)
    k_lane = lax.broadcasted_iota(jnp.int32, (TQ, LANE), 1)
    s_loc = lax.broadcasted_iota(jnp.int32, (TQ, LANE), 0).astype(F32)
    blk = lax.convert_element_type(i, F32)
    for h in range(ATT_HEADS):
        hi, lo = slopes[h]
        sl = slice(h * LANE, (h + 1) * LANE)
        qn = _head_rms(q_ref[0, :, sl], qw_ref[...])
        kn = _head_rms(k_ref[0, :, sl], kw_ref[...])
        qn_t = qn.T
        gate = jnp.dot(kmean_scr[h], qn_t, precision=HIGHEST, preferred_element_type=F32)
        gate = jnp.where(past, gate, -jnp.inf)
        cnt = jnp.zeros((nb, TQ), F32)
        for n2 in range(nb):
            gn = gate[n2:n2 + 1, :]
            ahead = jnp.where(gn > gate, 1.0, jnp.where(gn == gate, jnp.where(n_iota > n2, 1.0, 0.0), 0.0))
            cnt = cnt + ahead
        allowed = jnp.where(past, jnp.where(cnt < float(MOBA_TOPK), 1.0, 0.0), 0.0)
        bias = jnp.where(allowed > 0.5, 0.0, NEG)
        extra = jnp.where(r32 < 2, t_loc, 0.0)
        for r, val in ((2, hi), (3, lo), (4, MOBA_BLOCK * hi), (5, MOBA_BLOCK * lo)):
            extra = jnp.where(r32 == r, val, extra)
        extra = jnp.where((r32 == 6) | (r32 == 7), blk, extra)
        qat = jnp.concatenate([qn_t[0:ATT_DIM, :] * (1.0 / math.sqrt(ATT_DIM)), bias, extra], axis=0)
        qat_ref[0, h] = qat.astype(BF16)
        kx = jnp.where((k_lane == _AUG_EXTRA + 2) | (k_lane == _AUG_EXTRA + 3), s_loc, 0.0)
        kx = jnp.where((k_lane == _AUG_EXTRA + 4) | (k_lane == _AUG_EXTRA + 5), blk, kx)
        for c, val in ((0, -hi), (1, -lo), (6, -MOBA_BLOCK * hi), (7, -MOBA_BLOCK * lo)):
            kx = jnp.where(k_lane == _AUG_EXTRA + c, val, kx)
        kad = kn + kx
        kad_ref[0, h] = kad.astype(BF16)
        ka_ref[0, h] = (kad + jnp.where(k_lane - _AUG_ONEHOT == i, 1.0, 0.0)).astype(BF16)
        vt_ref[0, h] = _with_ones_row(v_ref[0, :, sl].T[0:ATT_DIM, :]).astype(BF16)
        xqn = _head_rms(xq_ref[0, :, sl], xqw_ref[...])
        xqt_ref[0, h] = (xqn.T * (1.0 / math.sqrt(ATT_DIM))).astype(BF16)
        kmean_scr[h, pl.ds(i, 1), :] = jnp.sum(kn, axis=0, keepdims=True) * (1.0 / MOBA_BLOCK)


def _attn_prep(q, k, v, xq, qw, kw, xqw):
    b, l, _ = q.shape
    nb = l // MOBA_BLOCK
    assert nb <= MAX_BLOCKS
    slopes = tuple(_bf16_split(s) for s in _SLOPES)
    tok = pl.BlockSpec((1, TQ, ATT_HEADS * LANE), lambda bi, i: (bi, i, 0))
    return pl.pallas_call(
        functools.partial(_attn_prep_body, slopes), grid=(b, nb),
        in_specs=[tok, tok, tok, tok, _full((1, LANE)), _full((1, LANE)), _full((1, LANE))],
        out_specs=[pl.BlockSpec((1, ATT_HEADS, LANE, TQ), lambda bi, i: (bi, 0, 0, i)),
                   pl.BlockSpec((1, ATT_HEADS, TQ, LANE), lambda bi, i: (bi, 0, i, 0)),
                   pl.BlockSpec((1, ATT_HEADS, TQ, LANE), lambda bi, i: (bi, 0, i, 0)),
                   pl.BlockSpec((1, ATT_HEADS, V_ROWS, TQ), lambda bi, i: (bi, 0, 0, i)),
                   pl.BlockSpec((1, ATT_HEADS, LANE, TQ), lambda bi, i: (bi, 0, 0, i))],
        out_shape=[jax.ShapeDtypeStruct((b, ATT_HEADS, LANE, l), BF16),
                   jax.ShapeDtypeStruct((b, ATT_HEADS, l, LANE), BF16),
                   jax.ShapeDtypeStruct((b, ATT_HEADS, l, LANE), BF16),
                   jax.ShapeDtypeStruct((b, ATT_HEADS, V_ROWS, l), BF16),
                   jax.ShapeDtypeStruct((b, ATT_HEADS, LANE, l), BF16)],
        scratch_shapes=[pltpu.VMEM((ATT_HEADS, MAX_BLOCKS, LANE), F32)],
        compiler_params=_cparams(("parallel", "arbitrary")), name="attn_prep",
    )(q, k, v, xq, qw, kw, xqw)


def _normalized_t(acc):
    return (acc[0:ATT_DIM, :] * (1.0 / acc[ATT_DIM:ATT_DIM + 1, :])).T


def _attn_body(qat_ref, ka_ref, kad_ref, vt_ref, xqt_ref, mk_ref, mvt_ref, o_ref, om_ref):
    i = pl.program_id(1)
    base = pl.multiple_of(i * MOBA_BLOCK, MOBA_BLOCK)
    key = lax.broadcasted_iota(jnp.int32, (MOBA_BLOCK, TQ), 0)
    qry = lax.broadcasted_iota(jnp.int32, (MOBA_BLOCK, TQ), 1)

    init = []
    for h in range(ATT_HEADS):
        s = jnp.dot(kad_ref[0, h], qat_ref[0, h], preferred_element_type=F32)
        s = jnp.where(key <= qry, s, NEG)
        m = jnp.max(s, axis=0, keepdims=True)
        p = jnp.exp(s - m).astype(BF16)
        init += [m, jnp.dot(vt_ref[0, h, :, pl.ds(base, MOBA_BLOCK)], p, preferred_element_type=F32)]

    span = ATT_GROUP * MOBA_BLOCK

    def body(g, carry):
        off = pl.multiple_of(g * span, span)
        out = []
        for h in range(ATT_HEADS):
            m, acc = carry[2 * h], carry[2 * h + 1]
            s = jnp.dot(ka_ref[0, h, pl.ds(off, span), :], qat_ref[0, h], preferred_element_type=F32)
            m_new = jnp.maximum(m, jnp.max(s, axis=0, keepdims=True))
            p = jnp.exp(s - m_new).astype(BF16)
            acc = acc * jnp.exp(m - m_new) + jnp.dot(vt_ref[0, h, :, pl.ds(off, span)], p,
                                                     preferred_element_type=F32)
            out += [m_new, acc]
        return tuple(out)

    fin = lax.fori_loop(0, (i + ATT_GROUP - 1) // ATT_GROUP, body, tuple(init))
    for h in range(ATT_HEADS):
        o_ref[0, h] = _normalized_t(fin[2 * h + 1])
        sm = jnp.dot(mk_ref[0, h], xqt_ref[0, h], preferred_element_type=F32)
        pm = jnp.exp(sm - jnp.max(sm, axis=0, keepdims=True)).astype(BF16)
        om_ref[0, h] = _normalized_t(jnp.dot(mvt_ref[0, h], pm, preferred_element_type=F32))


def _attn(qat, ka, kad, vt, xqt, mem_k, mem_vt):
    b, _, _, l = qat.shape
    assert l % (ATT_GROUP * MOBA_BLOCK) == 0
    nh = ATT_HEADS
    per_q = lambda r: pl.BlockSpec((1, nh, r, TQ), lambda bi, i: (bi, 0, 0, i))
    out = pl.BlockSpec((1, nh, TQ, ATT_DIM), lambda bi, i: (bi, 0, i, 0))
    return pl.pallas_call(
        _attn_body, grid=(b, l // TQ),
        in_specs=[per_q(LANE),
                  pl.BlockSpec((1, nh, l, LANE), lambda bi, i: (bi, 0, 0, 0)),
                  pl.BlockSpec((1, nh, TQ, LANE), lambda bi, i: (bi, 0, i, 0)),
                  pl.BlockSpec((1, nh, V_ROWS, l), lambda bi, i: (bi, 0, 0, 0)),
                  per_q(LANE),
                  pl.BlockSpec((1, nh, MEM_LEN, LANE), lambda bi, i: (bi, 0, 0, 0)),
                  pl.BlockSpec((1, nh, V_ROWS, MEM_LEN), lambda bi, i: (bi, 0, 0, 0))],
        out_specs=[out, out],
        out_shape=[jax.ShapeDtypeStruct((b, nh, l, ATT_DIM), F32)] * 2,
        compiler_params=_cparams(("parallel", "arbitrary")), name="attn",
    )(qat, ka, kad, vt, xqt, mem_k, mem_vt)


def _cand_layout():
    slabs = [("r1", 0, 16), ("r1", 1, 8), ("r1", 2, 8), ("r1", 3, 8), ("r2", 0, 16), ("r2", 1, 8), ("r2", 2, 8)]
    pos, valid = [], []
    for kind, fixed, n in slabs:
        for j in range(n):
            r1, r2 = (fixed, j) if kind == "r1" else (j, fixed)
            ok = (r1 + 1) * (r2 + 1) <= PEER_TOPK and ((kind == "r1") or r1 >= 4)
            pos.append(r1 * PEER_TOPK + r2)
            valid.append(ok)
    assert sum(valid) == 50
    return slabs, np.asarray(pos, np.int32), np.asarray(valid)


_CAND_SLABS, _CAND_POS, _CAND_VALID = _cand_layout()
_NCAND = len(_CAND_POS)


def _pack_rows(x):
    return pltpu.bitcast(x.astype(BF16), jnp.uint32)


def _unpack_rows(x):
    return pltpu.bitcast(x, BF16)


def _top16_pair(s1, s2, v1_scr, v2_scr, i1_scr):
    row = lax.broadcasted_iota(jnp.int32, s1.shape, 0)

    def argmax(s):
        m = jnp.max(s, axis=0, keepdims=True)
        return m, jnp.min(jnp.where(s == m, row, PEER_NKEYS), axis=0, keepdims=True)

    def body(r, carry):
        s1, s2, rank2 = carry
        m1, idx1 = argmax(s1)
        m2, idx2 = argmax(s2)
        v1_scr[pl.ds(r, 1), :] = m1
        v2_scr[pl.ds(r, 1), :] = m2
        i1_scr[pl.ds(r, 1), :] = idx1
        hit2 = row == idx2
        return (jnp.where(row == idx1, -jnp.inf, s1), jnp.where(hit2, -jnp.inf, s2),
                jnp.where(hit2, lax.convert_element_type(r, F32), rank2))

    init = (s1, s2, jnp.full(s2.shape, float(PEER_TOPK), F32))
    return lax.fori_loop(0, PEER_TOPK, body, init)[2]


def _peer_prep_body(x_ref, ys_ref, ym_ref, yx_ref, wos_ref, wom_ref, wox_ref, g_ref, wqt_ref,
                    k1_ref, k2_ref, pos_ref, cbias_ref,
                    h1_ref, hnt_ref, r2_ref, f_ref, n_ref, c_ref, q_scr, v1_scr, v2_scr, i1_scr):
    hres = x_ref[...] + jnp.dot(ys_ref[...], wos_ref[...], preferred_element_type=F32)
    for h in range(ATT_HEADS):
        hres = hres + jnp.dot(ym_ref[0, h].astype(BF16), wom_ref[h], preferred_element_type=F32)
        hres = hres + jnp.dot(yx_ref[0, h].astype(BF16), wox_ref[h], preferred_element_type=F32)
    h1_ref[...] = hres
    ms = jnp.mean(hres * hres, axis=-1, keepdims=True)
    hn = (hres * lax.rsqrt(ms + NORM_EPS)) * g_ref[...]
    hn_t = hn.T.astype(BF16)
    hnt_ref[...] = hn_t
    q_scr[...] = jnp.dot(wqt_ref[...], hn_t, preferred_element_type=F32)

    pos = pos_ref[...]
    cbias = cbias_ref[...]
    r16 = lax.broadcasted_iota(jnp.int32, (PEER_TOPK, TM_PP), 0)
    key_row = lax.broadcasted_iota(jnp.int32, (PEER_NKEYS, TM_PP), 0)

    def head(h, _):
        base = pl.multiple_of(h * (2 * PEER_HALF), 2 * PEER_HALF)
        s1 = jnp.dot(k1_ref[...], q_scr[pl.ds(base, PEER_HALF), :], precision=HIGHEST,
                     preferred_element_type=F32)
        s2 = jnp.dot(k2_ref[...], q_scr[pl.ds(base + PEER_HALF, PEER_HALF), :], precision=HIGHEST,
                     preferred_element_type=F32)
        rank2 = _top16_pair(s1, s2, v1_scr, v2_scr, i1_scr)
        v1 = v1_scr[...]
        v2 = v2_scr[...]
        parts = []
        for kind, fixed, n in _CAND_SLABS:
            if kind == "r1":
                parts.append(v1[fixed:fixed + 1, :] + v2[0:n, :])
            else:
                parts.append(v1[0:n, :] + v2[fixed:fixed + 1, :])
        cand = jnp.concatenate(parts, axis=0) + cbias
        top = v1[0:1, :] + v2[0:1, :]

        def pick(_, carry):
            cand, cnt, zsum = carry
            m = jnp.max(cand, axis=0, keepdims=True)
            p = jnp.min(jnp.where(cand == m, pos, 4 * PEER_TOPK * PEER_TOPK), axis=0, keepdims=True)
            cand = jnp.where(pos == p, -jnp.inf, cand)
            cnt = cnt + jnp.where(r16 == (p >> 4), 1.0, 0.0)
            return cand, cnt, zsum + jnp.exp(m - top)

        _, cnt, zsum = lax.fori_loop(
            0, PEER_TOPK, pick, (cand, jnp.zeros((PEER_TOPK, TM_PP), F32), jnp.zeros((1, TM_PP), F32)))
        i1 = i1_scr[...]
        nsel = jnp.zeros((PEER_NKEYS, TM_PP), F32)
        for r in range(PEER_TOPK):
            nsel = jnp.where(key_row == i1[r:r + 1, :], cnt[r:r + 1, :], nsel)
        r2_ref[h] = _pack_rows(rank2)
        f_ref[h] = _pack_rows(jnp.exp(s2 - v2[0:1, :]))
        cw = jnp.exp(s1 - v1[0:1, :]) * (0.5 / zsum)
        for lc in range(TM_PP // LANE):
            n_ref[h, lc] = nsel[:, lc * LANE:(lc + 1) * LANE]
            c_ref[h, lc] = cw[:, lc * LANE:(lc + 1) * LANE]
        return 0

    lax.fori_loop(0, PEER_HEADS, head, 0)


def _peer_prep(x2d, y_ssd, y_moba, y_mem, wo_ssd, wo_moba, wo_mem, gain, wq_t, k1, k2):
    t = x2d.shape[0]
    b, _, l, _ = y_moba.shape
    per_b = l // TM_PP
    pos = jnp.asarray(np.broadcast_to(_CAND_POS[:, None], (_NCAND, TM_PP)))
    cbias = jnp.asarray(np.broadcast_to(np.where(_CAND_VALID, 0.0, -np.inf).astype(np.float32)[:, None],
                                        (_NCAND, TM_PP)))
    tok = lambda w: pl.BlockSpec((TM_PP, w), lambda i: (i, 0))
    att = pl.BlockSpec((1, ATT_HEADS, TM_PP, ATT_DIM), lambda i: (i // per_b, 0, i % per_b, 0))
    meta = pl.BlockSpec((PEER_HEADS, TM_PP // LANE, PEER_NKEYS, LANE), lambda i: (0, i, 0, 0))
    meta_f32 = jax.ShapeDtypeStruct((PEER_HEADS, t // LANE, PEER_NKEYS, LANE), F32)
    packed = pl.BlockSpec((PEER_HEADS, PEER_NKEYS // 2, TM_PP), lambda i: (0, 0, i))
    meta_pk = jax.ShapeDtypeStruct((PEER_HEADS, PEER_NKEYS // 2, t), jnp.uint32)
    return pl.pallas_call(
        _peer_prep_body, grid=(t // TM_PP,),
        in_specs=[tok(D_MODEL), tok(SSD_WIDTH), att, att, _full(wo_ssd.shape), _full(wo_moba.shape),
                  _full(wo_mem.shape), _full((1, D_MODEL)), _full(wq_t.shape), _full(k1.shape), _full(k2.shape),
                  _full(pos.shape), _full(cbias.shape)],
        out_specs=[tok(D_MODEL), pl.BlockSpec((D_MODEL, TM_PP), lambda i: (0, i)), packed, packed, meta, meta],
        out_shape=[jax.ShapeDtypeStruct((t, D_MODEL), F32), jax.ShapeDtypeStruct((D_MODEL, t), BF16),
                   meta_pk, meta_pk, meta_f32, meta_f32],
        scratch_shapes=[pltpu.VMEM((PEER_HEADS * 2 * PEER_HALF, TM_PP), F32),
                        pltpu.VMEM((PEER_TOPK, TM_PP), F32), pltpu.VMEM((PEER_TOPK, TM_PP), F32),
                        pltpu.VMEM((PEER_TOPK, TM_PP), jnp.int32)],
        compiler_params=_cparams(("parallel",)), name="peer_prep",
    )(x2d, y_ssd, y_moba, y_mem, wo_ssd, wo_moba, wo_mem, gain, wq_t, k1, k2, pos, cbias)


_A_PER_SLAB = ES_PEER // PEER_NKEYS


def _peer_gate_chunk(s_ref, a_ref, a_base, k, lc, r2_ref, f_ref, n_ref, c_ref):
    zero = jnp.zeros((PEER_NKEYS, LANE), BF16)

    def bcast(row):
        return pltpu.repeat(jnp.broadcast_to(row, (16, LANE)).astype(BF16), PEER_NKEYS // 16, axis=0)

    a = a_base + k
    rows = slice(k * PEER_NKEYS, (k + 1) * PEER_NKEYS)
    ls = slice(lc * LANE, (lc + 1) * LANE)
    s = s_ref[rows, ls]
    act = s * (1.0 + lax.erf(s * math.sqrt(0.5)))
    g = zero
    for h in range(PEER_HEADS):
        nrow = bcast(n_ref[h, lc, pl.ds(a, 1), :])
        crow = bcast(c_ref[h, lc, pl.ds(a, 1), :])
        g = g + jnp.where(_unpack_rows(r2_ref[h, :, ls]) < nrow, _unpack_rows(f_ref[h, :, ls]) * crow, zero)
    a_ref[k * (PEER_NKEYS // 2):(k + 1) * (PEER_NKEYS // 2), ls] = pltpu.bitcast(act.astype(BF16) * g, jnp.uint32)


def _peer_half_step(gate_args, matmuls):
    chunks = [(k, lc) for lc in range(TT_PEER // LANE) for k in range(_A_PER_SLAB)]
    per = len(chunks) // len(matmuls)
    for j, mm in enumerate(matmuls):
        mm()
        for k, lc in chunks[j * per:(j + 1) * per]:
            _peer_gate_chunk(*gate_args[:3], k, lc, *gate_args[3:])


def _peer_body(wd_a0_ref, wd_b_ref, wd_an_ref, wut_a_ref, wut_bp_ref, hnt_ref, r2_ref, f_ref, n_ref, c_ref,
               h1_ref, o_ref, acc_scr, sa_scr, sb_scr, aa_scr, ab_scr):
    n = pl.program_id(1)
    last = pl.num_programs(1) - 1
    meta = (r2_ref, f_ref, n_ref, c_ref)

    @pl.when(n == 0)
    def _():
        acc_scr[...] = jnp.zeros_like(acc_scr)
        ab_scr[...] = jnp.zeros_like(ab_scr)
        sa_scr[...] = jnp.dot(wd_a0_ref[...], hnt_ref[...], preferred_element_type=F32)

    halves = [slice(j * (TT_PEER // 2), (j + 1) * (TT_PEER // 2)) for j in range(2)]

    def out_mm(wut_ref, a_ref, ls):
        def run():
            acc_scr[:, ls] += jnp.dot(wut_ref[...], _unpack_rows(a_ref[:, ls]), preferred_element_type=F32)
        return run

    def score_mm(wd_ref, s_ref, ls):
        def run():
            s_ref[:, ls] = jnp.dot(wd_ref[...], hnt_ref[:, ls], preferred_element_type=F32)
        return run

    @pl.when(n < last)
    def _():
        a0 = n * (2 * _A_PER_SLAB)
        _peer_half_step((sa_scr, aa_scr, a0) + meta,
                        [out_mm(wut_bp_ref, ab_scr, halves[0]), score_mm(wd_b_ref, sb_scr, halves[0]),
                         out_mm(wut_bp_ref, ab_scr, halves[1]), score_mm(wd_b_ref, sb_scr, halves[1])])
        _peer_half_step((sb_scr, ab_scr, a0 + _A_PER_SLAB) + meta,
                        [out_mm(wut_a_ref, aa_scr, halves[0]), score_mm(wd_an_ref, sa_scr, halves[0]),
                         out_mm(wut_a_ref, aa_scr, halves[1]), score_mm(wd_an_ref, sa_scr, halves[1])])

    @pl.when(n == last)
    def _():
        tail = jnp.dot(wut_bp_ref[...], _unpack_rows(ab_scr[...]), preferred_element_type=F32)
        o_ref[...] = h1_ref[...] + (acc_scr[...] + tail).T


def _peer(wd, wu_t, hn_t, rank2, f, nsel, c, h1):
    t = h1.shape[0]
    nblk = PEER_EXPERTS // ES_PEER
    steps = nblk // 2 + 1
    meta = pl.BlockSpec((PEER_HEADS, TT_PEER // LANE, PEER_NKEYS, LANE), lambda i, n: (0, i, 0, 0))
    packed = pl.BlockSpec((PEER_HEADS, PEER_NKEYS // 2, TT_PEER), lambda i, n: (0, 0, i))
    wd_blk = lambda f: pl.BlockSpec((ES_PEER, D_MODEL), lambda i, n: (f(n), 0))
    wut_blk = lambda f: pl.BlockSpec((D_MODEL, ES_PEER), lambda i, n: (0, f(n)))
    scores = pltpu.VMEM((ES_PEER, TT_PEER), F32)
    acts = pltpu.VMEM((ES_PEER // 2, TT_PEER), jnp.uint32)
    return pl.pallas_call(
        _peer_body, grid=(t // TT_PEER, steps),
        in_specs=[wd_blk(lambda n: 0),
                  wd_blk(lambda n: jnp.minimum(2 * n + 1, nblk - 1)),
                  wd_blk(lambda n: jnp.minimum(2 * n + 2, nblk - 1)),
                  wut_blk(lambda n: jnp.minimum(2 * n, nblk - 1)),
                  wut_blk(lambda n: jnp.maximum(2 * n - 1, 0)),
                  pl.BlockSpec((D_MODEL, TT_PEER), lambda i, n: (0, i)),
                  packed, packed, meta, meta,
                  pl.BlockSpec((TT_PEER, D_MODEL), lambda i, n: (i, 0))],
        out_specs=pl.BlockSpec((TT_PEER, D_MODEL), lambda i, n: (i, 0)),
        out_shape=jax.ShapeDtypeStruct((t, D_MODEL), F32),
        scratch_shapes=[pltpu.VMEM((D_MODEL, TT_PEER), F32), scores, scores, acts, acts],
        compiler_params=_cparams(("parallel", "arbitrary")), name="peer",
    )(wd, wd, wd, wu_t, wu_t, hn_t, rank2, f, nsel, c, h1)


def _pad_heads(w):
    r = w.shape[0]
    w = w.reshape(r, -1, ATT_DIM)
    return jnp.pad(w, ((0, 0), (0, 0), (0, LANE - ATT_DIM))).reshape(r, -1)


def _pad_row(w):
    return jnp.pad(w, (0, LANE - w.shape[0])).reshape(1, LANE)


def kernel(x, mem, mix_norm_w, w_in, ssd_conv_w, ssd_conv_b, ssd_dt_bias, ssd_a_log, ssd_d, ssd_norm_w,
           moba_q_norm_w, moba_k_norm_w, mem_norm_w, w_mem_kv, xattn_q_norm_w, xattn_k_norm_w, w_out,
           ffn_norm_w, peer_w_query, peer_sub_keys_1, peer_sub_keys_2, peer_expert_down, peer_expert_up):
    b, l, d = x.shape
    depth = w_in.shape[0]
    h = x.reshape(b * l, d)
    for li in range(depth):
        wi = w_in[li]
        o = 0
        cols = {}
        for name, width in (("z", SSD_WIDTH), ("xbc", SSD_CONV_CH), ("dt", SSD_HEADS), ("mq", ATT_WIDTH),
                            ("mk", ATT_WIDTH), ("mv", ATT_WIDTH), ("xq", ATT_WIDTH)):
            cols[name] = wi[:, o:o + width]
            o += width
        w_list = [cols["z"], cols["xbc"], jnp.pad(cols["dt"], ((0, 0), (0, LANE - SSD_HEADS))),
                  _pad_heads(cols["mq"]), _pad_heads(cols["mk"]), _pad_heads(cols["mv"]), _pad_heads(cols["xq"])]
        w_list = [w.astype(BF16) for w in w_list]
        z, xbc, dt, mq, mk, mv, xq = _in_proj(h, mix_norm_w[li].reshape(1, d), w_list)

        y_ssd = _ssd(z.reshape(b, l, -1), xbc.reshape(b, l, -1), dt.reshape(b, l, -1), ssd_conv_w[li],
                     ssd_conv_b[li], ssd_dt_bias[li], ssd_a_log[li], ssd_d[li], ssd_norm_w[li])

        wkv = w_mem_kv[li]
        mem_k, mem_vt = _mem_kv(mem, mem_norm_w[li].reshape(1, d), _pad_heads(wkv[:, :ATT_WIDTH]).astype(BF16),
                                _pad_heads(wkv[:, ATT_WIDTH:]).astype(BF16), _pad_row(xattn_k_norm_w[li]))
        r3 = lambda a: a.reshape(b, l, -1)
        qat, ka, kad, vt, xqt = _attn_prep(r3(mq), r3(mk), r3(mv), r3(xq), _pad_row(moba_q_norm_w[li]),
                                           _pad_row(moba_k_norm_w[li]), _pad_row(xattn_q_norm_w[li]))
        y_moba, y_mem = _attn(qat, ka, kad, vt, xqt, mem_k, mem_vt)

        wo = w_out[li].astype(BF16)
        wo_moba = wo[SSD_WIDTH:SSD_WIDTH + ATT_WIDTH].reshape(ATT_HEADS, ATT_DIM, d)
        wo_mem = wo[SSD_WIDTH + ATT_WIDTH:].reshape(ATT_HEADS, ATT_DIM, d)
        h1, hn_t, rank2, f, nsel, c = _peer_prep(
            h, y_ssd.reshape(b * l, -1), y_moba, y_mem, wo[:SSD_WIDTH], wo_moba, wo_mem,
            ffn_norm_w[li].reshape(1, d), peer_w_query[li].T.astype(BF16), peer_sub_keys_1[li], peer_sub_keys_2[li])
        h = _peer(peer_expert_down[li].astype(BF16), peer_expert_up[li].T.astype(BF16), hn_t, rank2, f, nsel, c, h1)
    return h.reshape(b, l, d)
```

```python
import functools
import math

import numpy as np
import jax
import jax.numpy as jnp
from jax import lax
from jax.experimental import pallas as pl
from jax.experimental.pallas import tpu as pltpu

F32 = jnp.float32
BF16 = jnp.bfloat16
HIGHEST = lax.Precision.HIGHEST

NORM_EPS = 1e-6
D_MODEL = 1024
SSD_HEADS = 8
SSD_HEAD_DIM = 64
SSD_WIDTH = 512
SSD_GROUPS = 2
SSD_STATE = 128
SSD_CONV = 4
SSD_CONV_CH = 1024
ATT_HEADS = 4
ATT_DIM = 64
ATT_WIDTH = 256
MOBA_BLOCK = 256
MOBA_TOPK = 3
MEM_LEN = 256
PEER_HEADS = 8
PEER_NKEYS = 128
PEER_TOPK = 16
PEER_HALF = 64
PEER_EXPERTS = PEER_NKEYS * PEER_NKEYS

LANE = 128
NEG = -1e30
VMEM_LIMIT = 56 * 1024 * 1024

TM_IN = 512
SSD_CHUNK = 256
TQ = MOBA_BLOCK
ATT_GROUP = 4
TM_PP = 256
TT_PEER = 512
ES_PEER = 512

_SLOPES = [2.0 ** (-8.0 * (i + 1) / ATT_HEADS) for i in range(ATT_HEADS)]


def _bf16_split(v):
    hi = float(np.float32(v).astype(BF16).astype(np.float32))
    lo = float(np.float32(v - hi).astype(BF16).astype(np.float32))
    return hi, lo


def _cparams(sem):
    return pltpu.CompilerParams(dimension_semantics=sem, vmem_limit_bytes=VMEM_LIMIT)


def _sigmoid(x):
    return 1.0 / (1.0 + jnp.exp(-x))


def _full(shape):
    n = len(shape)
    return pl.BlockSpec(shape, lambda *_: (0,) * n)


def _in_proj_body(x_ref, g_ref, *refs):
    n = len(refs) // 2
    w_refs, o_refs = refs[:n], refs[n:]
    x = x_ref[...]
    ms = jnp.mean(x * x, axis=-1, keepdims=True)
    xn = ((x * lax.rsqrt(ms + NORM_EPS)) * g_ref[...]).astype(BF16)
    for w_ref, o_ref in zip(w_refs, o_refs):
        o_ref[...] = jnp.dot(xn, w_ref[...], preferred_element_type=F32).astype(o_ref.dtype)


def _in_proj(x2d, gain, weights):
    t = x2d.shape[0]
    in_specs = [pl.BlockSpec((TM_IN, D_MODEL), lambda i: (i, 0)), _full((1, D_MODEL))]
    in_specs += [_full(w.shape) for w in weights]
    out_specs = [pl.BlockSpec((TM_IN, w.shape[1]), lambda i: (i, 0)) for w in weights]
    out_shape = [jax.ShapeDtypeStruct((t, w.shape[1]), F32) for w in weights]
    return pl.pallas_call(
        _in_proj_body, grid=(t // TM_IN,), in_specs=in_specs, out_specs=out_specs,
        out_shape=out_shape, compiler_params=_cparams(("parallel",)), name="in_proj",
    )(x2d, gain, *weights)


def _ssd_body(z_ref, xbc_ref, dt_ref, cw_ref, cb_ref, dtb_ref, alog_ref, dskip_ref, nw_ref, e_ref,
              y_ref, ext_scr, state_scr):
    q = SSD_CHUNK
    c = pl.program_id(1)

    @pl.when(c == 0)
    def _():
        ext_scr[0:8, :] = jnp.zeros((8, SSD_CONV_CH), F32)
        state_scr[...] = jnp.zeros_like(state_scr)

    u = xbc_ref[0]
    ext_scr[8:8 + q, :] = u
    acc = cb_ref[...] + cw_ref[3:4, :] * u
    acc = acc + cw_ref[2:3, :] * ext_scr[7:7 + q, :]
    acc = acc + cw_ref[1:2, :] * ext_scr[6:6 + q, :]
    acc = acc + cw_ref[0:1, :] * ext_scr[5:5 + q, :]
    ext_scr[0:8, :] = u[q - 8:q, :]
    act = acc * _sigmoid(acc)
    xs = act[:, 0:SSD_WIDTH]
    bm = act[:, SSD_WIDTH:SSD_WIDTH + SSD_GROUPS * SSD_STATE]
    cm = act[:, SSD_WIDTH + SSD_GROUPS * SSD_STATE:]

    dtr = dt_ref[0] + dtb_ref[...]
    dt = jnp.maximum(dtr, 0.0) + jnp.log(1.0 + jnp.exp(-jnp.abs(dtr)))
    a = dt * (-jnp.exp(alog_ref[...]))
    row = lax.broadcasted_iota(jnp.int32, (q, q), 0)
    col = lax.broadcasted_iota(jnp.int32, (q, q), 1)
    causal = row >= col
    a_cs = jnp.dot(causal.astype(F32), a, precision=HIGHEST, preferred_element_type=F32)
    a_cs_t = a_cs.T
    a_cs_w = jnp.dot(a_cs, e_ref[...], precision=HIGHEST, preferred_element_type=F32)
    dt_w = jnp.dot(dt, e_ref[...], precision=HIGHEST, preferred_element_type=F32)
    total_w = a_cs_w[q - 1:q, :]
    exp_cs_w = jnp.exp(a_cs_w)
    dte_w = jnp.exp(total_w - a_cs_w)
    cd_w = jnp.exp(total_w)
    xdt_w = xs * dt_w
    lane = lax.broadcasted_iota(jnp.int32, (1, LANE), 1)
    first = lane < SSD_HEAD_DIM

    z = z_ref[0]
    gated = []
    for g in range(SSD_GROUPS):
        bg = bm[:, g * SSD_STATE:(g + 1) * SSD_STATE]
        cg = cm[:, g * SSD_STATE:(g + 1) * SSD_STATE].astype(BF16)
        cb = lax.dot_general(cg, bg.astype(BF16), (((1,), (1,)), ((), ())), preferred_element_type=F32)
        bg_t = bg.T.astype(BF16)
        for kk in range(2):
            k = 2 * g + kk
            sl = slice(k * LANE, (k + 1) * LANE)
            xdt = xdt_w[:, sl]
            xdt_b = xdt.astype(BF16)
            yd = []
            for hh in range(2):
                h = 2 * k + hh
                seg = a_cs[:, h:h + 1] - a_cs_t[h:h + 1, :]
                lm = jnp.exp(jnp.where(causal, seg, -jnp.inf))
                yd.append(jnp.dot((cb * lm).astype(BF16), xdt_b, preferred_element_type=F32))
            y = jnp.where(first, yd[0], yd[1])
            s_old = state_scr[k]
            y = y + jnp.dot(cg, s_old.astype(BF16), preferred_element_type=F32) * exp_cs_w[:, sl]
            y = y + xs[:, sl] * dskip_ref[:, sl]
            state_scr[k] = s_old * cd_w[:, sl] + jnp.dot(
                bg_t, (xdt * dte_w[:, sl]).astype(BF16), preferred_element_type=F32)
            zz = z[:, sl]
            gated.append(y * (zz * _sigmoid(zz)))
    for g in range(SSD_GROUPS):
        y0, y1 = gated[2 * g], gated[2 * g + 1]
        ms = (jnp.sum(y0 * y0, axis=-1, keepdims=True)
              + jnp.sum(y1 * y1, axis=-1, keepdims=True)) * (1.0 / (SSD_WIDTH // SSD_GROUPS))
        r = lax.rsqrt(ms + NORM_EPS)
        lo = 2 * g * LANE
        y_ref[0, :, lo:lo + LANE] = (y0 * r * nw_ref[:, lo:lo + LANE]).astype(y_ref.dtype)
        y_ref[0, :, lo + LANE:lo + 2 * LANE] = (y1 * r * nw_ref[:, lo + LANE:lo + 2 * LANE]).astype(y_ref.dtype)


def _ssd(z, xbc, dt, conv_w, conv_b, dt_bias, a_log, d_skip, norm_w):
    b, l, _ = z.shape
    q = SSD_CHUNK
    pad = LANE - SSD_HEADS
    dtb = jnp.pad(dt_bias, (0, pad)).reshape(1, LANE)
    alog = jnp.pad(a_log, (0, pad)).reshape(1, LANE)
    dsk = jnp.repeat(d_skip, SSD_HEAD_DIM).reshape(1, SSD_WIDTH)
    expand = (jnp.arange(LANE)[:, None] == (jnp.arange(SSD_WIDTH)[None, :] // SSD_HEAD_DIM)).astype(F32)
    tok = lambda w: pl.BlockSpec((1, q, w), lambda bi, ci: (bi, ci, 0))
    return pl.pallas_call(
        _ssd_body, grid=(b, l // q),
        in_specs=[tok(SSD_WIDTH), tok(SSD_CONV_CH), tok(LANE), _full((SSD_CONV, SSD_CONV_CH)),
                  _full((1, SSD_CONV_CH)), _full((1, LANE)), _full((1, LANE)), _full((1, SSD_WIDTH)),
                  _full((1, SSD_WIDTH)), _full((LANE, SSD_WIDTH))],
        out_specs=tok(SSD_WIDTH),
        out_shape=jax.ShapeDtypeStruct((b, l, SSD_WIDTH), BF16),
        scratch_shapes=[pltpu.VMEM((q + 8, SSD_CONV_CH), F32),
                        pltpu.VMEM((SSD_HEADS // 2, SSD_STATE, LANE), F32)],
        compiler_params=_cparams(("parallel", "arbitrary")), name="ssd",
    )(z, xbc, dt, conv_w, conv_b.reshape(1, -1), dtb, alog, dsk, norm_w.reshape(1, -1), expand)


V_ROWS = ATT_DIM + 16


def _with_ones_row(v_t):
    n = v_t.shape[1]
    r = lax.broadcasted_iota(jnp.int32, (V_ROWS - ATT_DIM, n), 0)
    return jnp.concatenate([v_t, jnp.where(r == 0, 1.0, 0.0)], axis=0)


def _head_rms(xh, w_row):
    ms = jnp.sum(xh * xh, axis=-1, keepdims=True) * (1.0 / ATT_DIM)
    return xh * lax.rsqrt(ms + NORM_EPS) * w_row


def _mem_kv_body(mem_ref, g_ref, wk_ref, wv_ref, kw_ref, k_ref, vt_ref):
    x = mem_ref[0]
    ms = jnp.mean(x * x, axis=-1, keepdims=True)
    xn = ((x * lax.rsqrt(ms + NORM_EPS)) * g_ref[...]).astype(BF16)
    kp = jnp.dot(xn, wk_ref[...], preferred_element_type=F32)
    vp = jnp.dot(xn, wv_ref[...], preferred_element_type=F32)
    for h in range(ATT_HEADS):
        sl = slice(h * LANE, (h + 1) * LANE)
        k_ref[0, h] = _head_rms(kp[:, sl], kw_ref[...]).astype(BF16)
        vt_ref[0, h] = _with_ones_row(vp[:, sl].T[0:ATT_DIM, :]).astype(BF16)


def _mem_kv(mem, gain, wk, wv, k_norm_w):
    b = mem.shape[0]
    return pl.pallas_call(
        _mem_kv_body, grid=(b,),
        in_specs=[pl.BlockSpec((1, MEM_LEN, D_MODEL), lambda i: (i, 0, 0)), _full((1, D_MODEL)),
                  _full(wk.shape), _full(wv.shape), _full((1, LANE))],
        out_specs=[pl.BlockSpec((1, ATT_HEADS, MEM_LEN, LANE), lambda i: (i, 0, 0, 0)),
                   pl.BlockSpec((1, ATT_HEADS, V_ROWS, MEM_LEN), lambda i: (i, 0, 0, 0))],
        out_shape=[jax.ShapeDtypeStruct((b, ATT_HEADS, MEM_LEN, LANE), BF16),
                   jax.ShapeDtypeStruct((b, ATT_HEADS, V_ROWS, MEM_LEN), BF16)],
        compiler_params=_cparams(("parallel",)), name="mem_kv",
    )(mem, gain, wk, wv, k_norm_w)


MAX_BLOCKS = 32
_AUG_ONEHOT = ATT_DIM
_AUG_EXTRA = ATT_DIM + MAX_BLOCKS


def _attn_prep_body(slopes, q_ref, k_ref, v_ref, xq_ref, qw_ref, kw_ref, xqw_ref,
                    qat_ref, ka_ref, kad_ref, vt_ref, xqt_ref, kmean_scr):
    i = pl.program_id(1)
    nb = kmean_scr.shape[1]

    @pl.when(i == 0)
    def _():
        kmean_scr[...] = jnp.zeros_like(kmean_scr)

    n_iota = lax.broadcasted_iota(jnp.int32, (nb, TQ), 0)
    past = n_iota < i
    t_loc = lax.broadcasted_iota(jnp.int32, (32, TQ), 1).astype(F32)
    r32 = lax.broadcasted_iota(jnp.int32, (32, TQ), 0)
    k_lane = lax.broadcasted_iota(jnp.int32, (TQ, LANE), 1)
    s_loc = lax.broadcasted_iota(jnp.int32, (TQ, LANE), 0).astype(F32)
    blk = lax.convert_element_type(i, F32)
    for h in range(ATT_HEADS):
        hi, lo = slopes[h]
        sl = slice(h * LANE, (h + 1) * LANE)
        qn = _head_rms(q_ref[0, :, sl], qw_ref[...])
        kn = _head_rms(k_ref[0, :, sl], kw_ref[...])
        qn_t = qn.T
        gate = jnp.dot(kmean_scr[h], qn_t, precision=HIGHEST, preferred_element_type=F32)
        gate = jnp.where(past, gate, -jnp.inf)
        cnt = jnp.zeros((nb, TQ), F32)
        for n2 in range(nb):
            gn = gate[n2:n2 + 1, :]
            ahead = jnp.where(gn > gate, 1.0, jnp.where(gn == gate, jnp.where(n_iota > n2, 1.0, 0.0), 0.0))
            cnt = cnt + ahead
        allowed = jnp.where(past, jnp.where(cnt < float(MOBA_TOPK), 1.0, 0.0), 0.0)
        bias = jnp.where(allowed > 0.5, 0.0, NEG)
        extra = jnp.where(r32 < 2, t_loc, 0.0)
        for r, val in ((2, hi), (3, lo), (4, MOBA_BLOCK * hi), (5, MOBA_BLOCK * lo)):
            extra = jnp.where(r32 == r, val, extra)
        extra = jnp.where((r32 == 6) | (r32 == 7), blk, extra)
        qat = jnp.concatenate([qn_t[0:ATT_DIM, :] * (1.0 / math.sqrt(ATT_DIM)), bias, extra], axis=0)
        qat_ref[0, h] = qat.astype(BF16)
        kx = jnp.where((k_lane == _AUG_EXTRA + 2) | (k_lane == _AUG_EXTRA + 3), s_loc, 0.0)
        kx = jnp.where((k_lane == _AUG_EXTRA + 4) | (k_lane == _AUG_EXTRA + 5), blk, kx)
        for c, val in ((0, -hi), (1, -lo), (6, -MOBA_BLOCK * hi), (7, -MOBA_BLOCK * lo)):
            kx = jnp.where(k_lane == _AUG_EXTRA + c, val, kx)
        kad = kn + kx
        kad_ref[0, h] = kad.astype(BF16)
        ka_ref[0, h] = (kad + jnp.where(k_lane - _AUG_ONEHOT == i, 1.0, 0.0)).astype(BF16)
        vt_ref[0, h] = _with_ones_row(v_ref[0, :, sl].T[0:ATT_DIM, :]).astype(BF16)
        xqn = _head_rms(xq_ref[0, :, sl], xqw_ref[...])
        xqt_ref[0, h] = (xqn.T * (1.0 / math.sqrt(ATT_DIM))).astype(BF16)
        kmean_scr[h, pl.ds(i, 1), :] = jnp.sum(kn, axis=0, keepdims=True) * (1.0 / MOBA_BLOCK)


def _attn_prep(q, k, v, xq, qw, kw, xqw):
    b, l, _ = q.shape
    nb = l // MOBA_BLOCK
    assert nb <= MAX_BLOCKS
    slopes = tuple(_bf16_split(s) for s in _SLOPES)
    tok = pl.BlockSpec((1, TQ, ATT_HEADS * LANE), lambda bi, i: (bi, i, 0))
    return pl.pallas_call(
        functools.partial(_attn_prep_body, slopes), grid=(b, nb),
        in_specs=[tok, tok, tok, tok, _full((1, LANE)), _full((1, LANE)), _full((1, LANE))],
        out_specs=[pl.BlockSpec((1, ATT_HEADS, LANE, TQ), lambda bi, i: (bi, 0, 0, i)),
                   pl.BlockSpec((1, ATT_HEADS, TQ, LANE), lambda bi, i: (bi, 0, i, 0)),
                   pl.BlockSpec((1, ATT_HEADS, TQ, LANE), lambda bi, i: (bi, 0, i, 0)),
                   pl.BlockSpec((1, ATT_HEADS, V_ROWS, TQ), lambda bi, i: (bi, 0, 0, i)),
                   pl.BlockSpec((1, ATT_HEADS, LANE, TQ), lambda bi, i: (bi, 0, 0, i))],
        out_shape=[jax.ShapeDtypeStruct((b, ATT_HEADS, LANE, l), BF16),
                   jax.ShapeDtypeStruct((b, ATT_HEADS, l, LANE), BF16),
                   jax.ShapeDtypeStruct((b, ATT_HEADS, l, LANE), BF16),
                   jax.ShapeDtypeStruct((b, ATT_HEADS, V_ROWS, l), BF16),
                   jax.ShapeDtypeStruct((b, ATT_HEADS, LANE, l), BF16)],
        scratch_shapes=[pltpu.VMEM((ATT_HEADS, MAX_BLOCKS, LANE), F32)],
        compiler_params=_cparams(("parallel", "arbitrary")), name="attn_prep",
    )(q, k, v, xq, qw, kw, xqw)


def _normalized_t(acc):
    return (acc[0:ATT_DIM, :] * (1.0 / acc[ATT_DIM:ATT_DIM + 1, :])).T


def _attn_body(qat_ref, ka_ref, kad_ref, vt_ref, xqt_ref, mk_ref, mvt_ref, o_ref, om_ref, s_scr, p_scr):
    i = pl.program_id(1)
    base = pl.multiple_of(i * MOBA_BLOCK, MOBA_BLOCK)
    key = lax.broadcasted_iota(jnp.int32, (MOBA_BLOCK, TQ), 0)
    qry = lax.broadcasted_iota(jnp.int32, (MOBA_BLOCK, TQ), 1)

    span = ATT_GROUP * MOBA_BLOCK
    n_groups = (i + ATT_GROUP - 1) // ATT_GROUP
    last_group = ka_ref.shape[2] // span - 1

    def scores(h, g):
        off = pl.multiple_of(jnp.minimum(g, last_group) * span, span)
        return jnp.dot(ka_ref[0, h, pl.ds(off, span), :], qat_ref[0, h], preferred_element_type=F32)

    def values(h, g, p):
        off = pl.multiple_of(jnp.maximum(g, 0) * span, span)
        return jnp.dot(vt_ref[0, h, :, pl.ds(off, span)], p, preferred_element_type=F32)

    init = []
    for h in range(ATT_HEADS):
        s = jnp.dot(kad_ref[0, h], qat_ref[0, h], preferred_element_type=F32)
        s = jnp.where(key <= qry, s, NEG)
        m = jnp.max(s, axis=0, keepdims=True)
        p = jnp.exp(s - m).astype(BF16)
        acc = jnp.dot(vt_ref[0, h, :, pl.ds(base, MOBA_BLOCK)], p, preferred_element_type=F32)
        s_scr[h] = scores(h, 0)
        p_scr[h] = jnp.zeros((span, TQ), BF16)
        init += [jnp.ones((1, TQ), F32), m, acc]

    def body(g, carry):
        out = []
        for h in range(ATT_HEADS):
            alpha_prev, m, acc = carry[3 * h:3 * h + 3]
            acc = acc * alpha_prev + values(h, g - 1, p_scr[h])
            s = s_scr[h]
            m_new = jnp.maximum(m, jnp.max(s, axis=0, keepdims=True))
            p_scr[h] = jnp.exp(s - m_new).astype(BF16)
            s_scr[h] = scores(h, g + 1)
            out += [jnp.exp(m - m_new), m_new, acc]
        return tuple(out)

    fin = lax.fori_loop(0, n_groups, body, tuple(init))
    for h in range(ATT_HEADS):
        alpha_prev, _, acc = fin[3 * h:3 * h + 3]
        o_ref[0, h] = _normalized_t(acc * alpha_prev + values(h, n_groups - 1, p_scr[h]))
        sm = jnp.dot(mk_ref[0, h], xqt_ref[0, h], preferred_element_type=F32)
        pm = jnp.exp(sm - jnp.max(sm, axis=0, keepdims=True)).astype(BF16)
        om_ref[0, h] = _normalized_t(jnp.dot(mvt_ref[0, h], pm, preferred_element_type=F32))


def _attn(qat, ka, kad, vt, xqt, mem_k, mem_vt):
    b, _, _, l = qat.shape
    assert l % (ATT_GROUP * MOBA_BLOCK) == 0
    nh = ATT_HEADS
    per_q = lambda r: pl.BlockSpec((1, nh, r, TQ), lambda bi, i: (bi, 0, 0, i))
    out = pl.BlockSpec((1, nh, TQ, ATT_DIM), lambda bi, i: (bi, 0, i, 0))
    return pl.pallas_call(
        _attn_body, grid=(b, l // TQ),
        in_specs=[per_q(LANE),
                  pl.BlockSpec((1, nh, l, LANE), lambda bi, i: (bi, 0, 0, 0)),
                  pl.BlockSpec((1, nh, TQ, LANE), lambda bi, i: (bi, 0, i, 0)),
                  pl.BlockSpec((1, nh, V_ROWS, l), lambda bi, i: (bi, 0, 0, 0)),
                  per_q(LANE),
                  pl.BlockSpec((1, nh, MEM_LEN, LANE), lambda bi, i: (bi, 0, 0, 0)),
                  pl.BlockSpec((1, nh, V_ROWS, MEM_LEN), lambda bi, i: (bi, 0, 0, 0))],
        out_specs=[out, out],
        out_shape=[jax.ShapeDtypeStruct((b, nh, l, ATT_DIM), F32)] * 2,
        scratch_shapes=[pltpu.VMEM((nh, ATT_GROUP * MOBA_BLOCK, TQ), F32),
                        pltpu.VMEM((nh, ATT_GROUP * MOBA_BLOCK, TQ), BF16)],
        compiler_params=_cparams(("parallel", "arbitrary")), name="attn",
    )(qat, ka, kad, vt, xqt, mem_k, mem_vt)


def _cand_layout():
    slabs = [("r1", 0, 16), ("r1", 1, 8), ("r1", 2, 8), ("r1", 3, 8), ("r2", 0, 16), ("r2", 1, 8), ("r2", 2, 8)]
    pos, valid = [], []
    for kind, fixed, n in slabs:
        for j in range(n):
            r1, r2 = (fixed, j) if kind == "r1" else (j, fixed)
            ok = (r1 + 1) * (r2 + 1) <= PEER_TOPK and ((kind == "r1") or r1 >= 4)
            pos.append(r1 * PEER_TOPK + r2)
            valid.append(ok)
    assert sum(valid) == 50
    return slabs, np.asarray(pos, np.int32), np.asarray(valid)


_CAND_SLABS, _CAND_POS, _CAND_VALID = _cand_layout()
_NCAND = len(_CAND_POS)


def _pack_rows(x):
    return pltpu.bitcast(x.astype(BF16), jnp.uint32)


def _unpack_rows(x):
    return pltpu.bitcast(x, BF16)


def _top16_pair(w1_scr, w2_scr, rank2_scr, v1_scr, v2_scr, i1_scr):
    row = lax.broadcasted_iota(jnp.int32, (PEER_NKEYS, LANE), 0).astype(F32)
    rank2_scr[...] = jnp.full(rank2_scr.shape, float(PEER_TOPK), F32)

    def body(r, _):
        rf = lax.convert_element_type(r, F32)
        for half in range(TM_PP // LANE):
            ls = slice(half * LANE, (half + 1) * LANE)
            for first, w_scr, v_scr in ((True, w1_scr, v1_scr), (False, w2_scr, v2_scr)):
                s = w_scr[:, ls]
                m = jnp.max(s, axis=0, keepdims=True)
                idx = jnp.min(jnp.where(s == m, row, float(PEER_NKEYS)), axis=0, keepdims=True)
                hit = row == idx
                w_scr[:, ls] = jnp.where(hit, -jnp.inf, s)
                v_scr[half, pl.ds(r, 1), :] = m
                if first:
                    i1_scr[half, pl.ds(r, 1), :] = idx
                else:
                    rank2_scr[:, ls] = jnp.where(hit, rf, rank2_scr[:, ls])
        return 0

    lax.fori_loop(0, PEER_TOPK, body, 0)


def _peer_prep_body(x_ref, ys_ref, ym_ref, yx_ref, wos_ref, wom_ref, wox_ref, g_ref, wqt_ref,
                    k1_ref, k2_ref, pos_ref, cbias_ref,
                    h1_ref, hnt_ref, r2_ref, f_ref, n_ref, c_ref,
                    q_scr, w1_scr, w2_scr, rank2_scr, v1_scr, v2_scr, i1_scr):
    hres = x_ref[...] + jnp.dot(ys_ref[...], wos_ref[...], preferred_element_type=F32)
    for h in range(ATT_HEADS):
        hres = hres + jnp.dot(ym_ref[0, h].astype(BF16), wom_ref[h], preferred_element_type=F32)
        hres = hres + jnp.dot(yx_ref[0, h].astype(BF16), wox_ref[h], preferred_element_type=F32)
    h1_ref[...] = hres
    ms = jnp.mean(hres * hres, axis=-1, keepdims=True)
    hn = (hres * lax.rsqrt(ms + NORM_EPS)) * g_ref[...]
    hn_t = hn.T.astype(BF16)
    hnt_ref[...] = hn_t
    q_scr[...] = jnp.dot(wqt_ref[...], hn_t, preferred_element_type=F32)

    pos = pos_ref[...]
    cbias = cbias_ref[...]
    r16 = lax.broadcasted_iota(jnp.int32, (PEER_TOPK, TM_PP), 0)
    key_row = lax.broadcasted_iota(jnp.int32, (PEER_NKEYS, TM_PP), 0).astype(F32)
    halves = lambda scr: jnp.concatenate([scr[j] for j in range(TM_PP // LANE)], axis=1)

    def head(h, _):
        base = pl.multiple_of(h * (2 * PEER_HALF), 2 * PEER_HALF)
        s1 = jnp.dot(k1_ref[...], q_scr[pl.ds(base, PEER_HALF), :], precision=HIGHEST,
                     preferred_element_type=F32)
        s2 = jnp.dot(k2_ref[...], q_scr[pl.ds(base + PEER_HALF, PEER_HALF), :], precision=HIGHEST,
                     preferred_element_type=F32)
        w1_scr[...] = s1
        w2_scr[...] = s2
        _top16_pair(w1_scr, w2_scr, rank2_scr, v1_scr, v2_scr, i1_scr)
        rank2 = rank2_scr[...]
        v1 = halves(v1_scr)
        v2 = halves(v2_scr)
        parts = []
        for kind, fixed, n in _CAND_SLABS:
            if kind == "r1":
                parts.append(v1[fixed:fixed + 1, :] + v2[0:n, :])
            else:
                parts.append(v1[0:n, :] + v2[fixed:fixed + 1, :])
        cand = jnp.concatenate(parts, axis=0) + cbias
        top = v1[0:1, :] + v2[0:1, :]

        def pick(_, carry):
            cand, cnt, zsum = carry
            m = jnp.max(cand, axis=0, keepdims=True)
            p = jnp.min(jnp.where(cand == m, pos, 4 * PEER_TOPK * PEER_TOPK), axis=0, keepdims=True)
            cand = jnp.where(pos == p, -jnp.inf, cand)
            cnt = cnt + jnp.where(r16 == (p >> 4), 1.0, 0.0)
            return cand, cnt, zsum + jnp.exp(m - top)

        _, cnt, zsum = lax.fori_loop(
            0, PEER_TOPK, pick, (cand, jnp.zeros((PEER_TOPK, TM_PP), F32), jnp.zeros((1, TM_PP), F32)))
        i1 = halves(i1_scr)
        nsel = jnp.zeros((PEER_NKEYS, TM_PP), F32)
        for r in range(PEER_TOPK):
            nsel = jnp.where(key_row == i1[r:r + 1, :], cnt[r:r + 1, :], nsel)
        r2_ref[h] = _pack_rows(rank2)
        f_ref[h] = _pack_rows(jnp.exp(s2 - v2[0:1, :]))
        cw = jnp.exp(s1 - v1[0:1, :]) * (0.5 / zsum)
        for lc in range(TM_PP // LANE):
            n_ref[h, lc] = nsel[:, lc * LANE:(lc + 1) * LANE]
            c_ref[h, lc] = cw[:, lc * LANE:(lc + 1) * LANE]
        return 0

    lax.fori_loop(0, PEER_HEADS, head, 0)


def _peer_prep(x2d, y_ssd, y_moba, y_mem, wo_ssd, wo_moba, wo_mem, gain, wq_t, k1, k2):
    t = x2d.shape[0]
    b, _, l, _ = y_moba.shape
    per_b = l // TM_PP
    pos = jnp.asarray(np.broadcast_to(_CAND_POS[:, None], (_NCAND, TM_PP)))
    cbias = jnp.asarray(np.broadcast_to(np.where(_CAND_VALID, 0.0, -np.inf).astype(np.float32)[:, None],
                                        (_NCAND, TM_PP)))
    tok = lambda w: pl.BlockSpec((TM_PP, w), lambda i: (i, 0))
    att = pl.BlockSpec((1, ATT_HEADS, TM_PP, ATT_DIM), lambda i: (i // per_b, 0, i % per_b, 0))
    meta = pl.BlockSpec((PEER_HEADS, TM_PP // LANE, PEER_NKEYS, LANE), lambda i: (0, i, 0, 0))
    meta_f32 = jax.ShapeDtypeStruct((PEER_HEADS, t // LANE, PEER_NKEYS, LANE), F32)
    packed = pl.BlockSpec((PEER_HEADS, PEER_NKEYS // 2, TM_PP), lambda i: (0, 0, i))
    meta_pk = jax.ShapeDtypeStruct((PEER_HEADS, PEER_NKEYS // 2, t), jnp.uint32)
    return pl.pallas_call(
        _peer_prep_body, grid=(t // TM_PP,),
        in_specs=[tok(D_MODEL), tok(SSD_WIDTH), att, att, _full(wo_ssd.shape), _full(wo_moba.shape),
                  _full(wo_mem.shape), _full((1, D_MODEL)), _full(wq_t.shape), _full(k1.shape), _full(k2.shape),
                  _full(pos.shape), _full(cbias.shape)],
        out_specs=[tok(D_MODEL), pl.BlockSpec((D_MODEL, TM_PP), lambda i: (0, i)), packed, packed, meta, meta],
        out_shape=[jax.ShapeDtypeStruct((t, D_MODEL), F32), jax.ShapeDtypeStruct((D_MODEL, t), BF16),
                   meta_pk, meta_pk, meta_f32, meta_f32],
        scratch_shapes=[pltpu.VMEM((PEER_HEADS * 2 * PEER_HALF, TM_PP), F32)]
        + [pltpu.VMEM((PEER_NKEYS, TM_PP), F32)] * 3
        + [pltpu.VMEM((TM_PP // LANE, PEER_TOPK, LANE), F32)] * 3,
        compiler_params=_cparams(("parallel",)), name="peer_prep",
    )(x2d, y_ssd, y_moba, y_mem, wo_ssd, wo_moba, wo_mem, gain, wq_t, k1, k2, pos, cbias)


_A_PER_SLAB = ES_PEER // PEER_NKEYS


def _peer_gate_chunk(s_ref, a_ref, a_base, k, lc, r2_ref, f_ref, n_ref, c_ref):
    zero = jnp.zeros((PEER_NKEYS, LANE), BF16)

    def bcast(row):
        return pltpu.repeat(jnp.broadcast_to(row, (16, LANE)).astype(BF16), PEER_NKEYS // 16, axis=0)

    a = a_base + k
    rows = slice(k * PEER_NKEYS, (k + 1) * PEER_NKEYS)
    ls = slice(lc * LANE, (lc + 1) * LANE)
    s = s_ref[rows, ls]
    act = s * (1.0 + lax.erf(s * math.sqrt(0.5)))
    g = zero
    for h in range(PEER_HEADS):
        nrow = bcast(n_ref[h, lc, pl.ds(a, 1), :])
        crow = bcast(c_ref[h, lc, pl.ds(a, 1), :])
        g = g + jnp.where(_unpack_rows(r2_ref[h, :, ls]) < nrow, _unpack_rows(f_ref[h, :, ls]) * crow, zero)
    a_ref[k * (PEER_NKEYS // 2):(k + 1) * (PEER_NKEYS // 2), ls] = pltpu.bitcast(act.astype(BF16) * g, jnp.uint32)


def _peer_half_step(gate_args, matmuls):
    chunks = [(k, lc) for lc in range(TT_PEER // LANE) for k in range(_A_PER_SLAB)]
    per = len(chunks) // len(matmuls)
    for j, mm in enumerate(matmuls):
        mm()
        for k, lc in chunks[j * per:(j + 1) * per]:
            _peer_gate_chunk(*gate_args[:3], k, lc, *gate_args[3:])


def _peer_body(wd_a0_ref, wd_b_ref, wd_an_ref, wut_a_ref, wut_bp_ref, hnt_ref, r2_ref, f_ref, n_ref, c_ref,
               h1_ref, o_ref, acc_scr, sa_scr, sb_scr, aa_scr, ab_scr):
    n = pl.program_id(1)
    last = pl.num_programs(1) - 1
    meta = (r2_ref, f_ref, n_ref, c_ref)

    @pl.when(n == 0)
    def _():
        acc_scr[...] = jnp.zeros_like(acc_scr)
        ab_scr[...] = jnp.zeros_like(ab_scr)
        sa_scr[...] = jnp.dot(wd_a0_ref[...], hnt_ref[...], preferred_element_type=F32)

    halves = [slice(j * (TT_PEER // 2), (j + 1) * (TT_PEER // 2)) for j in range(2)]

    def out_mm(wut_ref, a_ref, ls):
        def run():
            acc_scr[:, ls] += jnp.dot(wut_ref[...], _unpack_rows(a_ref[:, ls]), preferred_element_type=F32)
        return run

    def score_mm(wd_ref, s_ref, ls):
        def run():
            s_ref[:, ls] = jnp.dot(wd_ref[...], hnt_ref[:, ls], preferred_element_type=F32)
        return run

    @pl.when(n < last)
    def _():
        a0 = n * (2 * _A_PER_SLAB)
        _peer_half_step((sa_scr, aa_scr, a0) + meta,
                        [out_mm(wut_bp_ref, ab_scr, halves[0]), score_mm(wd_b_ref, sb_scr, halves[0]),
                         out_mm(wut_bp_ref, ab_scr, halves[1]), score_mm(wd_b_ref, sb_scr, halves[1])])
        _peer_half_step((sb_scr, ab_scr, a0 + _A_PER_SLAB) + meta,
                        [out_mm(wut_a_ref, aa_scr, halves[0]), score_mm(wd_an_ref, sa_scr, halves[0]),
                         out_mm(wut_a_ref, aa_scr, halves[1]), score_mm(wd_an_ref, sa_scr, halves[1])])

    @pl.when(n == last)
    def _():
        tail = jnp.dot(wut_bp_ref[...], _unpack_rows(ab_scr[...]), preferred_element_type=F32)
        o_ref[...] = h1_ref[...] + (acc_scr[...] + tail).T


def _peer(wd, wu_t, hn_t, rank2, f, nsel, c, h1):
    t = h1.shape[0]
    nblk = PEER_EXPERTS // ES_PEER
    steps = nblk // 2 + 1
    meta = pl.BlockSpec((PEER_HEADS, TT_PEER // LANE, PEER_NKEYS, LANE), lambda i, n: (0, i, 0, 0))
    packed = pl.BlockSpec((PEER_HEADS, PEER_NKEYS // 2, TT_PEER), lambda i, n: (0, 0, i))
    wd_blk = lambda f: pl.BlockSpec((ES_PEER, D_MODEL), lambda i, n: (f(n), 0))
    wut_blk = lambda f: pl.BlockSpec((None, D_MODEL, ES_PEER), lambda i, n: (f(n), 0, 0))
    scores = pltpu.VMEM((ES_PEER, TT_PEER), F32)
    acts = pltpu.VMEM((ES_PEER // 2, TT_PEER), jnp.uint32)
    return pl.pallas_call(
        _peer_body, grid=(t // TT_PEER, steps),
        in_specs=[wd_blk(lambda n: 0),
                  wd_blk(lambda n: jnp.minimum(2 * n + 1, nblk - 1)),
                  wd_blk(lambda n: jnp.minimum(2 * n + 2, nblk - 1)),
                  wut_blk(lambda n: jnp.minimum(2 * n, nblk - 1)),
                  wut_blk(lambda n: jnp.maximum(2 * n - 1, 0)),
                  pl.BlockSpec((D_MODEL, TT_PEER), lambda i, n: (0, i)),
                  packed, packed, meta, meta,
                  pl.BlockSpec((TT_PEER, D_MODEL), lambda i, n: (i, 0))],
        out_specs=pl.BlockSpec((TT_PEER, D_MODEL), lambda i, n: (i, 0)),
        out_shape=jax.ShapeDtypeStruct((t, D_MODEL), F32),
        scratch_shapes=[pltpu.VMEM((D_MODEL, TT_PEER), F32), scores, scores, acts, acts],
        compiler_params=_cparams(("parallel", "arbitrary")), name="peer",
    )(wd, wd, wd, wu_t, wu_t, hn_t, rank2, f, nsel, c, h1)


def _pad_heads(w):
    r = w.shape[0]
    w = w.reshape(r, -1, ATT_DIM)
    return jnp.pad(w, ((0, 0), (0, 0), (0, LANE - ATT_DIM))).reshape(r, -1)


def _pad_row(w):
    return jnp.pad(w, (0, LANE - w.shape[0])).reshape(1, LANE)


def kernel(x, mem, mix_norm_w, w_in, ssd_conv_w, ssd_conv_b, ssd_dt_bias, ssd_a_log, ssd_d, ssd_norm_w,
           moba_q_norm_w, moba_k_norm_w, mem_norm_w, w_mem_kv, xattn_q_norm_w, xattn_k_norm_w, w_out,
           ffn_norm_w, peer_w_query, peer_sub_keys_1, peer_sub_keys_2, peer_expert_down, peer_expert_up):
    b, l, d = x.shape
    depth = w_in.shape[0]
    h = x.reshape(b * l, d)
    for li in range(depth):
        wi = w_in[li]
        o = 0
        cols = {}
        for name, width in (("z", SSD_WIDTH), ("xbc", SSD_CONV_CH), ("dt", SSD_HEADS), ("mq", ATT_WIDTH),
                            ("mk", ATT_WIDTH), ("mv", ATT_WIDTH), ("xq", ATT_WIDTH)):
            cols[name] = wi[:, o:o + width]
            o += width
        w_list = [cols["z"], cols["xbc"], jnp.pad(cols["dt"], ((0, 0), (0, LANE - SSD_HEADS))),
                  _pad_heads(cols["mq"]), _pad_heads(cols["mk"]), _pad_heads(cols["mv"]), _pad_heads(cols["xq"])]
        w_list = [w.astype(BF16) for w in w_list]
        z, xbc, dt, mq, mk, mv, xq = _in_proj(h, mix_norm_w[li].reshape(1, d), w_list)

        y_ssd = _ssd(z.reshape(b, l, -1), xbc.reshape(b, l, -1), dt.reshape(b, l, -1), ssd_conv_w[li],
                     ssd_conv_b[li], ssd_dt_bias[li], ssd_a_log[li], ssd_d[li], ssd_norm_w[li])

        wkv = w_mem_kv[li]
        mem_k, mem_vt = _mem_kv(mem, mem_norm_w[li].reshape(1, d), _pad_heads(wkv[:, :ATT_WIDTH]).astype(BF16),
                                _pad_heads(wkv[:, ATT_WIDTH:]).astype(BF16), _pad_row(xattn_k_norm_w[li]))
        r3 = lambda a: a.reshape(b, l, -1)
        qat, ka, kad, vt, xqt = _attn_prep(r3(mq), r3(mk), r3(mv), r3(xq), _pad_row(moba_q_norm_w[li]),
                                           _pad_row(moba_k_norm_w[li]), _pad_row(xattn_q_norm_w[li]))
        y_moba, y_mem = _attn(qat, ka, kad, vt, xqt, mem_k, mem_vt)

        wo = w_out[li].astype(BF16)
        wo_moba = wo[SSD_WIDTH:SSD_WIDTH + ATT_WIDTH].reshape(ATT_HEADS, ATT_DIM, d)
        wo_mem = wo[SSD_WIDTH + ATT_WIDTH:].reshape(ATT_HEADS, ATT_DIM, d)
        h1, hn_t, rank2, f, nsel, c = _peer_prep(
            h, y_ssd.reshape(b * l, -1), y_moba, y_mem, wo[:SSD_WIDTH], wo_moba, wo_mem,
            ffn_norm_w[li].reshape(1, d), peer_w_query[li].T.astype(BF16), peer_sub_keys_1[li], peer_sub_keys_2[li])
        wu_t = peer_expert_up[li].reshape(-1, ES_PEER, d).transpose(0, 2, 1).astype(BF16)
        h = _peer(peer_expert_down[li].astype(BF16), wu_t, hn_t, rank2, f, nsel, c, h1)
    return h.reshape(b, l, d)
```

```python
import functools
import math

import numpy as np
import jax
import jax.numpy as jnp
from jax import lax
from jax.experimental import pallas as pl
from jax.experimental.pallas import tpu as pltpu

F32 = jnp.float32
BF16 = jnp.bfloat16
HIGHEST = lax.Precision.HIGHEST

NORM_EPS = 1e-6
D_MODEL = 1024
SSD_HEADS = 8
SSD_HEAD_DIM = 64
SSD_WIDTH = 512
SSD_GROUPS = 2
SSD_STATE = 128
SSD_CONV = 4
SSD_CONV_CH = 1024
ATT_HEADS = 4
ATT_DIM = 64
ATT_WIDTH = 256
MOBA_BLOCK = 256
MOBA_TOPK = 3
MEM_LEN = 256
PEER_HEADS = 8
PEER_NKEYS = 128
PEER_TOPK = 16
PEER_HALF = 64
PEER_EXPERTS = PEER_NKEYS * PEER_NKEYS

LANE = 128
NEG = -1e30
VMEM_LIMIT = 56 * 1024 * 1024

TM_IN = 512
SSD_CHUNK = 256
TQ = MOBA_BLOCK
ATT_GROUP = 4
TM_PP = 512
TT_PEER = 512
ES_PEER = 512

_SLOPES = [2.0 ** (-8.0 * (i + 1) / ATT_HEADS) for i in range(ATT_HEADS)]


def _bf16_split(v):
    hi = float(np.float32(v).astype(BF16).astype(np.float32))
    lo = float(np.float32(v - hi).astype(BF16).astype(np.float32))
    return hi, lo


def _cparams(sem):
    return pltpu.CompilerParams(dimension_semantics=sem, vmem_limit_bytes=VMEM_LIMIT)


def _sigmoid(x):
    return 1.0 / (1.0 + jnp.exp(-x))


def _full(shape):
    n = len(shape)
    return pl.BlockSpec(shape, lambda *_: (0,) * n)


def _in_proj_body(x_ref, g_ref, *refs):
    n = len(refs) // 2
    w_refs, o_refs = refs[:n], refs[n:]
    x = x_ref[...]
    ms = jnp.mean(x * x, axis=-1, keepdims=True)
    xn = ((x * lax.rsqrt(ms + NORM_EPS)) * g_ref[...]).astype(BF16)
    for w_ref, o_ref in zip(w_refs, o_refs):
        o_ref[...] = jnp.dot(xn, w_ref[...], preferred_element_type=F32).astype(o_ref.dtype)


def _in_proj(x2d, gain, weights):
    t = x2d.shape[0]
    in_specs = [pl.BlockSpec((TM_IN, D_MODEL), lambda i: (i, 0)), _full((1, D_MODEL))]
    in_specs += [_full(w.shape) for w in weights]
    out_specs = [pl.BlockSpec((TM_IN, w.shape[1]), lambda i: (i, 0)) for w in weights]
    out_shape = [jax.ShapeDtypeStruct((t, w.shape[1]), F32) for w in weights]
    return pl.pallas_call(
        _in_proj_body, grid=(t // TM_IN,), in_specs=in_specs, out_specs=out_specs,
        out_shape=out_shape, compiler_params=_cparams(("parallel",)), name="in_proj",
    )(x2d, gain, *weights)


def _ssd_body(z_ref, xbc_ref, dt_ref, cw_ref, cb_ref, dtb_ref, alog_ref, dskip_ref, nw_ref, e_ref,
              y_ref, ext_scr, state_scr):
    q = SSD_CHUNK
    c = pl.program_id(1)

    @pl.when(c == 0)
    def _():
        ext_scr[0:8, :] = jnp.zeros((8, SSD_CONV_CH), F32)
        state_scr[...] = jnp.zeros_like(state_scr)

    u = xbc_ref[0]
    ext_scr[8:8 + q, :] = u
    acc = cb_ref[...] + cw_ref[3:4, :] * u
    acc = acc + cw_ref[2:3, :] * ext_scr[7:7 + q, :]
    acc = acc + cw_ref[1:2, :] * ext_scr[6:6 + q, :]
    acc = acc + cw_ref[0:1, :] * ext_scr[5:5 + q, :]
    ext_scr[0:8, :] = u[q - 8:q, :]
    act = acc * _sigmoid(acc)
    xs = act[:, 0:SSD_WIDTH]
    bm = act[:, SSD_WIDTH:SSD_WIDTH + SSD_GROUPS * SSD_STATE]
    cm = act[:, SSD_WIDTH + SSD_GROUPS * SSD_STATE:]

    dtr = dt_ref[0] + dtb_ref[...]
    dt = jnp.maximum(dtr, 0.0) + jnp.log(1.0 + jnp.exp(-jnp.abs(dtr)))
    a = dt * (-jnp.exp(alog_ref[...]))
    row = lax.broadcasted_iota(jnp.int32, (q, q), 0)
    col = lax.broadcasted_iota(jnp.int32, (q, q), 1)
    causal = row >= col
    a_cs = jnp.dot(causal.astype(F32), a, precision=HIGHEST, preferred_element_type=F32)
    a_cs_t = a_cs.T
    a_cs_w = jnp.dot(a_cs, e_ref[...], precision=HIGHEST, preferred_element_type=F32)
    dt_w = jnp.dot(dt, e_ref[...], precision=HIGHEST, preferred_element_type=F32)
    total_w = a_cs_w[q - 1:q, :]
    exp_cs_w = jnp.exp(a_cs_w)
    dte_w = jnp.exp(total_w - a_cs_w)
    cd_w = jnp.exp(total_w)
    xdt_w = xs * dt_w
    lane = lax.broadcasted_iota(jnp.int32, (1, LANE), 1)
    first = lane < SSD_HEAD_DIM

    z = z_ref[0]
    gated = []
    for g in range(SSD_GROUPS):
        bg = bm[:, g * SSD_STATE:(g + 1) * SSD_STATE]
        cg = cm[:, g * SSD_STATE:(g + 1) * SSD_STATE].astype(BF16)
        cb = lax.dot_general(cg, bg.astype(BF16), (((1,), (1,)), ((), ())), preferred_element_type=F32)
        bg_t = bg.T.astype(BF16)
        for kk in range(2):
            k = 2 * g + kk
            sl = slice(k * LANE, (k + 1) * LANE)
            xdt = xdt_w[:, sl]
            xdt_b = xdt.astype(BF16)
            yd = []
            for hh in range(2):
                h = 2 * k + hh
                seg = a_cs[:, h:h + 1] - a_cs_t[h:h + 1, :]
                lm = jnp.exp(jnp.where(causal, seg, -jnp.inf))
                yd.append(jnp.dot((cb * lm).astype(BF16), xdt_b, preferred_element_type=F32))
            y = jnp.where(first, yd[0], yd[1])
            s_old = state_scr[k]
            y = y + jnp.dot(cg, s_old.astype(BF16), preferred_element_type=F32) * exp_cs_w[:, sl]
            y = y + xs[:, sl] * dskip_ref[:, sl]
            state_scr[k] = s_old * cd_w[:, sl] + jnp.dot(
                bg_t, (xdt * dte_w[:, sl]).astype(BF16), preferred_element_type=F32)
            zz = z[:, sl]
            gated.append(y * (zz * _sigmoid(zz)))
    for g in range(SSD_GROUPS):
        y0, y1 = gated[2 * g], gated[2 * g + 1]
        ms = (jnp.sum(y0 * y0, axis=-1, keepdims=True)
              + jnp.sum(y1 * y1, axis=-1, keepdims=True)) * (1.0 / (SSD_WIDTH // SSD_GROUPS))
        r = lax.rsqrt(ms + NORM_EPS)
        lo = 2 * g * LANE
        y_ref[0, :, lo:lo + LANE] = (y0 * r * nw_ref[:, lo:lo + LANE]).astype(y_ref.dtype)
        y_ref[0, :, lo + LANE:lo + 2 * LANE] = (y1 * r * nw_ref[:, lo + LANE:lo + 2 * LANE]).astype(y_ref.dtype)


def _ssd(z, xbc, dt, conv_w, conv_b, dt_bias, a_log, d_skip, norm_w):
    b, l, _ = z.shape
    q = SSD_CHUNK
    pad = LANE - SSD_HEADS
    dtb = jnp.pad(dt_bias, (0, pad)).reshape(1, LANE)
    alog = jnp.pad(a_log, (0, pad)).reshape(1, LANE)
    dsk = jnp.repeat(d_skip, SSD_HEAD_DIM).reshape(1, SSD_WIDTH)
    expand = (jnp.arange(LANE)[:, None] == (jnp.arange(SSD_WIDTH)[None, :] // SSD_HEAD_DIM)).astype(F32)
    tok = lambda w: pl.BlockSpec((1, q, w), lambda bi, ci: (bi, ci, 0))
    return pl.pallas_call(
        _ssd_body, grid=(b, l // q),
        in_specs=[tok(SSD_WIDTH), tok(SSD_CONV_CH), tok(LANE), _full((SSD_CONV, SSD_CONV_CH)),
                  _full((1, SSD_CONV_CH)), _full((1, LANE)), _full((1, LANE)), _full((1, SSD_WIDTH)),
                  _full((1, SSD_WIDTH)), _full((LANE, SSD_WIDTH))],
        out_specs=tok(SSD_WIDTH),
        out_shape=jax.ShapeDtypeStruct((b, l, SSD_WIDTH), BF16),
        scratch_shapes=[pltpu.VMEM((q + 8, SSD_CONV_CH), F32),
                        pltpu.VMEM((SSD_HEADS // 2, SSD_STATE, LANE), F32)],
        compiler_params=_cparams(("parallel", "arbitrary")), name="ssd",
    )(z, xbc, dt, conv_w, conv_b.reshape(1, -1), dtb, alog, dsk, norm_w.reshape(1, -1), expand)


V_ROWS = ATT_DIM + 16


def _with_ones_row(v_t):
    n = v_t.shape[1]
    r = lax.broadcasted_iota(jnp.int32, (V_ROWS - ATT_DIM, n), 0)
    return jnp.concatenate([v_t, jnp.where(r == 0, 1.0, 0.0)], axis=0)


def _head_rms(xh, w_row):
    ms = jnp.sum(xh * xh, axis=-1, keepdims=True) * (1.0 / ATT_DIM)
    return xh * lax.rsqrt(ms + NORM_EPS) * w_row


def _mem_kv_body(mem_ref, g_ref, wk_ref, wv_ref, kw_ref, k_ref, vt_ref):
    x = mem_ref[0]
    ms = jnp.mean(x * x, axis=-1, keepdims=True)
    xn = ((x * lax.rsqrt(ms + NORM_EPS)) * g_ref[...]).astype(BF16)
    kp = jnp.dot(xn, wk_ref[...], preferred_element_type=F32)
    vp = jnp.dot(xn, wv_ref[...], preferred_element_type=F32)
    for h in range(ATT_HEADS):
        sl = slice(h * LANE, (h + 1) * LANE)
        k_ref[0, h] = _head_rms(kp[:, sl], kw_ref[...]).astype(BF16)
        vt_ref[0, h] = _with_ones_row(vp[:, sl].T[0:ATT_DIM, :]).astype(BF16)


def _mem_kv(mem, gain, wk, wv, k_norm_w):
    b = mem.shape[0]
    return pl.pallas_call(
        _mem_kv_body, grid=(b,),
        in_specs=[pl.BlockSpec((1, MEM_LEN, D_MODEL), lambda i: (i, 0, 0)), _full((1, D_MODEL)),
                  _full(wk.shape), _full(wv.shape), _full((1, LANE))],
        out_specs=[pl.BlockSpec((1, ATT_HEADS, MEM_LEN, LANE), lambda i: (i, 0, 0, 0)),
                   pl.BlockSpec((1, ATT_HEADS, V_ROWS, MEM_LEN), lambda i: (i, 0, 0, 0))],
        out_shape=[jax.ShapeDtypeStruct((b, ATT_HEADS, MEM_LEN, LANE), BF16),
                   jax.ShapeDtypeStruct((b, ATT_HEADS, V_ROWS, MEM_LEN), BF16)],
        compiler_params=_cparams(("parallel",)), name="mem_kv",
    )(mem, gain, wk, wv, k_norm_w)


MAX_BLOCKS = 32
_AUG_ONEHOT = ATT_DIM
_AUG_EXTRA = ATT_DIM + MAX_BLOCKS


def _attn_prep_body(slopes, q_ref, k_ref, v_ref, xq_ref, qw_ref, kw_ref, xqw_ref,
                    qat_ref, ka_ref, kad_ref, vt_ref, xqt_ref, kmean_scr):
    i = pl.program_id(1)
    nb = kmean_scr.shape[1]

    @pl.when(i == 0)
    def _():
        kmean_scr[...] = jnp.zeros_like(kmean_scr)

    n_iota = lax.broadcasted_iota(jnp.int32, (nb, TQ), 0)
    past = n_iota < i
    t_loc = lax.broadcasted_iota(jnp.int32, (32, TQ), 1).astype(F32)
    r32 = lax.broadcasted_iota(jnp.int32, (32, TQ), 0)
    k_lane = lax.broadcasted_iota(jnp.int32, (TQ, LANE), 1)
    s_loc = lax.broadcasted_iota(jnp.int32, (TQ, LANE), 0).astype(F32)
    blk = lax.convert_element_type(i, F32)
    for h in range(ATT_HEADS):
        hi, lo = slopes[h]
        sl = slice(h * LANE, (h + 1) * LANE)
        qn = _head_rms(q_ref[0, :, sl], qw_ref[...])
        kn = _head_rms(k_ref[0, :, sl], kw_ref[...])
        qn_t = qn.T
        gate = jnp.dot(kmean_scr[h], qn_t, precision=HIGHEST, preferred_element_type=F32)
        gate = jnp.where(past, gate, -jnp.inf)
        cnt = jnp.zeros((nb, TQ), F32)
        for n2 in range(nb):
            gn = gate[n2:n2 + 1, :]
            ahead = jnp.where(gn > gate, 1.0, jnp.where(gn == gate, jnp.where(n_iota > n2, 1.0, 0.0), 0.0))
            cnt = cnt + ahead
        allowed = jnp.where(past, jnp.where(cnt < float(MOBA_TOPK), 1.0, 0.0), 0.0)
        bias = jnp.where(allowed > 0.5, 0.0, NEG)
        extra = jnp.where(r32 < 2, t_loc, 0.0)
        for r, val in ((2, hi), (3, lo), (4, MOBA_BLOCK * hi), (5, MOBA_BLOCK * lo)):
            extra = jnp.where(r32 == r, val, extra)
        extra = jnp.where((r32 == 6) | (r32 == 7), blk, extra)
        qat = jnp.concatenate([qn_t[0:ATT_DIM, :] * (1.0 / math.sqrt(ATT_DIM)), bias, extra], axis=0)
        qat_ref[0, h] = qat.astype(BF16)
        kx = jnp.where((k_lane == _AUG_EXTRA + 2) | (k_lane == _AUG_EXTRA + 3), s_loc, 0.0)
        kx = jnp.where((k_lane == _AUG_EXTRA + 4) | (k_lane == _AUG_EXTRA + 5), blk, kx)
        for c, val in ((0, -hi), (1, -lo), (6, -MOBA_BLOCK * hi), (7, -MOBA_BLOCK * lo)):
            kx = jnp.where(k_lane == _AUG_EXTRA + c, val, kx)
        kad = kn + kx
        kad_ref[0, h] = kad.astype(BF16)
        ka_ref[0, h] = (kad + jnp.where(k_lane - _AUG_ONEHOT == i, 1.0, 0.0)).astype(BF16)
        vt_ref[0, h] = _with_ones_row(v_ref[0, :, sl].T[0:ATT_DIM, :]).astype(BF16)
        xqn = _head_rms(xq_ref[0, :, sl], xqw_ref[...])
        xqt_ref[0, h] = (xqn.T * (1.0 / math.sqrt(ATT_DIM))).astype(BF16)
        kmean_scr[h, pl.ds(i, 1), :] = jnp.sum(kn, axis=0, keepdims=True) * (1.0 / MOBA_BLOCK)


def _attn_prep(q, k, v, xq, qw, kw, xqw):
    b, l, _ = q.shape
    nb = l // MOBA_BLOCK
    assert nb <= MAX_BLOCKS
    slopes = tuple(_bf16_split(s) for s in _SLOPES)
    tok = pl.BlockSpec((1, TQ, ATT_HEADS * LANE), lambda bi, i: (bi, i, 0))
    return pl.pallas_call(
        functools.partial(_attn_prep_body, slopes), grid=(b, nb),
        in_specs=[tok, tok, tok, tok, _full((1, LANE)), _full((1, LANE)), _full((1, LANE))],
        out_specs=[pl.BlockSpec((1, ATT_HEADS, LANE, TQ), lambda bi, i: (bi, 0, 0, i)),
                   pl.BlockSpec((1, ATT_HEADS, TQ, LANE), lambda bi, i: (bi, 0, i, 0)),
                   pl.BlockSpec((1, ATT_HEADS, TQ, LANE), lambda bi, i: (bi, 0, i, 0)),
                   pl.BlockSpec((1, ATT_HEADS, V_ROWS, TQ), lambda bi, i: (bi, 0, 0, i)),
                   pl.BlockSpec((1, ATT_HEADS, LANE, TQ), lambda bi, i: (bi, 0, 0, i))],
        out_shape=[jax.ShapeDtypeStruct((b, ATT_HEADS, LANE, l), BF16),
                   jax.ShapeDtypeStruct((b, ATT_HEADS, l, LANE), BF16),
                   jax.ShapeDtypeStruct((b, ATT_HEADS, l, LANE), BF16),
                   jax.ShapeDtypeStruct((b, ATT_HEADS, V_ROWS, l), BF16),
                   jax.ShapeDtypeStruct((b, ATT_HEADS, LANE, l), BF16)],
        scratch_shapes=[pltpu.VMEM((ATT_HEADS, MAX_BLOCKS, LANE), F32)],
        compiler_params=_cparams(("parallel", "arbitrary")), name="attn_prep",
    )(q, k, v, xq, qw, kw, xqw)


def _normalized_t(acc):
    return (acc[0:ATT_DIM, :] * (1.0 / acc[ATT_DIM:ATT_DIM + 1, :])).T


def _attn_body(qat_ref, ka_ref, kad_ref, vt_ref, xqt_ref, mk_ref, mvt_ref, o_ref, om_ref, s_scr, p_scr):
    i = pl.program_id(1)
    base = pl.multiple_of(i * MOBA_BLOCK, MOBA_BLOCK)
    key = lax.broadcasted_iota(jnp.int32, (MOBA_BLOCK, TQ), 0)
    qry = lax.broadcasted_iota(jnp.int32, (MOBA_BLOCK, TQ), 1)

    span = ATT_GROUP * MOBA_BLOCK
    n_groups = (i + ATT_GROUP - 1) // ATT_GROUP
    last_group = ka_ref.shape[2] // span - 1

    def scores(h, g):
        off = pl.multiple_of(jnp.minimum(g, last_group) * span, span)
        return jnp.dot(ka_ref[0, h, pl.ds(off, span), :], qat_ref[0, h], preferred_element_type=F32)

    def values(h, g, p):
        off = pl.multiple_of(jnp.maximum(g, 0) * span, span)
        return jnp.dot(vt_ref[0, h, :, pl.ds(off, span)], p, preferred_element_type=F32)

    init = []
    for h in range(ATT_HEADS):
        s = jnp.dot(kad_ref[0, h], qat_ref[0, h], preferred_element_type=F32)
        s = jnp.where(key <= qry, s, NEG)
        m = jnp.max(s, axis=0, keepdims=True)
        p = jnp.exp(s - m).astype(BF16)
        acc = jnp.dot(vt_ref[0, h, :, pl.ds(base, MOBA_BLOCK)], p, preferred_element_type=F32)
        s_scr[h] = scores(h, 0)
        p_scr[h] = jnp.zeros((span, TQ), BF16)
        init += [jnp.ones((1, TQ), F32), m, acc]

    def body(g, carry):
        out = []
        for h in range(ATT_HEADS):
            alpha_prev, m, acc = carry[3 * h:3 * h + 3]
            acc = acc * alpha_prev + values(h, g - 1, p_scr[h])
            s = s_scr[h]
            m_new = jnp.maximum(m, jnp.max(s, axis=0, keepdims=True))
            p_scr[h] = jnp.exp(s - m_new).astype(BF16)
            s_scr[h] = scores(h, g + 1)
            out += [jnp.exp(m - m_new), m_new, acc]
        return tuple(out)

    fin = lax.fori_loop(0, n_groups, body, tuple(init))
    for h in range(ATT_HEADS):
        alpha_prev, _, acc = fin[3 * h:3 * h + 3]
        o_ref[0, h] = _normalized_t(acc * alpha_prev + values(h, n_groups - 1, p_scr[h]))
        sm = jnp.dot(mk_ref[0, h], xqt_ref[0, h], preferred_element_type=F32)
        pm = jnp.exp(sm - jnp.max(sm, axis=0, keepdims=True)).astype(BF16)
        om_ref[0, h] = _normalized_t(jnp.dot(mvt_ref[0, h], pm, preferred_element_type=F32))


def _attn(qat, ka, kad, vt, xqt, mem_k, mem_vt):
    b, _, _, l = qat.shape
    assert l % (ATT_GROUP * MOBA_BLOCK) == 0
    nh = ATT_HEADS
    per_q = lambda r: pl.BlockSpec((1, nh, r, TQ), lambda bi, i: (bi, 0, 0, i))
    out = pl.BlockSpec((1, nh, TQ, ATT_DIM), lambda bi, i: (bi, 0, i, 0))
    return pl.pallas_call(
        _attn_body, grid=(b, l // TQ),
        in_specs=[per_q(LANE),
                  pl.BlockSpec((1, nh, l, LANE), lambda bi, i: (bi, 0, 0, 0)),
                  pl.BlockSpec((1, nh, TQ, LANE), lambda bi, i: (bi, 0, i, 0)),
                  pl.BlockSpec((1, nh, V_ROWS, l), lambda bi, i: (bi, 0, 0, 0)),
                  per_q(LANE),
                  pl.BlockSpec((1, nh, MEM_LEN, LANE), lambda bi, i: (bi, 0, 0, 0)),
                  pl.BlockSpec((1, nh, V_ROWS, MEM_LEN), lambda bi, i: (bi, 0, 0, 0))],
        out_specs=[out, out],
        out_shape=[jax.ShapeDtypeStruct((b, nh, l, ATT_DIM), F32)] * 2,
        scratch_shapes=[pltpu.VMEM((nh, ATT_GROUP * MOBA_BLOCK, TQ), F32),
                        pltpu.VMEM((nh, ATT_GROUP * MOBA_BLOCK, TQ), BF16)],
        compiler_params=_cparams(("parallel", "arbitrary")), name="attn",
    )(qat, ka, kad, vt, xqt, mem_k, mem_vt)


def _cand_layout():
    slabs = [("r1", 0, 16), ("r1", 1, 8), ("r1", 2, 8), ("r1", 3, 8), ("r2", 0, 16), ("r2", 1, 8), ("r2", 2, 8)]
    pos, valid = [], []
    for kind, fixed, n in slabs:
        for j in range(n):
            r1, r2 = (fixed, j) if kind == "r1" else (j, fixed)
            ok = (r1 + 1) * (r2 + 1) <= PEER_TOPK and ((kind == "r1") or r1 >= 4)
            pos.append(r1 * PEER_TOPK + r2)
            valid.append(ok)
    assert sum(valid) == 50
    return slabs, np.asarray(pos, np.int32), np.asarray(valid)


_CAND_SLABS, _CAND_POS, _CAND_VALID = _cand_layout()
_NCAND = len(_CAND_POS)


def _split_cols(k):
    hi = k.astype(BF16)
    lo = (k - hi.astype(F32)).astype(BF16)
    return jnp.concatenate([hi, hi, lo], axis=1)


def _dot_split(k_cat, q):
    hi = q.astype(BF16)
    lo = (q - hi.astype(F32)).astype(BF16)
    return jnp.dot(k_cat, jnp.concatenate([hi, lo, hi], axis=0), preferred_element_type=F32)


def _pack_rows(x):
    return pltpu.bitcast(x.astype(BF16), jnp.uint32)


def _unpack_rows(x):
    return pltpu.bitcast(x, BF16)


def _top16_pair(w1_scr, w2_scr, rank2_scr, v1_scr, v2_scr, i1_scr):
    row = lax.broadcasted_iota(jnp.int32, (PEER_NKEYS, LANE), 0).astype(F32)
    rank2_scr[...] = jnp.full(rank2_scr.shape, float(PEER_TOPK), F32)

    def body(r, _):
        rf = lax.convert_element_type(r, F32)
        for half in range(TM_PP // LANE):
            ls = slice(half * LANE, (half + 1) * LANE)
            for first, w_scr, v_scr in ((True, w1_scr, v1_scr), (False, w2_scr, v2_scr)):
                s = w_scr[:, ls]
                m = jnp.max(s, axis=0, keepdims=True)
                idx = jnp.min(jnp.where(s == m, row, float(PEER_NKEYS)), axis=0, keepdims=True)
                hit = row == idx
                w_scr[:, ls] = jnp.where(hit, -jnp.inf, s)
                v_scr[half, pl.ds(r, 1), :] = m
                if first:
                    i1_scr[half, pl.ds(r, 1), :] = idx
                else:
                    rank2_scr[:, ls] = jnp.where(hit, rf, rank2_scr[:, ls])
        return 0

    lax.fori_loop(0, PEER_TOPK, body, 0)


def _peer_prep_body(x_ref, ys_ref, ym_ref, yx_ref, wos_ref, wom_ref, wox_ref, g_ref, wqt_ref,
                    k1_ref, k2_ref, pos_ref, cbias_ref,
                    h1_ref, hnt_ref, r2_ref, f_ref, n_ref, c_ref,
                    q_scr, w1_scr, w2_scr, rank2_scr, v1_scr, v2_scr, i1_scr):
    hres = x_ref[...] + jnp.dot(ys_ref[...], wos_ref[...], preferred_element_type=F32)
    for h in range(ATT_HEADS):
        hres = hres + jnp.dot(ym_ref[0, h].astype(BF16), wom_ref[h], preferred_element_type=F32)
        hres = hres + jnp.dot(yx_ref[0, h].astype(BF16), wox_ref[h], preferred_element_type=F32)
    h1_ref[...] = hres
    ms = jnp.mean(hres * hres, axis=-1, keepdims=True)
    hn = (hres * lax.rsqrt(ms + NORM_EPS)) * g_ref[...]
    hn_t = hn.T.astype(BF16)
    hnt_ref[...] = hn_t
    q_scr[...] = jnp.dot(wqt_ref[...], hn_t, preferred_element_type=F32)

    pos = pos_ref[...]
    cbias = cbias_ref[...]
    r16 = lax.broadcasted_iota(jnp.int32, (PEER_TOPK, TM_PP), 0).astype(F32)
    key_row = lax.broadcasted_iota(jnp.int32, (PEER_NKEYS, TM_PP), 0).astype(F32)
    halves = lambda scr: jnp.concatenate([scr[j] for j in range(TM_PP // LANE)], axis=1)

    def head(h, _):
        base = pl.multiple_of(h * (2 * PEER_HALF), 2 * PEER_HALF)
        s1 = _dot_split(k1_ref[...], q_scr[pl.ds(base, PEER_HALF), :])
        s2 = _dot_split(k2_ref[...], q_scr[pl.ds(base + PEER_HALF, PEER_HALF), :])
        w1_scr[...] = s1
        w2_scr[...] = s2
        _top16_pair(w1_scr, w2_scr, rank2_scr, v1_scr, v2_scr, i1_scr)
        rank2 = rank2_scr[...]
        v1 = halves(v1_scr)
        v2 = halves(v2_scr)
        parts = []
        for kind, fixed, n in _CAND_SLABS:
            if kind == "r1":
                parts.append(v1[fixed:fixed + 1, :] + v2[0:n, :])
            else:
                parts.append(v1[0:n, :] + v2[fixed:fixed + 1, :])
        cand = jnp.concatenate(parts, axis=0) + cbias
        top = v1[0:1, :] + v2[0:1, :]

        def pick(_, carry):
            cand, cnt, zsum = carry
            m = jnp.max(cand, axis=0, keepdims=True)
            p = jnp.min(jnp.where(cand == m, pos, float(4 * PEER_TOPK * PEER_TOPK)), axis=0, keepdims=True)
            cand = jnp.where(pos == p, -jnp.inf, cand)
            cnt = cnt + jnp.where(r16 == jnp.floor(p * (1.0 / PEER_TOPK)), 1.0, 0.0)
            return cand, cnt, zsum + jnp.exp(m - top)

        _, cnt, zsum = lax.fori_loop(
            0, PEER_TOPK, pick, (cand, jnp.zeros((PEER_TOPK, TM_PP), F32), jnp.zeros((1, TM_PP), F32)))
        i1 = halves(i1_scr)
        nsel = jnp.zeros((PEER_NKEYS, TM_PP), F32)
        for r in range(PEER_TOPK):
            nsel = jnp.where(key_row == i1[r:r + 1, :], cnt[r:r + 1, :], nsel)
        r2_ref[h] = _pack_rows(rank2)
        f_ref[h] = _pack_rows(jnp.exp(s2 - v2[0:1, :]))
        cw = jnp.exp(s1 - v1[0:1, :]) * (0.5 / zsum)
        for lc in range(TM_PP // LANE):
            n_ref[h, lc] = nsel[:, lc * LANE:(lc + 1) * LANE]
            c_ref[h, lc] = cw[:, lc * LANE:(lc + 1) * LANE]
        return 0

    lax.fori_loop(0, PEER_HEADS, head, 0)


def _peer_prep(x2d, y_ssd, y_moba, y_mem, wo_ssd, wo_moba, wo_mem, gain, wq_t, k1, k2):
    t = x2d.shape[0]
    b, _, l, _ = y_moba.shape
    per_b = l // TM_PP
    pos = jnp.asarray(np.broadcast_to(_CAND_POS.astype(np.float32)[:, None], (_NCAND, TM_PP)))
    cbias = jnp.asarray(np.broadcast_to(np.where(_CAND_VALID, 0.0, -np.inf).astype(np.float32)[:, None],
                                        (_NCAND, TM_PP)))
    tok = lambda w: pl.BlockSpec((TM_PP, w), lambda i: (i, 0))
    att = pl.BlockSpec((1, ATT_HEADS, TM_PP, ATT_DIM), lambda i: (i // per_b, 0, i % per_b, 0))
    meta = pl.BlockSpec((PEER_HEADS, TM_PP // LANE, PEER_NKEYS, LANE), lambda i: (0, i, 0, 0))
    meta_f32 = jax.ShapeDtypeStruct((PEER_HEADS, t // LANE, PEER_NKEYS, LANE), F32)
    packed = pl.BlockSpec((PEER_HEADS, PEER_NKEYS // 2, TM_PP), lambda i: (0, 0, i))
    meta_pk = jax.ShapeDtypeStruct((PEER_HEADS, PEER_NKEYS // 2, t), jnp.uint32)
    return pl.pallas_call(
        _peer_prep_body, grid=(t // TM_PP,),
        in_specs=[tok(D_MODEL), tok(SSD_WIDTH), att, att, _full(wo_ssd.shape), _full(wo_moba.shape),
                  _full(wo_mem.shape), _full((1, D_MODEL)), _full(wq_t.shape), _full(k1.shape), _full(k2.shape),
                  _full(pos.shape), _full(cbias.shape)],
        out_specs=[tok(D_MODEL), pl.BlockSpec((D_MODEL, TM_PP), lambda i: (0, i)), packed, packed, meta, meta],
        out_shape=[jax.ShapeDtypeStruct((t, D_MODEL), F32), jax.ShapeDtypeStruct((D_MODEL, t), BF16),
                   meta_pk, meta_pk, meta_f32, meta_f32],
        scratch_shapes=[pltpu.VMEM((PEER_HEADS * 2 * PEER_HALF, TM_PP), F32)]
        + [pltpu.VMEM((PEER_NKEYS, TM_PP), F32)] * 3
        + [pltpu.VMEM((TM_PP // LANE, PEER_TOPK, LANE), F32)] * 3,
        compiler_params=_cparams(("parallel",)), name="peer_prep",
    )(x2d, y_ssd, y_moba, y_mem, wo_ssd, wo_moba, wo_mem, gain, wq_t, k1, k2, pos, cbias)


_A_PER_SLAB = ES_PEER // PEER_NKEYS


def _peer_gate_chunk(s_ref, a_ref, a_base, k, lc, r2_ref, f_ref, n_ref, c_ref):
    zero = jnp.zeros((PEER_NKEYS, LANE), BF16)

    def bcast(row):
        tile = jnp.broadcast_to(row, (16, LANE)).astype(BF16)
        return jnp.concatenate([tile] * (PEER_NKEYS // 16), axis=0)

    a = a_base + k
    rows = slice(k * PEER_NKEYS, (k + 1) * PEER_NKEYS)
    ls = slice(lc * LANE, (lc + 1) * LANE)
    s = s_ref[rows, ls]
    act = s * (1.0 + lax.erf(s * math.sqrt(0.5)))
    g = zero
    for h in range(PEER_HEADS):
        nrow = bcast(n_ref[h, lc, pl.ds(a, 1), :])
        crow = bcast(c_ref[h, lc, pl.ds(a, 1), :])
        g = g + jnp.where(_unpack_rows(r2_ref[h, :, ls]) < nrow, _unpack_rows(f_ref[h, :, ls]) * crow, zero)
    a_ref[k * (PEER_NKEYS // 2):(k + 1) * (PEER_NKEYS // 2), ls] = pltpu.bitcast(act.astype(BF16) * g, jnp.uint32)


def _peer_half_step(gate_args, matmuls):
    chunks = [(k, lc) for lc in range(TT_PEER // LANE) for k in range(_A_PER_SLAB)]
    per = len(chunks) // len(matmuls)
    for j, mm in enumerate(matmuls):
        mm()
        for k, lc in chunks[j * per:(j + 1) * per]:
            _peer_gate_chunk(*gate_args[:3], k, lc, *gate_args[3:])


def _peer_body(wd_a0_ref, wd_b_ref, wd_an_ref, wut_a_ref, wut_bp_ref, hnt_ref, r2_ref, f_ref, n_ref, c_ref,
               h1_ref, o_ref, acc_scr, sa_scr, sb_scr, aa_scr, ab_scr):
    n = pl.program_id(1)
    last = pl.num_programs(1) - 1
    meta = (r2_ref, f_ref, n_ref, c_ref)

    @pl.when(n == 0)
    def _():
        acc_scr[...] = jnp.zeros_like(acc_scr)
        ab_scr[...] = jnp.zeros_like(ab_scr)
        sa_scr[...] = jnp.dot(wd_a0_ref[...], hnt_ref[...], preferred_element_type=F32)

    halves = [slice(j * (TT_PEER // 2), (j + 1) * (TT_PEER // 2)) for j in range(2)]

    def out_mm(wut_ref, a_ref, ls):
        def run():
            acc_scr[:, ls] += jnp.dot(wut_ref[...], _unpack_rows(a_ref[:, ls]), preferred_element_type=F32)
        return run

    def score_mm(wd_ref, s_ref, ls):
        def run():
            s_ref[:, ls] = jnp.dot(wd_ref[...], hnt_ref[:, ls], preferred_element_type=F32)
        return run

    @pl.when(n < last)
    def _():
        a0 = n * (2 * _A_PER_SLAB)
        _peer_half_step((sa_scr, aa_scr, a0) + meta,
                        [out_mm(wut_bp_ref, ab_scr, halves[0]), score_mm(wd_b_ref, sb_scr, halves[0]),
                         out_mm(wut_bp_ref, ab_scr, halves[1]), score_mm(wd_b_ref, sb_scr, halves[1])])
        _peer_half_step((sb_scr, ab_scr, a0 + _A_PER_SLAB) + meta,
                        [out_mm(wut_a_ref, aa_scr, halves[0]), score_mm(wd_an_ref, sa_scr, halves[0]),
                         out_mm(wut_a_ref, aa_scr, halves[1]), score_mm(wd_an_ref, sa_scr, halves[1])])

    @pl.when(n == last)
    def _():
        tail = jnp.dot(wut_bp_ref[...], _unpack_rows(ab_scr[...]), preferred_element_type=F32)
        o_ref[...] = h1_ref[...] + (acc_scr[...] + tail).T


def _peer(wd, wu_t, hn_t, rank2, f, nsel, c, h1):
    t = h1.shape[0]
    nblk = PEER_EXPERTS // ES_PEER
    steps = nblk // 2 + 1
    meta = pl.BlockSpec((PEER_HEADS, TT_PEER // LANE, PEER_NKEYS, LANE), lambda i, n: (0, i, 0, 0))
    packed = pl.BlockSpec((PEER_HEADS, PEER_NKEYS // 2, TT_PEER), lambda i, n: (0, 0, i))
    wd_blk = lambda f: pl.BlockSpec((ES_PEER, D_MODEL), lambda i, n: (f(n), 0))
    wut_blk = lambda f: pl.BlockSpec((None, D_MODEL, ES_PEER), lambda i, n: (f(n), 0, 0))
    scores = pltpu.VMEM((ES_PEER, TT_PEER), F32)
    acts = pltpu.VMEM((ES_PEER // 2, TT_PEER), jnp.uint32)
    return pl.pallas_call(
        _peer_body, grid=(t // TT_PEER, steps),
        in_specs=[wd_blk(lambda n: 0),
                  wd_blk(lambda n: jnp.minimum(2 * n + 1, nblk - 1)),
                  wd_blk(lambda n: jnp.minimum(2 * n + 2, nblk - 1)),
                  wut_blk(lambda n: jnp.minimum(2 * n, nblk - 1)),
                  wut_blk(lambda n: jnp.maximum(2 * n - 1, 0)),
                  pl.BlockSpec((D_MODEL, TT_PEER), lambda i, n: (0, i)),
                  packed, packed, meta, meta,
                  pl.BlockSpec((TT_PEER, D_MODEL), lambda i, n: (i, 0))],
        out_specs=pl.BlockSpec((TT_PEER, D_MODEL), lambda i, n: (i, 0)),
        out_shape=jax.ShapeDtypeStruct((t, D_MODEL), F32),
        scratch_shapes=[pltpu.VMEM((D_MODEL, TT_PEER), F32), scores, scores, acts, acts],
        compiler_params=_cparams(("parallel", "arbitrary")), name="peer",
    )(wd, wd, wd, wu_t, wu_t, hn_t, rank2, f, nsel, c, h1)


def _pad_heads(w):
    r = w.shape[0]
    w = w.reshape(r, -1, ATT_DIM)
    return jnp.pad(w, ((0, 0), (0, 0), (0, LANE - ATT_DIM))).reshape(r, -1)


def _pad_row(w):
    return jnp.pad(w, (0, LANE - w.shape[0])).reshape(1, LANE)


def kernel(x, mem, mix_norm_w, w_in, ssd_conv_w, ssd_conv_b, ssd_dt_bias, ssd_a_log, ssd_d, ssd_norm_w,
           moba_q_norm_w, moba_k_norm_w, mem_norm_w, w_mem_kv, xattn_q_norm_w, xattn_k_norm_w, w_out,
           ffn_norm_w, peer_w_query, peer_sub_keys_1, peer_sub_keys_2, peer_expert_down, peer_expert_up):
    b, l, d = x.shape
    depth = w_in.shape[0]
    h = x.reshape(b * l, d)
    for li in range(depth):
        wi = w_in[li]
        o = 0
        cols = {}
        for name, width in (("z", SSD_WIDTH), ("xbc", SSD_CONV_CH), ("dt", SSD_HEADS), ("mq", ATT_WIDTH),
                            ("mk", ATT_WIDTH), ("mv", ATT_WIDTH), ("xq", ATT_WIDTH)):
            cols[name] = wi[:, o:o + width]
            o += width
        w_list = [cols["z"], cols["xbc"], jnp.pad(cols["dt"], ((0, 0), (0, LANE - SSD_HEADS))),
                  _pad_heads(cols["mq"]), _pad_heads(cols["mk"]), _pad_heads(cols["mv"]), _pad_heads(cols["xq"])]
        w_list = [w.astype(BF16) for w in w_list]
        z, xbc, dt, mq, mk, mv, xq = _in_proj(h, mix_norm_w[li].reshape(1, d), w_list)

        y_ssd = _ssd(z.reshape(b, l, -1), xbc.reshape(b, l, -1), dt.reshape(b, l, -1), ssd_conv_w[li],
                     ssd_conv_b[li], ssd_dt_bias[li], ssd_a_log[li], ssd_d[li], ssd_norm_w[li])

        wkv = w_mem_kv[li]
        mem_k, mem_vt = _mem_kv(mem, mem_norm_w[li].reshape(1, d), _pad_heads(wkv[:, :ATT_WIDTH]).astype(BF16),
                                _pad_heads(wkv[:, ATT_WIDTH:]).astype(BF16), _pad_row(xattn_k_norm_w[li]))
        r3 = lambda a: a.reshape(b, l, -1)
        qat, ka, kad, vt, xqt = _attn_prep(r3(mq), r3(mk), r3(mv), r3(xq), _pad_row(moba_q_norm_w[li]),
                                           _pad_row(moba_k_norm_w[li]), _pad_row(xattn_q_norm_w[li]))
        y_moba, y_mem = _attn(qat, ka, kad, vt, xqt, mem_k, mem_vt)

        wo = w_out[li].astype(BF16)
        wo_moba = wo[SSD_WIDTH:SSD_WIDTH + ATT_WIDTH].reshape(ATT_HEADS, ATT_DIM, d)
        wo_mem = wo[SSD_WIDTH + ATT_WIDTH:].reshape(ATT_HEADS, ATT_DIM, d)
        h1, hn_t, rank2, f, nsel, c = _peer_prep(
            h, y_ssd.reshape(b * l, -1), y_moba, y_mem, wo[:SSD_WIDTH], wo_moba, wo_mem,
            ffn_norm_w[li].reshape(1, d), peer_w_query[li].T.astype(BF16), _split_cols(peer_sub_keys_1[li]),
            _split_cols(peer_sub_keys_2[li]))
        wu_t = peer_expert_up[li].reshape(-1, ES_PEER, d).transpose(0, 2, 1).astype(BF16)
        h = _peer(peer_expert_down[li].astype(BF16), wu_t, hn_t, rank2, f, nsel, c, h1)
    return h.reshape(b, l, d)
```

```python
import functools
import math

import numpy as np
import jax
import jax.numpy as jnp
from jax import lax
from jax.experimental import pallas as pl
from jax.experimental.pallas import tpu as pltpu

F32 = jnp.float32
BF16 = jnp.bfloat16
HIGHEST = lax.Precision.HIGHEST

NORM_EPS = 1e-6
D_MODEL = 1024
SSD_HEADS = 8
SSD_HEAD_DIM = 64
SSD_WIDTH = 512
SSD_GROUPS = 2
SSD_STATE = 128
SSD_CONV = 4
SSD_CONV_CH = 1024
ATT_HEADS = 4
ATT_DIM = 64
ATT_WIDTH = 256
MOBA_BLOCK = 256
MOBA_TOPK = 3
MEM_LEN = 256
PEER_HEADS = 8
PEER_NKEYS = 128
PEER_TOPK = 16
PEER_HALF = 64
PEER_EXPERTS = PEER_NKEYS * PEER_NKEYS

LANE = 128
NEG = -1e30
VMEM_LIMIT = 56 * 1024 * 1024

TM_IN = 512
SSD_CHUNK = 256
TQ = MOBA_BLOCK
ATT_GROUP = 4
TM_PP = 512
TT_PEER = 512
ES_PEER = 512

_SLOPES = [2.0 ** (-8.0 * (i + 1) / ATT_HEADS) for i in range(ATT_HEADS)]


def _bf16_split(v):
    hi = float(np.float32(v).astype(BF16).astype(np.float32))
    lo = float(np.float32(v - hi).astype(BF16).astype(np.float32))
    return hi, lo


def _cparams(sem):
    return pltpu.CompilerParams(dimension_semantics=sem, vmem_limit_bytes=VMEM_LIMIT)


def _sigmoid(x):
    return 1.0 / (1.0 + jnp.exp(-x))


def _full(shape):
    n = len(shape)
    return pl.BlockSpec(shape, lambda *_: (0,) * n)


def _in_proj_body(x_ref, g_ref, *refs):
    n = len(refs) // 2
    w_refs, o_refs = refs[:n], refs[n:]
    x = x_ref[...]
    ms = jnp.mean(x * x, axis=-1, keepdims=True)
    xn = ((x * lax.rsqrt(ms + NORM_EPS)) * g_ref[...]).astype(BF16)
    for w_ref, o_ref in zip(w_refs, o_refs):
        o_ref[...] = jnp.dot(xn, w_ref[...], preferred_element_type=F32).astype(o_ref.dtype)


def _in_proj(x2d, gain, weights):
    t = x2d.shape[0]
    in_specs = [pl.BlockSpec((TM_IN, D_MODEL), lambda i: (i, 0)), _full((1, D_MODEL))]
    in_specs += [_full(w.shape) for w in weights]
    out_specs = [pl.BlockSpec((TM_IN, w.shape[1]), lambda i: (i, 0)) for w in weights]
    out_shape = [jax.ShapeDtypeStruct((t, w.shape[1]), F32) for w in weights]
    return pl.pallas_call(
        _in_proj_body, grid=(t // TM_IN,), in_specs=in_specs, out_specs=out_specs,
        out_shape=out_shape, compiler_params=_cparams(("parallel",)), name="in_proj",
    )(x2d, gain, *weights)


def _ssd_body(z_ref, xbc_ref, dt_ref, cw_ref, cb_ref, dtb_ref, alog_ref, dskip_ref, nw_ref, e_ref,
              y_ref, ext_scr, state_scr):
    q = SSD_CHUNK
    c = pl.program_id(1)

    @pl.when(c == 0)
    def _():
        ext_scr[0:8, :] = jnp.zeros((8, SSD_CONV_CH), F32)
        state_scr[...] = jnp.zeros_like(state_scr)

    u = xbc_ref[0]
    ext_scr[8:8 + q, :] = u
    acc = cb_ref[...] + cw_ref[3:4, :] * u
    acc = acc + cw_ref[2:3, :] * ext_scr[7:7 + q, :]
    acc = acc + cw_ref[1:2, :] * ext_scr[6:6 + q, :]
    acc = acc + cw_ref[0:1, :] * ext_scr[5:5 + q, :]
    ext_scr[0:8, :] = u[q - 8:q, :]
    act = acc * _sigmoid(acc)
    xs = act[:, 0:SSD_WIDTH]
    bm = act[:, SSD_WIDTH:SSD_WIDTH + SSD_GROUPS * SSD_STATE]
    cm = act[:, SSD_WIDTH + SSD_GROUPS * SSD_STATE:]

    dtr = dt_ref[0] + dtb_ref[...]
    dt = jnp.maximum(dtr, 0.0) + jnp.log(1.0 + jnp.exp(-jnp.abs(dtr)))
    a = dt * (-jnp.exp(alog_ref[...]))
    row = lax.broadcasted_iota(jnp.int32, (q, q), 0)
    col = lax.broadcasted_iota(jnp.int32, (q, q), 1)
    causal = row >= col
    a_cs = jnp.dot(causal.astype(F32), a, precision=HIGHEST, preferred_element_type=F32)
    a_cs_t = a_cs.T
    a_cs_w = jnp.dot(a_cs, e_ref[...], precision=HIGHEST, preferred_element_type=F32)
    dt_w = jnp.dot(dt, e_ref[...], precision=HIGHEST, preferred_element_type=F32)
    total_w = a_cs_w[q - 1:q, :]
    exp_cs_w = jnp.exp(a_cs_w)
    dte_w = jnp.exp(total_w - a_cs_w)
    cd_w = jnp.exp(total_w)
    xdt_w = xs * dt_w
    lane = lax.broadcasted_iota(jnp.int32, (1, LANE), 1)
    first = lane < SSD_HEAD_DIM

    z = z_ref[0]
    gated = []
    for g in range(SSD_GROUPS):
        bg = bm[:, g * SSD_STATE:(g + 1) * SSD_STATE]
        cg = cm[:, g * SSD_STATE:(g + 1) * SSD_STATE].astype(BF16)
        cb = lax.dot_general(cg, bg.astype(BF16), (((1,), (1,)), ((), ())), preferred_element_type=F32)
        bg_t = bg.T.astype(BF16)
        for kk in range(2):
            k = 2 * g + kk
            sl = slice(k * LANE, (k + 1) * LANE)
            xdt = xdt_w[:, sl]
            xdt_b = xdt.astype(BF16)
            yd = []
            for hh in range(2):
                h = 2 * k + hh
                seg = a_cs[:, h:h + 1] - a_cs_t[h:h + 1, :]
                lm = jnp.exp(jnp.where(causal, seg, -jnp.inf))
                yd.append(jnp.dot((cb * lm).astype(BF16), xdt_b, preferred_element_type=F32))
            y = jnp.where(first, yd[0], yd[1])
            s_old = state_scr[k]
            y = y + jnp.dot(cg, s_old.astype(BF16), preferred_element_type=F32) * exp_cs_w[:, sl]
            y = y + xs[:, sl] * dskip_ref[:, sl]
            state_scr[k] = s_old * cd_w[:, sl] + jnp.dot(
                bg_t, (xdt * dte_w[:, sl]).astype(BF16), preferred_element_type=F32)
            zz = z[:, sl]
            gated.append(y * (zz * _sigmoid(zz)))
    for g in range(SSD_GROUPS):
        y0, y1 = gated[2 * g], gated[2 * g + 1]
        ms = (jnp.sum(y0 * y0, axis=-1, keepdims=True)
              + jnp.sum(y1 * y1, axis=-1, keepdims=True)) * (1.0 / (SSD_WIDTH // SSD_GROUPS))
        r = lax.rsqrt(ms + NORM_EPS)
        lo = 2 * g * LANE
        y_ref[0, :, lo:lo + LANE] = (y0 * r * nw_ref[:, lo:lo + LANE]).astype(y_ref.dtype)
        y_ref[0, :, lo + LANE:lo + 2 * LANE] = (y1 * r * nw_ref[:, lo + LANE:lo + 2 * LANE]).astype(y_ref.dtype)


def _ssd(z, xbc, dt, conv_w, conv_b, dt_bias, a_log, d_skip, norm_w):
    b, l, _ = z.shape
    q = SSD_CHUNK
    pad = LANE - SSD_HEADS
    dtb = jnp.pad(dt_bias, (0, pad)).reshape(1, LANE)
    alog = jnp.pad(a_log, (0, pad)).reshape(1, LANE)
    dsk = jnp.repeat(d_skip, SSD_HEAD_DIM).reshape(1, SSD_WIDTH)
    expand = (jnp.arange(LANE)[:, None] == (jnp.arange(SSD_WIDTH)[None, :] // SSD_HEAD_DIM)).astype(F32)
    tok = lambda w: pl.BlockSpec((1, q, w), lambda bi, ci: (bi, ci, 0))
    return pl.pallas_call(
        _ssd_body, grid=(b, l // q),
        in_specs=[tok(SSD_WIDTH), tok(SSD_CONV_CH), tok(LANE), _full((SSD_CONV, SSD_CONV_CH)),
                  _full((1, SSD_CONV_CH)), _full((1, LANE)), _full((1, LANE)), _full((1, SSD_WIDTH)),
                  _full((1, SSD_WIDTH)), _full((LANE, SSD_WIDTH))],
        out_specs=tok(SSD_WIDTH),
        out_shape=jax.ShapeDtypeStruct((b, l, SSD_WIDTH), BF16),
        scratch_shapes=[pltpu.VMEM((q + 8, SSD_CONV_CH), F32),
                        pltpu.VMEM((SSD_HEADS // 2, SSD_STATE, LANE), F32)],
        compiler_params=_cparams(("parallel", "arbitrary")), name="ssd",
    )(z, xbc, dt, conv_w, conv_b.reshape(1, -1), dtb, alog, dsk, norm_w.reshape(1, -1), expand)


V_ROWS = ATT_DIM + 16


def _with_ones_row(v_t):
    n = v_t.shape[1]
    r = lax.broadcasted_iota(jnp.int32, (V_ROWS - ATT_DIM, n), 0)
    return jnp.concatenate([v_t, jnp.where(r == 0, 1.0, 0.0)], axis=0)


def _head_rms(xh, w_row):
    ms = jnp.sum(xh * xh, axis=-1, keepdims=True) * (1.0 / ATT_DIM)
    return xh * lax.rsqrt(ms + NORM_EPS) * w_row


def _mem_kv_body(mem_ref, g_ref, wk_ref, wv_ref, kw_ref, k_ref, vt_ref):
    x = mem_ref[0]
    ms = jnp.mean(x * x, axis=-1, keepdims=True)
    xn = ((x * lax.rsqrt(ms + NORM_EPS)) * g_ref[...]).astype(BF16)
    kp = jnp.dot(xn, wk_ref[...], preferred_element_type=F32)
    vp = jnp.dot(xn, wv_ref[...], preferred_element_type=F32)
    for h in range(ATT_HEADS):
        sl = slice(h * LANE, (h + 1) * LANE)
        k_ref[0, h] = _head_rms(kp[:, sl], kw_ref[...]).astype(BF16)
        vt_ref[0, h] = _with_ones_row(vp[:, sl].T[0:ATT_DIM, :]).astype(BF16)


def _mem_kv(mem, gain, wk, wv, k_norm_w):
    b = mem.shape[0]
    return pl.pallas_call(
        _mem_kv_body, grid=(b,),
        in_specs=[pl.BlockSpec((1, MEM_LEN, D_MODEL), lambda i: (i, 0, 0)), _full((1, D_MODEL)),
                  _full(wk.shape), _full(wv.shape), _full((1, LANE))],
        out_specs=[pl.BlockSpec((1, ATT_HEADS, MEM_LEN, LANE), lambda i: (i, 0, 0, 0)),
                   pl.BlockSpec((1, ATT_HEADS, V_ROWS, MEM_LEN), lambda i: (i, 0, 0, 0))],
        out_shape=[jax.ShapeDtypeStruct((b, ATT_HEADS, MEM_LEN, LANE), BF16),
                   jax.ShapeDtypeStruct((b, ATT_HEADS, V_ROWS, MEM_LEN), BF16)],
        compiler_params=_cparams(("parallel",)), name="mem_kv",
    )(mem, gain, wk, wv, k_norm_w)


MAX_BLOCKS = 32
_AUG_ONEHOT = ATT_DIM
_AUG_EXTRA = ATT_DIM + MAX_BLOCKS


def _attn_prep_body(slopes, q_ref, k_ref, v_ref, xq_ref, qw_ref, kw_ref, xqw_ref,
                    qat_ref, ka_ref, kad_ref, vt_ref, xqt_ref, kmean_scr):
    i = pl.program_id(1)
    nb = kmean_scr.shape[1]

    @pl.when(i == 0)
    def _():
        kmean_scr[...] = jnp.zeros_like(kmean_scr)

    n_iota = lax.broadcasted_iota(jnp.int32, (nb, TQ), 0)
    past = n_iota < i
    t_loc = lax.broadcasted_iota(jnp.int32, (32, TQ), 1).astype(F32)
    r32 = lax.broadcasted_iota(jnp.int32, (32, TQ), 0)
    k_lane = lax.broadcasted_iota(jnp.int32, (TQ, LANE), 1)
    s_loc = lax.broadcasted_iota(jnp.int32, (TQ, LANE), 0).astype(F32)
    blk = lax.convert_element_type(i, F32)
    for h in range(ATT_HEADS):
        hi, lo = slopes[h]
        sl = slice(h * LANE, (h + 1) * LANE)
        qn = _head_rms(q_ref[0, :, sl], qw_ref[...])
        kn = _head_rms(k_ref[0, :, sl], kw_ref[...])
        qn_t = qn.T
        gate = jnp.dot(kmean_scr[h], qn_t, precision=HIGHEST, preferred_element_type=F32)
        gate = jnp.where(past, gate, -jnp.inf)
        cnt = jnp.zeros((nb, TQ), F32)
        for n2 in range(nb):
            gn = gate[n2:n2 + 1, :]
            ahead = jnp.where(gn > gate, 1.0, jnp.where(gn == gate, jnp.where(n_iota > n2, 1.0, 0.0), 0.0))
            cnt = cnt + ahead
        allowed = jnp.where(past, jnp.where(cnt < float(MOBA_TOPK), 1.0, 0.0), 0.0)
        bias = jnp.where(allowed > 0.5, 0.0, NEG)
        extra = jnp.where(r32 < 2, t_loc, 0.0)
        for r, val in ((2, hi), (3, lo), (4, MOBA_BLOCK * hi), (5, MOBA_BLOCK * lo)):
            extra = jnp.where(r32 == r, val, extra)
        extra = jnp.where((r32 == 6) | (r32 == 7), blk, extra)
        qat = jnp.concatenate([qn_t[0:ATT_DIM, :] * (1.0 / math.sqrt(ATT_DIM)), bias, extra], axis=0)
        qat_ref[0, h] = qat.astype(BF16)
        kx = jnp.where((k_lane == _AUG_EXTRA + 2) | (k_lane == _AUG_EXTRA + 3), s_loc, 0.0)
        kx = jnp.where((k_lane == _AUG_EXTRA + 4) | (k_lane == _AUG_EXTRA + 5), blk, kx)
        for c, val in ((0, -hi), (1, -lo), (6, -MOBA_BLOCK * hi), (7, -MOBA_BLOCK * lo)):
            kx = jnp.where(k_lane == _AUG_EXTRA + c, val, kx)
        kad = kn + kx
        kad_ref[0, h] = kad.astype(BF16)
        ka_ref[0, h] = (kad + jnp.where(k_lane - _AUG_ONEHOT == i, 1.0, 0.0)).astype(BF16)
        vt_ref[0, h] = _with_ones_row(v_ref[0, :, sl].T[0:ATT_DIM, :]).astype(BF16)
        xqn = _head_rms(xq_ref[0, :, sl], xqw_ref[...])
        xqt_ref[0, h] = (xqn.T * (1.0 / math.sqrt(ATT_DIM))).astype(BF16)
        kmean_scr[h, pl.ds(i, 1), :] = jnp.sum(kn, axis=0, keepdims=True) * (1.0 / MOBA_BLOCK)


def _attn_prep(q, k, v, xq, qw, kw, xqw):
    b, l, _ = q.shape
    nb = l // MOBA_BLOCK
    assert nb <= MAX_BLOCKS
    slopes = tuple(_bf16_split(s) for s in _SLOPES)
    tok = pl.BlockSpec((1, TQ, ATT_HEADS * LANE), lambda bi, i: (bi, i, 0))
    return pl.pallas_call(
        functools.partial(_attn_prep_body, slopes), grid=(b, nb),
        in_specs=[tok, tok, tok, tok, _full((1, LANE)), _full((1, LANE)), _full((1, LANE))],
        out_specs=[pl.BlockSpec((1, ATT_HEADS, LANE, TQ), lambda bi, i: (bi, 0, 0, i)),
                   pl.BlockSpec((1, ATT_HEADS, TQ, LANE), lambda bi, i: (bi, 0, i, 0)),
                   pl.BlockSpec((1, ATT_HEADS, TQ, LANE), lambda bi, i: (bi, 0, i, 0)),
                   pl.BlockSpec((1, ATT_HEADS, V_ROWS, TQ), lambda bi, i: (bi, 0, 0, i)),
                   pl.BlockSpec((1, ATT_HEADS, LANE, TQ), lambda bi, i: (bi, 0, 0, i))],
        out_shape=[jax.ShapeDtypeStruct((b, ATT_HEADS, LANE, l), BF16),
                   jax.ShapeDtypeStruct((b, ATT_HEADS, l, LANE), BF16),
                   jax.ShapeDtypeStruct((b, ATT_HEADS, l, LANE), BF16),
                   jax.ShapeDtypeStruct((b, ATT_HEADS, V_ROWS, l), BF16),
                   jax.ShapeDtypeStruct((b, ATT_HEADS, LANE, l), BF16)],
        scratch_shapes=[pltpu.VMEM((ATT_HEADS, MAX_BLOCKS, LANE), F32)],
        compiler_params=_cparams(("parallel", "arbitrary")), name="attn_prep",
    )(q, k, v, xq, qw, kw, xqw)


def _normalized_t(acc):
    return (acc[0:ATT_DIM, :] * (1.0 / acc[ATT_DIM:ATT_DIM + 1, :])).T


def _attn_body(qat_ref, ka_ref, kad_ref, vt_ref, xqt_ref, mk_ref, mvt_ref, o_ref, om_ref, s_scr, p_scr):
    i = pl.program_id(1)
    base = pl.multiple_of(i * MOBA_BLOCK, MOBA_BLOCK)
    key = lax.broadcasted_iota(jnp.int32, (MOBA_BLOCK, TQ), 0)
    qry = lax.broadcasted_iota(jnp.int32, (MOBA_BLOCK, TQ), 1)

    span = ATT_GROUP * MOBA_BLOCK
    n_groups = (i + ATT_GROUP - 1) // ATT_GROUP
    last_group = ka_ref.shape[2] // span - 1

    def scores(h, g):
        off = pl.multiple_of(jnp.minimum(g, last_group) * span, span)
        return jnp.dot(ka_ref[0, h, pl.ds(off, span), :], qat_ref[0, h], preferred_element_type=F32)

    def values(h, g, p):
        off = pl.multiple_of(jnp.maximum(g, 0) * span, span)
        return jnp.dot(vt_ref[0, h, :, pl.ds(off, span)], p, preferred_element_type=F32)

    init = []
    for h in range(ATT_HEADS):
        s = jnp.dot(kad_ref[0, h], qat_ref[0, h], preferred_element_type=F32)
        s = jnp.where(key <= qry, s, NEG)
        m = jnp.max(s, axis=0, keepdims=True)
        p = jnp.exp(s - m).astype(BF16)
        acc = jnp.dot(vt_ref[0, h, :, pl.ds(base, MOBA_BLOCK)], p, preferred_element_type=F32)
        s_scr[h] = scores(h, 0)
        p_scr[h] = jnp.zeros((span, TQ), BF16)
        init += [jnp.ones((1, TQ), F32), m, acc]

    def body(g, carry):
        out = []
        for h in range(ATT_HEADS):
            alpha_prev, m, acc = carry[3 * h:3 * h + 3]
            acc = acc * alpha_prev + values(h, g - 1, p_scr[h])
            s = s_scr[h]
            m_new = jnp.maximum(m, jnp.max(s, axis=0, keepdims=True))
            p_scr[h] = jnp.exp(s - m_new).astype(BF16)
            s_scr[h] = scores(h, g + 1)
            out += [jnp.exp(m - m_new), m_new, acc]
        return tuple(out)

    fin = lax.fori_loop(0, n_groups, body, tuple(init))
    for h in range(ATT_HEADS):
        alpha_prev, _, acc = fin[3 * h:3 * h + 3]
        o_ref[0, h] = _normalized_t(acc * alpha_prev + values(h, n_groups - 1, p_scr[h]))
        sm = jnp.dot(mk_ref[0, h], xqt_ref[0, h], preferred_element_type=F32)
        pm = jnp.exp(sm - jnp.max(sm, axis=0, keepdims=True)).astype(BF16)
        om_ref[0, h] = _normalized_t(jnp.dot(mvt_ref[0, h], pm, preferred_element_type=F32))


def _attn(qat, ka, kad, vt, xqt, mem_k, mem_vt):
    b, _, _, l = qat.shape
    assert l % (ATT_GROUP * MOBA_BLOCK) == 0
    nh = ATT_HEADS
    per_q = lambda r: pl.BlockSpec((1, nh, r, TQ), lambda bi, i: (bi, 0, 0, i))
    out = pl.BlockSpec((1, nh, TQ, ATT_DIM), lambda bi, i: (bi, 0, i, 0))
    return pl.pallas_call(
        _attn_body, grid=(b, l // TQ),
        in_specs=[per_q(LANE),
                  pl.BlockSpec((1, nh, l, LANE), lambda bi, i: (bi, 0, 0, 0)),
                  pl.BlockSpec((1, nh, TQ, LANE), lambda bi, i: (bi, 0, i, 0)),
                  pl.BlockSpec((1, nh, V_ROWS, l), lambda bi, i: (bi, 0, 0, 0)),
                  per_q(LANE),
                  pl.BlockSpec((1, nh, MEM_LEN, LANE), lambda bi, i: (bi, 0, 0, 0)),
                  pl.BlockSpec((1, nh, V_ROWS, MEM_LEN), lambda bi, i: (bi, 0, 0, 0))],
        out_specs=[out, out],
        out_shape=[jax.ShapeDtypeStruct((b, nh, l, ATT_DIM), F32)] * 2,
        scratch_shapes=[pltpu.VMEM((nh, ATT_GROUP * MOBA_BLOCK, TQ), F32),
                        pltpu.VMEM((nh, ATT_GROUP * MOBA_BLOCK, TQ), BF16)],
        compiler_params=_cparams(("parallel", "arbitrary")), name="attn",
    )(qat, ka, kad, vt, xqt, mem_k, mem_vt)


def _cand_layout():
    slabs = [("r1", 0, 16), ("r1", 1, 8), ("r1", 2, 8), ("r1", 3, 8), ("r2", 0, 16), ("r2", 1, 8), ("r2", 2, 8)]
    pos, valid = [], []
    for kind, fixed, n in slabs:
        for j in range(n):
            r1, r2 = (fixed, j) if kind == "r1" else (j, fixed)
            ok = (r1 + 1) * (r2 + 1) <= PEER_TOPK and ((kind == "r1") or r1 >= 4)
            pos.append(r1 * PEER_TOPK + r2)
            valid.append(ok)
    assert sum(valid) == 50
    return slabs, np.asarray(pos, np.int32), np.asarray(valid)


_CAND_SLABS, _CAND_POS, _CAND_VALID = _cand_layout()
_NCAND = len(_CAND_POS)


def _split_cols(k):
    hi = k.astype(BF16)
    lo = (k - hi.astype(F32)).astype(BF16)
    return jnp.concatenate([hi, hi, lo], axis=1)


def _dot_split(k_cat, q):
    hi = q.astype(BF16)
    lo = (q - hi.astype(F32)).astype(BF16)
    return jnp.dot(k_cat, jnp.concatenate([hi, lo, hi], axis=0), preferred_element_type=F32)


def _pack_rows(x):
    return pltpu.bitcast(x.astype(BF16), jnp.uint32)


def _unpack_rows(x):
    return pltpu.bitcast(x, BF16)


def _top16_pair(sets, exact):
    row = lax.broadcasted_iota(jnp.int32, (PEER_NKEYS, LANE), 0).astype(F32)
    for _, rank_scr, _ in sets:
        rank_scr[...] = jnp.full(rank_scr.shape, float(PEER_TOPK), F32)

    def body(r, _):
        rf = lax.convert_element_type(r, F32)
        for half in range(TM_PP // LANE):
            ls = slice(half * LANE, (half + 1) * LANE)
            for w_scr, rank_scr, v_scr in sets:
                s = w_scr[:, ls]
                m = jnp.max(s, axis=0, keepdims=True)
                hit = s == m
                if exact:
                    hit = row == jnp.min(jnp.where(hit, row, float(PEER_NKEYS)), axis=0, keepdims=True)
                w_scr[:, ls] = jnp.where(hit, -jnp.inf, s)
                v_scr[half, pl.ds(r, 1), :] = m
                rank_scr[:, ls] = jnp.where(hit, rf, rank_scr[:, ls])
        return 0

    lax.fori_loop(0, PEER_TOPK, body, 0)


def _peer_prep_body(x_ref, ys_ref, ym_ref, yx_ref, wos_ref, wom_ref, wox_ref, g_ref, wqt_ref,
                    k1_ref, k2_ref, pos_ref, cbias_ref,
                    h1_ref, hnt_ref, r2_ref, f_ref, n_ref, c_ref,
                    q_scr, w1_scr, w2_scr, rank1_scr, rank2_scr, v1_scr, v2_scr):
    hres = x_ref[...] + jnp.dot(ys_ref[...], wos_ref[...], preferred_element_type=F32)
    for h in range(ATT_HEADS):
        hres = hres + jnp.dot(ym_ref[0, h].astype(BF16), wom_ref[h], preferred_element_type=F32)
        hres = hres + jnp.dot(yx_ref[0, h].astype(BF16), wox_ref[h], preferred_element_type=F32)
    h1_ref[...] = hres
    ms = jnp.mean(hres * hres, axis=-1, keepdims=True)
    hn = (hres * lax.rsqrt(ms + NORM_EPS)) * g_ref[...]
    hn_t = hn.T.astype(BF16)
    hnt_ref[...] = hn_t
    q_scr[...] = jnp.dot(wqt_ref[...], hn_t, preferred_element_type=F32)

    pos = pos_ref[...]
    cbias = cbias_ref[...]
    r16 = lax.broadcasted_iota(jnp.int32, (PEER_TOPK, TM_PP), 0).astype(F32)
    halves = lambda scr: jnp.concatenate([scr[j] for j in range(TM_PP // LANE)], axis=1)

    def head(h, _):
        base = pl.multiple_of(h * (2 * PEER_HALF), 2 * PEER_HALF)
        s1 = _dot_split(k1_ref[...], q_scr[pl.ds(base, PEER_HALF), :])
        s2 = _dot_split(k2_ref[...], q_scr[pl.ds(base + PEER_HALF, PEER_HALF), :])
        sets = [(w1_scr, rank1_scr, v1_scr), (w2_scr, rank2_scr, v2_scr)]
        w1_scr[...] = s1
        w2_scr[...] = s2
        _top16_pair(sets, exact=False)
        ranked = sum(jnp.sum(jnp.where(scr[...] < float(PEER_TOPK), 1.0, 0.0), axis=0, keepdims=True)
                     for scr in (rank1_scr, rank2_scr))
        tied = jnp.max(jnp.abs(ranked - 2.0 * PEER_TOPK)) > 0.5

        @pl.when(tied)
        def _():
            w1_scr[...] = s1
            w2_scr[...] = s2
            _top16_pair(sets, exact=True)

        rank1 = rank1_scr[...]
        rank2 = rank2_scr[...]
        v1 = halves(v1_scr)
        v2 = halves(v2_scr)
        parts = []
        for kind, fixed, n in _CAND_SLABS:
            if kind == "r1":
                parts.append(v1[fixed:fixed + 1, :] + v2[0:n, :])
            else:
                parts.append(v1[0:n, :] + v2[fixed:fixed + 1, :])
        cand = jnp.concatenate(parts, axis=0) + cbias
        top = v1[0:1, :] + v2[0:1, :]

        def pick(_, carry):
            cand, cnt, zsum = carry
            m = jnp.max(cand, axis=0, keepdims=True)
            p = jnp.min(jnp.where(cand == m, pos, float(4 * PEER_TOPK * PEER_TOPK)), axis=0, keepdims=True)
            cand = jnp.where(pos == p, -jnp.inf, cand)
            cnt = cnt + jnp.where(r16 == jnp.floor(p * (1.0 / PEER_TOPK)), 1.0, 0.0)
            return cand, cnt, zsum + jnp.exp(m - top)

        _, cnt, zsum = lax.fori_loop(
            0, PEER_TOPK, pick, (cand, jnp.zeros((PEER_TOPK, TM_PP), F32), jnp.zeros((1, TM_PP), F32)))
        nsel = jnp.zeros((PEER_NKEYS, TM_PP), F32)
        for r in range(PEER_TOPK):
            nsel = jnp.where(rank1 == float(r), cnt[r:r + 1, :], nsel)
        r2_ref[h] = _pack_rows(rank2)
        f_ref[h] = _pack_rows(jnp.exp(s2 - v2[0:1, :]))
        cw = jnp.exp(s1 - v1[0:1, :]) * (0.5 / zsum)
        for lc in range(TM_PP // LANE):
            n_ref[h, lc] = nsel[:, lc * LANE:(lc + 1) * LANE]
            c_ref[h, lc] = cw[:, lc * LANE:(lc + 1) * LANE]
        return 0

    lax.fori_loop(0, PEER_HEADS, head, 0)


def _peer_prep(x2d, y_ssd, y_moba, y_mem, wo_ssd, wo_moba, wo_mem, gain, wq_t, k1, k2):
    t = x2d.shape[0]
    b, _, l, _ = y_moba.shape
    per_b = l // TM_PP
    pos = jnp.asarray(np.broadcast_to(_CAND_POS.astype(np.float32)[:, None], (_NCAND, TM_PP)))
    cbias = jnp.asarray(np.broadcast_to(np.where(_CAND_VALID, 0.0, -np.inf).astype(np.float32)[:, None],
                                        (_NCAND, TM_PP)))
    tok = lambda w: pl.BlockSpec((TM_PP, w), lambda i: (i, 0))
    att = pl.BlockSpec((1, ATT_HEADS, TM_PP, ATT_DIM), lambda i: (i // per_b, 0, i % per_b, 0))
    meta = pl.BlockSpec((PEER_HEADS, TM_PP // LANE, PEER_NKEYS, LANE), lambda i: (0, i, 0, 0))
    meta_f32 = jax.ShapeDtypeStruct((PEER_HEADS, t // LANE, PEER_NKEYS, LANE), F32)
    packed = pl.BlockSpec((PEER_HEADS, PEER_NKEYS // 2, TM_PP), lambda i: (0, 0, i))
    meta_pk = jax.ShapeDtypeStruct((PEER_HEADS, PEER_NKEYS // 2, t), jnp.uint32)
    return pl.pallas_call(
        _peer_prep_body, grid=(t // TM_PP,),
        in_specs=[tok(D_MODEL), tok(SSD_WIDTH), att, att, _full(wo_ssd.shape), _full(wo_moba.shape),
                  _full(wo_mem.shape), _full((1, D_MODEL)), _full(wq_t.shape), _full(k1.shape), _full(k2.shape),
                  _full(pos.shape), _full(cbias.shape)],
        out_specs=[tok(D_MODEL), pl.BlockSpec((D_MODEL, TM_PP), lambda i: (0, i)), packed, packed, meta, meta],
        out_shape=[jax.ShapeDtypeStruct((t, D_MODEL), F32), jax.ShapeDtypeStruct((D_MODEL, t), BF16),
                   meta_pk, meta_pk, meta_f32, meta_f32],
        scratch_shapes=[pltpu.VMEM((PEER_HEADS * 2 * PEER_HALF, TM_PP), F32)]
        + [pltpu.VMEM((PEER_NKEYS, TM_PP), F32)] * 4
        + [pltpu.VMEM((TM_PP // LANE, PEER_TOPK, LANE), F32)] * 2,
        compiler_params=_cparams(("parallel",)), name="peer_prep",
    )(x2d, y_ssd, y_moba, y_mem, wo_ssd, wo_moba, wo_mem, gain, wq_t, k1, k2, pos, cbias)


_A_PER_SLAB = ES_PEER // PEER_NKEYS
_PEER_STEPS = PEER_EXPERTS // (2 * ES_PEER)


def _peer_gate_chunk(s_ref, a_ref, a_base, k, lc, r2_ref, f_ref, n_ref, c_ref):
    zero = jnp.zeros((PEER_NKEYS, LANE), BF16)

    def bcast(row):
        tile = jnp.broadcast_to(row, (16, LANE)).astype(BF16)
        return jnp.concatenate([tile] * (PEER_NKEYS // 16), axis=0)

    a = a_base + k
    rows = slice(k * PEER_NKEYS, (k + 1) * PEER_NKEYS)
    ls = slice(lc * LANE, (lc + 1) * LANE)
    s = s_ref[rows, ls]
    act = s * (1.0 + lax.erf(s * math.sqrt(0.5)))
    g = zero
    for h in range(PEER_HEADS):
        nrow = bcast(n_ref[h, lc, pl.ds(a, 1), :])
        crow = bcast(c_ref[h, lc, pl.ds(a, 1), :])
        g = g + jnp.where(_unpack_rows(r2_ref[h, :, ls]) < nrow, _unpack_rows(f_ref[h, :, ls]) * crow, zero)
    a_ref[k * (PEER_NKEYS // 2):(k + 1) * (PEER_NKEYS // 2), ls] = pltpu.bitcast(act.astype(BF16) * g, jnp.uint32)


def _peer_half_step(gate_args, matmuls):
    for mm in matmuls:
        mm()
    for lc in range(TT_PEER // LANE):
        for k in range(_A_PER_SLAB):
            _peer_gate_chunk(*gate_args[:3], k, lc, *gate_args[3:])


def _peer_body(wd_a0_ref, wd_b_ref, wd_an_ref, wut_a_ref, wut_bp_ref, hnt_ref, r2_ref, f_ref, n_ref, c_ref,
               h1_ref, o_ref, acc_scr, sa_scr, sb_scr, aa_scr, ab_scr):
    n = pl.program_id(1)
    last = pl.num_programs(1) - 1
    meta = (r2_ref, f_ref, n_ref, c_ref)

    @pl.when(n == 0)
    def _():
        acc_scr[...] = jnp.zeros_like(acc_scr)
        ab_scr[...] = jnp.zeros_like(ab_scr)
        sa_scr[...] = jnp.dot(wd_a0_ref[...], hnt_ref[...], preferred_element_type=F32)

    halves = [slice(j * (TT_PEER // 2), (j + 1) * (TT_PEER // 2)) for j in range(2)]

    def out_mm(wut_ref, a_ref, ls):
        def run():
            acc_scr[:, ls] += jnp.dot(wut_ref[...], _unpack_rows(a_ref[:, ls]), preferred_element_type=F32)
        return run

    def score_mm(wd_ref, s_ref, ls):
        def run():
            s_ref[:, ls] = jnp.dot(wd_ref[...], hnt_ref[:, ls], preferred_element_type=F32)
        return run

    @pl.when(n < last)
    def _():
        a0 = n * (2 * _A_PER_SLAB)
        _peer_half_step((sa_scr, aa_scr, a0) + meta,
                        [out_mm(wut_bp_ref, ab_scr, halves[0]), score_mm(wd_b_ref, sb_scr, halves[0]),
                         out_mm(wut_bp_ref, ab_scr, halves[1]), score_mm(wd_b_ref, sb_scr, halves[1])])
        _peer_half_step((sb_scr, ab_scr, a0 + _A_PER_SLAB) + meta,
                        [out_mm(wut_a_ref, aa_scr, halves[0]), score_mm(wd_an_ref, sa_scr, halves[0]),
                         out_mm(wut_a_ref, aa_scr, halves[1]), score_mm(wd_an_ref, sa_scr, halves[1])])

    @pl.when(n == last)
    def _():
        tail = jnp.dot(wut_bp_ref[...], _unpack_rows(ab_scr[...]), preferred_element_type=F32)
        o_ref[...] = h1_ref[...] + (acc_scr[...] + tail).T


def _peer(wd, wu_t, hn_t, rank2, f, nsel, c, h1):
    t = h1.shape[0]
    nblk = 2 * _PEER_STEPS
    meta = pl.BlockSpec((PEER_HEADS, TT_PEER // LANE, PEER_NKEYS, LANE), lambda i, n: (0, i, 0, 0))
    packed = pl.BlockSpec((PEER_HEADS, PEER_NKEYS // 2, TT_PEER), lambda i, n: (0, 0, i))
    wd_blk = lambda f: pl.BlockSpec((ES_PEER, D_MODEL), lambda i, n: (f(n), 0))
    wut_blk = lambda f: pl.BlockSpec((None, D_MODEL, ES_PEER), lambda i, n: (f(n), 0, 0))
    scores = pltpu.VMEM((ES_PEER, TT_PEER), F32)
    acts = pltpu.VMEM((ES_PEER // 2, TT_PEER), jnp.uint32)
    return pl.pallas_call(
        _peer_body, grid=(t // TT_PEER, _PEER_STEPS + 1),
        in_specs=[wd_blk(lambda n: 0),
                  wd_blk(lambda n: jnp.minimum(2 * n + 1, nblk - 1)),
                  wd_blk(lambda n: jnp.minimum(2 * n + 2, nblk - 1)),
                  wut_blk(lambda n: jnp.minimum(2 * n, nblk - 1)),
                  wut_blk(lambda n: jnp.maximum(2 * n - 1, 0)),
                  pl.BlockSpec((D_MODEL, TT_PEER), lambda i, n: (0, i)),
                  packed, packed, meta, meta,
                  pl.BlockSpec((TT_PEER, D_MODEL), lambda i, n: (i, 0))],
        out_specs=pl.BlockSpec((TT_PEER, D_MODEL), lambda i, n: (i, 0)),
        out_shape=jax.ShapeDtypeStruct((t, D_MODEL), F32),
        scratch_shapes=[pltpu.VMEM((D_MODEL, TT_PEER), F32), scores, scores, acts, acts],
        compiler_params=_cparams(("parallel", "arbitrary")), name="peer",
    )(wd, wd, wd, wu_t, wu_t, hn_t, rank2, f, nsel, c, h1)


def _pad_heads(w):
    r = w.shape[0]
    w = w.reshape(r, -1, ATT_DIM)
    return jnp.pad(w, ((0, 0), (0, 0), (0, LANE - ATT_DIM))).reshape(r, -1)


def _pad_row(w):
    return jnp.pad(w, (0, LANE - w.shape[0])).reshape(1, LANE)


def kernel(x, mem, mix_norm_w, w_in, ssd_conv_w, ssd_conv_b, ssd_dt_bias, ssd_a_log, ssd_d, ssd_norm_w,
           moba_q_norm_w, moba_k_norm_w, mem_norm_w, w_mem_kv, xattn_q_norm_w, xattn_k_norm_w, w_out,
           ffn_norm_w, peer_w_query, peer_sub_keys_1, peer_sub_keys_2, peer_expert_down, peer_expert_up):
    b, l, d = x.shape
    depth = w_in.shape[0]
    h = x.reshape(b * l, d)
    for li in range(depth):
        wi = w_in[li]
        o = 0
        cols = {}
        for name, width in (("z", SSD_WIDTH), ("xbc", SSD_CONV_CH), ("dt", SSD_HEADS), ("mq", ATT_WIDTH),
                            ("mk", ATT_WIDTH), ("mv", ATT_WIDTH), ("xq", ATT_WIDTH)):
            cols[name] = wi[:, o:o + width]
            o += width
        w_list = [cols["z"], cols["xbc"], jnp.pad(cols["dt"], ((0, 0), (0, LANE - SSD_HEADS))),
                  _pad_heads(cols["mq"]), _pad_heads(cols["mk"]), _pad_heads(cols["mv"]), _pad_heads(cols["xq"])]
        w_list = [w.astype(BF16) for w in w_list]
        z, xbc, dt, mq, mk, mv, xq = _in_proj(h, mix_norm_w[li].reshape(1, d), w_list)

        y_ssd = _ssd(z.reshape(b, l, -1), xbc.reshape(b, l, -1), dt.reshape(b, l, -1), ssd_conv_w[li],
                     ssd_conv_b[li], ssd_dt_bias[li], ssd_a_log[li], ssd_d[li], ssd_norm_w[li])

        wkv = w_mem_kv[li]
        mem_k, mem_vt = _mem_kv(mem, mem_norm_w[li].reshape(1, d), _pad_heads(wkv[:, :ATT_WIDTH]).astype(BF16),
                                _pad_heads(wkv[:, ATT_WIDTH:]).astype(BF16), _pad_row(xattn_k_norm_w[li]))
        r3 = lambda a: a.reshape(b, l, -1)
        qat, ka, kad, vt, xqt = _attn_prep(r3(mq), r3(mk), r3(mv), r3(xq), _pad_row(moba_q_norm_w[li]),
                                           _pad_row(moba_k_norm_w[li]), _pad_row(xattn_q_norm_w[li]))
        y_moba, y_mem = _attn(qat, ka, kad, vt, xqt, mem_k, mem_vt)

        wo = w_out[li].astype(BF16)
        wo_moba = wo[SSD_WIDTH:SSD_WIDTH + ATT_WIDTH].reshape(ATT_HEADS, ATT_DIM, d)
        wo_mem = wo[SSD_WIDTH + ATT_WIDTH:].reshape(ATT_HEADS, ATT_DIM, d)
        h1, hn_t, rank2, f, nsel, c = _peer_prep(
            h, y_ssd.reshape(b * l, -1), y_moba, y_mem, wo[:SSD_WIDTH], wo_moba, wo_mem,
            ffn_norm_w[li].reshape(1, d), peer_w_query[li].T.astype(BF16), _split_cols(peer_sub_keys_1[li]),
            _split_cols(peer_sub_keys_2[li]))
        wu_t = peer_expert_up[li].reshape(-1, ES_PEER, d).transpose(0, 2, 1).astype(BF16)
        h = _peer(peer_expert_down[li].astype(BF16), wu_t, hn_t, rank2, f, nsel, c, h1)
    return h.reshape(b, l, d)
```

```python
import functools
import math

import numpy as np
import jax
import jax.numpy as jnp
from jax import lax
from jax.experimental import pallas as pl
from jax.experimental.pallas import tpu as pltpu

F32 = jnp.float32
BF16 = jnp.bfloat16
HIGHEST = lax.Precision.HIGHEST

NORM_EPS = 1e-6
D_MODEL = 1024
SSD_HEADS = 8
SSD_HEAD_DIM = 64
SSD_WIDTH = 512
SSD_GROUPS = 2
SSD_STATE = 128
SSD_CONV = 4
SSD_CONV_CH = 1024
ATT_HEADS = 4
ATT_DIM = 64
ATT_WIDTH = 256
MOBA_BLOCK = 256
MOBA_TOPK = 3
MEM_LEN = 256
PEER_HEADS = 8
PEER_NKEYS = 128
PEER_TOPK = 16
PEER_HALF = 64
PEER_EXPERTS = PEER_NKEYS * PEER_NKEYS

LANE = 128
NEG = -1e30
VMEM_LIMIT = 56 * 1024 * 1024

TM_IN = 512
SSD_CHUNK = 256
TQ = MOBA_BLOCK
ATT_GROUP = 4
TM_PP = 512
TT_PEER = 512
ES_PEER = 512

_SLOPES = [2.0 ** (-8.0 * (i + 1) / ATT_HEADS) for i in range(ATT_HEADS)]


def _bf16_split(v):
    hi = float(np.float32(v).astype(BF16).astype(np.float32))
    lo = float(np.float32(v - hi).astype(BF16).astype(np.float32))
    return hi, lo


def _cparams(sem):
    return pltpu.CompilerParams(dimension_semantics=sem, vmem_limit_bytes=VMEM_LIMIT)


def _sigmoid(x):
    return 1.0 / (1.0 + jnp.exp(-x))


def _full(shape):
    n = len(shape)
    return pl.BlockSpec(shape, lambda *_: (0,) * n)


def _in_proj_body(x_ref, g_ref, *refs):
    n = len(refs) // 2
    w_refs, o_refs = refs[:n], refs[n:]
    x = x_ref[...]
    ms = jnp.mean(x * x, axis=-1, keepdims=True)
    xn = ((x * lax.rsqrt(ms + NORM_EPS)) * g_ref[...]).astype(BF16)
    for w_ref, o_ref in zip(w_refs, o_refs):
        o_ref[...] = jnp.dot(xn, w_ref[...], preferred_element_type=F32).astype(o_ref.dtype)


def _in_proj(x2d, gain, weights):
    t = x2d.shape[0]
    in_specs = [pl.BlockSpec((TM_IN, D_MODEL), lambda i: (i, 0)), _full((1, D_MODEL))]
    in_specs += [_full(w.shape) for w in weights]
    out_specs = [pl.BlockSpec((TM_IN, w.shape[1]), lambda i: (i, 0)) for w in weights]
    out_shape = [jax.ShapeDtypeStruct((t, w.shape[1]), F32) for w in weights]
    return pl.pallas_call(
        _in_proj_body, grid=(t // TM_IN,), in_specs=in_specs, out_specs=out_specs,
        out_shape=out_shape, compiler_params=_cparams(("parallel",)), name="in_proj",
    )(x2d, gain, *weights)


def _ssd_body(z_ref, xbc_ref, dt_ref, cw_ref, cb_ref, dtb_ref, alog_ref, dskip_ref, nw_ref, e_ref,
              y_ref, ext_scr, state_scr):
    q = SSD_CHUNK
    c = pl.program_id(1)

    @pl.when(c == 0)
    def _():
        ext_scr[0:8, :] = jnp.zeros((8, SSD_CONV_CH), F32)
        state_scr[...] = jnp.zeros_like(state_scr)

    u = xbc_ref[0]
    ext_scr[8:8 + q, :] = u
    acc = cb_ref[...] + cw_ref[3:4, :] * u
    acc = acc + cw_ref[2:3, :] * ext_scr[7:7 + q, :]
    acc = acc + cw_ref[1:2, :] * ext_scr[6:6 + q, :]
    acc = acc + cw_ref[0:1, :] * ext_scr[5:5 + q, :]
    ext_scr[0:8, :] = u[q - 8:q, :]
    act = acc * _sigmoid(acc)
    xs = act[:, 0:SSD_WIDTH]
    bm = act[:, SSD_WIDTH:SSD_WIDTH + SSD_GROUPS * SSD_STATE]
    cm = act[:, SSD_WIDTH + SSD_GROUPS * SSD_STATE:]

    dtr = dt_ref[0] + dtb_ref[...]
    dt = jnp.maximum(dtr, 0.0) + jnp.log(1.0 + jnp.exp(-jnp.abs(dtr)))
    a = dt * (-jnp.exp(alog_ref[...]))
    row = lax.broadcasted_iota(jnp.int32, (q, q), 0)
    col = lax.broadcasted_iota(jnp.int32, (q, q), 1)
    causal = row >= col
    a_cs = jnp.dot(causal.astype(F32), a, precision=HIGHEST, preferred_element_type=F32)
    a_cs_t = a_cs.T
    a_cs_w = jnp.dot(a_cs, e_ref[...], precision=HIGHEST, preferred_element_type=F32)
    dt_w = jnp.dot(dt, e_ref[...], precision=HIGHEST, preferred_element_type=F32)
    total_w = a_cs_w[q - 1:q, :]
    exp_cs_w = jnp.exp(a_cs_w)
    dte_w = jnp.exp(total_w - a_cs_w)
    cd_w = jnp.exp(total_w)
    xdt_w = xs * dt_w
    lane = lax.broadcasted_iota(jnp.int32, (1, LANE), 1)
    first = lane < SSD_HEAD_DIM

    z = z_ref[0]
    gated = []
    for g in range(SSD_GROUPS):
        bg = bm[:, g * SSD_STATE:(g + 1) * SSD_STATE]
        cg = cm[:, g * SSD_STATE:(g + 1) * SSD_STATE].astype(BF16)
        cb = lax.dot_general(cg, bg.astype(BF16), (((1,), (1,)), ((), ())), preferred_element_type=F32)
        bg_t = bg.T.astype(BF16)
        for kk in range(2):
            k = 2 * g + kk
            sl = slice(k * LANE, (k + 1) * LANE)
            xdt = xdt_w[:, sl]
            xdt_b = xdt.astype(BF16)
            yd = []
            for hh in range(2):
                h = 2 * k + hh
                seg = a_cs[:, h:h + 1] - a_cs_t[h:h + 1, :]
                lm = jnp.exp(jnp.where(causal, seg, -jnp.inf))
                yd.append(jnp.dot((cb * lm).astype(BF16), xdt_b, preferred_element_type=F32))
            y = jnp.where(first, yd[0], yd[1])
            s_old = state_scr[k]
            y = y + jnp.dot(cg, s_old.astype(BF16), preferred_element_type=F32) * exp_cs_w[:, sl]
            y = y + xs[:, sl] * dskip_ref[:, sl]
            state_scr[k] = s_old * cd_w[:, sl] + jnp.dot(
                bg_t, (xdt * dte_w[:, sl]).astype(BF16), preferred_element_type=F32)
            zz = z[:, sl]
            gated.append(y * (zz * _sigmoid(zz)))
    for g in range(SSD_GROUPS):
        y0, y1 = gated[2 * g], gated[2 * g + 1]
        ms = (jnp.sum(y0 * y0, axis=-1, keepdims=True)
              + jnp.sum(y1 * y1, axis=-1, keepdims=True)) * (1.0 / (SSD_WIDTH // SSD_GROUPS))
        r = lax.rsqrt(ms + NORM_EPS)
        lo = 2 * g * LANE
        y_ref[0, :, lo:lo + LANE] = (y0 * r * nw_ref[:, lo:lo + LANE]).astype(y_ref.dtype)
        y_ref[0, :, lo + LANE:lo + 2 * LANE] = (y1 * r * nw_ref[:, lo + LANE:lo + 2 * LANE]).astype(y_ref.dtype)


def _ssd(z, xbc, dt, conv_w, conv_b, dt_bias, a_log, d_skip, norm_w):
    b, l, _ = z.shape
    q = SSD_CHUNK
    pad = LANE - SSD_HEADS
    dtb = jnp.pad(dt_bias, (0, pad)).reshape(1, LANE)
    alog = jnp.pad(a_log, (0, pad)).reshape(1, LANE)
    dsk = jnp.repeat(d_skip, SSD_HEAD_DIM).reshape(1, SSD_WIDTH)
    expand = (jnp.arange(LANE)[:, None] == (jnp.arange(SSD_WIDTH)[None, :] // SSD_HEAD_DIM)).astype(F32)
    tok = lambda w: pl.BlockSpec((1, q, w), lambda bi, ci: (bi, ci, 0))
    return pl.pallas_call(
        _ssd_body, grid=(b, l // q),
        in_specs=[tok(SSD_WIDTH), tok(SSD_CONV_CH), tok(LANE), _full((SSD_CONV, SSD_CONV_CH)),
                  _full((1, SSD_CONV_CH)), _full((1, LANE)), _full((1, LANE)), _full((1, SSD_WIDTH)),
                  _full((1, SSD_WIDTH)), _full((LANE, SSD_WIDTH))],
        out_specs=tok(SSD_WIDTH),
        out_shape=jax.ShapeDtypeStruct((b, l, SSD_WIDTH), BF16),
        scratch_shapes=[pltpu.VMEM((q + 8, SSD_CONV_CH), F32),
                        pltpu.VMEM((SSD_HEADS // 2, SSD_STATE, LANE), F32)],
        compiler_params=_cparams(("parallel", "arbitrary")), name="ssd",
    )(z, xbc, dt, conv_w, conv_b.reshape(1, -1), dtb, alog, dsk, norm_w.reshape(1, -1), expand)


V_ROWS = ATT_DIM + 16


def _with_ones_row(v_t):
    n = v_t.shape[1]
    r = lax.broadcasted_iota(jnp.int32, (V_ROWS - ATT_DIM, n), 0)
    return jnp.concatenate([v_t, jnp.where(r == 0, 1.0, 0.0)], axis=0)


def _head_rms(xh, w_row):
    ms = jnp.sum(xh * xh, axis=-1, keepdims=True) * (1.0 / ATT_DIM)
    return xh * lax.rsqrt(ms + NORM_EPS) * w_row


def _mem_kv_body(mem_ref, g_ref, wk_ref, wv_ref, kw_ref, k_ref, vt_ref):
    x = mem_ref[0]
    ms = jnp.mean(x * x, axis=-1, keepdims=True)
    xn = ((x * lax.rsqrt(ms + NORM_EPS)) * g_ref[...]).astype(BF16)
    kp = jnp.dot(xn, wk_ref[...], preferred_element_type=F32)
    vp = jnp.dot(xn, wv_ref[...], preferred_element_type=F32)
    for h in range(ATT_HEADS):
        sl = slice(h * LANE, (h + 1) * LANE)
        k_ref[0, h] = _head_rms(kp[:, sl], kw_ref[...]).astype(BF16)
        vt_ref[0, h] = _with_ones_row(vp[:, sl].T[0:ATT_DIM, :]).astype(BF16)


def _mem_kv(mem, gain, wk, wv, k_norm_w):
    b = mem.shape[0]
    return pl.pallas_call(
        _mem_kv_body, grid=(b,),
        in_specs=[pl.BlockSpec((1, MEM_LEN, D_MODEL), lambda i: (i, 0, 0)), _full((1, D_MODEL)),
                  _full(wk.shape), _full(wv.shape), _full((1, LANE))],
        out_specs=[pl.BlockSpec((1, ATT_HEADS, MEM_LEN, LANE), lambda i: (i, 0, 0, 0)),
                   pl.BlockSpec((1, ATT_HEADS, V_ROWS, MEM_LEN), lambda i: (i, 0, 0, 0))],
        out_shape=[jax.ShapeDtypeStruct((b, ATT_HEADS, MEM_LEN, LANE), BF16),
                   jax.ShapeDtypeStruct((b, ATT_HEADS, V_ROWS, MEM_LEN), BF16)],
        compiler_params=_cparams(("parallel",)), name="mem_kv",
    )(mem, gain, wk, wv, k_norm_w)


MAX_BLOCKS = 32
_AUG_ONEHOT = ATT_DIM
_AUG_EXTRA = ATT_DIM + MAX_BLOCKS


def _attn_prep_body(slopes, q_ref, k_ref, v_ref, xq_ref, qw_ref, kw_ref, xqw_ref,
                    qat_ref, ka_ref, kad_ref, vt_ref, xqt_ref, kmean_scr):
    i = pl.program_id(1)
    nb = kmean_scr.shape[1]

    @pl.when(i == 0)
    def _():
        kmean_scr[...] = jnp.zeros_like(kmean_scr)

    n_iota = lax.broadcasted_iota(jnp.int32, (nb, TQ), 0)
    past = n_iota < i
    t_loc = lax.broadcasted_iota(jnp.int32, (32, TQ), 1).astype(F32)
    r32 = lax.broadcasted_iota(jnp.int32, (32, TQ), 0)
    k_lane = lax.broadcasted_iota(jnp.int32, (TQ, LANE), 1)
    s_loc = lax.broadcasted_iota(jnp.int32, (TQ, LANE), 0).astype(F32)
    blk = lax.convert_element_type(i, F32)
    for h in range(ATT_HEADS):
        hi, lo = slopes[h]
        sl = slice(h * LANE, (h + 1) * LANE)
        qn = _head_rms(q_ref[0, :, sl], qw_ref[...])
        kn = _head_rms(k_ref[0, :, sl], kw_ref[...])
        qn_t = qn.T
        gate = jnp.dot(kmean_scr[h], qn_t, precision=HIGHEST, preferred_element_type=F32)
        gate = jnp.where(past, gate, -jnp.inf)
        cnt = jnp.zeros((nb, TQ), F32)
        for n2 in range(nb):
            gn = gate[n2:n2 + 1, :]
            ahead = jnp.where(gn > gate, 1.0, jnp.where(gn == gate, jnp.where(n_iota > n2, 1.0, 0.0), 0.0))
            cnt = cnt + ahead
        allowed = jnp.where(past, jnp.where(cnt < float(MOBA_TOPK), 1.0, 0.0), 0.0)
        bias = jnp.where(allowed > 0.5, 0.0, NEG)
        extra = jnp.where(r32 < 2, t_loc, 0.0)
        for r, val in ((2, hi), (3, lo), (4, MOBA_BLOCK * hi), (5, MOBA_BLOCK * lo)):
            extra = jnp.where(r32 == r, val, extra)
        extra = jnp.where((r32 == 6) | (r32 == 7), blk, extra)
        qat = jnp.concatenate([qn_t[0:ATT_DIM, :] * (1.0 / math.sqrt(ATT_DIM)), bias, extra], axis=0)
        qat_ref[0, h] = qat.astype(BF16)
        kx = jnp.where((k_lane == _AUG_EXTRA + 2) | (k_lane == _AUG_EXTRA + 3), s_loc, 0.0)
        kx = jnp.where((k_lane == _AUG_EXTRA + 4) | (k_lane == _AUG_EXTRA + 5), blk, kx)
        for c, val in ((0, -hi), (1, -lo), (6, -MOBA_BLOCK * hi), (7, -MOBA_BLOCK * lo)):
            kx = jnp.where(k_lane == _AUG_EXTRA + c, val, kx)
        kad = kn + kx
        kad_ref[0, h] = kad.astype(BF16)
        ka_ref[0, h] = (kad + jnp.where(k_lane - _AUG_ONEHOT == i, 1.0, 0.0)).astype(BF16)
        vt_ref[0, h] = _with_ones_row(v_ref[0, :, sl].T[0:ATT_DIM, :]).astype(BF16)
        xqn = _head_rms(xq_ref[0, :, sl], xqw_ref[...])
        xqt_ref[0, h] = (xqn.T * (1.0 / math.sqrt(ATT_DIM))).astype(BF16)
        kmean_scr[h, pl.ds(i, 1), :] = jnp.sum(kn, axis=0, keepdims=True) * (1.0 / MOBA_BLOCK)


def _attn_prep(q, k, v, xq, qw, kw, xqw):
    b, l, _ = q.shape
    nb = l // MOBA_BLOCK
    assert nb <= MAX_BLOCKS
    slopes = tuple(_bf16_split(s) for s in _SLOPES)
    tok = pl.BlockSpec((1, TQ, ATT_HEADS * LANE), lambda bi, i: (bi, i, 0))
    return pl.pallas_call(
        functools.partial(_attn_prep_body, slopes), grid=(b, nb),
        in_specs=[tok, tok, tok, tok, _full((1, LANE)), _full((1, LANE)), _full((1, LANE))],
        out_specs=[pl.BlockSpec((1, ATT_HEADS, LANE, TQ), lambda bi, i: (bi, 0, 0, i)),
                   pl.BlockSpec((1, ATT_HEADS, TQ, LANE), lambda bi, i: (bi, 0, i, 0)),
                   pl.BlockSpec((1, ATT_HEADS, TQ, LANE), lambda bi, i: (bi, 0, i, 0)),
                   pl.BlockSpec((1, ATT_HEADS, V_ROWS, TQ), lambda bi, i: (bi, 0, 0, i)),
                   pl.BlockSpec((1, ATT_HEADS, LANE, TQ), lambda bi, i: (bi, 0, 0, i))],
        out_shape=[jax.ShapeDtypeStruct((b, ATT_HEADS, LANE, l), BF16),
                   jax.ShapeDtypeStruct((b, ATT_HEADS, l, LANE), BF16),
                   jax.ShapeDtypeStruct((b, ATT_HEADS, l, LANE), BF16),
                   jax.ShapeDtypeStruct((b, ATT_HEADS, V_ROWS, l), BF16),
                   jax.ShapeDtypeStruct((b, ATT_HEADS, LANE, l), BF16)],
        scratch_shapes=[pltpu.VMEM((ATT_HEADS, MAX_BLOCKS, LANE), F32)],
        compiler_params=_cparams(("parallel", "arbitrary")), name="attn_prep",
    )(q, k, v, xq, qw, kw, xqw)


def _normalized_t(acc):
    return (acc[0:ATT_DIM, :] * (1.0 / acc[ATT_DIM:ATT_DIM + 1, :])).T


def _attn_body(qat_ref, ka_ref, kad_ref, vt_ref, xqt_ref, mk_ref, mvt_ref, o_ref, om_ref, s_scr, p_scr):
    i = pl.program_id(1)
    base = pl.multiple_of(i * MOBA_BLOCK, MOBA_BLOCK)
    key = lax.broadcasted_iota(jnp.int32, (MOBA_BLOCK, TQ), 0)
    qry = lax.broadcasted_iota(jnp.int32, (MOBA_BLOCK, TQ), 1)

    span = ATT_GROUP * MOBA_BLOCK
    n_groups = (i + ATT_GROUP - 1) // ATT_GROUP
    last_group = ka_ref.shape[2] // span - 1

    def scores(h, g):
        off = pl.multiple_of(jnp.minimum(g, last_group) * span, span)
        return jnp.dot(ka_ref[0, h, pl.ds(off, span), :], qat_ref[0, h], preferred_element_type=F32)

    def values(h, g, p):
        off = pl.multiple_of(jnp.maximum(g, 0) * span, span)
        return jnp.dot(vt_ref[0, h, :, pl.ds(off, span)], p, preferred_element_type=F32)

    init = []
    for h in range(ATT_HEADS):
        s = jnp.dot(kad_ref[0, h], qat_ref[0, h], preferred_element_type=F32)
        s = jnp.where(key <= qry, s, NEG)
        m = jnp.max(s, axis=0, keepdims=True)
        p = jnp.exp(s - m).astype(BF16)
        acc = jnp.dot(vt_ref[0, h, :, pl.ds(base, MOBA_BLOCK)], p, preferred_element_type=F32)
        s_scr[h] = scores(h, 0)
        p_scr[h] = jnp.zeros((span, TQ), BF16)
        init += [jnp.ones((1, TQ), F32), m, acc]

    def body(g, carry):
        out = []
        for h in range(ATT_HEADS):
            alpha_prev, m, acc = carry[3 * h:3 * h + 3]
            acc = acc * alpha_prev + values(h, g - 1, p_scr[h])
            s = s_scr[h]
            m_new = jnp.maximum(m, jnp.max(s, axis=0, keepdims=True))
            p_scr[h] = jnp.exp(s - m_new).astype(BF16)
            s_scr[h] = scores(h, g + 1)
            out += [jnp.exp(m - m_new), m_new, acc]
        return tuple(out)

    fin = lax.fori_loop(0, n_groups, body, tuple(init))
    for h in range(ATT_HEADS):
        alpha_prev, _, acc = fin[3 * h:3 * h + 3]
        o_ref[0, h] = _normalized_t(acc * alpha_prev + values(h, n_groups - 1, p_scr[h]))
        sm = jnp.dot(mk_ref[0, h], xqt_ref[0, h], preferred_element_type=F32)
        pm = jnp.exp(sm - jnp.max(sm, axis=0, keepdims=True)).astype(BF16)
        om_ref[0, h] = _normalized_t(jnp.dot(mvt_ref[0, h], pm, preferred_element_type=F32))


def _attn(qat, ka, kad, vt, xqt, mem_k, mem_vt):
    b, _, _, l = qat.shape
    assert l % (ATT_GROUP * MOBA_BLOCK) == 0
    nh = ATT_HEADS
    per_q = lambda r: pl.BlockSpec((1, nh, r, TQ), lambda bi, i: (bi, 0, 0, i))
    out = pl.BlockSpec((1, nh, TQ, ATT_DIM), lambda bi, i: (bi, 0, i, 0))
    return pl.pallas_call(
        _attn_body, grid=(b, l // TQ),
        in_specs=[per_q(LANE),
                  pl.BlockSpec((1, nh, l, LANE), lambda bi, i: (bi, 0, 0, 0)),
                  pl.BlockSpec((1, nh, TQ, LANE), lambda bi, i: (bi, 0, i, 0)),
                  pl.BlockSpec((1, nh, V_ROWS, l), lambda bi, i: (bi, 0, 0, 0)),
                  per_q(LANE),
                  pl.BlockSpec((1, nh, MEM_LEN, LANE), lambda bi, i: (bi, 0, 0, 0)),
                  pl.BlockSpec((1, nh, V_ROWS, MEM_LEN), lambda bi, i: (bi, 0, 0, 0))],
        out_specs=[out, out],
        out_shape=[jax.ShapeDtypeStruct((b, nh, l, ATT_DIM), F32)] * 2,
        scratch_shapes=[pltpu.VMEM((nh, ATT_GROUP * MOBA_BLOCK, TQ), F32),
                        pltpu.VMEM((nh, ATT_GROUP * MOBA_BLOCK, TQ), BF16)],
        compiler_params=_cparams(("parallel", "arbitrary")), name="attn",
    )(qat, ka, kad, vt, xqt, mem_k, mem_vt)


def _cand_layout():
    slabs = [("r1", 0, 16), ("r1", 1, 8), ("r1", 2, 8), ("r1", 3, 8), ("r2", 0, 16), ("r2", 1, 8), ("r2", 2, 8)]
    pos, valid = [], []
    for kind, fixed, n in slabs:
        for j in range(n):
            r1, r2 = (fixed, j) if kind == "r1" else (j, fixed)
            ok = (r1 + 1) * (r2 + 1) <= PEER_TOPK and ((kind == "r1") or r1 >= 4)
            pos.append(r1 * PEER_TOPK + r2)
            valid.append(ok)
    assert sum(valid) == 50
    return slabs, np.asarray(pos, np.int32), np.asarray(valid)


_CAND_SLABS, _CAND_POS, _CAND_VALID = _cand_layout()
_NCAND = len(_CAND_POS)


def _split_cols(k):
    hi = k.astype(BF16)
    lo = (k - hi.astype(F32)).astype(BF16)
    return jnp.concatenate([hi, hi, lo], axis=1)


def _dot_split(k_cat, q):
    hi = q.astype(BF16)
    lo = (q - hi.astype(F32)).astype(BF16)
    return jnp.dot(k_cat, jnp.concatenate([hi, lo, hi], axis=0), preferred_element_type=F32)


def _pack_rows(x):
    return pltpu.bitcast(x.astype(BF16), jnp.uint32)


def _unpack_rows(x):
    return pltpu.bitcast(x, BF16)


def _top16_exact(sets):
    row = lax.broadcasted_iota(jnp.int32, (PEER_NKEYS, LANE), 0).astype(F32)
    for _, rank_scr, _ in sets:
        rank_scr[...] = jnp.full(rank_scr.shape, float(PEER_TOPK), F32)

    def body(r, _):
        rf = lax.convert_element_type(r, F32)
        for half in range(TM_PP // LANE):
            ls = slice(half * LANE, (half + 1) * LANE)
            for w_scr, rank_scr, v_scr in sets:
                s = w_scr[:, ls]
                m = jnp.max(s, axis=0, keepdims=True)
                hit = row == jnp.min(jnp.where(s == m, row, float(PEER_NKEYS)), axis=0, keepdims=True)
                w_scr[:, ls] = jnp.where(hit, -jnp.inf, s)
                v_scr[half, pl.ds(r, 1), :] = m
                rank_scr[:, ls] = jnp.where(hit, rf, rank_scr[:, ls])
        return 0

    lax.fori_loop(0, PEER_TOPK, body, 0)


def _top16_distinct(sets):
    chains = [(w_scr, rank_scr, v_scr, half) for half in range(TM_PP // LANE) for w_scr, rank_scr, v_scr in sets]

    def body(r, prev):
        out = []
        for (w_scr, _, v_scr, half), m_prev in zip(chains, prev):
            s = w_scr[:, half * LANE:(half + 1) * LANE]
            m = jnp.max(jnp.where(s < m_prev, s, -jnp.inf), axis=0, keepdims=True)
            v_scr[half, pl.ds(r, 1), :] = m
            out.append(m)
        return tuple(out)

    lax.fori_loop(0, PEER_TOPK, body, tuple(jnp.full((1, LANE), jnp.inf, F32) for _ in chains))
    for w_scr, rank_scr, v_scr, half in chains:
        ls = slice(half * LANE, (half + 1) * LANE)
        s = w_scr[:, ls]
        v = v_scr[half]
        rank = jnp.full(s.shape, float(PEER_TOPK), F32)
        for r in range(PEER_TOPK):
            rank = jnp.where(s == v[r:r + 1, :], float(r), rank)
        rank_scr[:, ls] = rank


def _peer_prep_body(x_ref, ys_ref, ym_ref, yx_ref, wos_ref, wom_ref, wox_ref, g_ref, wqt_ref,
                    k1_ref, k2_ref, pos_ref, cbias_ref,
                    h1_ref, hnt_ref, r2_ref, f_ref, n_ref, c_ref,
                    q_scr, w1_scr, w2_scr, rank1_scr, rank2_scr, v1_scr, v2_scr):
    hres = x_ref[...] + jnp.dot(ys_ref[...], wos_ref[...], preferred_element_type=F32)
    for h in range(ATT_HEADS):
        hres = hres + jnp.dot(ym_ref[0, h].astype(BF16), wom_ref[h], preferred_element_type=F32)
        hres = hres + jnp.dot(yx_ref[0, h].astype(BF16), wox_ref[h], preferred_element_type=F32)
    h1_ref[...] = hres
    ms = jnp.mean(hres * hres, axis=-1, keepdims=True)
    hn = (hres * lax.rsqrt(ms + NORM_EPS)) * g_ref[...]
    hn_t = hn.T.astype(BF16)
    hnt_ref[...] = hn_t
    q_scr[...] = jnp.dot(wqt_ref[...], hn_t, preferred_element_type=F32)

    pos = pos_ref[...]
    cbias = cbias_ref[...]
    r16 = lax.broadcasted_iota(jnp.int32, (PEER_TOPK, TM_PP), 0).astype(F32)
    halves = lambda scr: jnp.concatenate([scr[j] for j in range(TM_PP // LANE)], axis=1)

    def head(h, _):
        base = pl.multiple_of(h * (2 * PEER_HALF), 2 * PEER_HALF)
        s1 = _dot_split(k1_ref[...], q_scr[pl.ds(base, PEER_HALF), :])
        s2 = _dot_split(k2_ref[...], q_scr[pl.ds(base + PEER_HALF, PEER_HALF), :])
        sets = [(w1_scr, rank1_scr, v1_scr), (w2_scr, rank2_scr, v2_scr)]
        w1_scr[...] = s1
        w2_scr[...] = s2
        _top16_distinct(sets)
        ranked = sum(jnp.sum(jnp.where(scr[...] < float(PEER_TOPK), 1.0, 0.0), axis=0, keepdims=True)
                     for scr in (rank1_scr, rank2_scr))
        tied = jnp.max(jnp.abs(ranked - 2.0 * PEER_TOPK)) > 0.5

        @pl.when(tied)
        def _():
            _top16_exact(sets)

        rank1 = rank1_scr[...]
        rank2 = rank2_scr[...]
        v1 = halves(v1_scr)
        v2 = halves(v2_scr)
        parts = []
        for kind, fixed, n in _CAND_SLABS:
            if kind == "r1":
                parts.append(v1[fixed:fixed + 1, :] + v2[0:n, :])
            else:
                parts.append(v1[0:n, :] + v2[fixed:fixed + 1, :])
        cand = jnp.concatenate(parts, axis=0) + cbias
        top = v1[0:1, :] + v2[0:1, :]

        def pick(_, carry):
            cand, cnt, zsum = carry
            m = jnp.max(cand, axis=0, keepdims=True)
            p = jnp.min(jnp.where(cand == m, pos, float(4 * PEER_TOPK * PEER_TOPK)), axis=0, keepdims=True)
            cand = jnp.where(pos == p, -jnp.inf, cand)
            cnt = cnt + jnp.where(r16 == jnp.floor(p * (1.0 / PEER_TOPK)), 1.0, 0.0)
            return cand, cnt, zsum + jnp.exp(m - top)

        _, cnt, zsum = lax.fori_loop(
            0, PEER_TOPK, pick, (cand, jnp.zeros((PEER_TOPK, TM_PP), F32), jnp.zeros((1, TM_PP), F32)))
        nsel = jnp.zeros((PEER_NKEYS, TM_PP), F32)
        for r in range(PEER_TOPK):
            nsel = jnp.where(rank1 == float(r), cnt[r:r + 1, :], nsel)
        r2_ref[h] = _pack_rows(rank2)
        f_ref[h] = _pack_rows(jnp.exp(s2 - v2[0:1, :]))
        cw = jnp.exp(s1 - v1[0:1, :]) * (0.5 / zsum)
        for lc in range(TM_PP // LANE):
            n_ref[h, lc] = nsel[:, lc * LANE:(lc + 1) * LANE]
            c_ref[h, lc] = cw[:, lc * LANE:(lc + 1) * LANE]
        return 0

    lax.fori_loop(0, PEER_HEADS, head, 0)


def _peer_prep(x2d, y_ssd, y_moba, y_mem, wo_ssd, wo_moba, wo_mem, gain, wq_t, k1, k2):
    t = x2d.shape[0]
    b, _, l, _ = y_moba.shape
    per_b = l // TM_PP
    pos = jnp.asarray(np.broadcast_to(_CAND_POS.astype(np.float32)[:, None], (_NCAND, TM_PP)))
    cbias = jnp.asarray(np.broadcast_to(np.where(_CAND_VALID, 0.0, -np.inf).astype(np.float32)[:, None],
                                        (_NCAND, TM_PP)))
    tok = lambda w: pl.BlockSpec((TM_PP, w), lambda i: (i, 0))
    att = pl.BlockSpec((1, ATT_HEADS, TM_PP, ATT_DIM), lambda i: (i // per_b, 0, i % per_b, 0))
    meta = pl.BlockSpec((PEER_HEADS, TM_PP // LANE, PEER_NKEYS, LANE), lambda i: (0, i, 0, 0))
    meta_f32 = jax.ShapeDtypeStruct((PEER_HEADS, t // LANE, PEER_NKEYS, LANE), F32)
    packed = pl.BlockSpec((PEER_HEADS, PEER_NKEYS // 2, TM_PP), lambda i: (0, 0, i))
    meta_pk = jax.ShapeDtypeStruct((PEER_HEADS, PEER_NKEYS // 2, t), jnp.uint32)
    return pl.pallas_call(
        _peer_prep_body, grid=(t // TM_PP,),
        in_specs=[tok(D_MODEL), tok(SSD_WIDTH), att, att, _full(wo_ssd.shape), _full(wo_moba.shape),
                  _full(wo_mem.shape), _full((1, D_MODEL)), _full(wq_t.shape), _full(k1.shape), _full(k2.shape),
                  _full(pos.shape), _full(cbias.shape)],
        out_specs=[tok(D_MODEL), pl.BlockSpec((D_MODEL, TM_PP), lambda i: (0, i)), packed, packed, meta, meta],
        out_shape=[jax.ShapeDtypeStruct((t, D_MODEL), F32), jax.ShapeDtypeStruct((D_MODEL, t), BF16),
                   meta_pk, meta_pk, meta_f32, meta_f32],
        scratch_shapes=[pltpu.VMEM((PEER_HEADS * 2 * PEER_HALF, TM_PP), F32)]
        + [pltpu.VMEM((PEER_NKEYS, TM_PP), F32)] * 4
        + [pltpu.VMEM((TM_PP // LANE, PEER_TOPK, LANE), F32)] * 2,
        compiler_params=_cparams(("parallel",)), name="peer_prep",
    )(x2d, y_ssd, y_moba, y_mem, wo_ssd, wo_moba, wo_mem, gain, wq_t, k1, k2, pos, cbias)


_A_PER_SLAB = ES_PEER // PEER_NKEYS
_PEER_STEPS = PEER_EXPERTS // (2 * ES_PEER)


def _peer_gate_chunk(s_ref, a_ref, a_base, k, lc, r2_ref, f_ref, n_ref, c_ref):
    zero = jnp.zeros((PEER_NKEYS, LANE), BF16)

    def bcast(row):
        tile = jnp.broadcast_to(row, (16, LANE)).astype(BF16)
        return jnp.concatenate([tile] * (PEER_NKEYS // 16), axis=0)

    a = a_base + k
    rows = slice(k * PEER_NKEYS, (k + 1) * PEER_NKEYS)
    ls = slice(lc * LANE, (lc + 1) * LANE)
    s = s_ref[rows, ls]
    act = s * (1.0 + lax.erf(s * math.sqrt(0.5)))
    g = zero
    for h in range(PEER_HEADS):
        nrow = bcast(n_ref[h, lc, pl.ds(a, 1), :])
        crow = bcast(c_ref[h, lc, pl.ds(a, 1), :])
        g = g + jnp.where(_unpack_rows(r2_ref[h, :, ls]) < nrow, _unpack_rows(f_ref[h, :, ls]) * crow, zero)
    a_ref[k * (PEER_NKEYS // 2):(k + 1) * (PEER_NKEYS // 2), ls] = pltpu.bitcast(act.astype(BF16) * g, jnp.uint32)


def _peer_half_step(gate_args, matmuls):
    for mm in matmuls:
        mm()
    for lc in range(TT_PEER // LANE):
        for k in range(_A_PER_SLAB):
            _peer_gate_chunk(*gate_args[:3], k, lc, *gate_args[3:])


def _peer_body(wd_a0_ref, wd_b_ref, wd_an_ref, wut_a_ref, wut_bp_ref, hnt_ref, r2_ref, f_ref, n_ref, c_ref,
               h1_ref, o_ref, acc_scr, sa_scr, sb_scr, aa_scr, ab_scr):
    n = pl.program_id(1)
    last = pl.num_programs(1) - 1
    meta = (r2_ref, f_ref, n_ref, c_ref)

    @pl.when(n == 0)
    def _():
        acc_scr[...] = jnp.zeros_like(acc_scr)
        ab_scr[...] = jnp.zeros_like(ab_scr)
        sa_scr[...] = jnp.dot(wd_a0_ref[...], hnt_ref[...], preferred_element_type=F32)

    halves = [slice(j * (TT_PEER // 2), (j + 1) * (TT_PEER // 2)) for j in range(2)]

    def out_mm(wut_ref, a_ref, ls):
        def run():
            acc_scr[:, ls] += jnp.dot(wut_ref[...], _unpack_rows(a_ref[:, ls]), preferred_element_type=F32)
        return run

    def score_mm(wd_ref, s_ref, ls):
        def run():
            s_ref[:, ls] = jnp.dot(wd_ref[...], hnt_ref[:, ls], preferred_element_type=F32)
        return run

    @pl.when(n < last)
    def _():
        a0 = n * (2 * _A_PER_SLAB)
        _peer_half_step((sa_scr, aa_scr, a0) + meta,
                        [out_mm(wut_bp_ref, ab_scr, halves[0]), score_mm(wd_b_ref, sb_scr, halves[0]),
                         out_mm(wut_bp_ref, ab_scr, halves[1]), score_mm(wd_b_ref, sb_scr, halves[1])])
        _peer_half_step((sb_scr, ab_scr, a0 + _A_PER_SLAB) + meta,
                        [out_mm(wut_a_ref, aa_scr, halves[0]), score_mm(wd_an_ref, sa_scr, halves[0]),
                         out_mm(wut_a_ref, aa_scr, halves[1]), score_mm(wd_an_ref, sa_scr, halves[1])])

    @pl.when(n == last)
    def _():
        tail = jnp.dot(wut_bp_ref[...], _unpack_rows(ab_scr[...]), preferred_element_type=F32)
        o_ref[...] = h1_ref[...] + (acc_scr[...] + tail).T


def _peer(wd, wu_t, hn_t, rank2, f, nsel, c, h1):
    t = h1.shape[0]
    nblk = 2 * _PEER_STEPS
    meta = pl.BlockSpec((PEER_HEADS, TT_PEER // LANE, PEER_NKEYS, LANE), lambda i, n: (0, i, 0, 0))
    packed = pl.BlockSpec((PEER_HEADS, PEER_NKEYS // 2, TT_PEER), lambda i, n: (0, 0, i))
    wd_blk = lambda f: pl.BlockSpec((ES_PEER, D_MODEL), lambda i, n: (f(n), 0))
    wut_blk = lambda f: pl.BlockSpec((None, D_MODEL, ES_PEER), lambda i, n: (f(n), 0, 0))
    scores = pltpu.VMEM((ES_PEER, TT_PEER), F32)
    acts = pltpu.VMEM((ES_PEER // 2, TT_PEER), jnp.uint32)
    return pl.pallas_call(
        _peer_body, grid=(t // TT_PEER, _PEER_STEPS + 1),
        in_specs=[wd_blk(lambda n: 0),
                  wd_blk(lambda n: jnp.minimum(2 * n + 1, nblk - 1)),
                  wd_blk(lambda n: jnp.minimum(2 * n + 2, nblk - 1)),
                  wut_blk(lambda n: jnp.minimum(2 * n, nblk - 1)),
                  wut_blk(lambda n: jnp.maximum(2 * n - 1, 0)),
                  pl.BlockSpec((D_MODEL, TT_PEER), lambda i, n: (0, i)),
                  packed, packed, meta, meta,
                  pl.BlockSpec((TT_PEER, D_MODEL), lambda i, n: (i, 0))],
        out_specs=pl.BlockSpec((TT_PEER, D_MODEL), lambda i, n: (i, 0)),
        out_shape=jax.ShapeDtypeStruct((t, D_MODEL), F32),
        scratch_shapes=[pltpu.VMEM((D_MODEL, TT_PEER), F32), scores, scores, acts, acts],
        compiler_params=_cparams(("parallel", "arbitrary")), name="peer",
    )(wd, wd, wd, wu_t, wu_t, hn_t, rank2, f, nsel, c, h1)


def _pad_heads(w):
    r = w.shape[0]
    w = w.reshape(r, -1, ATT_DIM)
    return jnp.pad(w, ((0, 0), (0, 0), (0, LANE - ATT_DIM))).reshape(r, -1)


def _pad_row(w):
    return jnp.pad(w, (0, LANE - w.shape[0])).reshape(1, LANE)


def kernel(x, mem, mix_norm_w, w_in, ssd_conv_w, ssd_conv_b, ssd_dt_bias, ssd_a_log, ssd_d, ssd_norm_w,
           moba_q_norm_w, moba_k_norm_w, mem_norm_w, w_mem_kv, xattn_q_norm_w, xattn_k_norm_w, w_out,
           ffn_norm_w, peer_w_query, peer_sub_keys_1, peer_sub_keys_2, peer_expert_down, peer_expert_up):
    b, l, d = x.shape
    depth = w_in.shape[0]
    h = x.reshape(b * l, d)
    for li in range(depth):
        wi = w_in[li]
        o = 0
        cols = {}
        for name, width in (("z", SSD_WIDTH), ("xbc", SSD_CONV_CH), ("dt", SSD_HEADS), ("mq", ATT_WIDTH),
                            ("mk", ATT_WIDTH), ("mv", ATT_WIDTH), ("xq", ATT_WIDTH)):
            cols[name] = wi[:, o:o + width]
            o += width
        w_list = [cols["z"], cols["xbc"], jnp.pad(cols["dt"], ((0, 0), (0, LANE - SSD_HEADS))),
                  _pad_heads(cols["mq"]), _pad_heads(cols["mk"]), _pad_heads(cols["mv"]), _pad_heads(cols["xq"])]
        w_list = [w.astype(BF16) for w in w_list]
        z, xbc, dt, mq, mk, mv, xq = _in_proj(h, mix_norm_w[li].reshape(1, d), w_list)

        y_ssd = _ssd(z.reshape(b, l, -1), xbc.reshape(b, l, -1), dt.reshape(b, l, -1), ssd_conv_w[li],
                     ssd_conv_b[li], ssd_dt_bias[li], ssd_a_log[li], ssd_d[li], ssd_norm_w[li])

        wkv = w_mem_kv[li]
        mem_k, mem_vt = _mem_kv(mem, mem_norm_w[li].reshape(1, d), _pad_heads(wkv[:, :ATT_WIDTH]).astype(BF16),
                                _pad_heads(wkv[:, ATT_WIDTH:]).astype(BF16), _pad_row(xattn_k_norm_w[li]))
        r3 = lambda a: a.reshape(b, l, -1)
        qat, ka, kad, vt, xqt = _attn_prep(r3(mq), r3(mk), r3(mv), r3(xq), _pad_row(moba_q_norm_w[li]),
                                           _pad_row(moba_k_norm_w[li]), _pad_row(xattn_q_norm_w[li]))
        y_moba, y_mem = _attn(qat, ka, kad, vt, xqt, mem_k, mem_vt)

        wo = w_out[li].astype(BF16)
        wo_moba = wo[SSD_WIDTH:SSD_WIDTH + ATT_WIDTH].reshape(ATT_HEADS, ATT_DIM, d)
        wo_mem = wo[SSD_WIDTH + ATT_WIDTH:].reshape(ATT_HEADS, ATT_DIM, d)
        h1, hn_t, rank2, f, nsel, c = _peer_prep(
            h, y_ssd.reshape(b * l, -1), y_moba, y_mem, wo[:SSD_WIDTH], wo_moba, wo_mem,
            ffn_norm_w[li].reshape(1, d), peer_w_query[li].T.astype(BF16), _split_cols(peer_sub_keys_1[li]),
            _split_cols(peer_sub_keys_2[li]))
        wu_t = peer_expert_up[li].reshape(-1, ES_PEER, d).transpose(0, 2, 1).astype(BF16)
        h = _peer(peer_expert_down[li].astype(BF16), wu_t, hn_t, rank2, f, nsel, c, h1)
    return h.reshape(b, l, d)
```

```python
import functools
import math

import numpy as np
import jax
import jax.numpy as jnp
from jax import lax
from jax.experimental import pallas as pl
from jax.experimental.pallas import tpu as pltpu

F32 = jnp.float32
BF16 = jnp.bfloat16
HIGHEST = lax.Precision.HIGHEST

NORM_EPS = 1e-6
D_MODEL = 1024
SSD_HEADS = 8
SSD_HEAD_DIM = 64
SSD_WIDTH = 512
SSD_GROUPS = 2
SSD_STATE = 128
SSD_CONV = 4
SSD_CONV_CH = 1024
ATT_HEADS = 4
ATT_DIM = 64
ATT_WIDTH = 256
MOBA_BLOCK = 256
MOBA_TOPK = 3
MEM_LEN = 256
PEER_HEADS = 8
PEER_NKEYS = 128
PEER_TOPK = 16
PEER_HALF = 64
PEER_EXPERTS = PEER_NKEYS * PEER_NKEYS

LANE = 128
NEG = -1e30
VMEM_LIMIT = 56 * 1024 * 1024

TM_IN = 512
SSD_CHUNK = 256
TQ = MOBA_BLOCK
ATT_GROUP = 4
TM_PP = 512
TT_PEER = 512
ES_PEER = 512
PEER_SLABS = 4

_SLOPES = [2.0 ** (-8.0 * (i + 1) / ATT_HEADS) for i in range(ATT_HEADS)]


def _bf16_split(v):
    hi = float(np.float32(v).astype(BF16).astype(np.float32))
    lo = float(np.float32(v - hi).astype(BF16).astype(np.float32))
    return hi, lo


def _cparams(sem):
    return pltpu.CompilerParams(dimension_semantics=sem, vmem_limit_bytes=VMEM_LIMIT)


def _sigmoid(x):
    return 1.0 / (1.0 + jnp.exp(-x))


def _full(shape):
    n = len(shape)
    return pl.BlockSpec(shape, lambda *_: (0,) * n)


def _in_proj_body(x_ref, g_ref, *refs):
    n = len(refs) // 2
    w_refs, o_refs = refs[:n], refs[n:]
    x = x_ref[...]
    ms = jnp.mean(x * x, axis=-1, keepdims=True)
    xn = ((x * lax.rsqrt(ms + NORM_EPS)) * g_ref[...]).astype(BF16)
    for w_ref, o_ref in zip(w_refs, o_refs):
        o_ref[...] = jnp.dot(xn, w_ref[...], preferred_element_type=F32).astype(o_ref.dtype)


def _in_proj(x2d, gain, weights):
    t = x2d.shape[0]
    in_specs = [pl.BlockSpec((TM_IN, D_MODEL), lambda i: (i, 0)), _full((1, D_MODEL))]
    in_specs += [_full(w.shape) for w in weights]
    out_specs = [pl.BlockSpec((TM_IN, w.shape[1]), lambda i: (i, 0)) for w in weights]
    out_shape = [jax.ShapeDtypeStruct((t, w.shape[1]), F32) for w in weights]
    return pl.pallas_call(
        _in_proj_body, grid=(t // TM_IN,), in_specs=in_specs, out_specs=out_specs,
        out_shape=out_shape, compiler_params=_cparams(("parallel",)), name="in_proj",
    )(x2d, gain, *weights)


def _ssd_body(z_ref, xbc_ref, dt_ref, cw_ref, cb_ref, dtb_ref, alog_ref, dskip_ref, nw_ref, e_ref,
              y_ref, ext_scr, state_scr):
    q = SSD_CHUNK
    c = pl.program_id(1)

    @pl.when(c == 0)
    def _():
        ext_scr[0:8, :] = jnp.zeros((8, SSD_CONV_CH), F32)
        state_scr[...] = jnp.zeros_like(state_scr)

    u = xbc_ref[0]
    ext_scr[8:8 + q, :] = u
    acc = cb_ref[...] + cw_ref[3:4, :] * u
    acc = acc + cw_ref[2:3, :] * ext_scr[7:7 + q, :]
    acc = acc + cw_ref[1:2, :] * ext_scr[6:6 + q, :]
    acc = acc + cw_ref[0:1, :] * ext_scr[5:5 + q, :]
    ext_scr[0:8, :] = u[q - 8:q, :]
    act = acc * _sigmoid(acc)
    xs = act[:, 0:SSD_WIDTH]
    bm = act[:, SSD_WIDTH:SSD_WIDTH + SSD_GROUPS * SSD_STATE]
    cm = act[:, SSD_WIDTH + SSD_GROUPS * SSD_STATE:]

    dtr = dt_ref[0] + dtb_ref[...]
    dt = jnp.maximum(dtr, 0.0) + jnp.log(1.0 + jnp.exp(-jnp.abs(dtr)))
    a = dt * (-jnp.exp(alog_ref[...]))
    row = lax.broadcasted_iota(jnp.int32, (q, q), 0)
    col = lax.broadcasted_iota(jnp.int32, (q, q), 1)
    causal = row >= col
    a_cs = jnp.dot(causal.astype(F32), a, precision=HIGHEST, preferred_element_type=F32)
    a_cs_t = a_cs.T
    a_cs_w = jnp.dot(a_cs, e_ref[...], precision=HIGHEST, preferred_element_type=F32)
    dt_w = jnp.dot(dt, e_ref[...], precision=HIGHEST, preferred_element_type=F32)
    total_w = a_cs_w[q - 1:q, :]
    exp_cs_w = jnp.exp(a_cs_w)
    dte_w = jnp.exp(total_w - a_cs_w)
    cd_w = jnp.exp(total_w)
    xdt_w = xs * dt_w
    lane = lax.broadcasted_iota(jnp.int32, (1, LANE), 1)
    first = lane < SSD_HEAD_DIM

    z = z_ref[0]
    gated = []
    for g in range(SSD_GROUPS):
        bg = bm[:, g * SSD_STATE:(g + 1) * SSD_STATE]
        cg = cm[:, g * SSD_STATE:(g + 1) * SSD_STATE].astype(BF16)
        cb = lax.dot_general(cg, bg.astype(BF16), (((1,), (1,)), ((), ())), preferred_element_type=F32)
        bg_t = bg.T.astype(BF16)
        for kk in range(2):
            k = 2 * g + kk
            sl = slice(k * LANE, (k + 1) * LANE)
            xdt = xdt_w[:, sl]
            xdt_b = xdt.astype(BF16)
            yd = []
            for hh in range(2):
                h = 2 * k + hh
                seg = a_cs[:, h:h + 1] - a_cs_t[h:h + 1, :]
                lm = jnp.exp(jnp.where(causal, seg, -jnp.inf))
                yd.append(jnp.dot((cb * lm).astype(BF16), xdt_b, preferred_element_type=F32))
            y = jnp.where(first, yd[0], yd[1])
            s_old = state_scr[k]
            y = y + jnp.dot(cg, s_old.astype(BF16), preferred_element_type=F32) * exp_cs_w[:, sl]
            y = y + xs[:, sl] * dskip_ref[:, sl]
            state_scr[k] = s_old * cd_w[:, sl] + jnp.dot(
                bg_t, (xdt * dte_w[:, sl]).astype(BF16), preferred_element_type=F32)
            zz = z[:, sl]
            gated.append(y * (zz * _sigmoid(zz)))
    for g in range(SSD_GROUPS):
        y0, y1 = gated[2 * g], gated[2 * g + 1]
        ms = (jnp.sum(y0 * y0, axis=-1, keepdims=True)
              + jnp.sum(y1 * y1, axis=-1, keepdims=True)) * (1.0 / (SSD_WIDTH // SSD_GROUPS))
        r = lax.rsqrt(ms + NORM_EPS)
        lo = 2 * g * LANE
        y_ref[0, :, lo:lo + LANE] = (y0 * r * nw_ref[:, lo:lo + LANE]).astype(y_ref.dtype)
        y_ref[0, :, lo + LANE:lo + 2 * LANE] = (y1 * r * nw_ref[:, lo + LANE:lo + 2 * LANE]).astype(y_ref.dtype)


def _ssd(z, xbc, dt, conv_w, conv_b, dt_bias, a_log, d_skip, norm_w):
    b, l, _ = z.shape
    q = SSD_CHUNK
    pad = LANE - SSD_HEADS
    dtb = jnp.pad(dt_bias, (0, pad)).reshape(1, LANE)
    alog = jnp.pad(a_log, (0, pad)).reshape(1, LANE)
    dsk = jnp.repeat(d_skip, SSD_HEAD_DIM).reshape(1, SSD_WIDTH)
    expand = (jnp.arange(LANE)[:, None] == (jnp.arange(SSD_WIDTH)[None, :] // SSD_HEAD_DIM)).astype(F32)
    tok = lambda w: pl.BlockSpec((1, q, w), lambda bi, ci: (bi, ci, 0))
    return pl.pallas_call(
        _ssd_body, grid=(b, l // q),
        in_specs=[tok(SSD_WIDTH), tok(SSD_CONV_CH), tok(LANE), _full((SSD_CONV, SSD_CONV_CH)),
                  _full((1, SSD_CONV_CH)), _full((1, LANE)), _full((1, LANE)), _full((1, SSD_WIDTH)),
                  _full((1, SSD_WIDTH)), _full((LANE, SSD_WIDTH))],
        out_specs=tok(SSD_WIDTH),
        out_shape=jax.ShapeDtypeStruct((b, l, SSD_WIDTH), BF16),
        scratch_shapes=[pltpu.VMEM((q + 8, SSD_CONV_CH), F32),
                        pltpu.VMEM((SSD_HEADS // 2, SSD_STATE, LANE), F32)],
        compiler_params=_cparams(("parallel", "arbitrary")), name="ssd",
    )(z, xbc, dt, conv_w, conv_b.reshape(1, -1), dtb, alog, dsk, norm_w.reshape(1, -1), expand)


V_ROWS = ATT_DIM + 16


def _with_ones_row(v_t):
    n = v_t.shape[1]
    r = lax.broadcasted_iota(jnp.int32, (V_ROWS - ATT_DIM, n), 0)
    return jnp.concatenate([v_t, jnp.where(r == 0, 1.0, 0.0)], axis=0)


def _head_rms(xh, w_row):
    ms = jnp.sum(xh * xh, axis=-1, keepdims=True) * (1.0 / ATT_DIM)
    return xh * lax.rsqrt(ms + NORM_EPS) * w_row


def _mem_kv_body(mem_ref, g_ref, wk_ref, wv_ref, kw_ref, k_ref, vt_ref):
    x = mem_ref[0]
    ms = jnp.mean(x * x, axis=-1, keepdims=True)
    xn = ((x * lax.rsqrt(ms + NORM_EPS)) * g_ref[...]).astype(BF16)
    kp = jnp.dot(xn, wk_ref[...], preferred_element_type=F32)
    vp = jnp.dot(xn, wv_ref[...], preferred_element_type=F32)
    for h in range(ATT_HEADS):
        sl = slice(h * LANE, (h + 1) * LANE)
        k_ref[0, h] = _head_rms(kp[:, sl], kw_ref[...]).astype(BF16)
        vt_ref[0, h] = _with_ones_row(vp[:, sl].T[0:ATT_DIM, :]).astype(BF16)


def _mem_kv(mem, gain, wk, wv, k_norm_w):
    b = mem.shape[0]
    return pl.pallas_call(
        _mem_kv_body, grid=(b,),
        in_specs=[pl.BlockSpec((1, MEM_LEN, D_MODEL), lambda i: (i, 0, 0)), _full((1, D_MODEL)),
                  _full(wk.shape), _full(wv.shape), _full((1, LANE))],
        out_specs=[pl.BlockSpec((1, ATT_HEADS, MEM_LEN, LANE), lambda i: (i, 0, 0, 0)),
                   pl.BlockSpec((1, ATT_HEADS, V_ROWS, MEM_LEN), lambda i: (i, 0, 0, 0))],
        out_shape=[jax.ShapeDtypeStruct((b, ATT_HEADS, MEM_LEN, LANE), BF16),
                   jax.ShapeDtypeStruct((b, ATT_HEADS, V_ROWS, MEM_LEN), BF16)],
        compiler_params=_cparams(("parallel",)), name="mem_kv",
    )(mem, gain, wk, wv, k_norm_w)


MAX_BLOCKS = 32
_AUG_ONEHOT = ATT_DIM
_AUG_EXTRA = ATT_DIM + MAX_BLOCKS


def _attn_prep_body(slopes, q_ref, k_ref, v_ref, xq_ref, qw_ref, kw_ref, xqw_ref,
                    qat_ref, ka_ref, kad_ref, vt_ref, xqt_ref, kmean_scr):
    i = pl.program_id(1)
    nb = kmean_scr.shape[1]

    @pl.when(i == 0)
    def _():
        kmean_scr[...] = jnp.zeros_like(kmean_scr)

    n_iota = lax.broadcasted_iota(jnp.int32, (nb, TQ), 0)
    past = n_iota < i
    t_loc = lax.broadcasted_iota(jnp.int32, (32, TQ), 1).astype(F32)
    r32 = lax.broadcasted_iota(jnp.int32, (32, TQ), 0)
    k_lane = lax.broadcasted_iota(jnp.int32, (TQ, LANE), 1)
    s_loc = lax.broadcasted_iota(jnp.int32, (TQ, LANE), 0).astype(F32)
    blk = lax.convert_element_type(i, F32)
    for h in range(ATT_HEADS):
        hi, lo = slopes[h]
        sl = slice(h * LANE, (h + 1) * LANE)
        qn = _head_rms(q_ref[0, :, sl], qw_ref[...])
        kn = _head_rms(k_ref[0, :, sl], kw_ref[...])
        qn_t = qn.T
        gate = jnp.dot(kmean_scr[h], qn_t, precision=HIGHEST, preferred_element_type=F32)
        gate = jnp.where(past, gate, -jnp.inf)
        cnt = jnp.zeros((nb, TQ), F32)
        for n2 in range(nb):
            gn = gate[n2:n2 + 1, :]
            ahead = jnp.where(gn > gate, 1.0, jnp.where(gn == gate, jnp.where(n_iota > n2, 1.0, 0.0), 0.0))
            cnt = cnt + ahead
        allowed = jnp.where(past, jnp.where(cnt < float(MOBA_TOPK), 1.0, 0.0), 0.0)
        bias = jnp.where(allowed > 0.5, 0.0, NEG)
        extra = jnp.where(r32 < 2, t_loc, 0.0)
        for r, val in ((2, hi), (3, lo), (4, MOBA_BLOCK * hi), (5, MOBA_BLOCK * lo)):
            extra = jnp.where(r32 == r, val, extra)
        extra = jnp.where((r32 == 6) | (r32 == 7), blk, extra)
        qat = jnp.concatenate([qn_t[0:ATT_DIM, :] * (1.0 / math.sqrt(ATT_DIM)), bias, extra], axis=0)
        qat_ref[0, h] = qat.astype(BF16)
        kx = jnp.where((k_lane == _AUG_EXTRA + 2) | (k_lane == _AUG_EXTRA + 3), s_loc, 0.0)
        kx = jnp.where((k_lane == _AUG_EXTRA + 4) | (k_lane == _AUG_EXTRA + 5), blk, kx)
        for c, val in ((0, -hi), (1, -lo), (6, -MOBA_BLOCK * hi), (7, -MOBA_BLOCK * lo)):
            kx = jnp.where(k_lane == _AUG_EXTRA + c, val, kx)
        kad = kn + kx
        kad_ref[0, h] = kad.astype(BF16)
        ka_ref[0, h] = (kad + jnp.where(k_lane - _AUG_ONEHOT == i, 1.0, 0.0)).astype(BF16)
        vt_ref[0, h] = _with_ones_row(v_ref[0, :, sl].T[0:ATT_DIM, :]).astype(BF16)
        xqn = _head_rms(xq_ref[0, :, sl], xqw_ref[...])
        xqt_ref[0, h] = (xqn.T * (1.0 / math.sqrt(ATT_DIM))).astype(BF16)
        kmean_scr[h, pl.ds(i, 1), :] = jnp.sum(kn, axis=0, keepdims=True) * (1.0 / MOBA_BLOCK)


def _attn_prep(q, k, v, xq, qw, kw, xqw):
    b, l, _ = q.shape
    nb = l // MOBA_BLOCK
    assert nb <= MAX_BLOCKS
    slopes = tuple(_bf16_split(s) for s in _SLOPES)
    tok = pl.BlockSpec((1, TQ, ATT_HEADS * LANE), lambda bi, i: (bi, i, 0))
    return pl.pallas_call(
        functools.partial(_attn_prep_body, slopes), grid=(b, nb),
        in_specs=[tok, tok, tok, tok, _full((1, LANE)), _full((1, LANE)), _full((1, LANE))],
        out_specs=[pl.BlockSpec((1, ATT_HEADS, LANE, TQ), lambda bi, i: (bi, 0, 0, i)),
                   pl.BlockSpec((1, ATT_HEADS, TQ, LANE), lambda bi, i: (bi, 0, i, 0)),
                   pl.BlockSpec((1, ATT_HEADS, TQ, LANE), lambda bi, i: (bi, 0, i, 0)),
                   pl.BlockSpec((1, ATT_HEADS, V_ROWS, TQ), lambda bi, i: (bi, 0, 0, i)),
                   pl.BlockSpec((1, ATT_HEADS, LANE, TQ), lambda bi, i: (bi, 0, 0, i))],
        out_shape=[jax.ShapeDtypeStruct((b, ATT_HEADS, LANE, l), BF16),
                   jax.ShapeDtypeStruct((b, ATT_HEADS, l, LANE), BF16),
                   jax.ShapeDtypeStruct((b, ATT_HEADS, l, LANE), BF16),
                   jax.ShapeDtypeStruct((b, ATT_HEADS, V_ROWS, l), BF16),
                   jax.ShapeDtypeStruct((b, ATT_HEADS, LANE, l), BF16)],
        scratch_shapes=[pltpu.VMEM((ATT_HEADS, MAX_BLOCKS, LANE), F32)],
        compiler_params=_cparams(("parallel", "arbitrary")), name="attn_prep",
    )(q, k, v, xq, qw, kw, xqw)


def _normalized_t(acc):
    return (acc[0:ATT_DIM, :] * (1.0 / acc[ATT_DIM:ATT_DIM + 1, :])).T


def _attn_body(qat_ref, ka_ref, kad_ref, vt_ref, xqt_ref, mk_ref, mvt_ref, o_ref, om_ref, s_scr, p_scr):
    i = pl.program_id(1)
    base = pl.multiple_of(i * MOBA_BLOCK, MOBA_BLOCK)
    key = lax.broadcasted_iota(jnp.int32, (MOBA_BLOCK, TQ), 0)
    qry = lax.broadcasted_iota(jnp.int32, (MOBA_BLOCK, TQ), 1)

    span = ATT_GROUP * MOBA_BLOCK
    n_groups = (i + ATT_GROUP - 1) // ATT_GROUP
    last_group = ka_ref.shape[2] // span - 1

    def scores(h, g):
        off = pl.multiple_of(jnp.minimum(g, last_group) * span, span)
        return jnp.dot(ka_ref[0, h, pl.ds(off, span), :], qat_ref[0, h], preferred_element_type=F32)

    def values(h, g, p):
        off = pl.multiple_of(jnp.maximum(g, 0) * span, span)
        return jnp.dot(vt_ref[0, h, :, pl.ds(off, span)], p, preferred_element_type=F32)

    init = []
    for h in range(ATT_HEADS):
        s = jnp.dot(kad_ref[0, h], qat_ref[0, h], preferred_element_type=F32)
        s = jnp.where(key <= qry, s, NEG)
        m = jnp.max(s, axis=0, keepdims=True)
        p = jnp.exp(s - m).astype(BF16)
        acc = jnp.dot(vt_ref[0, h, :, pl.ds(base, MOBA_BLOCK)], p, preferred_element_type=F32)
        s_scr[h] = scores(h, 0)
        p_scr[h] = jnp.zeros((span, TQ), BF16)
        init += [jnp.ones((1, TQ), F32), m, acc]

    def body(g, carry):
        out = []
        for h in range(ATT_HEADS):
            alpha_prev, m, acc = carry[3 * h:3 * h + 3]
            acc = acc * alpha_prev + values(h, g - 1, p_scr[h])
            s = s_scr[h]
            m_new = jnp.maximum(m, jnp.max(s, axis=0, keepdims=True))
            p_scr[h] = jnp.exp(s - m_new).astype(BF16)
            s_scr[h] = scores(h, g + 1)
            out += [jnp.exp(m - m_new), m_new, acc]
        return tuple(out)

    fin = lax.fori_loop(0, n_groups, body, tuple(init))
    for h in range(ATT_HEADS):
        alpha_prev, _, acc = fin[3 * h:3 * h + 3]
        o_ref[0, h] = _normalized_t(acc * alpha_prev + values(h, n_groups - 1, p_scr[h]))
        sm = jnp.dot(mk_ref[0, h], xqt_ref[0, h], preferred_element_type=F32)
        pm = jnp.exp(sm - jnp.max(sm, axis=0, keepdims=True)).astype(BF16)
        om_ref[0, h] = _normalized_t(jnp.dot(mvt_ref[0, h], pm, preferred_element_type=F32))


def _attn(qat, ka, kad, vt, xqt, mem_k, mem_vt):
    b, _, _, l = qat.shape
    assert l % (ATT_GROUP * MOBA_BLOCK) == 0
    nh = ATT_HEADS
    per_q = lambda r: pl.BlockSpec((1, nh, r, TQ), lambda bi, i: (bi, 0, 0, i))
    out = pl.BlockSpec((1, nh, TQ, ATT_DIM), lambda bi, i: (bi, 0, i, 0))
    return pl.pallas_call(
        _attn_body, grid=(b, l // TQ),
        in_specs=[per_q(LANE),
                  pl.BlockSpec((1, nh, l, LANE), lambda bi, i: (bi, 0, 0, 0)),
                  pl.BlockSpec((1, nh, TQ, LANE), lambda bi, i: (bi, 0, i, 0)),
                  pl.BlockSpec((1, nh, V_ROWS, l), lambda bi, i: (bi, 0, 0, 0)),
                  per_q(LANE),
                  pl.BlockSpec((1, nh, MEM_LEN, LANE), lambda bi, i: (bi, 0, 0, 0)),
                  pl.BlockSpec((1, nh, V_ROWS, MEM_LEN), lambda bi, i: (bi, 0, 0, 0))],
        out_specs=[out, out],
        out_shape=[jax.ShapeDtypeStruct((b, nh, l, ATT_DIM), F32)] * 2,
        scratch_shapes=[pltpu.VMEM((nh, ATT_GROUP * MOBA_BLOCK, TQ), F32),
                        pltpu.VMEM((nh, ATT_GROUP * MOBA_BLOCK, TQ), BF16)],
        compiler_params=_cparams(("parallel", "arbitrary")), name="attn",
    )(qat, ka, kad, vt, xqt, mem_k, mem_vt)


def _cand_layout():
    slabs = [("r1", 0, 16), ("r1", 1, 8), ("r1", 2, 8), ("r1", 3, 8), ("r2", 0, 16), ("r2", 1, 8), ("r2", 2, 8)]
    pos, valid = [], []
    for kind, fixed, n in slabs:
        for j in range(n):
            r1, r2 = (fixed, j) if kind == "r1" else (j, fixed)
            ok = (r1 + 1) * (r2 + 1) <= PEER_TOPK and ((kind == "r1") or r1 >= 4)
            pos.append(r1 * PEER_TOPK + r2)
            valid.append(ok)
    assert sum(valid) == 50
    return slabs, np.asarray(pos, np.int32), np.asarray(valid)


_CAND_SLABS, _CAND_POS, _CAND_VALID = _cand_layout()
_NCAND = len(_CAND_POS)


def _split_cols(k):
    hi = k.astype(BF16)
    lo = (k - hi.astype(F32)).astype(BF16)
    return jnp.concatenate([hi, hi, lo], axis=1)


def _dot_split(k_cat, q):
    hi = q.astype(BF16)
    lo = (q - hi.astype(F32)).astype(BF16)
    return jnp.dot(k_cat, jnp.concatenate([hi, lo, hi], axis=0), preferred_element_type=F32)


def _pack_rows(x):
    return pltpu.bitcast(x.astype(BF16), jnp.uint32)


def _unpack_rows(x):
    return pltpu.bitcast(x, BF16)


def _top16_exact(sets):
    row = lax.broadcasted_iota(jnp.int32, (PEER_NKEYS, LANE), 0).astype(F32)
    for _, rank_scr, _ in sets:
        rank_scr[...] = jnp.full(rank_scr.shape, float(PEER_TOPK), F32)

    def body(r, _):
        rf = lax.convert_element_type(r, F32)
        for half in range(TM_PP // LANE):
            ls = slice(half * LANE, (half + 1) * LANE)
            for w_scr, rank_scr, v_scr in sets:
                s = w_scr[:, ls]
                m = jnp.max(s, axis=0, keepdims=True)
                hit = row == jnp.min(jnp.where(s == m, row, float(PEER_NKEYS)), axis=0, keepdims=True)
                w_scr[:, ls] = jnp.where(hit, -jnp.inf, s)
                v_scr[half, pl.ds(r, 1), :] = m
                rank_scr[:, ls] = jnp.where(hit, rf, rank_scr[:, ls])
        return 0

    lax.fori_loop(0, PEER_TOPK, body, 0)


def _top16_distinct(sets):
    chains = [(w_scr, rank_scr, v_scr, half) for half in range(TM_PP // LANE) for w_scr, rank_scr, v_scr in sets]

    def body(r, prev):
        out = []
        for (w_scr, _, v_scr, half), m_prev in zip(chains, prev):
            s = w_scr[:, half * LANE:(half + 1) * LANE]
            m = jnp.max(jnp.where(s < m_prev, s, -jnp.inf), axis=0, keepdims=True)
            v_scr[half, pl.ds(r, 1), :] = m
            out.append(m)
        return tuple(out)

    lax.fori_loop(0, PEER_TOPK, body, tuple(jnp.full((1, LANE), jnp.inf, F32) for _ in chains))
    for w_scr, rank_scr, v_scr, half in chains:
        ls = slice(half * LANE, (half + 1) * LANE)
        s = w_scr[:, ls]
        v = v_scr[half]
        rank = jnp.full(s.shape, float(PEER_TOPK), F32)
        for r in range(PEER_TOPK):
            rank = jnp.where(s == v[r:r + 1, :], float(r), rank)
        rank_scr[:, ls] = rank


def _peer_prep_body(x_ref, ys_ref, ym_ref, yx_ref, wos_ref, wom_ref, wox_ref, g_ref, wqt_ref,
                    k1_ref, k2_ref, pos_ref, cbias_ref,
                    h1_ref, hnt_ref, r2_ref, f_ref, n_ref, c_ref,
                    q_scr, w1_scr, w2_scr, rank1_scr, rank2_scr, v1_scr, v2_scr):
    hres = x_ref[...] + jnp.dot(ys_ref[...], wos_ref[...], preferred_element_type=F32)
    for h in range(ATT_HEADS):
        hres = hres + jnp.dot(ym_ref[0, h].astype(BF16), wom_ref[h], preferred_element_type=F32)
        hres = hres + jnp.dot(yx_ref[0, h].astype(BF16), wox_ref[h], preferred_element_type=F32)
    h1_ref[...] = hres
    ms = jnp.mean(hres * hres, axis=-1, keepdims=True)
    hn = (hres * lax.rsqrt(ms + NORM_EPS)) * g_ref[...]
    hn_t = hn.T.astype(BF16)
    hnt_ref[...] = hn_t
    q_scr[...] = jnp.dot(wqt_ref[...], hn_t, preferred_element_type=F32)

    pos = pos_ref[...]
    cbias = cbias_ref[...]
    r16 = lax.broadcasted_iota(jnp.int32, (PEER_TOPK, TM_PP), 0).astype(F32)
    halves = lambda scr: jnp.concatenate([scr[j] for j in range(TM_PP // LANE)], axis=1)

    def head(h, _):
        base = pl.multiple_of(h * (2 * PEER_HALF), 2 * PEER_HALF)
        s1 = _dot_split(k1_ref[...], q_scr[pl.ds(base, PEER_HALF), :])
        s2 = _dot_split(k2_ref[...], q_scr[pl.ds(base + PEER_HALF, PEER_HALF), :])
        sets = [(w1_scr, rank1_scr, v1_scr), (w2_scr, rank2_scr, v2_scr)]
        w1_scr[...] = s1
        w2_scr[...] = s2
        _top16_distinct(sets)
        ranked = sum(jnp.sum(jnp.where(scr[...] < float(PEER_TOPK), 1.0, 0.0), axis=0, keepdims=True)
                     for scr in (rank1_scr, rank2_scr))
        tied = jnp.max(jnp.abs(ranked - 2.0 * PEER_TOPK)) > 0.5

        @pl.when(tied)
        def _():
            _top16_exact(sets)

        rank1 = rank1_scr[...]
        rank2 = rank2_scr[...]
        v1 = halves(v1_scr)
        v2 = halves(v2_scr)
        parts = []
        for kind, fixed, n in _CAND_SLABS:
            if kind == "r1":
                parts.append(v1[fixed:fixed + 1, :] + v2[0:n, :])
            else:
                parts.append(v1[0:n, :] + v2[fixed:fixed + 1, :])
        cand = jnp.concatenate(parts, axis=0) + cbias
        top = v1[0:1, :] + v2[0:1, :]

        def pick(_, carry):
            cand, cnt, zsum = carry
            m = jnp.max(cand, axis=0, keepdims=True)
            p = jnp.min(jnp.where(cand == m, pos, float(4 * PEER_TOPK * PEER_TOPK)), axis=0, keepdims=True)
            cand = jnp.where(pos == p, -jnp.inf, cand)
            cnt = cnt + jnp.where(r16 == jnp.floor(p * (1.0 / PEER_TOPK)), 1.0, 0.0)
            return cand, cnt, zsum + jnp.exp(m - top)

        _, cnt, zsum = lax.fori_loop(
            0, PEER_TOPK, pick, (cand, jnp.zeros((PEER_TOPK, TM_PP), F32), jnp.zeros((1, TM_PP), F32)))
        nsel = jnp.zeros((PEER_NKEYS, TM_PP), F32)
        for r in range(PEER_TOPK):
            nsel = jnp.where(rank1 == float(r), cnt[r:r + 1, :], nsel)
        r2_ref[h] = _pack_rows(rank2)
        f_ref[h] = _pack_rows(jnp.exp(s2 - v2[0:1, :]))
        cw = jnp.exp(s1 - v1[0:1, :]) * (0.5 / zsum)
        for lc in range(TM_PP // LANE):
            n_ref[h, lc] = nsel[:, lc * LANE:(lc + 1) * LANE]
            c_ref[h, lc] = cw[:, lc * LANE:(lc + 1) * LANE]
        return 0

    lax.fori_loop(0, PEER_HEADS, head, 0)


def _peer_prep(x2d, y_ssd, y_moba, y_mem, wo_ssd, wo_moba, wo_mem, gain, wq_t, k1, k2):
    t = x2d.shape[0]
    b, _, l, _ = y_moba.shape
    per_b = l // TM_PP
    pos = jnp.asarray(np.broadcast_to(_CAND_POS.astype(np.float32)[:, None], (_NCAND, TM_PP)))
    cbias = jnp.asarray(np.broadcast_to(np.where(_CAND_VALID, 0.0, -np.inf).astype(np.float32)[:, None],
                                        (_NCAND, TM_PP)))
    tok = lambda w: pl.BlockSpec((TM_PP, w), lambda i: (i, 0))
    att = pl.BlockSpec((1, ATT_HEADS, TM_PP, ATT_DIM), lambda i: (i // per_b, 0, i % per_b, 0))
    meta = pl.BlockSpec((PEER_HEADS, TM_PP // LANE, PEER_NKEYS, LANE), lambda i: (0, i, 0, 0))
    meta_f32 = jax.ShapeDtypeStruct((PEER_HEADS, t // LANE, PEER_NKEYS, LANE), F32)
    packed = pl.BlockSpec((PEER_HEADS, PEER_NKEYS // 2, TM_PP), lambda i: (0, 0, i))
    meta_pk = jax.ShapeDtypeStruct((PEER_HEADS, PEER_NKEYS // 2, t), jnp.uint32)
    return pl.pallas_call(
        _peer_prep_body, grid=(t // TM_PP,),
        in_specs=[tok(D_MODEL), tok(SSD_WIDTH), att, att, _full(wo_ssd.shape), _full(wo_moba.shape),
                  _full(wo_mem.shape), _full((1, D_MODEL)), _full(wq_t.shape), _full(k1.shape), _full(k2.shape),
                  _full(pos.shape), _full(cbias.shape)],
        out_specs=[tok(D_MODEL), pl.BlockSpec((D_MODEL, TM_PP), lambda i: (0, i)), packed, packed, meta, meta],
        out_shape=[jax.ShapeDtypeStruct((t, D_MODEL), F32), jax.ShapeDtypeStruct((D_MODEL, t), BF16),
                   meta_pk, meta_pk, meta_f32, meta_f32],
        scratch_shapes=[pltpu.VMEM((PEER_HEADS * 2 * PEER_HALF, TM_PP), F32)]
        + [pltpu.VMEM((PEER_NKEYS, TM_PP), F32)] * 4
        + [pltpu.VMEM((TM_PP // LANE, PEER_TOPK, LANE), F32)] * 2,
        compiler_params=_cparams(("parallel",)), name="peer_prep",
    )(x2d, y_ssd, y_moba, y_mem, wo_ssd, wo_moba, wo_mem, gain, wq_t, k1, k2, pos, cbias)


_A_PER_SLAB = ES_PEER // PEER_NKEYS
_PEER_STEPS = PEER_EXPERTS // (PEER_SLABS * ES_PEER)


def _peer_gate_chunk(s_ref, a_ref, a_base, k, lc, r2_ref, f_ref, n_ref, c_ref):
    zero = jnp.zeros((PEER_NKEYS, LANE), BF16)

    def bcast(row):
        tile = jnp.broadcast_to(row, (16, LANE)).astype(BF16)
        return jnp.concatenate([tile] * (PEER_NKEYS // 16), axis=0)

    a = a_base + k
    rows = slice(k * PEER_NKEYS, (k + 1) * PEER_NKEYS)
    ls = slice(lc * LANE, (lc + 1) * LANE)
    s = s_ref[rows, ls]
    act = s * (1.0 + lax.erf(s * math.sqrt(0.5)))
    g = zero
    for h in range(PEER_HEADS):
        nrow = bcast(n_ref[h, lc, pl.ds(a, 1), :])
        crow = bcast(c_ref[h, lc, pl.ds(a, 1), :])
        g = g + jnp.where(_unpack_rows(r2_ref[h, :, ls]) < nrow, _unpack_rows(f_ref[h, :, ls]) * crow, zero)
    a_ref[k * (PEER_NKEYS // 2):(k + 1) * (PEER_NKEYS // 2), ls] = pltpu.bitcast(act.astype(BF16) * g, jnp.uint32)


def _peer_body(*refs):
    s_n = PEER_SLABS
    wd0_ref = refs[0]
    wd_next = refs[1:1 + s_n]
    wut_prev = refs[1 + s_n:1 + 2 * s_n]
    hnt_ref, r2_ref, f_ref, n_ref, c_ref, h1_ref, o_ref, acc_scr, s_scr, a_scr = refs[1 + 2 * s_n:]
    n = pl.program_id(1)
    last = pl.num_programs(1) - 1
    meta = (r2_ref, f_ref, n_ref, c_ref)

    @pl.when(n == 0)
    def _():
        acc_scr[...] = jnp.zeros_like(acc_scr)
        a_scr[s_n - 1] = jnp.zeros(a_scr.shape[1:], a_scr.dtype)
        s_scr[0] = jnp.dot(wd0_ref[...], hnt_ref[...], preferred_element_type=F32)

    @pl.when(n < last)
    def _():
        for j in range(s_n):
            for half in range(2):
                ls = slice(half * (TT_PEER // 2), (half + 1) * (TT_PEER // 2))
                acc_scr[:, ls] += jnp.dot(wut_prev[j][...], _unpack_rows(a_scr[(j - 1) % s_n, :, ls]),
                                          preferred_element_type=F32)
                s_scr[(j + 1) % s_n, :, ls] = jnp.dot(wd_next[j][...], hnt_ref[:, ls],
                                                      preferred_element_type=F32)
            a_base = (n * s_n + j) * _A_PER_SLAB
            for lc in range(TT_PEER // LANE):
                for k in range(_A_PER_SLAB):
                    _peer_gate_chunk(s_scr.at[j], a_scr.at[j], a_base, k, lc, *meta)

    @pl.when(n == last)
    def _():
        tail = jnp.dot(wut_prev[0][...], _unpack_rows(a_scr[s_n - 1]), preferred_element_type=F32)
        o_ref[...] = h1_ref[...] + (acc_scr[...] + tail).T


def _peer(wd, wu_t, hn_t, rank2, f, nsel, c, h1):
    t = h1.shape[0]
    s_n = PEER_SLABS
    nblk = s_n * _PEER_STEPS
    meta = pl.BlockSpec((PEER_HEADS, TT_PEER // LANE, PEER_NKEYS, LANE), lambda i, n: (0, i, 0, 0))
    packed = pl.BlockSpec((PEER_HEADS, PEER_NKEYS // 2, TT_PEER), lambda i, n: (0, 0, i))

    def wd_blk(off):
        return pl.BlockSpec((ES_PEER, D_MODEL), lambda i, n: (jnp.clip(s_n * n + off, 0, nblk - 1), 0))

    def wut_blk(off):
        return pl.BlockSpec((None, D_MODEL, ES_PEER), lambda i, n: (jnp.clip(s_n * n + off, 0, nblk - 1), 0, 0))

    return pl.pallas_call(
        _peer_body, grid=(t // TT_PEER, _PEER_STEPS + 1),
        in_specs=[pl.BlockSpec((ES_PEER, D_MODEL), lambda i, n: (0, 0))]
        + [wd_blk(j + 1) for j in range(s_n)] + [wut_blk(j - 1) for j in range(s_n)]
        + [pl.BlockSpec((D_MODEL, TT_PEER), lambda i, n: (0, i)), packed, packed, meta, meta,
           pl.BlockSpec((TT_PEER, D_MODEL), lambda i, n: (i, 0))],
        out_specs=pl.BlockSpec((TT_PEER, D_MODEL), lambda i, n: (i, 0)),
        out_shape=jax.ShapeDtypeStruct((t, D_MODEL), F32),
        scratch_shapes=[pltpu.VMEM((D_MODEL, TT_PEER), F32),
                        pltpu.VMEM((s_n, ES_PEER, TT_PEER), F32),
                        pltpu.VMEM((s_n, ES_PEER // 2, TT_PEER), jnp.uint32)],
        compiler_params=_cparams(("parallel", "arbitrary")), name="peer",
    )(*([wd] * (s_n + 1) + [wu_t] * s_n + [hn_t, rank2, f, nsel, c, h1]))


def _pad_heads(w):
    r = w.shape[0]
    w = w.reshape(r, -1, ATT_DIM)
    return jnp.pad(w, ((0, 0), (0, 0), (0, LANE - ATT_DIM))).reshape(r, -1)


def _pad_row(w):
    return jnp.pad(w, (0, LANE - w.shape[0])).reshape(1, LANE)


def kernel(x, mem, mix_norm_w, w_in, ssd_conv_w, ssd_conv_b, ssd_dt_bias, ssd_a_log, ssd_d, ssd_norm_w,
           moba_q_norm_w, moba_k_norm_w, mem_norm_w, w_mem_kv, xattn_q_norm_w, xattn_k_norm_w, w_out,
           ffn_norm_w, peer_w_query, peer_sub_keys_1, peer_sub_keys_2, peer_expert_down, peer_expert_up):
    b, l, d = x.shape
    depth = w_in.shape[0]
    h = x.reshape(b * l, d)
    for li in range(depth):
        wi = w_in[li]
        o = 0
        cols = {}
        for name, width in (("z", SSD_WIDTH), ("xbc", SSD_CONV_CH), ("dt", SSD_HEADS), ("mq", ATT_WIDTH),
                            ("mk", ATT_WIDTH), ("mv", ATT_WIDTH), ("xq", ATT_WIDTH)):
            cols[name] = wi[:, o:o + width]
            o += width
        w_list = [cols["z"], cols["xbc"], jnp.pad(cols["dt"], ((0, 0), (0, LANE - SSD_HEADS))),
                  _pad_heads(cols["mq"]), _pad_heads(cols["mk"]), _pad_heads(cols["mv"]), _pad_heads(cols["xq"])]
        w_list = [w.astype(BF16) for w in w_list]
        z, xbc, dt, mq, mk, mv, xq = _in_proj(h, mix_norm_w[li].reshape(1, d), w_list)

        y_ssd = _ssd(z.reshape(b, l, -1), xbc.reshape(b, l, -1), dt.reshape(b, l, -1), ssd_conv_w[li],
                     ssd_conv_b[li], ssd_dt_bias[li], ssd_a_log[li], ssd_d[li], ssd_norm_w[li])

        wkv = w_mem_kv[li]
        mem_k, mem_vt = _mem_kv(mem, mem_norm_w[li].reshape(1, d), _pad_heads(wkv[:, :ATT_WIDTH]).astype(BF16),
                                _pad_heads(wkv[:, ATT_WIDTH:]).astype(BF16), _pad_row(xattn_k_norm_w[li]))
        r3 = lambda a: a.reshape(b, l, -1)
        qat, ka, kad, vt, xqt = _attn_prep(r3(mq), r3(mk), r3(mv), r3(xq), _pad_row(moba_q_norm_w[li]),
                                           _pad_row(moba_k_norm_w[li]), _pad_row(xattn_q_norm_w[li]))
        y_moba, y_mem = _attn(qat, ka, kad, vt, xqt, mem_k, mem_vt)

        wo = w_out[li].astype(BF16)
        wo_moba = wo[SSD_WIDTH:SSD_WIDTH + ATT_WIDTH].reshape(ATT_HEADS, ATT_DIM, d)
        wo_mem = wo[SSD_WIDTH + ATT_WIDTH:].reshape(ATT_HEADS, ATT_DIM, d)
        h1, hn_t, rank2, f, nsel, c = _peer_prep(
            h, y_ssd.reshape(b * l, -1), y_moba, y_mem, wo[:SSD_WIDTH], wo_moba, wo_mem,
            ffn_norm_w[li].reshape(1, d), peer_w_query[li].T.astype(BF16), _split_cols(peer_sub_keys_1[li]),
            _split_cols(peer_sub_keys_2[li]))
        wu_t = peer_expert_up[li].reshape(-1, ES_PEER, d).transpose(0, 2, 1).astype(BF16)
        h = _peer(peer_expert_down[li].astype(BF16), wu_t, hn_t, rank2, f, nsel, c, h1)
    return h.reshape(b, l, d)
```

```python
import functools
import math

import numpy as np
import jax
import jax.numpy as jnp
from jax import lax
from jax.experimental import pallas as pl
from jax.experimental.pallas import tpu as pltpu

F32 = jnp.float32
BF16 = jnp.bfloat16
HIGHEST = lax.Precision.HIGHEST

NORM_EPS = 1e-6
D_MODEL = 1024
SSD_HEADS = 8
SSD_HEAD_DIM = 64
SSD_WIDTH = 512
SSD_GROUPS = 2
SSD_STATE = 128
SSD_CONV = 4
SSD_CONV_CH = 1024
ATT_HEADS = 4
ATT_DIM = 64
ATT_WIDTH = 256
MOBA_BLOCK = 256
MOBA_TOPK = 3
MEM_LEN = 256
PEER_HEADS = 8
PEER_NKEYS = 128
PEER_TOPK = 16
PEER_HALF = 64
PEER_EXPERTS = PEER_NKEYS * PEER_NKEYS

LANE = 128
NEG = -1e30
VMEM_LIMIT = 56 * 1024 * 1024

TM_IN = 512
SSD_CHUNK = 256
TQ = MOBA_BLOCK
ATT_GROUP = 4
TM_PP = 512
TT_PEER = 512
ES_PEER = 512
PEER_SLABS = 4

_SLOPES = [2.0 ** (-8.0 * (i + 1) / ATT_HEADS) for i in range(ATT_HEADS)]


def _bf16_split(v):
    hi = float(np.float32(v).astype(BF16).astype(np.float32))
    lo = float(np.float32(v - hi).astype(BF16).astype(np.float32))
    return hi, lo


def _cparams(sem):
    return pltpu.CompilerParams(dimension_semantics=sem, vmem_limit_bytes=VMEM_LIMIT)


def _sigmoid(x):
    return 1.0 / (1.0 + jnp.exp(-x))


def _full(shape):
    n = len(shape)
    return pl.BlockSpec(shape, lambda *_: (0,) * n)


def _in_proj_body(x_ref, g_ref, *refs):
    n = len(refs) // 2
    w_refs, o_refs = refs[:n], refs[n:]
    x = x_ref[...]
    ms = jnp.mean(x * x, axis=-1, keepdims=True)
    xn = ((x * lax.rsqrt(ms + NORM_EPS)) * g_ref[...]).astype(BF16)
    for w_ref, o_ref in zip(w_refs, o_refs):
        o_ref[...] = jnp.dot(xn, w_ref[...], preferred_element_type=F32).astype(o_ref.dtype)


def _in_proj(x2d, gain, weights):
    t = x2d.shape[0]
    in_specs = [pl.BlockSpec((TM_IN, D_MODEL), lambda i: (i, 0)), _full((1, D_MODEL))]
    in_specs += [_full(w.shape) for w in weights]
    out_specs = [pl.BlockSpec((TM_IN, w.shape[1]), lambda i: (i, 0)) for w in weights]
    out_shape = [jax.ShapeDtypeStruct((t, w.shape[1]), F32) for w in weights]
    return pl.pallas_call(
        _in_proj_body, grid=(t // TM_IN,), in_specs=in_specs, out_specs=out_specs,
        out_shape=out_shape, compiler_params=_cparams(("parallel",)), name="in_proj",
    )(x2d, gain, *weights)


def _ssd_body(z_ref, xbc_ref, dt_ref, cw_ref, cb_ref, dtb_ref, alog_ref, dskip_ref, nw_ref, e_ref,
              y_ref, ext_scr, state_scr):
    q = SSD_CHUNK
    c = pl.program_id(1)

    @pl.when(c == 0)
    def _():
        ext_scr[0:8, :] = jnp.zeros((8, SSD_CONV_CH), F32)
        state_scr[...] = jnp.zeros_like(state_scr)

    u = xbc_ref[0]
    ext_scr[8:8 + q, :] = u
    acc = cb_ref[...] + cw_ref[3:4, :] * u
    acc = acc + cw_ref[2:3, :] * ext_scr[7:7 + q, :]
    acc = acc + cw_ref[1:2, :] * ext_scr[6:6 + q, :]
    acc = acc + cw_ref[0:1, :] * ext_scr[5:5 + q, :]
    ext_scr[0:8, :] = u[q - 8:q, :]
    act = acc * _sigmoid(acc)
    xs = act[:, 0:SSD_WIDTH]
    bm = act[:, SSD_WIDTH:SSD_WIDTH + SSD_GROUPS * SSD_STATE]
    cm = act[:, SSD_WIDTH + SSD_GROUPS * SSD_STATE:]

    dtr = dt_ref[0] + dtb_ref[...]
    dt = jnp.maximum(dtr, 0.0) + jnp.log(1.0 + jnp.exp(-jnp.abs(dtr)))
    a = dt * (-jnp.exp(alog_ref[...]))
    row = lax.broadcasted_iota(jnp.int32, (q, q), 0)
    col = lax.broadcasted_iota(jnp.int32, (q, q), 1)
    causal = row >= col
    a_cs = jnp.dot(causal.astype(F32), a, precision=HIGHEST, preferred_element_type=F32)
    a_cs_t = a_cs.T
    a_cs_w = jnp.dot(a_cs, e_ref[...], precision=HIGHEST, preferred_element_type=F32)
    dt_w = jnp.dot(dt, e_ref[...], precision=HIGHEST, preferred_element_type=F32)
    total_w = a_cs_w[q - 1:q, :]
    exp_cs_w = jnp.exp(a_cs_w)
    dte_w = jnp.exp(total_w - a_cs_w)
    cd_w = jnp.exp(total_w)
    xdt_w = xs * dt_w
    lane = lax.broadcasted_iota(jnp.int32, (1, LANE), 1)
    first = lane < SSD_HEAD_DIM

    z = z_ref[0]
    gated = []
    for g in range(SSD_GROUPS):
        bg = bm[:, g * SSD_STATE:(g + 1) * SSD_STATE]
        cg = cm[:, g * SSD_STATE:(g + 1) * SSD_STATE].astype(BF16)
        cb = lax.dot_general(cg, bg.astype(BF16), (((1,), (1,)), ((), ())), preferred_element_type=F32)
        bg_t = bg.T.astype(BF16)
        for kk in range(2):
            k = 2 * g + kk
            sl = slice(k * LANE, (k + 1) * LANE)
            xdt = xdt_w[:, sl]
            xdt_b = xdt.astype(BF16)
            yd = []
            for hh in range(2):
                h = 2 * k + hh
                seg = a_cs[:, h:h + 1] - a_cs_t[h:h + 1, :]
                lm = jnp.exp(jnp.where(causal, seg, -jnp.inf))
                yd.append(jnp.dot((cb * lm).astype(BF16), xdt_b, preferred_element_type=F32))
            y = jnp.where(first, yd[0], yd[1])
            s_old = state_scr[k]
            y = y + jnp.dot(cg, s_old.astype(BF16), preferred_element_type=F32) * exp_cs_w[:, sl]
            y = y + xs[:, sl] * dskip_ref[:, sl]
            state_scr[k] = s_old * cd_w[:, sl] + jnp.dot(
                bg_t, (xdt * dte_w[:, sl]).astype(BF16), preferred_element_type=F32)
            zz = z[:, sl]
            gated.append(y * (zz * _sigmoid(zz)))
    for g in range(SSD_GROUPS):
        y0, y1 = gated[2 * g], gated[2 * g + 1]
        ms = (jnp.sum(y0 * y0, axis=-1, keepdims=True)
              + jnp.sum(y1 * y1, axis=-1, keepdims=True)) * (1.0 / (SSD_WIDTH // SSD_GROUPS))
        r = lax.rsqrt(ms + NORM_EPS)
        lo = 2 * g * LANE
        y_ref[0, :, lo:lo + LANE] = (y0 * r * nw_ref[:, lo:lo + LANE]).astype(y_ref.dtype)
        y_ref[0, :, lo + LANE:lo + 2 * LANE] = (y1 * r * nw_ref[:, lo + LANE:lo + 2 * LANE]).astype(y_ref.dtype)


def _ssd(z, xbc, dt, conv_w, conv_b, dt_bias, a_log, d_skip, norm_w):
    b, l, _ = z.shape
    q = SSD_CHUNK
    pad = LANE - SSD_HEADS
    dtb = jnp.pad(dt_bias, (0, pad)).reshape(1, LANE)
    alog = jnp.pad(a_log, (0, pad)).reshape(1, LANE)
    dsk = jnp.repeat(d_skip, SSD_HEAD_DIM).reshape(1, SSD_WIDTH)
    expand = (jnp.arange(LANE)[:, None] == (jnp.arange(SSD_WIDTH)[None, :] // SSD_HEAD_DIM)).astype(F32)
    tok = lambda w: pl.BlockSpec((1, q, w), lambda bi, ci: (bi, ci, 0))
    return pl.pallas_call(
        _ssd_body, grid=(b, l // q),
        in_specs=[tok(SSD_WIDTH), tok(SSD_CONV_CH), tok(LANE), _full((SSD_CONV, SSD_CONV_CH)),
                  _full((1, SSD_CONV_CH)), _full((1, LANE)), _full((1, LANE)), _full((1, SSD_WIDTH)),
                  _full((1, SSD_WIDTH)), _full((LANE, SSD_WIDTH))],
        out_specs=tok(SSD_WIDTH),
        out_shape=jax.ShapeDtypeStruct((b, l, SSD_WIDTH), BF16),
        scratch_shapes=[pltpu.VMEM((q + 8, SSD_CONV_CH), F32),
                        pltpu.VMEM((SSD_HEADS // 2, SSD_STATE, LANE), F32)],
        compiler_params=_cparams(("parallel", "arbitrary")), name="ssd",
    )(z, xbc, dt, conv_w, conv_b.reshape(1, -1), dtb, alog, dsk, norm_w.reshape(1, -1), expand)


V_ROWS = ATT_DIM + 16


def _with_ones_row(v_t):
    n = v_t.shape[1]
    r = lax.broadcasted_iota(jnp.int32, (V_ROWS - ATT_DIM, n), 0)
    return jnp.concatenate([v_t, jnp.where(r == 0, 1.0, 0.0)], axis=0)


def _head_rms(xh, w_row):
    ms = jnp.sum(xh * xh, axis=-1, keepdims=True) * (1.0 / ATT_DIM)
    return xh * lax.rsqrt(ms + NORM_EPS) * w_row


def _mem_kv_body(mem_ref, g_ref, wk_ref, wv_ref, kw_ref, k_ref, vt_ref):
    x = mem_ref[0]
    ms = jnp.mean(x * x, axis=-1, keepdims=True)
    xn = ((x * lax.rsqrt(ms + NORM_EPS)) * g_ref[...]).astype(BF16)
    kp = jnp.dot(xn, wk_ref[...], preferred_element_type=F32)
    vp = jnp.dot(xn, wv_ref[...], preferred_element_type=F32)
    for h in range(ATT_HEADS):
        sl = slice(h * LANE, (h + 1) * LANE)
        k_ref[0, h] = _head_rms(kp[:, sl], kw_ref[...]).astype(BF16)
        vt_ref[0, h] = _with_ones_row(vp[:, sl].T[0:ATT_DIM, :]).astype(BF16)


def _mem_kv(mem, gain, wk, wv, k_norm_w):
    b = mem.shape[0]
    return pl.pallas_call(
        _mem_kv_body, grid=(b,),
        in_specs=[pl.BlockSpec((1, MEM_LEN, D_MODEL), lambda i: (i, 0, 0)), _full((1, D_MODEL)),
                  _full(wk.shape), _full(wv.shape), _full((1, LANE))],
        out_specs=[pl.BlockSpec((1, ATT_HEADS, MEM_LEN, LANE), lambda i: (i, 0, 0, 0)),
                   pl.BlockSpec((1, ATT_HEADS, V_ROWS, MEM_LEN), lambda i: (i, 0, 0, 0))],
        out_shape=[jax.ShapeDtypeStruct((b, ATT_HEADS, MEM_LEN, LANE), BF16),
                   jax.ShapeDtypeStruct((b, ATT_HEADS, V_ROWS, MEM_LEN), BF16)],
        compiler_params=_cparams(("parallel",)), name="mem_kv",
    )(mem, gain, wk, wv, k_norm_w)


MAX_BLOCKS = 32
_AUG_ONEHOT = ATT_DIM
_AUG_EXTRA = ATT_DIM + MAX_BLOCKS


def _attn_prep_body(slopes, q_ref, k_ref, v_ref, xq_ref, qw_ref, kw_ref, xqw_ref,
                    qat_ref, ka_ref, kad_ref, vt_ref, xqt_ref, kmean_scr):
    i = pl.program_id(1)
    nb = kmean_scr.shape[1]

    @pl.when(i == 0)
    def _():
        kmean_scr[...] = jnp.zeros_like(kmean_scr)

    n_iota = lax.broadcasted_iota(jnp.int32, (nb, TQ), 0)
    past = n_iota < i
    t_loc = lax.broadcasted_iota(jnp.int32, (32, TQ), 1).astype(F32)
    r32 = lax.broadcasted_iota(jnp.int32, (32, TQ), 0)
    k_lane = lax.broadcasted_iota(jnp.int32, (TQ, LANE), 1)
    s_loc = lax.broadcasted_iota(jnp.int32, (TQ, LANE), 0).astype(F32)
    blk = lax.convert_element_type(i, F32)
    for h in range(ATT_HEADS):
        hi, lo = slopes[h]
        sl = slice(h * LANE, (h + 1) * LANE)
        qn = _head_rms(q_ref[0, :, sl], qw_ref[...])
        kn = _head_rms(k_ref[0, :, sl], kw_ref[...])
        qn_t = qn.T
        gate = jnp.dot(kmean_scr[h], qn_t, precision=HIGHEST, preferred_element_type=F32)
        gate = jnp.where(past, gate, -jnp.inf)
        cnt = jnp.zeros((nb, TQ), F32)
        for n2 in range(nb):
            gn = gate[n2:n2 + 1, :]
            ahead = jnp.where(gn > gate, 1.0, jnp.where(gn == gate, jnp.where(n_iota > n2, 1.0, 0.0), 0.0))
            cnt = cnt + ahead
        allowed = jnp.where(past, jnp.where(cnt < float(MOBA_TOPK), 1.0, 0.0), 0.0)
        bias = jnp.where(allowed > 0.5, 0.0, NEG)
        extra = jnp.where(r32 < 2, t_loc, 0.0)
        for r, val in ((2, hi), (3, lo), (4, MOBA_BLOCK * hi), (5, MOBA_BLOCK * lo)):
            extra = jnp.where(r32 == r, val, extra)
        extra = jnp.where((r32 == 6) | (r32 == 7), blk, extra)
        qat = jnp.concatenate([qn_t[0:ATT_DIM, :] * (1.0 / math.sqrt(ATT_DIM)), bias, extra], axis=0)
        qat_ref[0, h] = qat.astype(BF16)
        kx = jnp.where((k_lane == _AUG_EXTRA + 2) | (k_lane == _AUG_EXTRA + 3), s_loc, 0.0)
        kx = jnp.where((k_lane == _AUG_EXTRA + 4) | (k_lane == _AUG_EXTRA + 5), blk, kx)
        for c, val in ((0, -hi), (1, -lo), (6, -MOBA_BLOCK * hi), (7, -MOBA_BLOCK * lo)):
            kx = jnp.where(k_lane == _AUG_EXTRA + c, val, kx)
        kad = kn + kx
        kad_ref[0, h] = kad.astype(BF16)
        ka_ref[0, h] = (kad + jnp.where(k_lane - _AUG_ONEHOT == i, 1.0, 0.0)).astype(BF16)
        vt_ref[0, h] = _with_ones_row(v_ref[0, :, sl].T[0:ATT_DIM, :]).astype(BF16)
        xqn = _head_rms(xq_ref[0, :, sl], xqw_ref[...])
        xqt_ref[0, h] = (xqn.T * (1.0 / math.sqrt(ATT_DIM))).astype(BF16)
        kmean_scr[h, pl.ds(i, 1), :] = jnp.sum(kn, axis=0, keepdims=True) * (1.0 / MOBA_BLOCK)


def _attn_prep(q, k, v, xq, qw, kw, xqw):
    b, l, _ = q.shape
    nb = l // MOBA_BLOCK
    assert nb <= MAX_BLOCKS
    slopes = tuple(_bf16_split(s) for s in _SLOPES)
    tok = pl.BlockSpec((1, TQ, ATT_HEADS * LANE), lambda bi, i: (bi, i, 0))
    return pl.pallas_call(
        functools.partial(_attn_prep_body, slopes), grid=(b, nb),
        in_specs=[tok, tok, tok, tok, _full((1, LANE)), _full((1, LANE)), _full((1, LANE))],
        out_specs=[pl.BlockSpec((1, ATT_HEADS, LANE, TQ), lambda bi, i: (bi, 0, 0, i)),
                   pl.BlockSpec((1, ATT_HEADS, TQ, LANE), lambda bi, i: (bi, 0, i, 0)),
                   pl.BlockSpec((1, ATT_HEADS, TQ, LANE), lambda bi, i: (bi, 0, i, 0)),
                   pl.BlockSpec((1, ATT_HEADS, V_ROWS, TQ), lambda bi, i: (bi, 0, 0, i)),
                   pl.BlockSpec((1, ATT_HEADS, LANE, TQ), lambda bi, i: (bi, 0, 0, i))],
        out_shape=[jax.ShapeDtypeStruct((b, ATT_HEADS, LANE, l), BF16),
                   jax.ShapeDtypeStruct((b, ATT_HEADS, l, LANE), BF16),
                   jax.ShapeDtypeStruct((b, ATT_HEADS, l, LANE), BF16),
                   jax.ShapeDtypeStruct((b, ATT_HEADS, V_ROWS, l), BF16),
                   jax.ShapeDtypeStruct((b, ATT_HEADS, LANE, l), BF16)],
        scratch_shapes=[pltpu.VMEM((ATT_HEADS, MAX_BLOCKS, LANE), F32)],
        compiler_params=_cparams(("parallel", "arbitrary")), name="attn_prep",
    )(q, k, v, xq, qw, kw, xqw)


def _normalized_t(acc):
    return (acc[0:ATT_DIM, :] * (1.0 / acc[ATT_DIM:ATT_DIM + 1, :])).T


def _attn_body(qat_ref, ka_ref, kad_ref, vt_ref, xqt_ref, mk_ref, mvt_ref, o_ref, om_ref, s_scr, p_scr):
    i = pl.program_id(1)
    base = pl.multiple_of(i * MOBA_BLOCK, MOBA_BLOCK)
    key = lax.broadcasted_iota(jnp.int32, (MOBA_BLOCK, TQ), 0)
    qry = lax.broadcasted_iota(jnp.int32, (MOBA_BLOCK, TQ), 1)

    span = ATT_GROUP * MOBA_BLOCK
    n_groups = (i + ATT_GROUP - 1) // ATT_GROUP
    last_group = ka_ref.shape[2] // span - 1

    def scores(h, g):
        off = pl.multiple_of(jnp.minimum(g, last_group) * span, span)
        return jnp.dot(ka_ref[0, h, pl.ds(off, span), :], qat_ref[0, h], preferred_element_type=F32)

    def values(h, g, p):
        off = pl.multiple_of(jnp.maximum(g, 0) * span, span)
        return jnp.dot(vt_ref[0, h, :, pl.ds(off, span)], p, preferred_element_type=F32)

    init = []
    for h in range(ATT_HEADS):
        s = jnp.dot(kad_ref[0, h], qat_ref[0, h], preferred_element_type=F32)
        s = jnp.where(key <= qry, s, NEG)
        m = jnp.max(s, axis=0, keepdims=True)
        p = jnp.exp(s - m).astype(BF16)
        acc = jnp.dot(vt_ref[0, h, :, pl.ds(base, MOBA_BLOCK)], p, preferred_element_type=F32)
        s_scr[h] = scores(h, 0)
        p_scr[h] = jnp.zeros((span, TQ), BF16)
        init += [jnp.ones((1, TQ), F32), m, acc]

    def body(g, carry):
        out = []
        for h in range(ATT_HEADS):
            alpha_prev, m, acc = carry[3 * h:3 * h + 3]
            acc = acc * alpha_prev + values(h, g - 1, p_scr[h])
            s = s_scr[h]
            m_new = jnp.maximum(m, jnp.max(s, axis=0, keepdims=True))
            p_scr[h] = jnp.exp(s - m_new).astype(BF16)
            s_scr[h] = scores(h, g + 1)
            out += [jnp.exp(m - m_new), m_new, acc]
        return tuple(out)

    fin = lax.fori_loop(0, n_groups, body, tuple(init))
    for h in range(ATT_HEADS):
        alpha_prev, _, acc = fin[3 * h:3 * h + 3]
        o_ref[0, h] = _normalized_t(acc * alpha_prev + values(h, n_groups - 1, p_scr[h]))
        sm = jnp.dot(mk_ref[0, h], xqt_ref[0, h], preferred_element_type=F32)
        pm = jnp.exp(sm - jnp.max(sm, axis=0, keepdims=True)).astype(BF16)
        om_ref[0, h] = _normalized_t(jnp.dot(mvt_ref[0, h], pm, preferred_element_type=F32))


def _attn(qat, ka, kad, vt, xqt, mem_k, mem_vt):
    b, _, _, l = qat.shape
    assert l % (ATT_GROUP * MOBA_BLOCK) == 0
    nh = ATT_HEADS
    per_q = lambda r: pl.BlockSpec((1, nh, r, TQ), lambda bi, i: (bi, 0, 0, i))
    out = pl.BlockSpec((1, nh, TQ, ATT_DIM), lambda bi, i: (bi, 0, i, 0))
    return pl.pallas_call(
        _attn_body, grid=(b, l // TQ),
        in_specs=[per_q(LANE),
                  pl.BlockSpec((1, nh, l, LANE), lambda bi, i: (bi, 0, 0, 0)),
                  pl.BlockSpec((1, nh, TQ, LANE), lambda bi, i: (bi, 0, i, 0)),
                  pl.BlockSpec((1, nh, V_ROWS, l), lambda bi, i: (bi, 0, 0, 0)),
                  per_q(LANE),
                  pl.BlockSpec((1, nh, MEM_LEN, LANE), lambda bi, i: (bi, 0, 0, 0)),
                  pl.BlockSpec((1, nh, V_ROWS, MEM_LEN), lambda bi, i: (bi, 0, 0, 0))],
        out_specs=[out, out],
        out_shape=[jax.ShapeDtypeStruct((b, nh, l, ATT_DIM), F32)] * 2,
        scratch_shapes=[pltpu.VMEM((nh, ATT_GROUP * MOBA_BLOCK, TQ), F32),
                        pltpu.VMEM((nh, ATT_GROUP * MOBA_BLOCK, TQ), BF16)],
        compiler_params=_cparams(("parallel", "arbitrary")), name="attn",
    )(qat, ka, kad, vt, xqt, mem_k, mem_vt)


def _cand_layout():
    slabs = [("r1", 0, 16), ("r1", 1, 8), ("r1", 2, 8), ("r1", 3, 8), ("r2", 0, 16), ("r2", 1, 8), ("r2", 2, 8)]
    pos, valid = [], []
    for kind, fixed, n in slabs:
        for j in range(n):
            r1, r2 = (fixed, j) if kind == "r1" else (j, fixed)
            ok = (r1 + 1) * (r2 + 1) <= PEER_TOPK and ((kind == "r1") or r1 >= 4)
            pos.append(r1 * PEER_TOPK + r2)
            valid.append(ok)
    assert sum(valid) == 50
    return slabs, np.asarray(pos, np.int32), np.asarray(valid)


_CAND_SLABS, _CAND_POS, _CAND_VALID = _cand_layout()
_NCAND = len(_CAND_POS)


def _split_cols(k):
    hi = k.astype(BF16)
    lo = (k - hi.astype(F32)).astype(BF16)
    return jnp.concatenate([hi, hi, lo], axis=1)


def _dot_split(k_cat, q):
    hi = q.astype(BF16)
    lo = (q - hi.astype(F32)).astype(BF16)
    return jnp.dot(k_cat, jnp.concatenate([hi, lo, hi], axis=0), preferred_element_type=F32)


def _pack_rows(x):
    return pltpu.bitcast(x.astype(BF16), jnp.uint32)


def _unpack_rows(x):
    return pltpu.bitcast(x, BF16)


def _top16_exact(sets):
    row = lax.broadcasted_iota(jnp.int32, (PEER_NKEYS, LANE), 0).astype(F32)
    for _, rank_scr, _ in sets:
        rank_scr[...] = jnp.full(rank_scr.shape, float(PEER_TOPK), F32)

    def body(r, _):
        rf = lax.convert_element_type(r, F32)
        for half in range(TM_PP // LANE):
            ls = slice(half * LANE, (half + 1) * LANE)
            for w_scr, rank_scr, v_scr in sets:
                s = w_scr[:, ls]
                m = jnp.max(s, axis=0, keepdims=True)
                hit = row == jnp.min(jnp.where(s == m, row, float(PEER_NKEYS)), axis=0, keepdims=True)
                w_scr[:, ls] = jnp.where(hit, -jnp.inf, s)
                v_scr[half, pl.ds(r, 1), :] = m
                rank_scr[:, ls] = jnp.where(hit, rf, rank_scr[:, ls])
        return 0

    lax.fori_loop(0, PEER_TOPK, body, 0)


def _top16_distinct(sets):
    chains = [(w_scr, rank_scr, v_scr, half) for half in range(TM_PP // LANE) for w_scr, rank_scr, v_scr in sets]

    def body(r, prev):
        out = []
        for (w_scr, _, v_scr, half), m_prev in zip(chains, prev):
            s = w_scr[:, half * LANE:(half + 1) * LANE]
            m = jnp.max(jnp.where(s < m_prev, s, -jnp.inf), axis=0, keepdims=True)
            v_scr[half, pl.ds(r, 1), :] = m
            out.append(m)
        return tuple(out)

    lax.fori_loop(0, PEER_TOPK, body, tuple(jnp.full((1, LANE), jnp.inf, F32) for _ in chains))
    for w_scr, rank_scr, v_scr, half in chains:
        ls = slice(half * LANE, (half + 1) * LANE)
        s = w_scr[:, ls]
        v = v_scr[half]
        rank = jnp.full(s.shape, float(PEER_TOPK), F32)
        for r in range(PEER_TOPK):
            rank = jnp.where(s == v[r:r + 1, :], float(r), rank)
        rank_scr[:, ls] = rank


def _peer_prep_body(x_ref, ys_ref, ym_ref, yx_ref, wos_ref, wom_ref, wox_ref, g_ref, wqt_ref,
                    k1_ref, k2_ref, pos_ref, cbias_ref,
                    h1_ref, hnt_ref, r2_ref, f_ref, n_ref, c_ref,
                    q_scr, w1_scr, w2_scr, rank1_scr, rank2_scr, v1_scr, v2_scr, cnt_scr, z_scr):
    hres = x_ref[...] + jnp.dot(ys_ref[...], wos_ref[...], preferred_element_type=F32)
    for h in range(ATT_HEADS):
        hres = hres + jnp.dot(ym_ref[0, h].astype(BF16), wom_ref[h], preferred_element_type=F32)
        hres = hres + jnp.dot(yx_ref[0, h].astype(BF16), wox_ref[h], preferred_element_type=F32)
    h1_ref[...] = hres
    ms = jnp.mean(hres * hres, axis=-1, keepdims=True)
    hn = (hres * lax.rsqrt(ms + NORM_EPS)) * g_ref[...]
    hn_t = hn.T.astype(BF16)
    hnt_ref[...] = hn_t
    q_scr[...] = jnp.dot(wqt_ref[...], hn_t, preferred_element_type=F32)

    pos = pos_ref[...]
    cbias = cbias_ref[...]
    r16 = lax.broadcasted_iota(jnp.int32, (PEER_TOPK, TM_PP), 0).astype(F32)
    halves = lambda scr: jnp.concatenate([scr[j] for j in range(TM_PP // LANE)], axis=1)

    def head(h, _):
        base = pl.multiple_of(h * (2 * PEER_HALF), 2 * PEER_HALF)
        s1 = _dot_split(k1_ref[...], q_scr[pl.ds(base, PEER_HALF), :])
        s2 = _dot_split(k2_ref[...], q_scr[pl.ds(base + PEER_HALF, PEER_HALF), :])
        sets = [(w1_scr, rank1_scr, v1_scr), (w2_scr, rank2_scr, v2_scr)]
        w1_scr[...] = s1
        w2_scr[...] = s2
        _top16_distinct(sets)
        ranked = sum(jnp.sum(jnp.where(scr[...] < float(PEER_TOPK), 1.0, 0.0), axis=0, keepdims=True)
                     for scr in (rank1_scr, rank2_scr))
        tied = jnp.max(jnp.abs(ranked - 2.0 * PEER_TOPK)) > 0.5

        @pl.when(tied)
        def _():
            _top16_exact(sets)

        rank1 = rank1_scr[...]
        rank2 = rank2_scr[...]
        v1 = halves(v1_scr)
        v2 = halves(v2_scr)
        parts = []
        for kind, fixed, n in _CAND_SLABS:
            if kind == "r1":
                parts.append(v1[fixed:fixed + 1, :] + v2[0:n, :])
            else:
                parts.append(v1[0:n, :] + v2[fixed:fixed + 1, :])
        cand = jnp.concatenate(parts, axis=0) + cbias
        top = v1[0:1, :] + v2[0:1, :]

        def descend(_, carry):
            m_prev, zsum = carry
            m = jnp.max(jnp.where(cand < m_prev, cand, -jnp.inf), axis=0, keepdims=True)
            return m, zsum + jnp.exp(m - top)

        tau, zsum = lax.fori_loop(0, PEER_TOPK, descend,
                                  (jnp.full((1, TM_PP), jnp.inf, F32), jnp.zeros((1, TM_PP), F32)))
        taken = jnp.where(cand >= tau, 1.0, 0.0)
        cnt = jnp.zeros((PEER_TOPK, TM_PP), F32)
        row0 = 0
        for kind, fixed, n_rows in _CAND_SLABS:
            blk = taken[row0:row0 + n_rows, :]
            if kind == "r1":
                cnt = cnt + jnp.where(r16 == float(fixed), jnp.sum(blk, axis=0, keepdims=True), 0.0)
            elif n_rows == PEER_TOPK:
                cnt = cnt + blk
            else:
                cnt = cnt + jnp.concatenate([blk, jnp.zeros((PEER_TOPK - n_rows, TM_PP), F32)], axis=0)
            row0 += n_rows
        cnt_scr[...] = cnt
        z_scr[...] = zsum
        tied2 = jnp.max(jnp.abs(jnp.sum(cnt, axis=0, keepdims=True) - float(PEER_TOPK))) > 0.5

        @pl.when(tied2)
        def _():
            def pick(_, carry):
                cand, cnt, zsum = carry
                m = jnp.max(cand, axis=0, keepdims=True)
                p = jnp.min(jnp.where(cand == m, pos, float(4 * PEER_TOPK * PEER_TOPK)), axis=0, keepdims=True)
                cand = jnp.where(pos == p, -jnp.inf, cand)
                cnt = cnt + jnp.where(r16 == jnp.floor(p * (1.0 / PEER_TOPK)), 1.0, 0.0)
                return cand, cnt, zsum + jnp.exp(m - top)

            _, cnt, zsum = lax.fori_loop(
                0, PEER_TOPK, pick, (cand, jnp.zeros((PEER_TOPK, TM_PP), F32), jnp.zeros((1, TM_PP), F32)))
            cnt_scr[...] = cnt
            z_scr[...] = zsum

        cnt = cnt_scr[...]
        zsum = z_scr[...]
        nsel = jnp.zeros((PEER_NKEYS, TM_PP), F32)
        for r in range(PEER_TOPK):
            nsel = jnp.where(rank1 == float(r), cnt[r:r + 1, :], nsel)
        r2_ref[h] = _pack_rows(rank2)
        f_ref[h] = _pack_rows(jnp.exp(s2 - v2[0:1, :]))
        cw = jnp.exp(s1 - v1[0:1, :]) * (0.5 / zsum)
        for lc in range(TM_PP // LANE):
            n_ref[h, lc] = nsel[:, lc * LANE:(lc + 1) * LANE]
            c_ref[h, lc] = cw[:, lc * LANE:(lc + 1) * LANE]
        return 0

    lax.fori_loop(0, PEER_HEADS, head, 0)


def _peer_prep(x2d, y_ssd, y_moba, y_mem, wo_ssd, wo_moba, wo_mem, gain, wq_t, k1, k2):
    t = x2d.shape[0]
    b, _, l, _ = y_moba.shape
    per_b = l // TM_PP
    pos = jnp.asarray(np.broadcast_to(_CAND_POS.astype(np.float32)[:, None], (_NCAND, TM_PP)))
    cbias = jnp.asarray(np.broadcast_to(np.where(_CAND_VALID, 0.0, -np.inf).astype(np.float32)[:, None],
                                        (_NCAND, TM_PP)))
    tok = lambda w: pl.BlockSpec((TM_PP, w), lambda i: (i, 0))
    att = pl.BlockSpec((1, ATT_HEADS, TM_PP, ATT_DIM), lambda i: (i // per_b, 0, i % per_b, 0))
    meta = pl.BlockSpec((PEER_HEADS, TM_PP // LANE, PEER_NKEYS, LANE), lambda i: (0, i, 0, 0))
    meta_f32 = jax.ShapeDtypeStruct((PEER_HEADS, t // LANE, PEER_NKEYS, LANE), F32)
    packed = pl.BlockSpec((PEER_HEADS, PEER_NKEYS // 2, TM_PP), lambda i: (0, 0, i))
    meta_pk = jax.ShapeDtypeStruct((PEER_HEADS, PEER_NKEYS // 2, t), jnp.uint32)
    return pl.pallas_call(
        _peer_prep_body, grid=(t // TM_PP,),
        in_specs=[tok(D_MODEL), tok(SSD_WIDTH), att, att, _full(wo_ssd.shape), _full(wo_moba.shape),
                  _full(wo_mem.shape), _full((1, D_MODEL)), _full(wq_t.shape), _full(k1.shape), _full(k2.shape),
                  _full(pos.shape), _full(cbias.shape)],
        out_specs=[tok(D_MODEL), pl.BlockSpec((D_MODEL, TM_PP), lambda i: (0, i)), packed, packed, meta, meta],
        out_shape=[jax.ShapeDtypeStruct((t, D_MODEL), F32), jax.ShapeDtypeStruct((D_MODEL, t), BF16),
                   meta_pk, meta_pk, meta_f32, meta_f32],
        scratch_shapes=[pltpu.VMEM((PEER_HEADS * 2 * PEER_HALF, TM_PP), F32)]
        + [pltpu.VMEM((PEER_NKEYS, TM_PP), F32)] * 4
        + [pltpu.VMEM((TM_PP // LANE, PEER_TOPK, LANE), F32)] * 2
        + [pltpu.VMEM((PEER_TOPK, TM_PP), F32), pltpu.VMEM((1, TM_PP), F32)],
        compiler_params=_cparams(("parallel",)), name="peer_prep",
    )(x2d, y_ssd, y_moba, y_mem, wo_ssd, wo_moba, wo_mem, gain, wq_t, k1, k2, pos, cbias)


_A_PER_SLAB = ES_PEER // PEER_NKEYS
_PEER_STEPS = PEER_EXPERTS // (PEER_SLABS * ES_PEER)


def _peer_gate_chunk(s_ref, a_ref, a_base, k, lc, r2_ref, f_ref, n_ref, c_ref):
    zero = jnp.zeros((PEER_NKEYS, LANE), BF16)

    def bcast(row):
        tile = jnp.broadcast_to(row, (16, LANE)).astype(BF16)
        return jnp.concatenate([tile] * (PEER_NKEYS // 16), axis=0)

    a = a_base + k
    rows = slice(k * PEER_NKEYS, (k + 1) * PEER_NKEYS)
    ls = slice(lc * LANE, (lc + 1) * LANE)
    s = s_ref[rows, ls]
    act = s * (1.0 + lax.erf(s * math.sqrt(0.5)))
    g = zero
    for h in range(PEER_HEADS):
        nrow = bcast(n_ref[h, lc, pl.ds(a, 1), :])
        crow = bcast(c_ref[h, lc, pl.ds(a, 1), :])
        g = g + jnp.where(_unpack_rows(r2_ref[h, :, ls]) < nrow, _unpack_rows(f_ref[h, :, ls]) * crow, zero)
    a_ref[k * (PEER_NKEYS // 2):(k + 1) * (PEER_NKEYS // 2), ls] = pltpu.bitcast(act.astype(BF16) * g, jnp.uint32)


def _peer_body(*refs):
    s_n = PEER_SLABS
    wd0_ref = refs[0]
    wd_next = refs[1:1 + s_n]
    wut_prev = refs[1 + s_n:1 + 2 * s_n]
    (wut_last_ref, hnt_ref, r2_ref, f_ref, n_ref, c_ref, h1_ref, o_ref,
     acc_scr, s_scr, a_scr) = refs[1 + 2 * s_n:]
    n = pl.program_id(1)
    meta = (r2_ref, f_ref, n_ref, c_ref)

    @pl.when(n == 0)
    def _():
        acc_scr[...] = jnp.zeros_like(acc_scr)
        a_scr[s_n - 1] = jnp.zeros(a_scr.shape[1:], a_scr.dtype)
        s_scr[0] = jnp.dot(wd0_ref[...], hnt_ref[...], preferred_element_type=F32)

    for j in range(s_n):
        for half in range(2):
            ls = slice(half * (TT_PEER // 2), (half + 1) * (TT_PEER // 2))
            acc_scr[:, ls] += jnp.dot(wut_prev[j][...], _unpack_rows(a_scr[(j - 1) % s_n, :, ls]),
                                      preferred_element_type=F32)
            s_scr[(j + 1) % s_n, :, ls] = jnp.dot(wd_next[j][...], hnt_ref[:, ls],
                                                  preferred_element_type=F32)
        a_base = (n * s_n + j) * _A_PER_SLAB
        for lc in range(TT_PEER // LANE):
            for k in range(_A_PER_SLAB):
                _peer_gate_chunk(s_scr.at[j], a_scr.at[j], a_base, k, lc, *meta)

    @pl.when(n == pl.num_programs(1) - 1)
    def _():
        tail = jnp.dot(wut_last_ref[...], _unpack_rows(a_scr[s_n - 1]), preferred_element_type=F32)
        o_ref[...] = h1_ref[...] + (acc_scr[...] + tail).T


def _peer(wd, wu_t, hn_t, rank2, f, nsel, c, h1):
    t = h1.shape[0]
    s_n = PEER_SLABS
    nblk = s_n * _PEER_STEPS
    meta = pl.BlockSpec((PEER_HEADS, TT_PEER // LANE, PEER_NKEYS, LANE), lambda i, n: (0, i, 0, 0))
    packed = pl.BlockSpec((PEER_HEADS, PEER_NKEYS // 2, TT_PEER), lambda i, n: (0, 0, i))

    def wd_blk(off):
        return pl.BlockSpec((ES_PEER, D_MODEL), lambda i, n: (jnp.clip(s_n * n + off, 0, nblk - 1), 0))

    def wut_blk(off):
        return pl.BlockSpec((None, D_MODEL, ES_PEER), lambda i, n: (jnp.clip(s_n * n + off, 0, nblk - 1), 0, 0))

    return pl.pallas_call(
        _peer_body, grid=(t // TT_PEER, _PEER_STEPS),
        in_specs=[pl.BlockSpec((ES_PEER, D_MODEL), lambda i, n: (0, 0))]
        + [wd_blk(j + 1) for j in range(s_n)] + [wut_blk(j - 1) for j in range(s_n)]
        + [pl.BlockSpec((None, D_MODEL, ES_PEER), lambda i, n: (nblk - 1, 0, 0)),
           pl.BlockSpec((D_MODEL, TT_PEER), lambda i, n: (0, i)), packed, packed, meta, meta,
           pl.BlockSpec((TT_PEER, D_MODEL), lambda i, n: (i, 0))],
        out_specs=pl.BlockSpec((TT_PEER, D_MODEL), lambda i, n: (i, 0)),
        out_shape=jax.ShapeDtypeStruct((t, D_MODEL), F32),
        scratch_shapes=[pltpu.VMEM((D_MODEL, TT_PEER), F32),
                        pltpu.VMEM((s_n, ES_PEER, TT_PEER), F32),
                        pltpu.VMEM((s_n, ES_PEER // 2, TT_PEER), jnp.uint32)],
        compiler_params=_cparams(("parallel", "arbitrary")), name="peer",
    )(*([wd] * (s_n + 1) + [wu_t] * (s_n + 1) + [hn_t, rank2, f, nsel, c, h1]))


def _pad_heads(w):
    r = w.shape[0]
    w = w.reshape(r, -1, ATT_DIM)
    return jnp.pad(w, ((0, 0), (0, 0), (0, LANE - ATT_DIM))).reshape(r, -1)


def _pad_row(w):
    return jnp.pad(w, (0, LANE - w.shape[0])).reshape(1, LANE)


def kernel(x, mem, mix_norm_w, w_in, ssd_conv_w, ssd_conv_b, ssd_dt_bias, ssd_a_log, ssd_d, ssd_norm_w,
           moba_q_norm_w, moba_k_norm_w, mem_norm_w, w_mem_kv, xattn_q_norm_w, xattn_k_norm_w, w_out,
           ffn_norm_w, peer_w_query, peer_sub_keys_1, peer_sub_keys_2, peer_expert_down, peer_expert_up):
    b, l, d = x.shape
    depth = w_in.shape[0]
    h = x.reshape(b * l, d)
    for li in range(depth):
        wi = w_in[li]
        o = 0
        cols = {}
        for name, width in (("z", SSD_WIDTH), ("xbc", SSD_CONV_CH), ("dt", SSD_HEADS), ("mq", ATT_WIDTH),
                            ("mk", ATT_WIDTH), ("mv", ATT_WIDTH), ("xq", ATT_WIDTH)):
            cols[name] = wi[:, o:o + width]
            o += width
        w_list = [cols["z"], cols["xbc"], jnp.pad(cols["dt"], ((0, 0), (0, LANE - SSD_HEADS))),
                  _pad_heads(cols["mq"]), _pad_heads(cols["mk"]), _pad_heads(cols["mv"]), _pad_heads(cols["xq"])]
        w_list = [w.astype(BF16) for w in w_list]
        z, xbc, dt, mq, mk, mv, xq = _in_proj(h, mix_norm_w[li].reshape(1, d), w_list)

        y_ssd = _ssd(z.reshape(b, l, -1), xbc.reshape(b, l, -1), dt.reshape(b, l, -1), ssd_conv_w[li],
                     ssd_conv_b[li], ssd_dt_bias[li], ssd_a_log[li], ssd_d[li], ssd_norm_w[li])

        wkv = w_mem_kv[li]
        mem_k, mem_vt = _mem_kv(mem, mem_norm_w[li].reshape(1, d), _pad_heads(wkv[:, :ATT_WIDTH]).astype(BF16),
                                _pad_heads(wkv[:, ATT_WIDTH:]).astype(BF16), _pad_row(xattn_k_norm_w[li]))
        r3 = lambda a: a.reshape(b, l, -1)
        qat, ka, kad, vt, xqt = _attn_prep(r3(mq), r3(mk), r3(mv), r3(xq), _pad_row(moba_q_norm_w[li]),
                                           _pad_row(moba_k_norm_w[li]), _pad_row(xattn_q_norm_w[li]))
        y_moba, y_mem = _attn(qat, ka, kad, vt, xqt, mem_k, mem_vt)

        wo = w_out[li].astype(BF16)
        wo_moba = wo[SSD_WIDTH:SSD_WIDTH + ATT_WIDTH].reshape(ATT_HEADS, ATT_DIM, d)
        wo_mem = wo[SSD_WIDTH + ATT_WIDTH:].reshape(ATT_HEADS, ATT_DIM, d)
        h1, hn_t, rank2, f, nsel, c = _peer_prep(
            h, y_ssd.reshape(b * l, -1), y_moba, y_mem, wo[:SSD_WIDTH], wo_moba, wo_mem,
            ffn_norm_w[li].reshape(1, d), peer_w_query[li].T.astype(BF16), _split_cols(peer_sub_keys_1[li]),
            _split_cols(peer_sub_keys_2[li]))
        wu_t = peer_expert_up[li].reshape(-1, ES_PEER, d).transpose(0, 2, 1).astype(BF16)
        h = _peer(peer_expert_down[li].astype(BF16), wu_t, hn_t, rank2, f, nsel, c, h1)
    return h.reshape(b, l, d)
```

```python
import functools
import math

import numpy as np
import jax
import jax.numpy as jnp
from jax import lax
from jax.experimental import pallas as pl
from jax.experimental.pallas import tpu as pltpu

F32 = jnp.float32
BF16 = jnp.bfloat16
HIGHEST = lax.Precision.HIGHEST

NORM_EPS = 1e-6
D_MODEL = 1024
SSD_HEADS = 8
SSD_HEAD_DIM = 64
SSD_WIDTH = 512
SSD_GROUPS = 2
SSD_STATE = 128
SSD_CONV = 4
SSD_CONV_CH = 1024
ATT_HEADS = 4
ATT_DIM = 64
ATT_WIDTH = 256
MOBA_BLOCK = 256
MOBA_TOPK = 3
MEM_LEN = 256
PEER_HEADS = 8
PEER_NKEYS = 128
PEER_TOPK = 16
PEER_HALF = 64
PEER_EXPERTS = PEER_NKEYS * PEER_NKEYS

LANE = 128
NEG = -1e30
VMEM_LIMIT = 56 * 1024 * 1024

TM_IN = 512
SSD_CHUNK = 256
TQ = MOBA_BLOCK
ATT_GROUP = 4
TM_PP = 512
TT_PEER = 512
ES_PEER = 512
PEER_SLABS = 4

_SLOPES = [2.0 ** (-8.0 * (i + 1) / ATT_HEADS) for i in range(ATT_HEADS)]


def _bf16_split(v):
    hi = float(np.float32(v).astype(BF16).astype(np.float32))
    lo = float(np.float32(v - hi).astype(BF16).astype(np.float32))
    return hi, lo


def _cparams(sem):
    return pltpu.CompilerParams(dimension_semantics=sem, vmem_limit_bytes=VMEM_LIMIT)


def _sigmoid(x):
    return 1.0 / (1.0 + jnp.exp(-x))


def _full(shape):
    n = len(shape)
    return pl.BlockSpec(shape, lambda *_: (0,) * n)


def _in_proj_body(x_ref, g_ref, *refs):
    n = len(refs) // 2
    w_refs, o_refs = refs[:n], refs[n:]
    x = x_ref[...]
    ms = jnp.mean(x * x, axis=-1, keepdims=True)
    xn = ((x * lax.rsqrt(ms + NORM_EPS)) * g_ref[...]).astype(BF16)
    for w_ref, o_ref in zip(w_refs, o_refs):
        o_ref[...] = jnp.dot(xn, w_ref[...], preferred_element_type=F32).astype(o_ref.dtype)


def _in_proj(x2d, gain, weights):
    t = x2d.shape[0]
    in_specs = [pl.BlockSpec((TM_IN, D_MODEL), lambda i: (i, 0)), _full((1, D_MODEL))]
    in_specs += [_full(w.shape) for w in weights]
    out_specs = [pl.BlockSpec((TM_IN, w.shape[1]), lambda i: (i, 0)) for w in weights]
    out_shape = [jax.ShapeDtypeStruct((t, w.shape[1]), F32) for w in weights]
    return pl.pallas_call(
        _in_proj_body, grid=(t // TM_IN,), in_specs=in_specs, out_specs=out_specs,
        out_shape=out_shape, compiler_params=_cparams(("parallel",)), name="in_proj",
    )(x2d, gain, *weights)


def _ssd_body(z_ref, xbc_ref, dt_ref, cw_ref, cb_ref, dtb_ref, alog_ref, dskip_ref, nw_ref, e_ref,
              y_ref, ext_scr, state_scr):
    q = SSD_CHUNK
    c = pl.program_id(1)

    @pl.when(c == 0)
    def _():
        ext_scr[0:8, :] = jnp.zeros((8, SSD_CONV_CH), F32)
        state_scr[...] = jnp.zeros_like(state_scr)

    u = xbc_ref[0]
    ext_scr[8:8 + q, :] = u
    acc = cb_ref[...] + cw_ref[3:4, :] * u
    acc = acc + cw_ref[2:3, :] * ext_scr[7:7 + q, :]
    acc = acc + cw_ref[1:2, :] * ext_scr[6:6 + q, :]
    acc = acc + cw_ref[0:1, :] * ext_scr[5:5 + q, :]
    ext_scr[0:8, :] = u[q - 8:q, :]
    act = acc * _sigmoid(acc)
    xs = act[:, 0:SSD_WIDTH]
    bm = act[:, SSD_WIDTH:SSD_WIDTH + SSD_GROUPS * SSD_STATE]
    cm = act[:, SSD_WIDTH + SSD_GROUPS * SSD_STATE:]

    dtr = dt_ref[0] + dtb_ref[...]
    dt = jnp.maximum(dtr, 0.0) + jnp.log(1.0 + jnp.exp(-jnp.abs(dtr)))
    a = dt * (-jnp.exp(alog_ref[...]))
    row = lax.broadcasted_iota(jnp.int32, (q, q), 0)
    col = lax.broadcasted_iota(jnp.int32, (q, q), 1)
    causal = row >= col
    a_cs = jnp.dot(causal.astype(F32), a, precision=HIGHEST, preferred_element_type=F32)
    a_cs_t = a_cs.T
    a_cs_w = jnp.dot(a_cs, e_ref[...], precision=HIGHEST, preferred_element_type=F32)
    dt_w = jnp.dot(dt, e_ref[...], precision=HIGHEST, preferred_element_type=F32)
    total_w = a_cs_w[q - 1:q, :]
    exp_cs_w = jnp.exp(a_cs_w)
    dte_w = jnp.exp(total_w - a_cs_w)
    cd_w = jnp.exp(total_w)
    xdt_w = xs * dt_w
    lane = lax.broadcasted_iota(jnp.int32, (1, LANE), 1)
    first = lane < SSD_HEAD_DIM

    z = z_ref[0]
    gated = []
    for g in range(SSD_GROUPS):
        bg = bm[:, g * SSD_STATE:(g + 1) * SSD_STATE]
        cg = cm[:, g * SSD_STATE:(g + 1) * SSD_STATE].astype(BF16)
        cb = lax.dot_general(cg, bg.astype(BF16), (((1,), (1,)), ((), ())), preferred_element_type=F32)
        bg_t = bg.T.astype(BF16)
        for kk in range(2):
            k = 2 * g + kk
            sl = slice(k * LANE, (k + 1) * LANE)
            xdt = xdt_w[:, sl]
            xdt_b = xdt.astype(BF16)
            yd = []
            for hh in range(2):
                h = 2 * k + hh
                seg = a_cs[:, h:h + 1] - a_cs_t[h:h + 1, :]
                lm = jnp.exp(jnp.where(causal, seg, -jnp.inf))
                yd.append(jnp.dot((cb * lm).astype(BF16), xdt_b, preferred_element_type=F32))
            y = jnp.where(first, yd[0], yd[1])
            s_old = state_scr[k]
            y = y + jnp.dot(cg, s_old.astype(BF16), preferred_element_type=F32) * exp_cs_w[:, sl]
            y = y + xs[:, sl] * dskip_ref[:, sl]
            state_scr[k] = s_old * cd_w[:, sl] + jnp.dot(
                bg_t, (xdt * dte_w[:, sl]).astype(BF16), preferred_element_type=F32)
            zz = z[:, sl]
            gated.append(y * (zz * _sigmoid(zz)))
    for g in range(SSD_GROUPS):
        y0, y1 = gated[2 * g], gated[2 * g + 1]
        ms = (jnp.sum(y0 * y0, axis=-1, keepdims=True)
              + jnp.sum(y1 * y1, axis=-1, keepdims=True)) * (1.0 / (SSD_WIDTH // SSD_GROUPS))
        r = lax.rsqrt(ms + NORM_EPS)
        lo = 2 * g * LANE
        y_ref[0, :, lo:lo + LANE] = (y0 * r * nw_ref[:, lo:lo + LANE]).astype(y_ref.dtype)
        y_ref[0, :, lo + LANE:lo + 2 * LANE] = (y1 * r * nw_ref[:, lo + LANE:lo + 2 * LANE]).astype(y_ref.dtype)


def _ssd(z, xbc, dt, conv_w, conv_b, dt_bias, a_log, d_skip, norm_w):
    b, l, _ = z.shape
    q = SSD_CHUNK
    pad = LANE - SSD_HEADS
    dtb = jnp.pad(dt_bias, (0, pad)).reshape(1, LANE)
    alog = jnp.pad(a_log, (0, pad)).reshape(1, LANE)
    dsk = jnp.repeat(d_skip, SSD_HEAD_DIM).reshape(1, SSD_WIDTH)
    expand = (jnp.arange(LANE)[:, None] == (jnp.arange(SSD_WIDTH)[None, :] // SSD_HEAD_DIM)).astype(F32)
    tok = lambda w: pl.BlockSpec((1, q, w), lambda bi, ci: (bi, ci, 0))
    return pl.pallas_call(
        _ssd_body, grid=(b, l // q),
        in_specs=[tok(SSD_WIDTH), tok(SSD_CONV_CH), tok(LANE), _full((SSD_CONV, SSD_CONV_CH)),
                  _full((1, SSD_CONV_CH)), _full((1, LANE)), _full((1, LANE)), _full((1, SSD_WIDTH)),
                  _full((1, SSD_WIDTH)), _full((LANE, SSD_WIDTH))],
        out_specs=tok(SSD_WIDTH),
        out_shape=jax.ShapeDtypeStruct((b, l, SSD_WIDTH), BF16),
        scratch_shapes=[pltpu.VMEM((q + 8, SSD_CONV_CH), F32),
                        pltpu.VMEM((SSD_HEADS // 2, SSD_STATE, LANE), F32)],
        compiler_params=_cparams(("parallel", "arbitrary")), name="ssd",
    )(z, xbc, dt, conv_w, conv_b.reshape(1, -1), dtb, alog, dsk, norm_w.reshape(1, -1), expand)


V_ROWS = ATT_DIM + 16


def _with_ones_row(v_t):
    n = v_t.shape[1]
    r = lax.broadcasted_iota(jnp.int32, (V_ROWS - ATT_DIM, n), 0)
    return jnp.concatenate([v_t, jnp.where(r == 0, 1.0, 0.0)], axis=0)


def _head_rms(xh, w_row):
    ms = jnp.sum(xh * xh, axis=-1, keepdims=True) * (1.0 / ATT_DIM)
    return xh * lax.rsqrt(ms + NORM_EPS) * w_row


def _mem_kv_body(mem_ref, g_ref, wk_ref, wv_ref, kw_ref, k_ref, vt_ref):
    x = mem_ref[0]
    ms = jnp.mean(x * x, axis=-1, keepdims=True)
    xn = ((x * lax.rsqrt(ms + NORM_EPS)) * g_ref[...]).astype(BF16)
    kp = jnp.dot(xn, wk_ref[...], preferred_element_type=F32)
    vp = jnp.dot(xn, wv_ref[...], preferred_element_type=F32)
    for h in range(ATT_HEADS):
        sl = slice(h * LANE, (h + 1) * LANE)
        k_ref[0, h] = _head_rms(kp[:, sl], kw_ref[...]).astype(BF16)
        vt_ref[0, h] = _with_ones_row(vp[:, sl].T[0:ATT_DIM, :]).astype(BF16)


def _mem_kv(mem, gain, wk, wv, k_norm_w):
    b = mem.shape[0]
    return pl.pallas_call(
        _mem_kv_body, grid=(b,),
        in_specs=[pl.BlockSpec((1, MEM_LEN, D_MODEL), lambda i: (i, 0, 0)), _full((1, D_MODEL)),
                  _full(wk.shape), _full(wv.shape), _full((1, LANE))],
        out_specs=[pl.BlockSpec((1, ATT_HEADS, MEM_LEN, LANE), lambda i: (i, 0, 0, 0)),
                   pl.BlockSpec((1, ATT_HEADS, V_ROWS, MEM_LEN), lambda i: (i, 0, 0, 0))],
        out_shape=[jax.ShapeDtypeStruct((b, ATT_HEADS, MEM_LEN, LANE), BF16),
                   jax.ShapeDtypeStruct((b, ATT_HEADS, V_ROWS, MEM_LEN), BF16)],
        compiler_params=_cparams(("parallel",)), name="mem_kv",
    )(mem, gain, wk, wv, k_norm_w)


MAX_BLOCKS = 32
_AUG_ONEHOT = ATT_DIM
_AUG_EXTRA = ATT_DIM + MAX_BLOCKS


def _attn_prep_body(slopes, q_ref, k_ref, v_ref, xq_ref, qw_ref, kw_ref, xqw_ref,
                    qat_ref, ka_ref, kad_ref, vt_ref, xqt_ref, kmean_scr, allowed_scr):
    i = pl.program_id(1)
    nb = kmean_scr.shape[1]

    @pl.when(i == 0)
    def _():
        kmean_scr[...] = jnp.zeros_like(kmean_scr)

    n_iota = lax.broadcasted_iota(jnp.int32, (nb, TQ), 0)
    past = n_iota < i
    t_loc = lax.broadcasted_iota(jnp.int32, (32, TQ), 1).astype(F32)
    r32 = lax.broadcasted_iota(jnp.int32, (32, TQ), 0)
    k_lane = lax.broadcasted_iota(jnp.int32, (TQ, LANE), 1)
    s_loc = lax.broadcasted_iota(jnp.int32, (TQ, LANE), 0).astype(F32)
    blk = lax.convert_element_type(i, F32)
    for h in range(ATT_HEADS):
        hi, lo = slopes[h]
        sl = slice(h * LANE, (h + 1) * LANE)
        qn = _head_rms(q_ref[0, :, sl], qw_ref[...])
        kn = _head_rms(k_ref[0, :, sl], kw_ref[...])
        qn_t = qn.T
        gate = jnp.dot(kmean_scr[h], qn_t, precision=HIGHEST, preferred_element_type=F32)
        gate = jnp.where(past, gate, -jnp.inf)
        third = jnp.full((1, TQ), jnp.inf, F32)
        for _ in range(MOBA_TOPK):
            third = jnp.max(jnp.where(gate < third, gate, -jnp.inf), axis=0, keepdims=True)
        allowed_scr[...] = jnp.where(past, jnp.where(gate >= third, 1.0, 0.0), 0.0)
        want = lax.convert_element_type(jnp.minimum(i, MOBA_TOPK), F32)
        tied = jnp.max(jnp.abs(jnp.sum(allowed_scr[...], axis=0, keepdims=True) - want)) > 0.5

        @pl.when(tied)
        def _():
            cnt = jnp.zeros((nb, TQ), F32)
            for n2 in range(nb):
                gn = gate[n2:n2 + 1, :]
                ahead = jnp.where(gn > gate, 1.0, jnp.where(gn == gate, jnp.where(n_iota > n2, 1.0, 0.0), 0.0))
                cnt = cnt + ahead
            allowed_scr[...] = jnp.where(past, jnp.where(cnt < float(MOBA_TOPK), 1.0, 0.0), 0.0)

        bias = jnp.where(allowed_scr[...] > 0.5, 0.0, NEG)
        extra = jnp.where(r32 < 2, t_loc, 0.0)
        for r, val in ((2, hi), (3, lo), (4, MOBA_BLOCK * hi), (5, MOBA_BLOCK * lo)):
            extra = jnp.where(r32 == r, val, extra)
        extra = jnp.where((r32 == 6) | (r32 == 7), blk, extra)
        qat = jnp.concatenate([qn_t[0:ATT_DIM, :] * (1.0 / math.sqrt(ATT_DIM)), bias, extra], axis=0)
        qat_ref[0, h] = qat.astype(BF16)
        kx = jnp.where((k_lane == _AUG_EXTRA + 2) | (k_lane == _AUG_EXTRA + 3), s_loc, 0.0)
        kx = jnp.where((k_lane == _AUG_EXTRA + 4) | (k_lane == _AUG_EXTRA + 5), blk, kx)
        for c, val in ((0, -hi), (1, -lo), (6, -MOBA_BLOCK * hi), (7, -MOBA_BLOCK * lo)):
            kx = jnp.where(k_lane == _AUG_EXTRA + c, val, kx)
        kad = kn + kx
        kad_ref[0, h] = kad.astype(BF16)
        ka_ref[0, h] = (kad + jnp.where(k_lane - _AUG_ONEHOT == i, 1.0, 0.0)).astype(BF16)
        vt_ref[0, h] = _with_ones_row(v_ref[0, :, sl].T[0:ATT_DIM, :]).astype(BF16)
        xqn = _head_rms(xq_ref[0, :, sl], xqw_ref[...])
        xqt_ref[0, h] = (xqn.T * (1.0 / math.sqrt(ATT_DIM))).astype(BF16)
        kmean_scr[h, pl.ds(i, 1), :] = jnp.sum(kn, axis=0, keepdims=True) * (1.0 / MOBA_BLOCK)


def _attn_prep(q, k, v, xq, qw, kw, xqw):
    b, l, _ = q.shape
    nb = l // MOBA_BLOCK
    assert nb <= MAX_BLOCKS
    slopes = tuple(_bf16_split(s) for s in _SLOPES)
    tok = pl.BlockSpec((1, TQ, ATT_HEADS * LANE), lambda bi, i: (bi, i, 0))
    return pl.pallas_call(
        functools.partial(_attn_prep_body, slopes), grid=(b, nb),
        in_specs=[tok, tok, tok, tok, _full((1, LANE)), _full((1, LANE)), _full((1, LANE))],
        out_specs=[pl.BlockSpec((1, ATT_HEADS, LANE, TQ), lambda bi, i: (bi, 0, 0, i)),
                   pl.BlockSpec((1, ATT_HEADS, TQ, LANE), lambda bi, i: (bi, 0, i, 0)),
                   pl.BlockSpec((1, ATT_HEADS, TQ, LANE), lambda bi, i: (bi, 0, i, 0)),
                   pl.BlockSpec((1, ATT_HEADS, V_ROWS, TQ), lambda bi, i: (bi, 0, 0, i)),
                   pl.BlockSpec((1, ATT_HEADS, LANE, TQ), lambda bi, i: (bi, 0, 0, i))],
        out_shape=[jax.ShapeDtypeStruct((b, ATT_HEADS, LANE, l), BF16),
                   jax.ShapeDtypeStruct((b, ATT_HEADS, l, LANE), BF16),
                   jax.ShapeDtypeStruct((b, ATT_HEADS, l, LANE), BF16),
                   jax.ShapeDtypeStruct((b, ATT_HEADS, V_ROWS, l), BF16),
                   jax.ShapeDtypeStruct((b, ATT_HEADS, LANE, l), BF16)],
        scratch_shapes=[pltpu.VMEM((ATT_HEADS, MAX_BLOCKS, LANE), F32), pltpu.VMEM((MAX_BLOCKS, TQ), F32)],
        compiler_params=_cparams(("parallel", "arbitrary")), name="attn_prep",
    )(q, k, v, xq, qw, kw, xqw)


def _normalized_t(acc):
    return (acc[0:ATT_DIM, :] * (1.0 / acc[ATT_DIM:ATT_DIM + 1, :])).T


def _attn_body(qat_ref, ka_ref, kad_ref, vt_ref, xqt_ref, mk_ref, mvt_ref, o_ref, om_ref, s_scr, p_scr):
    i = pl.program_id(1)
    base = pl.multiple_of(i * MOBA_BLOCK, MOBA_BLOCK)
    key = lax.broadcasted_iota(jnp.int32, (MOBA_BLOCK, TQ), 0)
    qry = lax.broadcasted_iota(jnp.int32, (MOBA_BLOCK, TQ), 1)

    span = ATT_GROUP * MOBA_BLOCK
    n_groups = (i + ATT_GROUP - 1) // ATT_GROUP
    last_group = ka_ref.shape[2] // span - 1

    def scores(h, g):
        off = pl.multiple_of(jnp.minimum(g, last_group) * span, span)
        return jnp.dot(ka_ref[0, h, pl.ds(off, span), :], qat_ref[0, h], preferred_element_type=F32)

    def values(h, g, p):
        off = pl.multiple_of(jnp.maximum(g, 0) * span, span)
        return jnp.dot(vt_ref[0, h, :, pl.ds(off, span)], p, preferred_element_type=F32)

    init = []
    for h in range(ATT_HEADS):
        s = jnp.dot(kad_ref[0, h], qat_ref[0, h], preferred_element_type=F32)
        s = jnp.where(key <= qry, s, NEG)
        m = jnp.max(s, axis=0, keepdims=True)
        p = jnp.exp(s - m).astype(BF16)
        acc = jnp.dot(vt_ref[0, h, :, pl.ds(base, MOBA_BLOCK)], p, preferred_element_type=F32)
        s_scr[h] = scores(h, 0)
        p_scr[h] = jnp.zeros((span, TQ), BF16)
        init += [jnp.ones((1, TQ), F32), m, acc]

    def body(g, carry):
        out = []
        for h in range(ATT_HEADS):
            alpha_prev, m, acc = carry[3 * h:3 * h + 3]
            acc = acc * alpha_prev + values(h, g - 1, p_scr[h])
            s = s_scr[h]
            m_new = jnp.maximum(m, jnp.max(s, axis=0, keepdims=True))
            p_scr[h] = jnp.exp(s - m_new).astype(BF16)
            s_scr[h] = scores(h, g + 1)
            out += [jnp.exp(m - m_new), m_new, acc]
        return tuple(out)

    fin = lax.fori_loop(0, n_groups, body, tuple(init))
    for h in range(ATT_HEADS):
        alpha_prev, _, acc = fin[3 * h:3 * h + 3]
        o_ref[0, h] = _normalized_t(acc * alpha_prev + values(h, n_groups - 1, p_scr[h]))
        sm = jnp.dot(mk_ref[0, h], xqt_ref[0, h], preferred_element_type=F32)
        pm = jnp.exp(sm - jnp.max(sm, axis=0, keepdims=True)).astype(BF16)
        om_ref[0, h] = _normalized_t(jnp.dot(mvt_ref[0, h], pm, preferred_element_type=F32))


def _attn(qat, ka, kad, vt, xqt, mem_k, mem_vt):
    b, _, _, l = qat.shape
    assert l % (ATT_GROUP * MOBA_BLOCK) == 0
    nh = ATT_HEADS
    per_q = lambda r: pl.BlockSpec((1, nh, r, TQ), lambda bi, i: (bi, 0, 0, i))
    out = pl.BlockSpec((1, nh, TQ, ATT_DIM), lambda bi, i: (bi, 0, i, 0))
    return pl.pallas_call(
        _attn_body, grid=(b, l // TQ),
        in_specs=[per_q(LANE),
                  pl.BlockSpec((1, nh, l, LANE), lambda bi, i: (bi, 0, 0, 0)),
                  pl.BlockSpec((1, nh, TQ, LANE), lambda bi, i: (bi, 0, i, 0)),
                  pl.BlockSpec((1, nh, V_ROWS, l), lambda bi, i: (bi, 0, 0, 0)),
                  per_q(LANE),
                  pl.BlockSpec((1, nh, MEM_LEN, LANE), lambda bi, i: (bi, 0, 0, 0)),
                  pl.BlockSpec((1, nh, V_ROWS, MEM_LEN), lambda bi, i: (bi, 0, 0, 0))],
        out_specs=[out, out],
        out_shape=[jax.ShapeDtypeStruct((b, nh, l, ATT_DIM), F32)] * 2,
        scratch_shapes=[pltpu.VMEM((nh, ATT_GROUP * MOBA_BLOCK, TQ), F32),
                        pltpu.VMEM((nh, ATT_GROUP * MOBA_BLOCK, TQ), BF16)],
        compiler_params=_cparams(("parallel", "arbitrary")), name="attn",
    )(qat, ka, kad, vt, xqt, mem_k, mem_vt)


def _cand_layout():
    slabs = [("r1", 0, 16), ("r1", 1, 8), ("r1", 2, 8), ("r1", 3, 8), ("r2", 0, 16), ("r2", 1, 8), ("r2", 2, 8)]
    pos, valid = [], []
    for kind, fixed, n in slabs:
        for j in range(n):
            r1, r2 = (fixed, j) if kind == "r1" else (j, fixed)
            ok = (r1 + 1) * (r2 + 1) <= PEER_TOPK and ((kind == "r1") or r1 >= 4)
            pos.append(r1 * PEER_TOPK + r2)
            valid.append(ok)
    assert sum(valid) == 50
    return slabs, np.asarray(pos, np.int32), np.asarray(valid)


_CAND_SLABS, _CAND_POS, _CAND_VALID = _cand_layout()
_NCAND = len(_CAND_POS)


def _split_cols(k):
    hi = k.astype(BF16)
    lo = (k - hi.astype(F32)).astype(BF16)
    return jnp.concatenate([hi, hi, lo], axis=1)


def _dot_split(k_cat, q):
    hi = q.astype(BF16)
    lo = (q - hi.astype(F32)).astype(BF16)
    return jnp.dot(k_cat, jnp.concatenate([hi, lo, hi], axis=0), preferred_element_type=F32)


def _pack_rows(x):
    return pltpu.bitcast(x.astype(BF16), jnp.uint32)


def _unpack_rows(x):
    return pltpu.bitcast(x, BF16)


def _top16_exact(sets):
    row = lax.broadcasted_iota(jnp.int32, (PEER_NKEYS, LANE), 0).astype(F32)
    for _, rank_scr, _ in sets:
        rank_scr[...] = jnp.full(rank_scr.shape, float(PEER_TOPK), F32)

    def body(r, _):
        rf = lax.convert_element_type(r, F32)
        for half in range(TM_PP // LANE):
            ls = slice(half * LANE, (half + 1) * LANE)
            for w_scr, rank_scr, v_scr in sets:
                s = w_scr[:, ls]
                m = jnp.max(s, axis=0, keepdims=True)
                hit = row == jnp.min(jnp.where(s == m, row, float(PEER_NKEYS)), axis=0, keepdims=True)
                w_scr[:, ls] = jnp.where(hit, -jnp.inf, s)
                v_scr[half, pl.ds(r, 1), :] = m
                rank_scr[:, ls] = jnp.where(hit, rf, rank_scr[:, ls])
        return 0

    lax.fori_loop(0, PEER_TOPK, body, 0)


def _top16_distinct(sets):
    chains = [(w_scr, rank_scr, v_scr, half) for half in range(TM_PP // LANE) for w_scr, rank_scr, v_scr in sets]

    def body(r, prev):
        out = []
        for (w_scr, _, v_scr, half), m_prev in zip(chains, prev):
            s = w_scr[:, half * LANE:(half + 1) * LANE]
            m = jnp.max(jnp.where(s < m_prev, s, -jnp.inf), axis=0, keepdims=True)
            v_scr[half, pl.ds(r, 1), :] = m
            out.append(m)
        return tuple(out)

    lax.fori_loop(0, PEER_TOPK, body, tuple(jnp.full((1, LANE), jnp.inf, F32) for _ in chains))
    ranked = []
    for half in range(TM_PP // LANE):
        ls = slice(half * LANE, (half + 1) * LANE)
        total = jnp.zeros((1, LANE), F32)
        for w_scr, rank_scr, v_scr in sets:
            s = w_scr[:, ls]
            v = v_scr[half]
            if rank_scr is None:
                hit = s >= v[PEER_TOPK - 1:PEER_TOPK, :]
            else:
                rank = jnp.full(s.shape, float(PEER_TOPK), F32)
                for r in range(PEER_TOPK):
                    rank = jnp.where(s == v[r:r + 1, :], float(r), rank)
                rank_scr[:, ls] = rank
                hit = rank < float(PEER_TOPK)
            total = total + jnp.sum(jnp.where(hit, 1.0, 0.0), axis=0, keepdims=True)
        ranked.append(total)
    return jnp.concatenate(ranked, axis=1)


def _peer_prep_body(x_ref, ys_ref, ym_ref, yx_ref, wos_ref, wom_ref, wox_ref, g_ref, wqt_ref,
                    k1_ref, k2_ref, pos_ref, cbias_ref,
                    h1_ref, hnt_ref, r2_ref, f_ref, n_ref, c_ref,
                    q_scr, w1_scr, w2_scr, rank1_scr, rank2_scr, v1_scr, v2_scr, cnt_scr, z_scr):
    hres = x_ref[...] + jnp.dot(ys_ref[...], wos_ref[...], preferred_element_type=F32)
    for h in range(ATT_HEADS):
        hres = hres + jnp.dot(ym_ref[0, h].astype(BF16), wom_ref[h], preferred_element_type=F32)
        hres = hres + jnp.dot(yx_ref[0, h].astype(BF16), wox_ref[h], preferred_element_type=F32)
    h1_ref[...] = hres
    ms = jnp.mean(hres * hres, axis=-1, keepdims=True)
    hn = (hres * lax.rsqrt(ms + NORM_EPS)) * g_ref[...]
    hn_t = hn.T.astype(BF16)
    hnt_ref[...] = hn_t
    q_scr[...] = jnp.dot(wqt_ref[...], hn_t, preferred_element_type=F32)

    pos = pos_ref[...]
    cbias = cbias_ref[...]
    r16 = lax.broadcasted_iota(jnp.int32, (PEER_TOPK, TM_PP), 0).astype(F32)
    halves = lambda scr: jnp.concatenate([scr[j] for j in range(TM_PP // LANE)], axis=1)

    def head(h, _):
        base = pl.multiple_of(h * (2 * PEER_HALF), 2 * PEER_HALF)
        s1 = _dot_split(k1_ref[...], q_scr[pl.ds(base, PEER_HALF), :])
        s2 = _dot_split(k2_ref[...], q_scr[pl.ds(base + PEER_HALF, PEER_HALF), :])
        sets = [(w1_scr, rank1_scr, v1_scr), (w2_scr, rank2_scr, v2_scr)]
        w1_scr[...] = s1
        w2_scr[...] = s2
        ranked = _top16_distinct([(w1_scr, None, v1_scr), (w2_scr, rank2_scr, v2_scr)])
        tied = jnp.max(jnp.abs(ranked - 2.0 * PEER_TOPK)) > 0.5

        @pl.when(tied)
        def _():
            _top16_exact(sets)

        rank2 = rank2_scr[...]
        v1 = halves(v1_scr)
        v2 = halves(v2_scr)
        parts = []
        for kind, fixed, n in _CAND_SLABS:
            if kind == "r1":
                parts.append(v1[fixed:fixed + 1, :] + v2[0:n, :])
            else:
                parts.append(v1[0:n, :] + v2[fixed:fixed + 1, :])
        cand = jnp.concatenate(parts, axis=0) + cbias
        top = v1[0:1, :] + v2[0:1, :]

        def descend(_, carry):
            m_prev, zsum = carry
            m = jnp.max(jnp.where(cand < m_prev, cand, -jnp.inf), axis=0, keepdims=True)
            return m, zsum + jnp.exp(m - top)

        tau, zsum = lax.fori_loop(0, PEER_TOPK, descend,
                                  (jnp.full((1, TM_PP), jnp.inf, F32), jnp.zeros((1, TM_PP), F32)))
        taken = jnp.where(cand >= tau, 1.0, 0.0)
        cnt = jnp.zeros((PEER_TOPK, TM_PP), F32)
        row0 = 0
        for kind, fixed, n_rows in _CAND_SLABS:
            blk = taken[row0:row0 + n_rows, :]
            if kind == "r1":
                cnt = cnt + jnp.where(r16 == float(fixed), jnp.sum(blk, axis=0, keepdims=True), 0.0)
            elif n_rows == PEER_TOPK:
                cnt = cnt + blk
            else:
                cnt = cnt + jnp.concatenate([blk, jnp.zeros((PEER_TOPK - n_rows, TM_PP), F32)], axis=0)
            row0 += n_rows
        cnt_scr[...] = cnt
        z_scr[...] = zsum
        tied2 = jnp.max(jnp.abs(jnp.sum(cnt, axis=0, keepdims=True) - float(PEER_TOPK))) > 0.5

        @pl.when(tied2)
        def _():
            def pick(_, carry):
                cand, cnt, zsum = carry
                m = jnp.max(cand, axis=0, keepdims=True)
                p = jnp.min(jnp.where(cand == m, pos, float(4 * PEER_TOPK * PEER_TOPK)), axis=0, keepdims=True)
                cand = jnp.where(pos == p, -jnp.inf, cand)
                cnt = cnt + jnp.where(r16 == jnp.floor(p * (1.0 / PEER_TOPK)), 1.0, 0.0)
                return cand, cnt, zsum + jnp.exp(m - top)

            _, cnt, zsum = lax.fori_loop(
                0, PEER_TOPK, pick, (cand, jnp.zeros((PEER_TOPK, TM_PP), F32), jnp.zeros((1, TM_PP), F32)))
            cnt_scr[...] = cnt
            z_scr[...] = zsum

        cnt = cnt_scr[...]
        zsum = z_scr[...]
        def write_nsel(is_rank_r):
            nsel = jnp.zeros((PEER_NKEYS, TM_PP), F32)
            for r in range(PEER_TOPK):
                nsel = jnp.where(is_rank_r(r), cnt[r:r + 1, :], nsel)
            for lc in range(TM_PP // LANE):
                n_ref[h, lc] = nsel[:, lc * LANE:(lc + 1) * LANE]

        @pl.when(tied)
        def _():
            rank1 = rank1_scr[...]
            write_nsel(lambda r: rank1 == float(r))

        @pl.when(jnp.logical_not(tied))
        def _():
            write_nsel(lambda r: s1 == v1[r:r + 1, :])

        r2_ref[h] = _pack_rows(rank2)
        f_ref[h] = _pack_rows(jnp.exp(s2 - v2[0:1, :]))
        cw = jnp.exp(s1 - v1[0:1, :]) * (0.5 / zsum)
        for lc in range(TM_PP // LANE):
            c_ref[h, lc] = cw[:, lc * LANE:(lc + 1) * LANE]
        return 0

    lax.fori_loop(0, PEER_HEADS, head, 0)


def _peer_prep(x2d, y_ssd, y_moba, y_mem, wo_ssd, wo_moba, wo_mem, gain, wq_t, k1, k2):
    t = x2d.shape[0]
    b, _, l, _ = y_moba.shape
    per_b = l // TM_PP
    pos = jnp.asarray(np.broadcast_to(_CAND_POS.astype(np.float32)[:, None], (_NCAND, TM_PP)))
    cbias = jnp.asarray(np.broadcast_to(np.where(_CAND_VALID, 0.0, -np.inf).astype(np.float32)[:, None],
                                        (_NCAND, TM_PP)))
    tok = lambda w: pl.BlockSpec((TM_PP, w), lambda i: (i, 0))
    att = pl.BlockSpec((1, ATT_HEADS, TM_PP, ATT_DIM), lambda i: (i // per_b, 0, i % per_b, 0))
    meta = pl.BlockSpec((PEER_HEADS, TM_PP // LANE, PEER_NKEYS, LANE), lambda i: (0, i, 0, 0))
    meta_f32 = jax.ShapeDtypeStruct((PEER_HEADS, t // LANE, PEER_NKEYS, LANE), F32)
    packed = pl.BlockSpec((PEER_HEADS, PEER_NKEYS // 2, TM_PP), lambda i: (0, 0, i))
    meta_pk = jax.ShapeDtypeStruct((PEER_HEADS, PEER_NKEYS // 2, t), jnp.uint32)
    return pl.pallas_call(
        _peer_prep_body, grid=(t // TM_PP,),
        in_specs=[tok(D_MODEL), tok(SSD_WIDTH), att, att, _full(wo_ssd.shape), _full(wo_moba.shape),
                  _full(wo_mem.shape), _full((1, D_MODEL)), _full(wq_t.shape), _full(k1.shape), _full(k2.shape),
                  _full(pos.shape), _full(cbias.shape)],
        out_specs=[tok(D_MODEL), pl.BlockSpec((D_MODEL, TM_PP), lambda i: (0, i)), packed, packed, meta, meta],
        out_shape=[jax.ShapeDtypeStruct((t, D_MODEL), F32), jax.ShapeDtypeStruct((D_MODEL, t), BF16),
                   meta_pk, meta_pk, meta_f32, meta_f32],
        scratch_shapes=[pltpu.VMEM((PEER_HEADS * 2 * PEER_HALF, TM_PP), F32)]
        + [pltpu.VMEM((PEER_NKEYS, TM_PP), F32)] * 4
        + [pltpu.VMEM((TM_PP // LANE, PEER_TOPK, LANE), F32)] * 2
        + [pltpu.VMEM((PEER_TOPK, TM_PP), F32), pltpu.VMEM((1, TM_PP), F32)],
        compiler_params=_cparams(("parallel",)), name="peer_prep",
    )(x2d, y_ssd, y_moba, y_mem, wo_ssd, wo_moba, wo_mem, gain, wq_t, k1, k2, pos, cbias)


_A_PER_SLAB = ES_PEER // PEER_NKEYS
_PEER_STEPS = PEER_EXPERTS // (PEER_SLABS * ES_PEER)


def _peer_gate_chunk(s_ref, a_ref, a_base, k, lc, r2_ref, f_ref, n_ref, c_ref):
    zero = jnp.zeros((PEER_NKEYS, LANE), BF16)

    def bcast(row):
        tile = jnp.broadcast_to(row, (16, LANE)).astype(BF16)
        return jnp.concatenate([tile] * (PEER_NKEYS // 16), axis=0)

    a = a_base + k
    rows = slice(k * PEER_NKEYS, (k + 1) * PEER_NKEYS)
    ls = slice(lc * LANE, (lc + 1) * LANE)
    s = s_ref[rows, ls]
    act = s * (1.0 + lax.erf(s * math.sqrt(0.5)))
    g = zero
    for h in range(PEER_HEADS):
        nrow = bcast(n_ref[h, lc, pl.ds(a, 1), :])
        crow = bcast(c_ref[h, lc, pl.ds(a, 1), :])
        g = g + jnp.where(_unpack_rows(r2_ref[h, :, ls]) < nrow, _unpack_rows(f_ref[h, :, ls]) * crow, zero)
    a_ref[k * (PEER_NKEYS // 2):(k + 1) * (PEER_NKEYS // 2), ls] = pltpu.bitcast(act.astype(BF16) * g, jnp.uint32)


def _peer_body(*refs):
    s_n = PEER_SLABS
    wd0_ref = refs[0]
    wd_next = refs[1:1 + s_n]
    wut_prev = refs[1 + s_n:1 + 2 * s_n]
    (wut_last_ref, hnt_ref, r2_ref, f_ref, n_ref, c_ref, h1_ref, o_ref,
     acc_scr, s_scr, a_scr) = refs[1 + 2 * s_n:]
    n = pl.program_id(1)
    meta = (r2_ref, f_ref, n_ref, c_ref)

    @pl.when(n == 0)
    def _():
        acc_scr[...] = jnp.zeros_like(acc_scr)
        a_scr[s_n - 1] = jnp.zeros(a_scr.shape[1:], a_scr.dtype)
        s_scr[0] = jnp.dot(wd0_ref[...], hnt_ref[...], preferred_element_type=F32)

    for j in range(s_n):
        for half in range(2):
            ls = slice(half * (TT_PEER // 2), (half + 1) * (TT_PEER // 2))
            acc_scr[:, ls] += jnp.dot(wut_prev[j][...], _unpack_rows(a_scr[(j - 1) % s_n, :, ls]),
                                      preferred_element_type=F32)
            s_scr[(j + 1) % s_n, :, ls] = jnp.dot(wd_next[j][...], hnt_ref[:, ls],
                                                  preferred_element_type=F32)
        a_base = (n * s_n + j) * _A_PER_SLAB
        for lc in range(TT_PEER // LANE):
            for k in range(_A_PER_SLAB):
                _peer_gate_chunk(s_scr.at[j], a_scr.at[j], a_base, k, lc, *meta)

    @pl.when(n == pl.num_programs(1) - 1)
    def _():
        tail = jnp.dot(wut_last_ref[...], _unpack_rows(a_scr[s_n - 1]), preferred_element_type=F32)
        o_ref[...] = h1_ref[...] + (acc_scr[...] + tail).T


def _peer(wd, wu_t, hn_t, rank2, f, nsel, c, h1):
    t = h1.shape[0]
    s_n = PEER_SLABS
    nblk = s_n * _PEER_STEPS
    meta = pl.BlockSpec((PEER_HEADS, TT_PEER // LANE, PEER_NKEYS, LANE), lambda i, n: (0, i, 0, 0))
    packed = pl.BlockSpec((PEER_HEADS, PEER_NKEYS // 2, TT_PEER), lambda i, n: (0, 0, i))

    def wd_blk(off):
        return pl.BlockSpec((ES_PEER, D_MODEL), lambda i, n: (jnp.clip(s_n * n + off, 0, nblk - 1), 0))

    def wut_blk(off):
        return pl.BlockSpec((None, D_MODEL, ES_PEER), lambda i, n: (jnp.clip(s_n * n + off, 0, nblk - 1), 0, 0))

    return pl.pallas_call(
        _peer_body, grid=(t // TT_PEER, _PEER_STEPS),
        in_specs=[pl.BlockSpec((ES_PEER, D_MODEL), lambda i, n: (0, 0))]
        + [wd_blk(j + 1) for j in range(s_n)] + [wut_blk(j - 1) for j in range(s_n)]
        + [pl.BlockSpec((None, D_MODEL, ES_PEER), lambda i, n: (nblk - 1, 0, 0)),
           pl.BlockSpec((D_MODEL, TT_PEER), lambda i, n: (0, i)), packed, packed, meta, meta,
           pl.BlockSpec((TT_PEER, D_MODEL), lambda i, n: (i, 0))],
        out_specs=pl.BlockSpec((TT_PEER, D_MODEL), lambda i, n: (i, 0)),
        out_shape=jax.ShapeDtypeStruct((t, D_MODEL), F32),
        scratch_shapes=[pltpu.VMEM((D_MODEL, TT_PEER), F32),
                        pltpu.VMEM((s_n, ES_PEER, TT_PEER), F32),
                        pltpu.VMEM((s_n, ES_PEER // 2, TT_PEER), jnp.uint32)],
        compiler_params=_cparams(("parallel", "arbitrary")), name="peer",
    )(*([wd] * (s_n + 1) + [wu_t] * (s_n + 1) + [hn_t, rank2, f, nsel, c, h1]))


def _pad_heads(w):
    r = w.shape[0]
    w = w.reshape(r, -1, ATT_DIM)
    return jnp.pad(w, ((0, 0), (0, 0), (0, LANE - ATT_DIM))).reshape(r, -1)


def _pad_row(w):
    return jnp.pad(w, (0, LANE - w.shape[0])).reshape(1, LANE)


def kernel(x, mem, mix_norm_w, w_in, ssd_conv_w, ssd_conv_b, ssd_dt_bias, ssd_a_log, ssd_d, ssd_norm_w,
           moba_q_norm_w, moba_k_norm_w, mem_norm_w, w_mem_kv, xattn_q_norm_w, xattn_k_norm_w, w_out,
           ffn_norm_w, peer_w_query, peer_sub_keys_1, peer_sub_keys_2, peer_expert_down, peer_expert_up):
    b, l, d = x.shape
    depth = w_in.shape[0]
    h = x.reshape(b * l, d)
    for li in range(depth):
        wi = w_in[li]
        o = 0
        cols = {}
        for name, width in (("z", SSD_WIDTH), ("xbc", SSD_CONV_CH), ("dt", SSD_HEADS), ("mq", ATT_WIDTH),
                            ("mk", ATT_WIDTH), ("mv", ATT_WIDTH), ("xq", ATT_WIDTH)):
            cols[name] = wi[:, o:o + width]
            o += width
        w_list = [cols["z"], cols["xbc"], jnp.pad(cols["dt"], ((0, 0), (0, LANE - SSD_HEADS))),
                  _pad_heads(cols["mq"]), _pad_heads(cols["mk"]), _pad_heads(cols["mv"]), _pad_heads(cols["xq"])]
        w_list = [w.astype(BF16) for w in w_list]
        z, xbc, dt, mq, mk, mv, xq = _in_proj(h, mix_norm_w[li].reshape(1, d), w_list)

        y_ssd = _ssd(z.reshape(b, l, -1), xbc.reshape(b, l, -1), dt.reshape(b, l, -1), ssd_conv_w[li],
                     ssd_conv_b[li], ssd_dt_bias[li], ssd_a_log[li], ssd_d[li], ssd_norm_w[li])

        wkv = w_mem_kv[li]
        mem_k, mem_vt = _mem_kv(mem, mem_norm_w[li].reshape(1, d), _pad_heads(wkv[:, :ATT_WIDTH]).astype(BF16),
                                _pad_heads(wkv[:, ATT_WIDTH:]).astype(BF16), _pad_row(xattn_k_norm_w[li]))
        r3 = lambda a: a.reshape(b, l, -1)
        qat, ka, kad, vt, xqt = _attn_prep(r3(mq), r3(mk), r3(mv), r3(xq), _pad_row(moba_q_norm_w[li]),
                                           _pad_row(moba_k_norm_w[li]), _pad_row(xattn_q_norm_w[li]))
        y_moba, y_mem = _attn(qat, ka, kad, vt, xqt, mem_k, mem_vt)

        wo = w_out[li].astype(BF16)
        wo_moba = wo[SSD_WIDTH:SSD_WIDTH + ATT_WIDTH].reshape(ATT_HEADS, ATT_DIM, d)
        wo_mem = wo[SSD_WIDTH + ATT_WIDTH:].reshape(ATT_HEADS, ATT_DIM, d)
        h1, hn_t, rank2, f, nsel, c = _peer_prep(
            h, y_ssd.reshape(b * l, -1), y_moba, y_mem, wo[:SSD_WIDTH], wo_moba, wo_mem,
            ffn_norm_w[li].reshape(1, d), peer_w_query[li].T.astype(BF16), _split_cols(peer_sub_keys_1[li]),
            _split_cols(peer_sub_keys_2[li]))
        wu_t = peer_expert_up[li].reshape(-1, ES_PEER, d).transpose(0, 2, 1).astype(BF16)
        h = _peer(peer_expert_down[li].astype(BF16), wu_t, hn_t, rank2, f, nsel, c, h1)
    return h.reshape(b, l, d)
```

```python
import functools
import math

import numpy as np
import jax
import jax.numpy as jnp
from jax import lax
from jax.experimental import pallas as pl
from jax.experimental.pallas import tpu as pltpu

F32 = jnp.float32
BF16 = jnp.bfloat16
HIGHEST = lax.Precision.HIGHEST

NORM_EPS = 1e-6
D_MODEL = 1024
SSD_HEADS = 8
SSD_HEAD_DIM = 64
SSD_WIDTH = 512
SSD_GROUPS = 2
SSD_STATE = 128
SSD_CONV = 4
SSD_CONV_CH = 1024
ATT_HEADS = 4
ATT_DIM = 64
ATT_WIDTH = 256
MOBA_BLOCK = 256
MOBA_TOPK = 3
MEM_LEN = 256
PEER_HEADS = 8
PEER_NKEYS = 128
PEER_TOPK = 16
PEER_HALF = 64
PEER_EXPERTS = PEER_NKEYS * PEER_NKEYS

LANE = 128
NEG = -1e30
VMEM_LIMIT = 56 * 1024 * 1024

TM_IN = 512
SSD_CHUNK = 256
TQ = MOBA_BLOCK
ATT_GROUP = 4
TM_PP = 512
TT_PEER = 512
ES_PEER = 512
PEER_SLABS = 4

_SLOPES = [2.0 ** (-8.0 * (i + 1) / ATT_HEADS) for i in range(ATT_HEADS)]


def _bf16_split(v):
    hi = float(np.float32(v).astype(BF16).astype(np.float32))
    lo = float(np.float32(v - hi).astype(BF16).astype(np.float32))
    return hi, lo


def _cparams(sem):
    return pltpu.CompilerParams(dimension_semantics=sem, vmem_limit_bytes=VMEM_LIMIT)


def _sigmoid(x):
    return 1.0 / (1.0 + jnp.exp(-x))


def _full(shape):
    n = len(shape)
    return pl.BlockSpec(shape, lambda *_: (0,) * n)


def _in_proj_body(x_ref, g_ref, *refs):
    n = len(refs) // 2
    w_refs, o_refs = refs[:n], refs[n:]
    x = x_ref[...]
    ms = jnp.mean(x * x, axis=-1, keepdims=True)
    xn = ((x * lax.rsqrt(ms + NORM_EPS)) * g_ref[...]).astype(BF16)
    for w_ref, o_ref in zip(w_refs, o_refs):
        o_ref[...] = jnp.dot(xn, w_ref[...], preferred_element_type=F32).astype(o_ref.dtype)


def _in_proj(x2d, gain, weights):
    t = x2d.shape[0]
    in_specs = [pl.BlockSpec((TM_IN, D_MODEL), lambda i: (i, 0)), _full((1, D_MODEL))]
    in_specs += [_full(w.shape) for w in weights]
    out_specs = [pl.BlockSpec((TM_IN, w.shape[1]), lambda i: (i, 0)) for w in weights]
    out_shape = [jax.ShapeDtypeStruct((t, w.shape[1]), F32) for w in weights]
    return pl.pallas_call(
        _in_proj_body, grid=(t // TM_IN,), in_specs=in_specs, out_specs=out_specs,
        out_shape=out_shape, compiler_params=_cparams(("parallel",)), name="in_proj",
    )(x2d, gain, *weights)


def _ssd_body(z_ref, xbc_ref, dt_ref, cw_ref, cb_ref, dtb_ref, alog_ref, dskip_ref, nw_ref, e_ref,
              y_ref, ext_scr, state_scr):
    q = SSD_CHUNK
    c = pl.program_id(1)

    @pl.when(c == 0)
    def _():
        ext_scr[0:8, :] = jnp.zeros((8, SSD_CONV_CH), F32)
        state_scr[...] = jnp.zeros_like(state_scr)

    u = xbc_ref[0]
    ext_scr[8:8 + q, :] = u
    acc = cb_ref[...] + cw_ref[3:4, :] * u
    acc = acc + cw_ref[2:3, :] * ext_scr[7:7 + q, :]
    acc = acc + cw_ref[1:2, :] * ext_scr[6:6 + q, :]
    acc = acc + cw_ref[0:1, :] * ext_scr[5:5 + q, :]
    ext_scr[0:8, :] = u[q - 8:q, :]
    act = acc * _sigmoid(acc)
    xs = act[:, 0:SSD_WIDTH]
    bm = act[:, SSD_WIDTH:SSD_WIDTH + SSD_GROUPS * SSD_STATE]
    cm = act[:, SSD_WIDTH + SSD_GROUPS * SSD_STATE:]

    dtr = dt_ref[0] + dtb_ref[...]
    dt = jnp.maximum(dtr, 0.0) + jnp.log(1.0 + jnp.exp(-jnp.abs(dtr)))
    a = dt * (-jnp.exp(alog_ref[...]))
    row = lax.broadcasted_iota(jnp.int32, (q, q), 0)
    col = lax.broadcasted_iota(jnp.int32, (q, q), 1)
    causal = row >= col
    a_cs = jnp.dot(causal.astype(F32), a, precision=HIGHEST, preferred_element_type=F32)
    a_cs_t = a_cs.T
    a_cs_w = jnp.dot(a_cs, e_ref[...], precision=HIGHEST, preferred_element_type=F32)
    dt_w = jnp.dot(dt, e_ref[...], precision=HIGHEST, preferred_element_type=F32)
    total_w = a_cs_w[q - 1:q, :]
    exp_cs_w = jnp.exp(a_cs_w)
    dte_w = jnp.exp(total_w - a_cs_w)
    cd_w = jnp.exp(total_w)
    xdt_w = xs * dt_w
    lane = lax.broadcasted_iota(jnp.int32, (1, LANE), 1)
    first = lane < SSD_HEAD_DIM

    z = z_ref[0]
    gated = []
    for g in range(SSD_GROUPS):
        bg = bm[:, g * SSD_STATE:(g + 1) * SSD_STATE]
        cg = cm[:, g * SSD_STATE:(g + 1) * SSD_STATE].astype(BF16)
        cb = lax.dot_general(cg, bg.astype(BF16), (((1,), (1,)), ((), ())), preferred_element_type=F32)
        bg_t = bg.T.astype(BF16)
        for kk in range(2):
            k = 2 * g + kk
            sl = slice(k * LANE, (k + 1) * LANE)
            xdt = xdt_w[:, sl]
            xdt_b = xdt.astype(BF16)
            yd = []
            for hh in range(2):
                h = 2 * k + hh
                seg = a_cs[:, h:h + 1] - a_cs_t[h:h + 1, :]
                lm = jnp.exp(jnp.where(causal, seg, -jnp.inf))
                yd.append(jnp.dot((cb * lm).astype(BF16), xdt_b, preferred_element_type=F32))
            y = jnp.where(first, yd[0], yd[1])
            s_old = state_scr[k]
            y = y + jnp.dot(cg, s_old.astype(BF16), preferred_element_type=F32) * exp_cs_w[:, sl]
            y = y + xs[:, sl] * dskip_ref[:, sl]
            state_scr[k] = s_old * cd_w[:, sl] + jnp.dot(
                bg_t, (xdt * dte_w[:, sl]).astype(BF16), preferred_element_type=F32)
            zz = z[:, sl]
            gated.append(y * (zz * _sigmoid(zz)))
    for g in range(SSD_GROUPS):
        y0, y1 = gated[2 * g], gated[2 * g + 1]
        ms = (jnp.sum(y0 * y0, axis=-1, keepdims=True)
              + jnp.sum(y1 * y1, axis=-1, keepdims=True)) * (1.0 / (SSD_WIDTH // SSD_GROUPS))
        r = lax.rsqrt(ms + NORM_EPS)
        lo = 2 * g * LANE
        y_ref[0, :, lo:lo + LANE] = (y0 * r * nw_ref[:, lo:lo + LANE]).astype(y_ref.dtype)
        y_ref[0, :, lo + LANE:lo + 2 * LANE] = (y1 * r * nw_ref[:, lo + LANE:lo + 2 * LANE]).astype(y_ref.dtype)


def _ssd(z, xbc, dt, conv_w, conv_b, dt_bias, a_log, d_skip, norm_w):
    b, l, _ = z.shape
    q = SSD_CHUNK
    pad = LANE - SSD_HEADS
    dtb = jnp.pad(dt_bias, (0, pad)).reshape(1, LANE)
    alog = jnp.pad(a_log, (0, pad)).reshape(1, LANE)
    dsk = jnp.repeat(d_skip, SSD_HEAD_DIM).reshape(1, SSD_WIDTH)
    expand = (jnp.arange(LANE)[:, None] == (jnp.arange(SSD_WIDTH)[None, :] // SSD_HEAD_DIM)).astype(F32)
    tok = lambda w: pl.BlockSpec((1, q, w), lambda bi, ci: (bi, ci, 0))
    return pl.pallas_call(
        _ssd_body, grid=(b, l // q),
        in_specs=[tok(SSD_WIDTH), tok(SSD_CONV_CH), tok(LANE), _full((SSD_CONV, SSD_CONV_CH)),
                  _full((1, SSD_CONV_CH)), _full((1, LANE)), _full((1, LANE)), _full((1, SSD_WIDTH)),
                  _full((1, SSD_WIDTH)), _full((LANE, SSD_WIDTH))],
        out_specs=tok(SSD_WIDTH),
        out_shape=jax.ShapeDtypeStruct((b, l, SSD_WIDTH), BF16),
        scratch_shapes=[pltpu.VMEM((q + 8, SSD_CONV_CH), F32),
                        pltpu.VMEM((SSD_HEADS // 2, SSD_STATE, LANE), F32)],
        compiler_params=_cparams(("parallel", "arbitrary")), name="ssd",
    )(z, xbc, dt, conv_w, conv_b.reshape(1, -1), dtb, alog, dsk, norm_w.reshape(1, -1), expand)


V_ROWS = ATT_DIM + 16


def _with_ones_row(v_t):
    n = v_t.shape[1]
    r = lax.broadcasted_iota(jnp.int32, (V_ROWS - ATT_DIM, n), 0)
    return jnp.concatenate([v_t, jnp.where(r == 0, 1.0, 0.0)], axis=0)


def _head_rms(xh, w_row):
    ms = jnp.sum(xh * xh, axis=-1, keepdims=True) * (1.0 / ATT_DIM)
    return xh * lax.rsqrt(ms + NORM_EPS) * w_row


def _mem_kv_body(mem_ref, g_ref, wk_ref, wv_ref, kw_ref, k_ref, vt_ref):
    x = mem_ref[0]
    ms = jnp.mean(x * x, axis=-1, keepdims=True)
    xn = ((x * lax.rsqrt(ms + NORM_EPS)) * g_ref[...]).astype(BF16)
    kp = jnp.dot(xn, wk_ref[...], preferred_element_type=F32)
    vp = jnp.dot(xn, wv_ref[...], preferred_element_type=F32)
    for h in range(ATT_HEADS):
        sl = slice(h * LANE, (h + 1) * LANE)
        k_ref[0, h] = _head_rms(kp[:, sl], kw_ref[...]).astype(BF16)
        vt_ref[0, h] = _with_ones_row(vp[:, sl].T[0:ATT_DIM, :]).astype(BF16)


def _mem_kv(mem, gain, wk, wv, k_norm_w):
    b = mem.shape[0]
    return pl.pallas_call(
        _mem_kv_body, grid=(b,),
        in_specs=[pl.BlockSpec((1, MEM_LEN, D_MODEL), lambda i: (i, 0, 0)), _full((1, D_MODEL)),
                  _full(wk.shape), _full(wv.shape), _full((1, LANE))],
        out_specs=[pl.BlockSpec((1, ATT_HEADS, MEM_LEN, LANE), lambda i: (i, 0, 0, 0)),
                   pl.BlockSpec((1, ATT_HEADS, V_ROWS, MEM_LEN), lambda i: (i, 0, 0, 0))],
        out_shape=[jax.ShapeDtypeStruct((b, ATT_HEADS, MEM_LEN, LANE), BF16),
                   jax.ShapeDtypeStruct((b, ATT_HEADS, V_ROWS, MEM_LEN), BF16)],
        compiler_params=_cparams(("parallel",)), name="mem_kv",
    )(mem, gain, wk, wv, k_norm_w)


MAX_BLOCKS = 32
_AUG_ONEHOT = ATT_DIM
_AUG_EXTRA = ATT_DIM + MAX_BLOCKS


def _attn_prep_body(slopes, q_ref, k_ref, v_ref, xq_ref, qw_ref, kw_ref, xqw_ref,
                    qat_ref, ka_ref, kad_ref, vt_ref, xqt_ref, kmean_scr):
    i = pl.program_id(1)
    nb = kmean_scr.shape[1]

    @pl.when(i == 0)
    def _():
        kmean_scr[...] = jnp.zeros_like(kmean_scr)

    n_iota = lax.broadcasted_iota(jnp.int32, (nb, TQ), 0)
    past = n_iota < i
    t_loc = lax.broadcasted_iota(jnp.int32, (32, TQ), 1).astype(F32)
    r32 = lax.broadcasted_iota(jnp.int32, (32, TQ), 0)
    k_lane = lax.broadcasted_iota(jnp.int32, (TQ, LANE), 1)
    s_loc = lax.broadcasted_iota(jnp.int32, (TQ, LANE), 0).astype(F32)
    blk = lax.convert_element_type(i, F32)
    for h in range(ATT_HEADS):
        hi, lo = slopes[h]
        sl = slice(h * LANE, (h + 1) * LANE)
        qn = _head_rms(q_ref[0, :, sl], qw_ref[...])
        kn = _head_rms(k_ref[0, :, sl], kw_ref[...])
        qn_t = qn.T
        gate = jnp.dot(kmean_scr[h], qn_t, precision=HIGHEST, preferred_element_type=F32)
        gate = jnp.where(past, gate, -jnp.inf)
        cnt = jnp.zeros((nb, TQ), F32)
        for n2 in range(nb):
            gn = gate[n2:n2 + 1, :]
            ahead = jnp.where(gn > gate, 1.0, jnp.where(gn == gate, jnp.where(n_iota > n2, 1.0, 0.0), 0.0))
            cnt = cnt + ahead
        allowed = jnp.where(past, jnp.where(cnt < float(MOBA_TOPK), 1.0, 0.0), 0.0)
        bias = jnp.where(allowed > 0.5, 0.0, NEG)
        extra = jnp.where(r32 < 2, t_loc, 0.0)
        for r, val in ((2, hi), (3, lo), (4, MOBA_BLOCK * hi), (5, MOBA_BLOCK * lo)):
            extra = jnp.where(r32 == r, val, extra)
        extra = jnp.where((r32 == 6) | (r32 == 7), blk, extra)
        qat = jnp.concatenate([qn_t[0:ATT_DIM, :] * (1.0 / math.sqrt(ATT_DIM)), bias, extra], axis=0)
        qat_ref[0, h] = qat.astype(BF16)
        kx = jnp.where((k_lane == _AUG_EXTRA + 2) | (k_lane == _AUG_EXTRA + 3), s_loc, 0.0)
        kx = jnp.where((k_lane == _AUG_EXTRA + 4) | (k_lane == _AUG_EXTRA + 5), blk, kx)
        for c, val in ((0, -hi), (1, -lo), (6, -MOBA_BLOCK * hi), (7, -MOBA_BLOCK * lo)):
            kx = jnp.where(k_lane == _AUG_EXTRA + c, val, kx)
        kad = kn + kx
        kad_ref[0, h] = kad.astype(BF16)
        ka_ref[0, h] = (kad + jnp.where(k_lane - _AUG_ONEHOT == i, 1.0, 0.0)).astype(BF16)
        vt_ref[0, h] = _with_ones_row(v_ref[0, :, sl].T[0:ATT_DIM, :]).astype(BF16)
        xqn = _head_rms(xq_ref[0, :, sl], xqw_ref[...])
        xqt_ref[0, h] = (xqn.T * (1.0 / math.sqrt(ATT_DIM))).astype(BF16)
        kmean_scr[h, pl.ds(i, 1), :] = jnp.sum(kn, axis=0, keepdims=True) * (1.0 / MOBA_BLOCK)


def _attn_prep(q, k, v, xq, qw, kw, xqw):
    b, l, _ = q.shape
    nb = l // MOBA_BLOCK
    assert nb <= MAX_BLOCKS
    slopes = tuple(_bf16_split(s) for s in _SLOPES)
    tok = pl.BlockSpec((1, TQ, ATT_HEADS * LANE), lambda bi, i: (bi, i, 0))
    return pl.pallas_call(
        functools.partial(_attn_prep_body, slopes), grid=(b, nb),
        in_specs=[tok, tok, tok, tok, _full((1, LANE)), _full((1, LANE)), _full((1, LANE))],
        out_specs=[pl.BlockSpec((1, ATT_HEADS, LANE, TQ), lambda bi, i: (bi, 0, 0, i)),
                   pl.BlockSpec((1, ATT_HEADS, TQ, LANE), lambda bi, i: (bi, 0, i, 0)),
                   pl.BlockSpec((1, ATT_HEADS, TQ, LANE), lambda bi, i: (bi, 0, i, 0)),
                   pl.BlockSpec((1, ATT_HEADS, V_ROWS, TQ), lambda bi, i: (bi, 0, 0, i)),
                   pl.BlockSpec((1, ATT_HEADS, LANE, TQ), lambda bi, i: (bi, 0, 0, i))],
        out_shape=[jax.ShapeDtypeStruct((b, ATT_HEADS, LANE, l), BF16),
                   jax.ShapeDtypeStruct((b, ATT_HEADS, l, LANE), BF16),
                   jax.ShapeDtypeStruct((b, ATT_HEADS, l, LANE), BF16),
                   jax.ShapeDtypeStruct((b, ATT_HEADS, V_ROWS, l), BF16),
                   jax.ShapeDtypeStruct((b, ATT_HEADS, LANE, l), BF16)],
        scratch_shapes=[pltpu.VMEM((ATT_HEADS, MAX_BLOCKS, LANE), F32)],
        compiler_params=_cparams(("parallel", "arbitrary")), name="attn_prep",
    )(q, k, v, xq, qw, kw, xqw)


def _normalized_t(acc):
    return (acc[0:ATT_DIM, :] * (1.0 / acc[ATT_DIM:ATT_DIM + 1, :])).T


def _attn_body(qat_ref, ka_ref, kad_ref, vt_ref, xqt_ref, mk_ref, mvt_ref, o_ref, om_ref, s_scr, p_scr):
    i = pl.program_id(1)
    base = pl.multiple_of(i * MOBA_BLOCK, MOBA_BLOCK)
    key = lax.broadcasted_iota(jnp.int32, (MOBA_BLOCK, TQ), 0)
    qry = lax.broadcasted_iota(jnp.int32, (MOBA_BLOCK, TQ), 1)

    span = ATT_GROUP * MOBA_BLOCK
    n_groups = (i + ATT_GROUP - 1) // ATT_GROUP
    last_group = ka_ref.shape[2] // span - 1

    def scores(h, g):
        off = pl.multiple_of(jnp.minimum(g, last_group) * span, span)
        return jnp.dot(ka_ref[0, h, pl.ds(off, span), :], qat_ref[0, h], preferred_element_type=F32)

    def values(h, g, p):
        off = pl.multiple_of(jnp.maximum(g, 0) * span, span)
        return jnp.dot(vt_ref[0, h, :, pl.ds(off, span)], p, preferred_element_type=F32)

    init = []
    for h in range(ATT_HEADS):
        s = jnp.dot(kad_ref[0, h], qat_ref[0, h], preferred_element_type=F32)
        s = jnp.where(key <= qry, s, NEG)
        m = jnp.max(s, axis=0, keepdims=True)
        p = jnp.exp(s - m).astype(BF16)
        acc = jnp.dot(vt_ref[0, h, :, pl.ds(base, MOBA_BLOCK)], p, preferred_element_type=F32)
        s_scr[h] = scores(h, 0)
        p_scr[h] = jnp.zeros((span, TQ), BF16)
        init += [jnp.ones((1, TQ), F32), m, acc]

    def body(g, carry):
        out = []
        for h in range(ATT_HEADS):
            alpha_prev, m, acc = carry[3 * h:3 * h + 3]
            acc = acc * alpha_prev + values(h, g - 1, p_scr[h])
            s = s_scr[h]
            m_new = jnp.maximum(m, jnp.max(s, axis=0, keepdims=True))
            p_scr[h] = jnp.exp(s - m_new).astype(BF16)
            s_scr[h] = scores(h, g + 1)
            out += [jnp.exp(m - m_new), m_new, acc]
        return tuple(out)

    fin = lax.fori_loop(0, n_groups, body, tuple(init))
    for h in range(ATT_HEADS):
        alpha_prev, _, acc = fin[3 * h:3 * h + 3]
        o_ref[0, h] = _normalized_t(acc * alpha_prev + values(h, n_groups - 1, p_scr[h]))
        sm = jnp.dot(mk_ref[0, h], xqt_ref[0, h], preferred_element_type=F32)
        pm = jnp.exp(sm - jnp.max(sm, axis=0, keepdims=True)).astype(BF16)
        om_ref[0, h] = _normalized_t(jnp.dot(mvt_ref[0, h], pm, preferred_element_type=F32))


def _attn(qat, ka, kad, vt, xqt, mem_k, mem_vt):
    b, _, _, l = qat.shape
    assert l % (ATT_GROUP * MOBA_BLOCK) == 0
    nh = ATT_HEADS
    per_q = lambda r: pl.BlockSpec((1, nh, r, TQ), lambda bi, i: (bi, 0, 0, i))
    out = pl.BlockSpec((1, nh, TQ, ATT_DIM), lambda bi, i: (bi, 0, i, 0))
    return pl.pallas_call(
        _attn_body, grid=(b, l // TQ),
        in_specs=[per_q(LANE),
                  pl.BlockSpec((1, nh, l, LANE), lambda bi, i: (bi, 0, 0, 0)),
                  pl.BlockSpec((1, nh, TQ, LANE), lambda bi, i: (bi, 0, i, 0)),
                  pl.BlockSpec((1, nh, V_ROWS, l), lambda bi, i: (bi, 0, 0, 0)),
                  per_q(LANE),
                  pl.BlockSpec((1, nh, MEM_LEN, LANE), lambda bi, i: (bi, 0, 0, 0)),
                  pl.BlockSpec((1, nh, V_ROWS, MEM_LEN), lambda bi, i: (bi, 0, 0, 0))],
        out_specs=[out, out],
        out_shape=[jax.ShapeDtypeStruct((b, nh, l, ATT_DIM), F32)] * 2,
        scratch_shapes=[pltpu.VMEM((nh, ATT_GROUP * MOBA_BLOCK, TQ), F32),
                        pltpu.VMEM((nh, ATT_GROUP * MOBA_BLOCK, TQ), BF16)],
        compiler_params=_cparams(("parallel", "arbitrary")), name="attn",
    )(qat, ka, kad, vt, xqt, mem_k, mem_vt)


def _cand_layout():
    slabs = [("r1", 0, 16), ("r1", 1, 8), ("r1", 2, 8), ("r1", 3, 8), ("r2", 0, 16), ("r2", 1, 8), ("r2", 2, 8)]
    pos, valid = [], []
    for kind, fixed, n in slabs:
        for j in range(n):
            r1, r2 = (fixed, j) if kind == "r1" else (j, fixed)
            ok = (r1 + 1) * (r2 + 1) <= PEER_TOPK and ((kind == "r1") or r1 >= 4)
            pos.append(r1 * PEER_TOPK + r2)
            valid.append(ok)
    assert sum(valid) == 50
    return slabs, np.asarray(pos, np.int32), np.asarray(valid)


_CAND_SLABS, _CAND_POS, _CAND_VALID = _cand_layout()
_NCAND = len(_CAND_POS)


def _split_cols(k):
    hi = k.astype(BF16)
    lo = (k - hi.astype(F32)).astype(BF16)
    return jnp.concatenate([hi, hi, lo], axis=1)


def _dot_split(k_cat, q):
    hi = q.astype(BF16)
    lo = (q - hi.astype(F32)).astype(BF16)
    return jnp.dot(k_cat, jnp.concatenate([hi, lo, hi], axis=0), preferred_element_type=F32)


def _pack_rows(x):
    return pltpu.bitcast(x.astype(BF16), jnp.uint32)


def _unpack_rows(x):
    return pltpu.bitcast(x, BF16)


def _top16_exact(sets):
    row = lax.broadcasted_iota(jnp.int32, (PEER_NKEYS, LANE), 0).astype(F32)
    for _, rank_scr, _ in sets:
        rank_scr[...] = jnp.full(rank_scr.shape, float(PEER_TOPK), F32)

    def body(r, _):
        rf = lax.convert_element_type(r, F32)
        for half in range(TM_PP // LANE):
            ls = slice(half * LANE, (half + 1) * LANE)
            for w_scr, rank_scr, v_scr in sets:
                s = w_scr[:, ls]
                m = jnp.max(s, axis=0, keepdims=True)
                hit = row == jnp.min(jnp.where(s == m, row, float(PEER_NKEYS)), axis=0, keepdims=True)
                w_scr[:, ls] = jnp.where(hit, -jnp.inf, s)
                v_scr[half, pl.ds(r, 1), :] = m
                rank_scr[:, ls] = jnp.where(hit, rf, rank_scr[:, ls])
        return 0

    lax.fori_loop(0, PEER_TOPK, body, 0)


def _top16_distinct(sets):
    chains = [(w_scr, rank_scr, v_scr, half) for half in range(TM_PP // LANE) for w_scr, rank_scr, v_scr in sets]

    def body(r, prev):
        out = []
        for (w_scr, _, v_scr, half), m_prev in zip(chains, prev):
            s = w_scr[:, half * LANE:(half + 1) * LANE]
            m = jnp.max(jnp.where(s < m_prev, s, -jnp.inf), axis=0, keepdims=True)
            v_scr[half, pl.ds(r, 1), :] = m
            out.append(m)
        return tuple(out)

    lax.fori_loop(0, PEER_TOPK, body, tuple(jnp.full((1, LANE), jnp.inf, F32) for _ in chains))
    ranked = []
    for half in range(TM_PP // LANE):
        ls = slice(half * LANE, (half + 1) * LANE)
        total = jnp.zeros((1, LANE), F32)
        for w_scr, rank_scr, v_scr in sets:
            s = w_scr[:, ls]
            v = v_scr[half]
            if rank_scr is None:
                hit = s >= v[PEER_TOPK - 1:PEER_TOPK, :]
            else:
                rank = jnp.full(s.shape, float(PEER_TOPK), F32)
                for r in range(PEER_TOPK):
                    rank = jnp.where(s == v[r:r + 1, :], float(r), rank)
                rank_scr[:, ls] = rank
                hit = rank < float(PEER_TOPK)
            total = total + jnp.sum(jnp.where(hit, 1.0, 0.0), axis=0, keepdims=True)
        ranked.append(total)
    return jnp.concatenate(ranked, axis=1)


def _peer_prep_body(x_ref, ys_ref, ym_ref, yx_ref, wos_ref, wom_ref, wox_ref, g_ref, wqt_ref,
                    k1_ref, k2_ref, pos_ref, cbias_ref,
                    h1_ref, hnt_ref, r2_ref, f_ref, n_ref, c_ref,
                    q_scr, w1_scr, w2_scr, rank1_scr, rank2_scr, v1_scr, v2_scr, cnt_scr, z_scr):
    hres = x_ref[...] + jnp.dot(ys_ref[...], wos_ref[...], preferred_element_type=F32)
    for h in range(ATT_HEADS):
        hres = hres + jnp.dot(ym_ref[0, h].astype(BF16), wom_ref[h], preferred_element_type=F32)
        hres = hres + jnp.dot(yx_ref[0, h].astype(BF16), wox_ref[h], preferred_element_type=F32)
    h1_ref[...] = hres
    ms = jnp.mean(hres * hres, axis=-1, keepdims=True)
    hn = (hres * lax.rsqrt(ms + NORM_EPS)) * g_ref[...]
    hn_t = hn.T.astype(BF16)
    hnt_ref[...] = hn_t
    q_scr[...] = jnp.dot(wqt_ref[...], hn_t, preferred_element_type=F32)

    pos = pos_ref[...]
    cbias = cbias_ref[...]
    r16 = lax.broadcasted_iota(jnp.int32, (PEER_TOPK, TM_PP), 0).astype(F32)
    halves = lambda scr: jnp.concatenate([scr[j] for j in range(TM_PP // LANE)], axis=1)

    def head(h, _):
        base = pl.multiple_of(h * (2 * PEER_HALF), 2 * PEER_HALF)
        s1 = _dot_split(k1_ref[...], q_scr[pl.ds(base, PEER_HALF), :])
        s2 = _dot_split(k2_ref[...], q_scr[pl.ds(base + PEER_HALF, PEER_HALF), :])
        sets = [(w1_scr, rank1_scr, v1_scr), (w2_scr, rank2_scr, v2_scr)]
        w1_scr[...] = s1
        w2_scr[...] = s2
        ranked = _top16_distinct([(w1_scr, None, v1_scr), (w2_scr, rank2_scr, v2_scr)])
        tied = jnp.max(jnp.abs(ranked - 2.0 * PEER_TOPK)) > 0.5

        @pl.when(tied)
        def _():
            _top16_exact(sets)

        rank2 = rank2_scr[...]
        v1 = halves(v1_scr)
        v2 = halves(v2_scr)
        parts = []
        for kind, fixed, n in _CAND_SLABS:
            if kind == "r1":
                parts.append(v1[fixed:fixed + 1, :] + v2[0:n, :])
            else:
                parts.append(v1[0:n, :] + v2[fixed:fixed + 1, :])
        cand = jnp.concatenate(parts, axis=0) + cbias
        top = v1[0:1, :] + v2[0:1, :]

        def descend(_, carry):
            m_prev, zsum = carry
            m = jnp.max(jnp.where(cand < m_prev, cand, -jnp.inf), axis=0, keepdims=True)
            return m, zsum + jnp.exp(m - top)

        tau, zsum = lax.fori_loop(0, PEER_TOPK, descend,
                                  (jnp.full((1, TM_PP), jnp.inf, F32), jnp.zeros((1, TM_PP), F32)))
        taken = jnp.where(cand >= tau, 1.0, 0.0)
        cnt = jnp.zeros((PEER_TOPK, TM_PP), F32)
        row0 = 0
        for kind, fixed, n_rows in _CAND_SLABS:
            blk = taken[row0:row0 + n_rows, :]
            if kind == "r1":
                cnt = cnt + jnp.where(r16 == float(fixed), jnp.sum(blk, axis=0, keepdims=True), 0.0)
            elif n_rows == PEER_TOPK:
                cnt = cnt + blk
            else:
                cnt = cnt + jnp.concatenate([blk, jnp.zeros((PEER_TOPK - n_rows, TM_PP), F32)], axis=0)
            row0 += n_rows
        cnt_scr[...] = cnt
        z_scr[...] = zsum
        tied2 = jnp.max(jnp.abs(jnp.sum(cnt, axis=0, keepdims=True) - float(PEER_TOPK))) > 0.5

        @pl.when(tied2)
        def _():
            def pick(_, carry):
                cand, cnt, zsum = carry
                m = jnp.max(cand, axis=0, keepdims=True)
                p = jnp.min(jnp.where(cand == m, pos, float(4 * PEER_TOPK * PEER_TOPK)), axis=0, keepdims=True)
                cand = jnp.where(pos == p, -jnp.inf, cand)
                cnt = cnt + jnp.where(r16 == jnp.floor(p * (1.0 / PEER_TOPK)), 1.0, 0.0)
                return cand, cnt, zsum + jnp.exp(m - top)

            _, cnt, zsum = lax.fori_loop(
                0, PEER_TOPK, pick, (cand, jnp.zeros((PEER_TOPK, TM_PP), F32), jnp.zeros((1, TM_PP), F32)))
            cnt_scr[...] = cnt
            z_scr[...] = zsum

        cnt = cnt_scr[...]
        zsum = z_scr[...]
        def write_nsel(is_rank_r):
            nsel = jnp.zeros((PEER_NKEYS, TM_PP), F32)
            for r in range(PEER_TOPK):
                nsel = jnp.where(is_rank_r(r), cnt[r:r + 1, :], nsel)
            for lc in range(TM_PP // LANE):
                n_ref[h, lc] = nsel[:, lc * LANE:(lc + 1) * LANE]

        @pl.when(tied)
        def _():
            rank1 = rank1_scr[...]
            write_nsel(lambda r: rank1 == float(r))

        @pl.when(jnp.logical_not(tied))
        def _():
            write_nsel(lambda r: s1 == v1[r:r + 1, :])

        r2_ref[h] = _pack_rows(rank2)
        f_ref[h] = _pack_rows(jnp.exp(s2 - v2[0:1, :]))
        cw = jnp.exp(s1 - v1[0:1, :]) * (0.5 / zsum)
        for lc in range(TM_PP // LANE):
            c_ref[h, lc] = cw[:, lc * LANE:(lc + 1) * LANE]
        return 0

    lax.fori_loop(0, PEER_HEADS, head, 0)


def _peer_prep(x2d, y_ssd, y_moba, y_mem, wo_ssd, wo_moba, wo_mem, gain, wq_t, k1, k2):
    t = x2d.shape[0]
    b, _, l, _ = y_moba.shape
    per_b = l // TM_PP
    pos = jnp.asarray(np.broadcast_to(_CAND_POS.astype(np.float32)[:, None], (_NCAND, TM_PP)))
    cbias = jnp.asarray(np.broadcast_to(np.where(_CAND_VALID, 0.0, -np.inf).astype(np.float32)[:, None],
                                        (_NCAND, TM_PP)))
    tok = lambda w: pl.BlockSpec((TM_PP, w), lambda i: (i, 0))
    att = pl.BlockSpec((1, ATT_HEADS, TM_PP, ATT_DIM), lambda i: (i // per_b, 0, i % per_b, 0))
    meta = pl.BlockSpec((PEER_HEADS, TM_PP // LANE, PEER_NKEYS, LANE), lambda i: (0, i, 0, 0))
    meta_f32 = jax.ShapeDtypeStruct((PEER_HEADS, t // LANE, PEER_NKEYS, LANE), F32)
    packed = pl.BlockSpec((PEER_HEADS, PEER_NKEYS // 2, TM_PP), lambda i: (0, 0, i))
    meta_pk = jax.ShapeDtypeStruct((PEER_HEADS, PEER_NKEYS // 2, t), jnp.uint32)
    return pl.pallas_call(
        _peer_prep_body, grid=(t // TM_PP,),
        in_specs=[tok(D_MODEL), tok(SSD_WIDTH), att, att, _full(wo_ssd.shape), _full(wo_moba.shape),
                  _full(wo_mem.shape), _full((1, D_MODEL)), _full(wq_t.shape), _full(k1.shape), _full(k2.shape),
                  _full(pos.shape), _full(cbias.shape)],
        out_specs=[tok(D_MODEL), pl.BlockSpec((D_MODEL, TM_PP), lambda i: (0, i)), packed, packed, meta, meta],
        out_shape=[jax.ShapeDtypeStruct((t, D_MODEL), F32), jax.ShapeDtypeStruct((D_MODEL, t), BF16),
                   meta_pk, meta_pk, meta_f32, meta_f32],
        scratch_shapes=[pltpu.VMEM((PEER_HEADS * 2 * PEER_HALF, TM_PP), F32)]
        + [pltpu.VMEM((PEER_NKEYS, TM_PP), F32)] * 4
        + [pltpu.VMEM((TM_PP // LANE, PEER_TOPK, LANE), F32)] * 2
        + [pltpu.VMEM((PEER_TOPK, TM_PP), F32), pltpu.VMEM((1, TM_PP), F32)],
        compiler_params=_cparams(("parallel",)), name="peer_prep",
    )(x2d, y_ssd, y_moba, y_mem, wo_ssd, wo_moba, wo_mem, gain, wq_t, k1, k2, pos, cbias)


_A_PER_SLAB = ES_PEER // PEER_NKEYS
_PEER_STEPS = PEER_EXPERTS // (PEER_SLABS * ES_PEER)


def _peer_gate_chunk(s_ref, a_ref, a_base, k, lc, r2_ref, f_ref, n_ref, c_ref):
    zero = jnp.zeros((PEER_NKEYS, LANE), BF16)

    def bcast(row):
        tile = jnp.broadcast_to(row, (16, LANE)).astype(BF16)
        return jnp.concatenate([tile] * (PEER_NKEYS // 16), axis=0)

    a = a_base + k
    rows = slice(k * PEER_NKEYS, (k + 1) * PEER_NKEYS)
    ls = slice(lc * LANE, (lc + 1) * LANE)
    s = s_ref[rows, ls].astype(BF16)
    act = s * (1.0 + lax.erf(s * math.sqrt(0.5)))
    g = zero
    for h in range(PEER_HEADS):
        nrow = bcast(n_ref[h, lc, pl.ds(a, 1), :])
        crow = bcast(c_ref[h, lc, pl.ds(a, 1), :])
        g = g + jnp.where(_unpack_rows(r2_ref[h, :, ls]) < nrow, _unpack_rows(f_ref[h, :, ls]) * crow, zero)
    a_ref[k * (PEER_NKEYS // 2):(k + 1) * (PEER_NKEYS // 2), ls] = pltpu.bitcast(act * g, jnp.uint32)


def _peer_body(*refs):
    s_n = PEER_SLABS
    wd0_ref = refs[0]
    wd_next = refs[1:1 + s_n]
    wut_prev = refs[1 + s_n:1 + 2 * s_n]
    (wut_last_ref, hnt_ref, r2_ref, f_ref, n_ref, c_ref, h1_ref, o_ref,
     acc_scr, s_scr, a_scr) = refs[1 + 2 * s_n:]
    n = pl.program_id(1)
    meta = (r2_ref, f_ref, n_ref, c_ref)

    @pl.when(n == 0)
    def _():
        acc_scr[...] = jnp.zeros_like(acc_scr)
        a_scr[s_n - 1] = jnp.zeros(a_scr.shape[1:], a_scr.dtype)
        s_scr[0] = jnp.dot(wd0_ref[...], hnt_ref[...], preferred_element_type=F32)

    for j in range(s_n):
        for half in range(2):
            ls = slice(half * (TT_PEER // 2), (half + 1) * (TT_PEER // 2))
            acc_scr[:, ls] += jnp.dot(wut_prev[j][...], _unpack_rows(a_scr[(j - 1) % s_n, :, ls]),
                                      preferred_element_type=F32)
            s_scr[(j + 1) % s_n, :, ls] = jnp.dot(wd_next[j][...], hnt_ref[:, ls],
                                                  preferred_element_type=F32)
        a_base = (n * s_n + j) * _A_PER_SLAB
        for lc in range(TT_PEER // LANE):
            for k in range(_A_PER_SLAB):
                _peer_gate_chunk(s_scr.at[j], a_scr.at[j], a_base, k, lc, *meta)

    @pl.when(n == pl.num_programs(1) - 1)
    def _():
        tail = jnp.dot(wut_last_ref[...], _unpack_rows(a_scr[s_n - 1]), preferred_element_type=F32)
        o_ref[...] = h1_ref[...] + (acc_scr[...] + tail).T


def _peer(wd, wu_t, hn_t, rank2, f, nsel, c, h1):
    t = h1.shape[0]
    s_n = PEER_SLABS
    nblk = s_n * _PEER_STEPS
    meta = pl.BlockSpec((PEER_HEADS, TT_PEER // LANE, PEER_NKEYS, LANE), lambda i, n: (0, i, 0, 0))
    packed = pl.BlockSpec((PEER_HEADS, PEER_NKEYS // 2, TT_PEER), lambda i, n: (0, 0, i))

    def wd_blk(off):
        return pl.BlockSpec((ES_PEER, D_MODEL), lambda i, n: (jnp.clip(s_n * n + off, 0, nblk - 1), 0))

    def wut_blk(off):
        return pl.BlockSpec((None, D_MODEL, ES_PEER), lambda i, n: (jnp.clip(s_n * n + off, 0, nblk - 1), 0, 0))

    return pl.pallas_call(
        _peer_body, grid=(t // TT_PEER, _PEER_STEPS),
        in_specs=[pl.BlockSpec((ES_PEER, D_MODEL), lambda i, n: (0, 0))]
        + [wd_blk(j + 1) for j in range(s_n)] + [wut_blk(j - 1) for j in range(s_n)]
        + [pl.BlockSpec((None, D_MODEL, ES_PEER), lambda i, n: (nblk - 1, 0, 0)),
           pl.BlockSpec((D_MODEL, TT_PEER), lambda i, n: (0, i)), packed, packed, meta, meta,
           pl.BlockSpec((TT_PEER, D_MODEL), lambda i, n: (i, 0))],
        out_specs=pl.BlockSpec((TT_PEER, D_MODEL), lambda i, n: (i, 0)),
        out_shape=jax.ShapeDtypeStruct((t, D_MODEL), F32),
        scratch_shapes=[pltpu.VMEM((D_MODEL, TT_PEER), F32),
                        pltpu.VMEM((s_n, ES_PEER, TT_PEER), F32),
                        pltpu.VMEM((s_n, ES_PEER // 2, TT_PEER), jnp.uint32)],
        compiler_params=_cparams(("parallel", "arbitrary")), name="peer",
    )(*([wd] * (s_n + 1) + [wu_t] * (s_n + 1) + [hn_t, rank2, f, nsel, c, h1]))


def _pad_heads(w):
    r = w.shape[0]
    w = w.reshape(r, -1, ATT_DIM)
    return jnp.pad(w, ((0, 0), (0, 0), (0, LANE - ATT_DIM))).reshape(r, -1)


def _pad_row(w):
    return jnp.pad(w, (0, LANE - w.shape[0])).reshape(1, LANE)


def kernel(x, mem, mix_norm_w, w_in, ssd_conv_w, ssd_conv_b, ssd_dt_bias, ssd_a_log, ssd_d, ssd_norm_w,
           moba_q_norm_w, moba_k_norm_w, mem_norm_w, w_mem_kv, xattn_q_norm_w, xattn_k_norm_w, w_out,
           ffn_norm_w, peer_w_query, peer_sub_keys_1, peer_sub_keys_2, peer_expert_down, peer_expert_up):
    b, l, d = x.shape
    depth = w_in.shape[0]
    h = x.reshape(b * l, d)
    for li in range(depth):
        wi = w_in[li]
        o = 0
        cols = {}
        for name, width in (("z", SSD_WIDTH), ("xbc", SSD_CONV_CH), ("dt", SSD_HEADS), ("mq", ATT_WIDTH),
                            ("mk", ATT_WIDTH), ("mv", ATT_WIDTH), ("xq", ATT_WIDTH)):
            cols[name] = wi[:, o:o + width]
            o += width
        w_list = [cols["z"], cols["xbc"], jnp.pad(cols["dt"], ((0, 0), (0, LANE - SSD_HEADS))),
                  _pad_heads(cols["mq"]), _pad_heads(cols["mk"]), _pad_heads(cols["mv"]), _pad_heads(cols["xq"])]
        w_list = [w.astype(BF16) for w in w_list]
        z, xbc, dt, mq, mk, mv, xq = _in_proj(h, mix_norm_w[li].reshape(1, d), w_list)

        y_ssd = _ssd(z.reshape(b, l, -1), xbc.reshape(b, l, -1), dt.reshape(b, l, -1), ssd_conv_w[li],
                     ssd_conv_b[li], ssd_dt_bias[li], ssd_a_log[li], ssd_d[li], ssd_norm_w[li])

        wkv = w_mem_kv[li]
        mem_k, mem_vt = _mem_kv(mem, mem_norm_w[li].reshape(1, d), _pad_heads(wkv[:, :ATT_WIDTH]).astype(BF16),
                                _pad_heads(wkv[:, ATT_WIDTH:]).astype(BF16), _pad_row(xattn_k_norm_w[li]))
        r3 = lambda a: a.reshape(b, l, -1)
        qat, ka, kad, vt, xqt = _attn_prep(r3(mq), r3(mk), r3(mv), r3(xq), _pad_row(moba_q_norm_w[li]),
                                           _pad_row(moba_k_norm_w[li]), _pad_row(xattn_q_norm_w[li]))
        y_moba, y_mem = _attn(qat, ka, kad, vt, xqt, mem_k, mem_vt)

        wo = w_out[li].astype(BF16)
        wo_moba = wo[SSD_WIDTH:SSD_WIDTH + ATT_WIDTH].reshape(ATT_HEADS, ATT_DIM, d)
        wo_mem = wo[SSD_WIDTH + ATT_WIDTH:].reshape(ATT_HEADS, ATT_DIM, d)
        h1, hn_t, rank2, f, nsel, c = _peer_prep(
            h, y_ssd.reshape(b * l, -1), y_moba, y_mem, wo[:SSD_WIDTH], wo_moba, wo_mem,
            ffn_norm_w[li].reshape(1, d), peer_w_query[li].T.astype(BF16), _split_cols(peer_sub_keys_1[li]),
            _split_cols(peer_sub_keys_2[li]))
        wu_t = peer_expert_up[li].reshape(-1, ES_PEER, d).transpose(0, 2, 1).astype(BF16)
        h = _peer(peer_expert_down[li].astype(BF16), wu_t, hn_t, rank2, f, nsel, c, h1)
    return h.reshape(b, l, d)
```

```python
import functools
import math

import numpy as np
import jax
import jax.numpy as jnp
from jax import lax
from jax.experimental import pallas as pl
from jax.experimental.pallas import tpu as pltpu

F32 = jnp.float32
BF16 = jnp.bfloat16
HIGHEST = lax.Precision.HIGHEST

NORM_EPS = 1e-6
D_MODEL = 1024
SSD_HEADS = 8
SSD_HEAD_DIM = 64
SSD_WIDTH = 512
SSD_GROUPS = 2
SSD_STATE = 128
SSD_CONV = 4
SSD_CONV_CH = 1024
ATT_HEADS = 4
ATT_DIM = 64
ATT_WIDTH = 256
MOBA_BLOCK = 256
MOBA_TOPK = 3
MEM_LEN = 256
PEER_HEADS = 8
PEER_NKEYS = 128
PEER_TOPK = 16
PEER_HALF = 64
PEER_EXPERTS = PEER_NKEYS * PEER_NKEYS

LANE = 128
NEG = -1e30
VMEM_LIMIT = 56 * 1024 * 1024

TM_IN = 512
SSD_CHUNK = 256
TQ = MOBA_BLOCK
ATT_GROUP = 4
TM_PP = 512
TT_PEER = 512
ES_PEER = 512
PEER_SLABS = 4

_SLOPES = [2.0 ** (-8.0 * (i + 1) / ATT_HEADS) for i in range(ATT_HEADS)]


def _bf16_split(v):
    hi = float(np.float32(v).astype(BF16).astype(np.float32))
    lo = float(np.float32(v - hi).astype(BF16).astype(np.float32))
    return hi, lo


def _cparams(sem):
    return pltpu.CompilerParams(dimension_semantics=sem, vmem_limit_bytes=VMEM_LIMIT)


def _sigmoid(x):
    return 1.0 / (1.0 + jnp.exp(-x))


def _full(shape):
    n = len(shape)
    return pl.BlockSpec(shape, lambda *_: (0,) * n)


def _in_proj_body(x_ref, g_ref, *refs):
    n = len(refs) // 2
    w_refs, o_refs = refs[:n], refs[n:]
    x = x_ref[...]
    ms = jnp.mean(x * x, axis=-1, keepdims=True)
    xn = ((x * lax.rsqrt(ms + NORM_EPS)) * g_ref[...]).astype(BF16)
    for w_ref, o_ref in zip(w_refs, o_refs):
        o_ref[...] = jnp.dot(xn, w_ref[...], preferred_element_type=F32).astype(o_ref.dtype)


def _in_proj(x2d, gain, weights):
    t = x2d.shape[0]
    in_specs = [pl.BlockSpec((TM_IN, D_MODEL), lambda i: (i, 0)), _full((1, D_MODEL))]
    in_specs += [_full(w.shape) for w in weights]
    out_specs = [pl.BlockSpec((TM_IN, w.shape[1]), lambda i: (i, 0)) for w in weights]
    out_shape = [jax.ShapeDtypeStruct((t, w.shape[1]), F32) for w in weights]
    return pl.pallas_call(
        _in_proj_body, grid=(t // TM_IN,), in_specs=in_specs, out_specs=out_specs,
        out_shape=out_shape, compiler_params=_cparams(("parallel",)), name="in_proj",
    )(x2d, gain, *weights)


def _ssd_body(z_ref, xbc_ref, dt_ref, cw_ref, cb_ref, dtb_ref, alog_ref, dskip_ref, nw_ref, e_ref,
              y_ref, ext_scr, state_scr):
    q = SSD_CHUNK
    c = pl.program_id(1)

    @pl.when(c == 0)
    def _():
        ext_scr[0:8, :] = jnp.zeros((8, SSD_CONV_CH), F32)
        state_scr[...] = jnp.zeros_like(state_scr)

    u = xbc_ref[0]
    ext_scr[8:8 + q, :] = u
    acc = cb_ref[...] + cw_ref[3:4, :] * u
    acc = acc + cw_ref[2:3, :] * ext_scr[7:7 + q, :]
    acc = acc + cw_ref[1:2, :] * ext_scr[6:6 + q, :]
    acc = acc + cw_ref[0:1, :] * ext_scr[5:5 + q, :]
    ext_scr[0:8, :] = u[q - 8:q, :]
    act = acc * _sigmoid(acc)
    xs = act[:, 0:SSD_WIDTH]
    bm = act[:, SSD_WIDTH:SSD_WIDTH + SSD_GROUPS * SSD_STATE]
    cm = act[:, SSD_WIDTH + SSD_GROUPS * SSD_STATE:]

    dtr = dt_ref[0] + dtb_ref[...]
    dt = jnp.maximum(dtr, 0.0) + jnp.log(1.0 + jnp.exp(-jnp.abs(dtr)))
    a = dt * (-jnp.exp(alog_ref[...]))
    row = lax.broadcasted_iota(jnp.int32, (q, q), 0)
    col = lax.broadcasted_iota(jnp.int32, (q, q), 1)
    causal = row >= col
    a_cs = jnp.dot(causal.astype(F32), a, precision=HIGHEST, preferred_element_type=F32)
    a_cs_t = a_cs.T
    a_cs_w = jnp.dot(a_cs, e_ref[...], precision=HIGHEST, preferred_element_type=F32)
    dt_w = jnp.dot(dt, e_ref[...], precision=HIGHEST, preferred_element_type=F32)
    total_w = a_cs_w[q - 1:q, :]
    exp_cs_w = jnp.exp(a_cs_w)
    dte_w = jnp.exp(total_w - a_cs_w)
    cd_w = jnp.exp(total_w)
    xdt_w = xs * dt_w
    lane = lax.broadcasted_iota(jnp.int32, (1, LANE), 1)
    first = lane < SSD_HEAD_DIM

    z = z_ref[0]
    gated = []
    for g in range(SSD_GROUPS):
        bg = bm[:, g * SSD_STATE:(g + 1) * SSD_STATE]
        cg = cm[:, g * SSD_STATE:(g + 1) * SSD_STATE].astype(BF16)
        cb = lax.dot_general(cg, bg.astype(BF16), (((1,), (1,)), ((), ())), preferred_element_type=F32)
        bg_t = bg.T.astype(BF16)
        for kk in range(2):
            k = 2 * g + kk
            sl = slice(k * LANE, (k + 1) * LANE)
            xdt = xdt_w[:, sl]
            xdt_b = xdt.astype(BF16)
            yd = []
            for hh in range(2):
                h = 2 * k + hh
                seg = a_cs[:, h:h + 1] - a_cs_t[h:h + 1, :]
                lm = jnp.exp(jnp.where(causal, seg, -jnp.inf))
                yd.append(jnp.dot((cb * lm).astype(BF16), xdt_b, preferred_element_type=F32))
            y = jnp.where(first, yd[0], yd[1])
            s_old = state_scr[k]
            y = y + jnp.dot(cg, s_old.astype(BF16), preferred_element_type=F32) * exp_cs_w[:, sl]
            y = y + xs[:, sl] * dskip_ref[:, sl]
            state_scr[k] = s_old * cd_w[:, sl] + jnp.dot(
                bg_t, (xdt * dte_w[:, sl]).astype(BF16), preferred_element_type=F32)
            zz = z[:, sl]
            gated.append(y * (zz * _sigmoid(zz)))
    for g in range(SSD_GROUPS):
        y0, y1 = gated[2 * g], gated[2 * g + 1]
        ms = (jnp.sum(y0 * y0, axis=-1, keepdims=True)
              + jnp.sum(y1 * y1, axis=-1, keepdims=True)) * (1.0 / (SSD_WIDTH // SSD_GROUPS))
        r = lax.rsqrt(ms + NORM_EPS)
        lo = 2 * g * LANE
        y_ref[0, :, lo:lo + LANE] = (y0 * r * nw_ref[:, lo:lo + LANE]).astype(y_ref.dtype)
        y_ref[0, :, lo + LANE:lo + 2 * LANE] = (y1 * r * nw_ref[:, lo + LANE:lo + 2 * LANE]).astype(y_ref.dtype)


def _ssd(z, xbc, dt, conv_w, conv_b, dt_bias, a_log, d_skip, norm_w):
    b, l, _ = z.shape
    q = SSD_CHUNK
    pad = LANE - SSD_HEADS
    dtb = jnp.pad(dt_bias, (0, pad)).reshape(1, LANE)
    alog = jnp.pad(a_log, (0, pad)).reshape(1, LANE)
    dsk = jnp.repeat(d_skip, SSD_HEAD_DIM).reshape(1, SSD_WIDTH)
    expand = (jnp.arange(LANE)[:, None] == (jnp.arange(SSD_WIDTH)[None, :] // SSD_HEAD_DIM)).astype(F32)
    tok = lambda w: pl.BlockSpec((1, q, w), lambda bi, ci: (bi, ci, 0))
    return pl.pallas_call(
        _ssd_body, grid=(b, l // q),
        in_specs=[tok(SSD_WIDTH), tok(SSD_CONV_CH), tok(LANE), _full((SSD_CONV, SSD_CONV_CH)),
                  _full((1, SSD_CONV_CH)), _full((1, LANE)), _full((1, LANE)), _full((1, SSD_WIDTH)),
                  _full((1, SSD_WIDTH)), _full((LANE, SSD_WIDTH))],
        out_specs=tok(SSD_WIDTH),
        out_shape=jax.ShapeDtypeStruct((b, l, SSD_WIDTH), BF16),
        scratch_shapes=[pltpu.VMEM((q + 8, SSD_CONV_CH), F32),
                        pltpu.VMEM((SSD_HEADS // 2, SSD_STATE, LANE), F32)],
        compiler_params=_cparams(("parallel", "arbitrary")), name="ssd",
    )(z, xbc, dt, conv_w, conv_b.reshape(1, -1), dtb, alog, dsk, norm_w.reshape(1, -1), expand)


V_ROWS = ATT_DIM + 16


def _with_ones_row(v_t):
    n = v_t.shape[1]
    r = lax.broadcasted_iota(jnp.int32, (V_ROWS - ATT_DIM, n), 0)
    return jnp.concatenate([v_t, jnp.where(r == 0, 1.0, 0.0)], axis=0)


def _head_rms(xh, w_row):
    ms = jnp.sum(xh * xh, axis=-1, keepdims=True) * (1.0 / ATT_DIM)
    return xh * lax.rsqrt(ms + NORM_EPS) * w_row


def _mem_kv_body(mem_ref, g_ref, wk_ref, wv_ref, kw_ref, k_ref, vt_ref):
    x = mem_ref[0]
    ms = jnp.mean(x * x, axis=-1, keepdims=True)
    xn = ((x * lax.rsqrt(ms + NORM_EPS)) * g_ref[...]).astype(BF16)
    kp = jnp.dot(xn, wk_ref[...], preferred_element_type=F32)
    vp = jnp.dot(xn, wv_ref[...], preferred_element_type=F32)
    for h in range(ATT_HEADS):
        sl = slice(h * LANE, (h + 1) * LANE)
        k_ref[0, h] = _head_rms(kp[:, sl], kw_ref[...]).astype(BF16)
        vt_ref[0, h] = _with_ones_row(vp[:, sl].T[0:ATT_DIM, :]).astype(BF16)


def _mem_kv(mem, gain, wk, wv, k_norm_w):
    b = mem.shape[0]
    return pl.pallas_call(
        _mem_kv_body, grid=(b,),
        in_specs=[pl.BlockSpec((1, MEM_LEN, D_MODEL), lambda i: (i, 0, 0)), _full((1, D_MODEL)),
                  _full(wk.shape), _full(wv.shape), _full((1, LANE))],
        out_specs=[pl.BlockSpec((1, ATT_HEADS, MEM_LEN, LANE), lambda i: (i, 0, 0, 0)),
                   pl.BlockSpec((1, ATT_HEADS, V_ROWS, MEM_LEN), lambda i: (i, 0, 0, 0))],
        out_shape=[jax.ShapeDtypeStruct((b, ATT_HEADS, MEM_LEN, LANE), BF16),
                   jax.ShapeDtypeStruct((b, ATT_HEADS, V_ROWS, MEM_LEN), BF16)],
        compiler_params=_cparams(("parallel",)), name="mem_kv",
    )(mem, gain, wk, wv, k_norm_w)


MAX_BLOCKS = 32
_AUG_ONEHOT = ATT_DIM
_AUG_EXTRA = ATT_DIM + MAX_BLOCKS


def _attn_prep_body(slopes, q_ref, k_ref, v_ref, xq_ref, qw_ref, kw_ref, xqw_ref, seg_ref,
                    qat_ref, ka_ref, kad_ref, vt_ref, xqt_ref, kmean_scr):
    i = pl.program_id(1)
    nb = kmean_scr.shape[1]

    @pl.when(i == 0)
    def _():
        kmean_scr[...] = jnp.zeros_like(kmean_scr)

    def heads_rms_t(x, w_row):
        ms = jnp.dot(x * x, seg_ref[...], precision=HIGHEST, preferred_element_type=F32)
        return (x * lax.rsqrt(ms + NORM_EPS) * w_row).T

    qn_t_all = heads_rms_t(q_ref[0], qw_ref[...])
    xqn_t_all = heads_rms_t(xq_ref[0], xqw_ref[...])
    v_t_all = v_ref[0].T

    n_iota = lax.broadcasted_iota(jnp.int32, (nb, TQ), 0)
    past = n_iota < i
    t_loc = lax.broadcasted_iota(jnp.int32, (32, TQ), 1).astype(F32)
    r32 = lax.broadcasted_iota(jnp.int32, (32, TQ), 0)
    k_lane = lax.broadcasted_iota(jnp.int32, (TQ, LANE), 1)
    s_loc = lax.broadcasted_iota(jnp.int32, (TQ, LANE), 0).astype(F32)
    blk = lax.convert_element_type(i, F32)
    for h in range(ATT_HEADS):
        hi, lo = slopes[h]
        sl = slice(h * LANE, (h + 1) * LANE)
        hs = slice(h * ATT_DIM, (h + 1) * ATT_DIM)
        kn = _head_rms(k_ref[0, :, sl], kw_ref[...])
        qn_t = qn_t_all[hs, :]
        gate = jnp.dot(kmean_scr[h][:, 0:ATT_DIM], qn_t, precision=HIGHEST,
                       preferred_element_type=F32)
        gate = jnp.where(past, gate, -jnp.inf)
        cnt = jnp.zeros((nb, TQ), F32)
        for n2 in range(nb):
            gn = gate[n2:n2 + 1, :]
            ahead = jnp.where(gn > gate, 1.0, jnp.where(gn == gate, jnp.where(n_iota > n2, 1.0, 0.0), 0.0))
            cnt = cnt + ahead
        allowed = jnp.where(past, jnp.where(cnt < float(MOBA_TOPK), 1.0, 0.0), 0.0)
        bias = jnp.where(allowed > 0.5, 0.0, NEG)
        extra = jnp.where(r32 < 2, t_loc, 0.0)
        for r, val in ((2, hi), (3, lo), (4, MOBA_BLOCK * hi), (5, MOBA_BLOCK * lo)):
            extra = jnp.where(r32 == r, val, extra)
        extra = jnp.where((r32 == 6) | (r32 == 7), blk, extra)
        qat = jnp.concatenate([qn_t * (1.0 / math.sqrt(ATT_DIM)), bias, extra], axis=0)
        qat_ref[0, h] = qat.astype(BF16)
        kx = jnp.where((k_lane == _AUG_EXTRA + 2) | (k_lane == _AUG_EXTRA + 3), s_loc, 0.0)
        kx = jnp.where((k_lane == _AUG_EXTRA + 4) | (k_lane == _AUG_EXTRA + 5), blk, kx)
        for c, val in ((0, -hi), (1, -lo), (6, -MOBA_BLOCK * hi), (7, -MOBA_BLOCK * lo)):
            kx = jnp.where(k_lane == _AUG_EXTRA + c, val, kx)
        kad = kn + kx
        kad_ref[0, h] = kad.astype(BF16)
        ka_ref[0, h] = (kad + jnp.where(k_lane - _AUG_ONEHOT == i, 1.0, 0.0)).astype(BF16)
        vt_ref[0, h] = _with_ones_row(v_t_all[hs, :]).astype(BF16)
        xqt_ref[0, h] = jnp.concatenate([xqn_t_all[hs, :] * (1.0 / math.sqrt(ATT_DIM)),
                                         jnp.zeros((LANE - ATT_DIM, TQ), F32)], axis=0).astype(BF16)
        kmean_scr[h, pl.ds(i, 1), :] = jnp.sum(kn, axis=0, keepdims=True) * (1.0 / MOBA_BLOCK)


def _attn_prep(q, k, v, xq, qw, kw, xqw):
    b, l, _ = q.shape
    nb = l // MOBA_BLOCK
    assert nb <= MAX_BLOCKS
    slopes = tuple(_bf16_split(s) for s in _SLOPES)
    tok = pl.BlockSpec((1, TQ, ATT_HEADS * LANE), lambda bi, i: (bi, i, 0))
    flat = pl.BlockSpec((1, TQ, ATT_WIDTH), lambda bi, i: (bi, i, 0))
    head_of = jnp.arange(ATT_WIDTH) // ATT_DIM
    seg = (head_of[:, None] == head_of[None, :]).astype(F32) * (1.0 / ATT_DIM)
    return pl.pallas_call(
        functools.partial(_attn_prep_body, slopes), grid=(b, nb),
        in_specs=[flat, tok, flat, flat, _full((1, ATT_WIDTH)), _full((1, LANE)), _full((1, ATT_WIDTH)),
                  _full((ATT_WIDTH, ATT_WIDTH))],
        out_specs=[pl.BlockSpec((1, ATT_HEADS, LANE, TQ), lambda bi, i: (bi, 0, 0, i)),
                   pl.BlockSpec((1, ATT_HEADS, TQ, LANE), lambda bi, i: (bi, 0, i, 0)),
                   pl.BlockSpec((1, ATT_HEADS, TQ, LANE), lambda bi, i: (bi, 0, i, 0)),
                   pl.BlockSpec((1, ATT_HEADS, V_ROWS, TQ), lambda bi, i: (bi, 0, 0, i)),
                   pl.BlockSpec((1, ATT_HEADS, LANE, TQ), lambda bi, i: (bi, 0, 0, i))],
        out_shape=[jax.ShapeDtypeStruct((b, ATT_HEADS, LANE, l), BF16),
                   jax.ShapeDtypeStruct((b, ATT_HEADS, l, LANE), BF16),
                   jax.ShapeDtypeStruct((b, ATT_HEADS, l, LANE), BF16),
                   jax.ShapeDtypeStruct((b, ATT_HEADS, V_ROWS, l), BF16),
                   jax.ShapeDtypeStruct((b, ATT_HEADS, LANE, l), BF16)],
        scratch_shapes=[pltpu.VMEM((ATT_HEADS, MAX_BLOCKS, LANE), F32)],
        compiler_params=_cparams(("parallel", "arbitrary")), name="attn_prep",
    )(q, k, v, xq, qw, kw, xqw, seg)


def _normalized_t(acc):
    return (acc[0:ATT_DIM, :] * (1.0 / acc[ATT_DIM:ATT_DIM + 1, :])).T


def _attn_body(qat_ref, ka_ref, kad_ref, vt_ref, xqt_ref, mk_ref, mvt_ref, o_ref, om_ref, s_scr, p_scr):
    i = pl.program_id(1)
    base = pl.multiple_of(i * MOBA_BLOCK, MOBA_BLOCK)
    key = lax.broadcasted_iota(jnp.int32, (MOBA_BLOCK, TQ), 0)
    qry = lax.broadcasted_iota(jnp.int32, (MOBA_BLOCK, TQ), 1)

    span = ATT_GROUP * MOBA_BLOCK
    n_groups = (i + ATT_GROUP - 1) // ATT_GROUP
    last_group = ka_ref.shape[2] // span - 1

    def scores(h, g):
        off = pl.multiple_of(jnp.minimum(g, last_group) * span, span)
        return jnp.dot(ka_ref[0, h, pl.ds(off, span), :], qat_ref[0, h], preferred_element_type=F32)

    def values(h, g, p):
        off = pl.multiple_of(jnp.maximum(g, 0) * span, span)
        return jnp.dot(vt_ref[0, h, :, pl.ds(off, span)], p, preferred_element_type=F32)

    init = []
    for h in range(ATT_HEADS):
        s = jnp.dot(kad_ref[0, h], qat_ref[0, h], preferred_element_type=F32)
        s = jnp.where(key <= qry, s, NEG)
        m = jnp.max(s, axis=0, keepdims=True)
        p = jnp.exp(s - m).astype(BF16)
        acc = jnp.dot(vt_ref[0, h, :, pl.ds(base, MOBA_BLOCK)], p, preferred_element_type=F32)
        s_scr[h] = scores(h, 0)
        p_scr[h] = jnp.zeros((span, TQ), BF16)
        init += [jnp.ones((1, TQ), F32), m, acc]

    def body(g, carry):
        out = []
        for h in range(ATT_HEADS):
            alpha_prev, m, acc = carry[3 * h:3 * h + 3]
            acc = acc * alpha_prev + values(h, g - 1, p_scr[h])
            s = s_scr[h]
            m_new = jnp.maximum(m, jnp.max(s, axis=0, keepdims=True))
            p_scr[h] = jnp.exp(s - m_new).astype(BF16)
            s_scr[h] = scores(h, g + 1)
            out += [jnp.exp(m - m_new), m_new, acc]
        return tuple(out)

    fin = lax.fori_loop(0, n_groups, body, tuple(init))
    for h in range(ATT_HEADS):
        alpha_prev, _, acc = fin[3 * h:3 * h + 3]
        o_ref[0, h] = _normalized_t(acc * alpha_prev + values(h, n_groups - 1, p_scr[h]))
        sm = jnp.dot(mk_ref[0, h], xqt_ref[0, h], preferred_element_type=F32)
        pm = jnp.exp(sm - jnp.max(sm, axis=0, keepdims=True)).astype(BF16)
        om_ref[0, h] = _normalized_t(jnp.dot(mvt_ref[0, h], pm, preferred_element_type=F32))


def _attn(qat, ka, kad, vt, xqt, mem_k, mem_vt):
    b, _, _, l = qat.shape
    assert l % (ATT_GROUP * MOBA_BLOCK) == 0
    nh = ATT_HEADS
    per_q = lambda r: pl.BlockSpec((1, nh, r, TQ), lambda bi, i: (bi, 0, 0, i))
    out = pl.BlockSpec((1, nh, TQ, ATT_DIM), lambda bi, i: (bi, 0, i, 0))
    return pl.pallas_call(
        _attn_body, grid=(b, l // TQ),
        in_specs=[per_q(LANE),
                  pl.BlockSpec((1, nh, l, LANE), lambda bi, i: (bi, 0, 0, 0)),
                  pl.BlockSpec((1, nh, TQ, LANE), lambda bi, i: (bi, 0, i, 0)),
                  pl.BlockSpec((1, nh, V_ROWS, l), lambda bi, i: (bi, 0, 0, 0)),
                  per_q(LANE),
                  pl.BlockSpec((1, nh, MEM_LEN, LANE), lambda bi, i: (bi, 0, 0, 0)),
                  pl.BlockSpec((1, nh, V_ROWS, MEM_LEN), lambda bi, i: (bi, 0, 0, 0))],
        out_specs=[out, out],
        out_shape=[jax.ShapeDtypeStruct((b, nh, l, ATT_DIM), F32)] * 2,
        scratch_shapes=[pltpu.VMEM((nh, ATT_GROUP * MOBA_BLOCK, TQ), F32),
                        pltpu.VMEM((nh, ATT_GROUP * MOBA_BLOCK, TQ), BF16)],
        compiler_params=_cparams(("parallel", "arbitrary")), name="attn",
    )(qat, ka, kad, vt, xqt, mem_k, mem_vt)


def _cand_layout():
    slabs = [("r1", 0, 16), ("r1", 1, 8), ("r1", 2, 8), ("r1", 3, 8), ("r2", 0, 16), ("r2", 1, 8), ("r2", 2, 8)]
    pos, valid = [], []
    for kind, fixed, n in slabs:
        for j in range(n):
            r1, r2 = (fixed, j) if kind == "r1" else (j, fixed)
            ok = (r1 + 1) * (r2 + 1) <= PEER_TOPK and ((kind == "r1") or r1 >= 4)
            pos.append(r1 * PEER_TOPK + r2)
            valid.append(ok)
    assert sum(valid) == 50
    return slabs, np.asarray(pos, np.int32), np.asarray(valid)


_CAND_SLABS, _CAND_POS, _CAND_VALID = _cand_layout()
_NCAND = len(_CAND_POS)


def _split_cols(k):
    hi = k.astype(BF16)
    lo = (k - hi.astype(F32)).astype(BF16)
    return jnp.concatenate([hi, hi, lo], axis=1)


def _dot_split(k_cat, q):
    hi = q.astype(BF16)
    lo = (q - hi.astype(F32)).astype(BF16)
    return jnp.dot(k_cat, jnp.concatenate([hi, lo, hi], axis=0), preferred_element_type=F32)


def _pack_rows(x):
    return pltpu.bitcast(x.astype(BF16), jnp.uint32)


def _unpack_rows(x):
    return pltpu.bitcast(x, BF16)


def _top16_exact(sets):
    row = lax.broadcasted_iota(jnp.int32, (PEER_NKEYS, LANE), 0).astype(F32)
    for _, rank_scr, _ in sets:
        rank_scr[...] = jnp.full(rank_scr.shape, float(PEER_TOPK), F32)

    def body(r, _):
        rf = lax.convert_element_type(r, F32)
        for half in range(TM_PP // LANE):
            ls = slice(half * LANE, (half + 1) * LANE)
            for w_scr, rank_scr, v_scr in sets:
                s = w_scr[:, ls]
                m = jnp.max(s, axis=0, keepdims=True)
                hit = row == jnp.min(jnp.where(s == m, row, float(PEER_NKEYS)), axis=0, keepdims=True)
                w_scr[:, ls] = jnp.where(hit, -jnp.inf, s)
                v_scr[half, pl.ds(r, 1), :] = m
                rank_scr[:, ls] = jnp.where(hit, rf, rank_scr[:, ls])
        return 0

    lax.fori_loop(0, PEER_TOPK, body, 0)


def _top16_distinct(sets):
    chains = [(w_scr, rank_scr, v_scr, half) for half in range(TM_PP // LANE) for w_scr, rank_scr, v_scr in sets]

    def body(r, prev):
        out = []
        for (w_scr, _, v_scr, half), m_prev in zip(chains, prev):
            s = w_scr[:, half * LANE:(half + 1) * LANE]
            m = jnp.max(jnp.where(s < m_prev, s, -jnp.inf), axis=0, keepdims=True)
            v_scr[half, pl.ds(r, 1), :] = m
            out.append(m)
        return tuple(out)

    lax.fori_loop(0, PEER_TOPK, body, tuple(jnp.full((1, LANE), jnp.inf, F32) for _ in chains))
    ranked = []
    for half in range(TM_PP // LANE):
        ls = slice(half * LANE, (half + 1) * LANE)
        total = jnp.zeros((1, LANE), F32)
        for w_scr, rank_scr, v_scr in sets:
            s = w_scr[:, ls]
            v = v_scr[half]
            if rank_scr is None:
                hit = s >= v[PEER_TOPK - 1:PEER_TOPK, :]
            else:
                rank = jnp.full(s.shape, float(PEER_TOPK), F32)
                for r in range(PEER_TOPK):
                    rank = jnp.where(s == v[r:r + 1, :], float(r), rank)
                rank_scr[:, ls] = rank
                hit = rank < float(PEER_TOPK)
            total = total + jnp.sum(jnp.where(hit, 1.0, 0.0), axis=0, keepdims=True)
        ranked.append(total)
    return jnp.concatenate(ranked, axis=1)


def _peer_prep_body(x_ref, ys_ref, ym_ref, yx_ref, wos_ref, wom_ref, wox_ref, g_ref, wqt_ref,
                    k1_ref, k2_ref, pos_ref, cbias_ref,
                    h1_ref, hnt_ref, r2_ref, f_ref, n_ref, c_ref,
                    q_scr, w1_scr, w2_scr, rank1_scr, rank2_scr, v1_scr, v2_scr, cnt_scr, z_scr):
    hres = x_ref[...] + jnp.dot(ys_ref[...], wos_ref[...], preferred_element_type=F32)
    for h in range(ATT_HEADS):
        hres = hres + jnp.dot(ym_ref[0, h].astype(BF16), wom_ref[h], preferred_element_type=F32)
        hres = hres + jnp.dot(yx_ref[0, h].astype(BF16), wox_ref[h], preferred_element_type=F32)
    h1_ref[...] = hres
    ms = jnp.mean(hres * hres, axis=-1, keepdims=True)
    hn = (hres * lax.rsqrt(ms + NORM_EPS)) * g_ref[...]
    hn_t = hn.T.astype(BF16)
    hnt_ref[...] = hn_t
    q_scr[...] = jnp.dot(wqt_ref[...], hn_t, preferred_element_type=F32)

    pos = pos_ref[...]
    cbias = cbias_ref[...]
    r16 = lax.broadcasted_iota(jnp.int32, (PEER_TOPK, TM_PP), 0).astype(F32)
    halves = lambda scr: jnp.concatenate([scr[j] for j in range(TM_PP // LANE)], axis=1)

    def head(h, _):
        base = pl.multiple_of(h * (2 * PEER_HALF), 2 * PEER_HALF)
        s1 = _dot_split(k1_ref[...], q_scr[pl.ds(base, PEER_HALF), :])
        s2 = _dot_split(k2_ref[...], q_scr[pl.ds(base + PEER_HALF, PEER_HALF), :])
        sets = [(w1_scr, rank1_scr, v1_scr), (w2_scr, rank2_scr, v2_scr)]
        w1_scr[...] = s1
        w2_scr[...] = s2
        ranked = _top16_distinct([(w1_scr, None, v1_scr), (w2_scr, rank2_scr, v2_scr)])
        tied = jnp.max(jnp.abs(ranked - 2.0 * PEER_TOPK)) > 0.5

        @pl.when(tied)
        def _():
            _top16_exact(sets)

        rank2 = rank2_scr[...]
        v1 = halves(v1_scr)
        v2 = halves(v2_scr)
        parts = []
        for kind, fixed, n in _CAND_SLABS:
            if kind == "r1":
                parts.append(v1[fixed:fixed + 1, :] + v2[0:n, :])
            else:
                parts.append(v1[0:n, :] + v2[fixed:fixed + 1, :])
        cand = jnp.concatenate(parts, axis=0) + cbias
        top = v1[0:1, :] + v2[0:1, :]

        def descend(_, carry):
            m_prev, zsum = carry
            m = jnp.max(jnp.where(cand < m_prev, cand, -jnp.inf), axis=0, keepdims=True)
            return m, zsum + jnp.exp(m - top)

        tau, zsum = lax.fori_loop(0, PEER_TOPK, descend,
                                  (jnp.full((1, TM_PP), jnp.inf, F32), jnp.zeros((1, TM_PP), F32)))
        taken = jnp.where(cand >= tau, 1.0, 0.0)
        cnt = jnp.zeros((PEER_TOPK, TM_PP), F32)
        row0 = 0
        for kind, fixed, n_rows in _CAND_SLABS:
            blk = taken[row0:row0 + n_rows, :]
            if kind == "r1":
                cnt = cnt + jnp.where(r16 == float(fixed), jnp.sum(blk, axis=0, keepdims=True), 0.0)
            elif n_rows == PEER_TOPK:
                cnt = cnt + blk
            else:
                cnt = cnt + jnp.concatenate([blk, jnp.zeros((PEER_TOPK - n_rows, TM_PP), F32)], axis=0)
            row0 += n_rows
        cnt_scr[...] = cnt
        z_scr[...] = zsum
        tied2 = jnp.max(jnp.abs(jnp.sum(cnt, axis=0, keepdims=True) - float(PEER_TOPK))) > 0.5

        @pl.when(tied2)
        def _():
            def pick(_, carry):
                cand, cnt, zsum = carry
                m = jnp.max(cand, axis=0, keepdims=True)
                p = jnp.min(jnp.where(cand == m, pos, float(4 * PEER_TOPK * PEER_TOPK)), axis=0, keepdims=True)
                cand = jnp.where(pos == p, -jnp.inf, cand)
                cnt = cnt + jnp.where(r16 == jnp.floor(p * (1.0 / PEER_TOPK)), 1.0, 0.0)
                return cand, cnt, zsum + jnp.exp(m - top)

            _, cnt, zsum = lax.fori_loop(
                0, PEER_TOPK, pick, (cand, jnp.zeros((PEER_TOPK, TM_PP), F32), jnp.zeros((1, TM_PP), F32)))
            cnt_scr[...] = cnt
            z_scr[...] = zsum

        cnt = cnt_scr[...]
        zsum = z_scr[...]
        def write_nsel(is_rank_r):
            nsel = jnp.zeros((PEER_NKEYS, TM_PP), F32)
            for r in range(PEER_TOPK):
                nsel = jnp.where(is_rank_r(r), cnt[r:r + 1, :], nsel)
            for lc in range(TM_PP // LANE):
                n_ref[h, lc] = nsel[:, lc * LANE:(lc + 1) * LANE]

        @pl.when(tied)
        def _():
            rank1 = rank1_scr[...]
            write_nsel(lambda r: rank1 == float(r))

        @pl.when(jnp.logical_not(tied))
        def _():
            write_nsel(lambda r: s1 == v1[r:r + 1, :])

        r2_ref[h] = _pack_rows(rank2)
        f_ref[h] = _pack_rows(jnp.exp(s2 - v2[0:1, :]))
        cw = jnp.exp(s1 - v1[0:1, :]) * (0.5 / zsum)
        for lc in range(TM_PP // LANE):
            c_ref[h, lc] = cw[:, lc * LANE:(lc + 1) * LANE]
        return 0

    lax.fori_loop(0, PEER_HEADS, head, 0)


def _peer_prep(x2d, y_ssd, y_moba, y_mem, wo_ssd, wo_moba, wo_mem, gain, wq_t, k1, k2):
    t = x2d.shape[0]
    b, _, l, _ = y_moba.shape
    per_b = l // TM_PP
    pos = jnp.asarray(np.broadcast_to(_CAND_POS.astype(np.float32)[:, None], (_NCAND, TM_PP)))
    cbias = jnp.asarray(np.broadcast_to(np.where(_CAND_VALID, 0.0, -np.inf).astype(np.float32)[:, None],
                                        (_NCAND, TM_PP)))
    tok = lambda w: pl.BlockSpec((TM_PP, w), lambda i: (i, 0))
    att = pl.BlockSpec((1, ATT_HEADS, TM_PP, ATT_DIM), lambda i: (i // per_b, 0, i % per_b, 0))
    meta = pl.BlockSpec((PEER_HEADS, TM_PP // LANE, PEER_NKEYS, LANE), lambda i: (0, i, 0, 0))
    meta_f32 = jax.ShapeDtypeStruct((PEER_HEADS, t // LANE, PEER_NKEYS, LANE), F32)
    packed = pl.BlockSpec((PEER_HEADS, PEER_NKEYS // 2, TM_PP), lambda i: (0, 0, i))
    meta_pk = jax.ShapeDtypeStruct((PEER_HEADS, PEER_NKEYS // 2, t), jnp.uint32)
    return pl.pallas_call(
        _peer_prep_body, grid=(t // TM_PP,),
        in_specs=[tok(D_MODEL), tok(SSD_WIDTH), att, att, _full(wo_ssd.shape), _full(wo_moba.shape),
                  _full(wo_mem.shape), _full((1, D_MODEL)), _full(wq_t.shape), _full(k1.shape), _full(k2.shape),
                  _full(pos.shape), _full(cbias.shape)],
        out_specs=[tok(D_MODEL), pl.BlockSpec((D_MODEL, TM_PP), lambda i: (0, i)), packed, packed, meta, meta],
        out_shape=[jax.ShapeDtypeStruct((t, D_MODEL), F32), jax.ShapeDtypeStruct((D_MODEL, t), BF16),
                   meta_pk, meta_pk, meta_f32, meta_f32],
        scratch_shapes=[pltpu.VMEM((PEER_HEADS * 2 * PEER_HALF, TM_PP), F32)]
        + [pltpu.VMEM((PEER_NKEYS, TM_PP), F32)] * 4
        + [pltpu.VMEM((TM_PP // LANE, PEER_TOPK, LANE), F32)] * 2
        + [pltpu.VMEM((PEER_TOPK, TM_PP), F32), pltpu.VMEM((1, TM_PP), F32)],
        compiler_params=_cparams(("parallel",)), name="peer_prep",
    )(x2d, y_ssd, y_moba, y_mem, wo_ssd, wo_moba, wo_mem, gain, wq_t, k1, k2, pos, cbias)


_A_PER_SLAB = ES_PEER // PEER_NKEYS
_PEER_STEPS = PEER_EXPERTS // (PEER_SLABS * ES_PEER)


def _peer_gate_chunk(s_ref, a_ref, a_base, k, lc, r2_ref, f_ref, n_ref, c_ref):
    zero = jnp.zeros((PEER_NKEYS, LANE), BF16)

    def bcast(row):
        tile = jnp.broadcast_to(row, (16, LANE)).astype(BF16)
        return jnp.concatenate([tile] * (PEER_NKEYS // 16), axis=0)

    a = a_base + k
    rows = slice(k * PEER_NKEYS, (k + 1) * PEER_NKEYS)
    ls = slice(lc * LANE, (lc + 1) * LANE)
    s = s_ref[rows, ls].astype(BF16)
    act = s * (1.0 + lax.erf(s * math.sqrt(0.5)))
    g = zero
    for h in range(PEER_HEADS):
        nrow = bcast(n_ref[h, lc, pl.ds(a, 1), :])
        crow = bcast(c_ref[h, lc, pl.ds(a, 1), :])
        g = g + jnp.where(_unpack_rows(r2_ref[h, :, ls]) < nrow, _unpack_rows(f_ref[h, :, ls]) * crow, zero)
    a_ref[k * (PEER_NKEYS // 2):(k + 1) * (PEER_NKEYS // 2), ls] = pltpu.bitcast(act * g, jnp.uint32)


def _peer_body(*refs):
    s_n = PEER_SLABS
    wd0_ref = refs[0]
    wd_next = refs[1:1 + s_n]
    wut_prev = refs[1 + s_n:1 + 2 * s_n]
    (wut_last_ref, hnt_ref, r2_ref, f_ref, n_ref, c_ref, h1_ref, o_ref,
     acc_scr, s_scr, a_scr) = refs[1 + 2 * s_n:]
    n = pl.program_id(1)
    meta = (r2_ref, f_ref, n_ref, c_ref)

    @pl.when(n == 0)
    def _():
        acc_scr[...] = jnp.zeros_like(acc_scr)
        a_scr[s_n - 1] = jnp.zeros(a_scr.shape[1:], a_scr.dtype)
        s_scr[0] = jnp.dot(wd0_ref[...], hnt_ref[...], preferred_element_type=F32)

    for j in range(s_n):
        for half in range(2):
            ls = slice(half * (TT_PEER // 2), (half + 1) * (TT_PEER // 2))
            acc_scr[:, ls] += jnp.dot(wut_prev[j][...], _unpack_rows(a_scr[(j - 1) % s_n, :, ls]),
                                      preferred_element_type=F32)
            s_scr[(j + 1) % s_n, :, ls] = jnp.dot(wd_next[j][...], hnt_ref[:, ls],
                                                  preferred_element_type=F32)
        a_base = (n * s_n + j) * _A_PER_SLAB
        for lc in range(TT_PEER // LANE):
            for k in range(_A_PER_SLAB):
                _peer_gate_chunk(s_scr.at[j], a_scr.at[j], a_base, k, lc, *meta)

    @pl.when(n == pl.num_programs(1) - 1)
    def _():
        tail = jnp.dot(wut_last_ref[...], _unpack_rows(a_scr[s_n - 1]), preferred_element_type=F32)
        o_ref[...] = h1_ref[...] + (acc_scr[...] + tail).T


def _peer(wd, wu_t, hn_t, rank2, f, nsel, c, h1):
    t = h1.shape[0]
    s_n = PEER_SLABS
    nblk = s_n * _PEER_STEPS
    meta = pl.BlockSpec((PEER_HEADS, TT_PEER // LANE, PEER_NKEYS, LANE), lambda i, n: (0, i, 0, 0))
    packed = pl.BlockSpec((PEER_HEADS, PEER_NKEYS // 2, TT_PEER), lambda i, n: (0, 0, i))

    def wd_blk(off):
        return pl.BlockSpec((ES_PEER, D_MODEL), lambda i, n: (jnp.clip(s_n * n + off, 0, nblk - 1), 0))

    def wut_blk(off):
        return pl.BlockSpec((None, D_MODEL, ES_PEER), lambda i, n: (jnp.clip(s_n * n + off, 0, nblk - 1), 0, 0))

    return pl.pallas_call(
        _peer_body, grid=(t // TT_PEER, _PEER_STEPS),
        in_specs=[pl.BlockSpec((ES_PEER, D_MODEL), lambda i, n: (0, 0))]
        + [wd_blk(j + 1) for j in range(s_n)] + [wut_blk(j - 1) for j in range(s_n)]
        + [pl.BlockSpec((None, D_MODEL, ES_PEER), lambda i, n: (nblk - 1, 0, 0)),
           pl.BlockSpec((D_MODEL, TT_PEER), lambda i, n: (0, i)), packed, packed, meta, meta,
           pl.BlockSpec((TT_PEER, D_MODEL), lambda i, n: (i, 0))],
        out_specs=pl.BlockSpec((TT_PEER, D_MODEL), lambda i, n: (i, 0)),
        out_shape=jax.ShapeDtypeStruct((t, D_MODEL), F32),
        scratch_shapes=[pltpu.VMEM((D_MODEL, TT_PEER), F32),
                        pltpu.VMEM((s_n, ES_PEER, TT_PEER), F32),
                        pltpu.VMEM((s_n, ES_PEER // 2, TT_PEER), jnp.uint32)],
        compiler_params=_cparams(("parallel", "arbitrary")), name="peer",
    )(*([wd] * (s_n + 1) + [wu_t] * (s_n + 1) + [hn_t, rank2, f, nsel, c, h1]))


def _pad_heads(w):
    r = w.shape[0]
    w = w.reshape(r, -1, ATT_DIM)
    return jnp.pad(w, ((0, 0), (0, 0), (0, LANE - ATT_DIM))).reshape(r, -1)


def _pad_row(w):
    return jnp.pad(w, (0, LANE - w.shape[0])).reshape(1, LANE)


def kernel(x, mem, mix_norm_w, w_in, ssd_conv_w, ssd_conv_b, ssd_dt_bias, ssd_a_log, ssd_d, ssd_norm_w,
           moba_q_norm_w, moba_k_norm_w, mem_norm_w, w_mem_kv, xattn_q_norm_w, xattn_k_norm_w, w_out,
           ffn_norm_w, peer_w_query, peer_sub_keys_1, peer_sub_keys_2, peer_expert_down, peer_expert_up):
    b, l, d = x.shape
    depth = w_in.shape[0]
    h = x.reshape(b * l, d)
    for li in range(depth):
        wi = w_in[li]
        o = 0
        cols = {}
        for name, width in (("z", SSD_WIDTH), ("xbc", SSD_CONV_CH), ("dt", SSD_HEADS), ("mq", ATT_WIDTH),
                            ("mk", ATT_WIDTH), ("mv", ATT_WIDTH), ("xq", ATT_WIDTH)):
            cols[name] = wi[:, o:o + width]
            o += width
        w_list = [cols["z"], cols["xbc"], jnp.pad(cols["dt"], ((0, 0), (0, LANE - SSD_HEADS))),
                  cols["mq"], _pad_heads(cols["mk"]), cols["mv"], cols["xq"]]
        w_list = [w.astype(BF16) for w in w_list]
        z, xbc, dt, mq, mk, mv, xq = _in_proj(h, mix_norm_w[li].reshape(1, d), w_list)

        y_ssd = _ssd(z.reshape(b, l, -1), xbc.reshape(b, l, -1), dt.reshape(b, l, -1), ssd_conv_w[li],
                     ssd_conv_b[li], ssd_dt_bias[li], ssd_a_log[li], ssd_d[li], ssd_norm_w[li])

        wkv = w_mem_kv[li]
        mem_k, mem_vt = _mem_kv(mem, mem_norm_w[li].reshape(1, d), _pad_heads(wkv[:, :ATT_WIDTH]).astype(BF16),
                                _pad_heads(wkv[:, ATT_WIDTH:]).astype(BF16), _pad_row(xattn_k_norm_w[li]))
        r3 = lambda a: a.reshape(b, l, -1)
        per_head = lambda w: jnp.tile(w, ATT_HEADS).reshape(1, ATT_WIDTH)
        qat, ka, kad, vt, xqt = _attn_prep(r3(mq), r3(mk), r3(mv), r3(xq), per_head(moba_q_norm_w[li]),
                                           _pad_row(moba_k_norm_w[li]), per_head(xattn_q_norm_w[li]))
        y_moba, y_mem = _attn(qat, ka, kad, vt, xqt, mem_k, mem_vt)

        wo = w_out[li].astype(BF16)
        wo_moba = wo[SSD_WIDTH:SSD_WIDTH + ATT_WIDTH].reshape(ATT_HEADS, ATT_DIM, d)
        wo_mem = wo[SSD_WIDTH + ATT_WIDTH:].reshape(ATT_HEADS, ATT_DIM, d)
        h1, hn_t, rank2, f, nsel, c = _peer_prep(
            h, y_ssd.reshape(b * l, -1), y_moba, y_mem, wo[:SSD_WIDTH], wo_moba, wo_mem,
            ffn_norm_w[li].reshape(1, d), peer_w_query[li].T.astype(BF16), _split_cols(peer_sub_keys_1[li]),
            _split_cols(peer_sub_keys_2[li]))
        wu_t = peer_expert_up[li].reshape(-1, ES_PEER, d).transpose(0, 2, 1).astype(BF16)
        h = _peer(peer_expert_down[li].astype(BF16), wu_t, hn_t, rank2, f, nsel, c, h1)
    return h.reshape(b, l, d)
```

```python
import functools
import math

import numpy as np
import jax
import jax.numpy as jnp
from jax import lax
from jax.experimental import pallas as pl
from jax.experimental.pallas import tpu as pltpu

F32 = jnp.float32
BF16 = jnp.bfloat16
HIGHEST = lax.Precision.HIGHEST

NORM_EPS = 1e-6
D_MODEL = 1024
SSD_HEADS = 8
SSD_HEAD_DIM = 64
SSD_WIDTH = 512
SSD_GROUPS = 2
SSD_STATE = 128
SSD_CONV = 4
SSD_CONV_CH = 1024
ATT_HEADS = 4
ATT_DIM = 64
ATT_WIDTH = 256
MOBA_BLOCK = 256
MOBA_TOPK = 3
MEM_LEN = 256
PEER_HEADS = 8
PEER_NKEYS = 128
PEER_TOPK = 16
PEER_HALF = 64
PEER_EXPERTS = PEER_NKEYS * PEER_NKEYS

LANE = 128
NEG = -1e30
VMEM_LIMIT = 56 * 1024 * 1024

TM_IN = 512
SSD_CHUNK = 256
TQ = MOBA_BLOCK
ATT_GROUP = 4
TM_PP = 512
TT_PEER = 512
ES_PEER = 512
PEER_SLABS = 4

_SLOPES = [2.0 ** (-8.0 * (i + 1) / ATT_HEADS) for i in range(ATT_HEADS)]


def _bf16_split(v):
    hi = float(np.float32(v).astype(BF16).astype(np.float32))
    lo = float(np.float32(v - hi).astype(BF16).astype(np.float32))
    return hi, lo


def _cparams(sem):
    return pltpu.CompilerParams(dimension_semantics=sem, vmem_limit_bytes=VMEM_LIMIT)


def _sigmoid(x):
    return 1.0 / (1.0 + jnp.exp(-x))


def _full(shape):
    n = len(shape)
    return pl.BlockSpec(shape, lambda *_: (0,) * n)


def _in_proj_body(x_ref, g_ref, *refs):
    n = len(refs) // 2
    w_refs, o_refs = refs[:n], refs[n:]
    x = x_ref[...]
    ms = jnp.mean(x * x, axis=-1, keepdims=True)
    xn = ((x * lax.rsqrt(ms + NORM_EPS)) * g_ref[...]).astype(BF16)
    for w_ref, o_ref in zip(w_refs, o_refs):
        o_ref[...] = jnp.dot(xn, w_ref[...], preferred_element_type=F32).astype(o_ref.dtype)


def _in_proj(x2d, gain, weights):
    t = x2d.shape[0]
    in_specs = [pl.BlockSpec((TM_IN, D_MODEL), lambda i: (i, 0)), _full((1, D_MODEL))]
    in_specs += [_full(w.shape) for w in weights]
    out_specs = [pl.BlockSpec((TM_IN, w.shape[1]), lambda i: (i, 0)) for w in weights]
    out_shape = [jax.ShapeDtypeStruct((t, w.shape[1]), F32) for w in weights]
    return pl.pallas_call(
        _in_proj_body, grid=(t // TM_IN,), in_specs=in_specs, out_specs=out_specs,
        out_shape=out_shape, compiler_params=_cparams(("parallel",)), name="in_proj",
    )(x2d, gain, *weights)


def _ssd_body(z_ref, xbc_ref, dt_ref, cw_ref, cb_ref, dtb_ref, alog_ref, dskip_ref, nw_ref, e_ref,
              y_ref, ext_scr, state_scr):
    q = SSD_CHUNK
    c = pl.program_id(1)

    @pl.when(c == 0)
    def _():
        ext_scr[0:8, :] = jnp.zeros((8, SSD_CONV_CH), F32)
        state_scr[...] = jnp.zeros_like(state_scr)

    u = xbc_ref[0]
    ext_scr[8:8 + q, :] = u
    acc = cb_ref[...] + cw_ref[3:4, :] * u
    acc = acc + cw_ref[2:3, :] * ext_scr[7:7 + q, :]
    acc = acc + cw_ref[1:2, :] * ext_scr[6:6 + q, :]
    acc = acc + cw_ref[0:1, :] * ext_scr[5:5 + q, :]
    ext_scr[0:8, :] = u[q - 8:q, :]
    act = acc * _sigmoid(acc)
    xs = act[:, 0:SSD_WIDTH]
    bm = act[:, SSD_WIDTH:SSD_WIDTH + SSD_GROUPS * SSD_STATE]
    cm = act[:, SSD_WIDTH + SSD_GROUPS * SSD_STATE:]

    dtr = dt_ref[0] + dtb_ref[...]
    dt = jnp.maximum(dtr, 0.0) + jnp.log(1.0 + jnp.exp(-jnp.abs(dtr)))
    a = dt * (-jnp.exp(alog_ref[...]))
    row = lax.broadcasted_iota(jnp.int32, (q, q), 0)
    col = lax.broadcasted_iota(jnp.int32, (q, q), 1)
    causal = row >= col
    a_cs = jnp.dot(causal.astype(F32), a, precision=HIGHEST, preferred_element_type=F32)
    a_cs_t = a_cs.T
    a_cs_w = jnp.dot(a_cs, e_ref[...], precision=HIGHEST, preferred_element_type=F32)
    dt_w = jnp.dot(dt, e_ref[...], precision=HIGHEST, preferred_element_type=F32)
    total_w = a_cs_w[q - 1:q, :]
    exp_cs_w = jnp.exp(a_cs_w)
    dte_w = jnp.exp(total_w - a_cs_w)
    cd_w = jnp.exp(total_w)
    xdt_w = xs * dt_w
    lane = lax.broadcasted_iota(jnp.int32, (1, LANE), 1)
    first = lane < SSD_HEAD_DIM

    z = z_ref[0]
    gated = []
    for g in range(SSD_GROUPS):
        bg = bm[:, g * SSD_STATE:(g + 1) * SSD_STATE]
        cg = cm[:, g * SSD_STATE:(g + 1) * SSD_STATE].astype(BF16)
        cb = lax.dot_general(cg, bg.astype(BF16), (((1,), (1,)), ((), ())), preferred_element_type=F32)
        bg_t = bg.T.astype(BF16)
        for kk in range(2):
            k = 2 * g + kk
            sl = slice(k * LANE, (k + 1) * LANE)
            xdt = xdt_w[:, sl]
            xdt_b = xdt.astype(BF16)
            yd = []
            for hh in range(2):
                h = 2 * k + hh
                seg = a_cs[:, h:h + 1] - a_cs_t[h:h + 1, :]
                lm = jnp.exp(jnp.where(causal, seg, -jnp.inf))
                yd.append(jnp.dot((cb * lm).astype(BF16), xdt_b, preferred_element_type=F32))
            y = jnp.where(first, yd[0], yd[1])
            s_old = state_scr[k]
            y = y + jnp.dot(cg, s_old.astype(BF16), preferred_element_type=F32) * exp_cs_w[:, sl]
            y = y + xs[:, sl] * dskip_ref[:, sl]
            state_scr[k] = s_old * cd_w[:, sl] + jnp.dot(
                bg_t, (xdt * dte_w[:, sl]).astype(BF16), preferred_element_type=F32)
            zz = z[:, sl]
            gated.append(y * (zz * _sigmoid(zz)))
    for g in range(SSD_GROUPS):
        y0, y1 = gated[2 * g], gated[2 * g + 1]
        ms = (jnp.sum(y0 * y0, axis=-1, keepdims=True)
              + jnp.sum(y1 * y1, axis=-1, keepdims=True)) * (1.0 / (SSD_WIDTH // SSD_GROUPS))
        r = lax.rsqrt(ms + NORM_EPS)
        lo = 2 * g * LANE
        y_ref[0, :, lo:lo + LANE] = (y0 * r * nw_ref[:, lo:lo + LANE]).astype(y_ref.dtype)
        y_ref[0, :, lo + LANE:lo + 2 * LANE] = (y1 * r * nw_ref[:, lo + LANE:lo + 2 * LANE]).astype(y_ref.dtype)


def _ssd(z, xbc, dt, conv_w, conv_b, dt_bias, a_log, d_skip, norm_w):
    b, l, _ = z.shape
    q = SSD_CHUNK
    pad = LANE - SSD_HEADS
    dtb = jnp.pad(dt_bias, (0, pad)).reshape(1, LANE)
    alog = jnp.pad(a_log, (0, pad)).reshape(1, LANE)
    dsk = jnp.repeat(d_skip, SSD_HEAD_DIM).reshape(1, SSD_WIDTH)
    expand = (jnp.arange(LANE)[:, None] == (jnp.arange(SSD_WIDTH)[None, :] // SSD_HEAD_DIM)).astype(F32)
    tok = lambda w: pl.BlockSpec((1, q, w), lambda bi, ci: (bi, ci, 0))
    return pl.pallas_call(
        _ssd_body, grid=(b, l // q),
        in_specs=[tok(SSD_WIDTH), tok(SSD_CONV_CH), tok(LANE), _full((SSD_CONV, SSD_CONV_CH)),
                  _full((1, SSD_CONV_CH)), _full((1, LANE)), _full((1, LANE)), _full((1, SSD_WIDTH)),
                  _full((1, SSD_WIDTH)), _full((LANE, SSD_WIDTH))],
        out_specs=tok(SSD_WIDTH),
        out_shape=jax.ShapeDtypeStruct((b, l, SSD_WIDTH), BF16),
        scratch_shapes=[pltpu.VMEM((q + 8, SSD_CONV_CH), F32),
                        pltpu.VMEM((SSD_HEADS // 2, SSD_STATE, LANE), F32)],
        compiler_params=_cparams(("parallel", "arbitrary")), name="ssd",
    )(z, xbc, dt, conv_w, conv_b.reshape(1, -1), dtb, alog, dsk, norm_w.reshape(1, -1), expand)


V_ROWS = ATT_DIM + 16


def _with_ones_row(v_t):
    n = v_t.shape[1]
    r = lax.broadcasted_iota(jnp.int32, (V_ROWS - ATT_DIM, n), 0)
    return jnp.concatenate([v_t, jnp.where(r == 0, 1.0, 0.0)], axis=0)


def _head_rms(xh, w_row):
    ms = jnp.sum(xh * xh, axis=-1, keepdims=True) * (1.0 / ATT_DIM)
    return xh * lax.rsqrt(ms + NORM_EPS) * w_row


def _mem_kv_body(mem_ref, g_ref, wk_ref, wv_ref, kw_ref, k_ref, vt_ref):
    x = mem_ref[0]
    ms = jnp.mean(x * x, axis=-1, keepdims=True)
    xn = ((x * lax.rsqrt(ms + NORM_EPS)) * g_ref[...]).astype(BF16)
    kp = jnp.dot(xn, wk_ref[...], preferred_element_type=F32)
    vp = jnp.dot(xn, wv_ref[...], preferred_element_type=F32)
    for h in range(ATT_HEADS):
        sl = slice(h * LANE, (h + 1) * LANE)
        k_ref[0, h] = _head_rms(kp[:, sl], kw_ref[...]).astype(BF16)
        vt_ref[0, h] = _with_ones_row(vp[:, sl].T[0:ATT_DIM, :]).astype(BF16)


def _mem_kv(mem, gain, wk, wv, k_norm_w):
    b = mem.shape[0]
    return pl.pallas_call(
        _mem_kv_body, grid=(b,),
        in_specs=[pl.BlockSpec((1, MEM_LEN, D_MODEL), lambda i: (i, 0, 0)), _full((1, D_MODEL)),
                  _full(wk.shape), _full(wv.shape), _full((1, LANE))],
        out_specs=[pl.BlockSpec((1, ATT_HEADS, MEM_LEN, LANE), lambda i: (i, 0, 0, 0)),
                   pl.BlockSpec((1, ATT_HEADS, V_ROWS, MEM_LEN), lambda i: (i, 0, 0, 0))],
        out_shape=[jax.ShapeDtypeStruct((b, ATT_HEADS, MEM_LEN, LANE), BF16),
                   jax.ShapeDtypeStruct((b, ATT_HEADS, V_ROWS, MEM_LEN), BF16)],
        compiler_params=_cparams(("parallel",)), name="mem_kv",
    )(mem, gain, wk, wv, k_norm_w)


MAX_BLOCKS = 32
_AUG_ONEHOT = ATT_DIM
_AUG_EXTRA = ATT_DIM + MAX_BLOCKS


def _attn_prep_body(slopes, q_ref, k_ref, v_ref, xq_ref, qw_ref, kw_ref, xqw_ref, seg_ref,
                    qat_ref, ka_ref, kad_ref, vt_ref, xqt_ref, kmean_scr):
    i = pl.program_id(1)
    nb = kmean_scr.shape[1]

    @pl.when(i == 0)
    def _():
        kmean_scr[...] = jnp.zeros_like(kmean_scr)

    def heads_rms_t(x, w_row):
        ms = jnp.dot(x * x, seg_ref[...], precision=HIGHEST, preferred_element_type=F32)
        return (x * lax.rsqrt(ms + NORM_EPS) * w_row).T

    qn_t_all = heads_rms_t(q_ref[0], qw_ref[...])
    xqn_t_all = heads_rms_t(xq_ref[0], xqw_ref[...])
    v_t_all = v_ref[0].T

    n_iota = lax.broadcasted_iota(jnp.int32, (nb, TQ), 0)
    past = n_iota < i
    t_loc = lax.broadcasted_iota(jnp.int32, (32, TQ), 1).astype(F32)
    r32 = lax.broadcasted_iota(jnp.int32, (32, TQ), 0)
    k_lane = lax.broadcasted_iota(jnp.int32, (TQ, LANE), 1)
    s_loc = lax.broadcasted_iota(jnp.int32, (TQ, LANE), 0).astype(F32)
    blk = lax.convert_element_type(i, F32)
    for h in range(ATT_HEADS):
        hi, lo = slopes[h]
        sl = slice(h * LANE, (h + 1) * LANE)
        hs = slice(h * ATT_DIM, (h + 1) * ATT_DIM)
        kn = _head_rms(k_ref[0, :, sl], kw_ref[...])
        qn_t = qn_t_all[hs, :]
        gate = jnp.dot(kmean_scr[h][:, 0:ATT_DIM], qn_t, precision=HIGHEST,
                       preferred_element_type=F32)
        gate = jnp.where(past, gate, -jnp.inf)
        cnt = jnp.zeros((nb, TQ), F32)
        for n2 in range(nb):
            gn = gate[n2:n2 + 1, :]
            ahead = jnp.where(gn > gate, 1.0, jnp.where(gn == gate, jnp.where(n_iota > n2, 1.0, 0.0), 0.0))
            cnt = cnt + ahead
        allowed = jnp.where(past, jnp.where(cnt < float(MOBA_TOPK), 1.0, 0.0), 0.0)
        bias = jnp.where(allowed > 0.5, 0.0, NEG)
        extra = jnp.where(r32 < 2, t_loc, 0.0)
        for r, val in ((2, hi), (3, lo), (4, MOBA_BLOCK * hi), (5, MOBA_BLOCK * lo)):
            extra = jnp.where(r32 == r, val, extra)
        extra = jnp.where((r32 == 6) | (r32 == 7), blk, extra)
        qat = jnp.concatenate([qn_t * (1.0 / math.sqrt(ATT_DIM)), bias, extra], axis=0)
        qat_ref[0, h] = qat.astype(BF16)
        kx = jnp.where((k_lane == _AUG_EXTRA + 2) | (k_lane == _AUG_EXTRA + 3), s_loc, 0.0)
        kx = jnp.where((k_lane == _AUG_EXTRA + 4) | (k_lane == _AUG_EXTRA + 5), blk, kx)
        for c, val in ((0, -hi), (1, -lo), (6, -MOBA_BLOCK * hi), (7, -MOBA_BLOCK * lo)):
            kx = jnp.where(k_lane == _AUG_EXTRA + c, val, kx)
        kad = kn + kx
        kad_ref[0, h] = kad.astype(BF16)
        ka_ref[0, h] = (kad + jnp.where(k_lane - _AUG_ONEHOT == i, 1.0, 0.0)).astype(BF16)
        vt_ref[0, h] = _with_ones_row(v_t_all[hs, :]).astype(BF16)
        xqt_ref[0, h] = jnp.concatenate([xqn_t_all[hs, :] * (1.0 / math.sqrt(ATT_DIM)),
                                         jnp.zeros((LANE - ATT_DIM, TQ), F32)], axis=0).astype(BF16)
        kmean_scr[h, pl.ds(i, 1), :] = jnp.sum(kn, axis=0, keepdims=True) * (1.0 / MOBA_BLOCK)


def _attn_prep(q, k, v, xq, qw, kw, xqw):
    b, l, _ = q.shape
    nb = l // MOBA_BLOCK
    assert nb <= MAX_BLOCKS
    slopes = tuple(_bf16_split(s) for s in _SLOPES)
    tok = pl.BlockSpec((1, TQ, ATT_HEADS * LANE), lambda bi, i: (bi, i, 0))
    flat = pl.BlockSpec((1, TQ, ATT_WIDTH), lambda bi, i: (bi, i, 0))
    head_of = jnp.arange(ATT_WIDTH) // ATT_DIM
    seg = (head_of[:, None] == head_of[None, :]).astype(F32) * (1.0 / ATT_DIM)
    return pl.pallas_call(
        functools.partial(_attn_prep_body, slopes), grid=(b, nb),
        in_specs=[flat, tok, flat, flat, _full((1, ATT_WIDTH)), _full((1, LANE)), _full((1, ATT_WIDTH)),
                  _full((ATT_WIDTH, ATT_WIDTH))],
        out_specs=[pl.BlockSpec((1, ATT_HEADS, LANE, TQ), lambda bi, i: (bi, 0, 0, i)),
                   pl.BlockSpec((1, ATT_HEADS, TQ, LANE), lambda bi, i: (bi, 0, i, 0)),
                   pl.BlockSpec((1, ATT_HEADS, TQ, LANE), lambda bi, i: (bi, 0, i, 0)),
                   pl.BlockSpec((1, ATT_HEADS, V_ROWS, TQ), lambda bi, i: (bi, 0, 0, i)),
                   pl.BlockSpec((1, ATT_HEADS, LANE, TQ), lambda bi, i: (bi, 0, 0, i))],
        out_shape=[jax.ShapeDtypeStruct((b, ATT_HEADS, LANE, l), BF16),
                   jax.ShapeDtypeStruct((b, ATT_HEADS, l, LANE), BF16),
                   jax.ShapeDtypeStruct((b, ATT_HEADS, l, LANE), BF16),
                   jax.ShapeDtypeStruct((b, ATT_HEADS, V_ROWS, l), BF16),
                   jax.ShapeDtypeStruct((b, ATT_HEADS, LANE, l), BF16)],
        scratch_shapes=[pltpu.VMEM((ATT_HEADS, MAX_BLOCKS, LANE), F32)],
        compiler_params=_cparams(("parallel", "arbitrary")), name="attn_prep",
    )(q, k, v, xq, qw, kw, xqw, seg)


def _normalized_t(acc):
    return (acc[0:ATT_DIM, :] * (1.0 / acc[ATT_DIM:ATT_DIM + 1, :])).T


def _attn_body(qat_ref, ka_ref, kad_ref, vt_ref, xqt_ref, mk_ref, mvt_ref, o_ref, om_ref, s_scr, p_scr):
    i = pl.program_id(1)
    base = pl.multiple_of(i * MOBA_BLOCK, MOBA_BLOCK)
    key = lax.broadcasted_iota(jnp.int32, (MOBA_BLOCK, TQ), 0)
    qry = lax.broadcasted_iota(jnp.int32, (MOBA_BLOCK, TQ), 1)

    span = ATT_GROUP * MOBA_BLOCK
    n_groups = (i + ATT_GROUP - 1) // ATT_GROUP
    last_group = ka_ref.shape[2] // span - 1

    def scores(h, g):
        off = pl.multiple_of(jnp.minimum(g, last_group) * span, span)
        return jnp.dot(ka_ref[0, h, pl.ds(off, span), :], qat_ref[0, h], preferred_element_type=F32)

    def values(h, g, p):
        off = pl.multiple_of(jnp.maximum(g, 0) * span, span)
        return jnp.dot(vt_ref[0, h, :, pl.ds(off, span)], p, preferred_element_type=F32)

    init = []
    for h in range(ATT_HEADS):
        s = jnp.dot(kad_ref[0, h], qat_ref[0, h], preferred_element_type=F32)
        s = jnp.where(key <= qry, s, NEG)
        m = jnp.max(s, axis=0, keepdims=True)
        p = jnp.exp(s - m).astype(BF16)
        acc = jnp.dot(vt_ref[0, h, :, pl.ds(base, MOBA_BLOCK)], p, preferred_element_type=F32)
        s_scr[h] = scores(h, 0)
        p_scr[h] = jnp.zeros((span, TQ), BF16)
        init += [jnp.ones((1, TQ), F32), m, acc]

    def body(g, carry):
        out = []
        for h in range(ATT_HEADS):
            alpha_prev, m, acc = carry[3 * h:3 * h + 3]
            acc = acc * alpha_prev + values(h, g - 1, p_scr[h])
            s = s_scr[h]
            m_new = jnp.maximum(m, jnp.max(s, axis=0, keepdims=True))
            p_scr[h] = jnp.exp(s - m_new).astype(BF16)
            s_scr[h] = scores(h, g + 1)
            out += [jnp.exp(m - m_new), m_new, acc]
        return tuple(out)

    fin = lax.fori_loop(0, n_groups, body, tuple(init))
    for h in range(ATT_HEADS):
        alpha_prev, _, acc = fin[3 * h:3 * h + 3]
        o_ref[0, h] = _normalized_t(acc * alpha_prev + values(h, n_groups - 1, p_scr[h]))
        sm = jnp.dot(mk_ref[0, h], xqt_ref[0, h], preferred_element_type=F32)
        pm = jnp.exp(sm - jnp.max(sm, axis=0, keepdims=True)).astype(BF16)
        om_ref[0, h] = _normalized_t(jnp.dot(mvt_ref[0, h], pm, preferred_element_type=F32))


def _attn(qat, ka, kad, vt, xqt, mem_k, mem_vt):
    b, _, _, l = qat.shape
    assert l % (ATT_GROUP * MOBA_BLOCK) == 0
    nh = ATT_HEADS
    per_q = lambda r: pl.BlockSpec((1, nh, r, TQ), lambda bi, i: (bi, 0, 0, i))
    out = pl.BlockSpec((1, nh, TQ, ATT_DIM), lambda bi, i: (bi, 0, i, 0))
    return pl.pallas_call(
        _attn_body, grid=(b, l // TQ),
        in_specs=[per_q(LANE),
                  pl.BlockSpec((1, nh, l, LANE), lambda bi, i: (bi, 0, 0, 0)),
                  pl.BlockSpec((1, nh, TQ, LANE), lambda bi, i: (bi, 0, i, 0)),
                  pl.BlockSpec((1, nh, V_ROWS, l), lambda bi, i: (bi, 0, 0, 0)),
                  per_q(LANE),
                  pl.BlockSpec((1, nh, MEM_LEN, LANE), lambda bi, i: (bi, 0, 0, 0)),
                  pl.BlockSpec((1, nh, V_ROWS, MEM_LEN), lambda bi, i: (bi, 0, 0, 0))],
        out_specs=[out, out],
        out_shape=[jax.ShapeDtypeStruct((b, nh, l, ATT_DIM), F32)] * 2,
        scratch_shapes=[pltpu.VMEM((nh, ATT_GROUP * MOBA_BLOCK, TQ), F32),
                        pltpu.VMEM((nh, ATT_GROUP * MOBA_BLOCK, TQ), BF16)],
        compiler_params=_cparams(("parallel", "arbitrary")), name="attn",
    )(qat, ka, kad, vt, xqt, mem_k, mem_vt)


def _cand_layout():
    slabs = [("r1", 0, 16), ("r1", 1, 8), ("r1", 2, 8), ("r1", 3, 8), ("r2", 0, 16), ("r2", 1, 8), ("r2", 2, 8)]
    pos, valid = [], []
    for kind, fixed, n in slabs:
        for j in range(n):
            r1, r2 = (fixed, j) if kind == "r1" else (j, fixed)
            ok = (r1 + 1) * (r2 + 1) <= PEER_TOPK and ((kind == "r1") or r1 >= 4)
            pos.append(r1 * PEER_TOPK + r2)
            valid.append(ok)
    assert sum(valid) == 50
    return slabs, np.asarray(pos, np.int32), np.asarray(valid)


_CAND_SLABS, _CAND_POS, _CAND_VALID = _cand_layout()
_NCAND = len(_CAND_POS)


def _split_cols(k):
    hi = k.astype(BF16)
    lo = (k - hi.astype(F32)).astype(BF16)
    return jnp.concatenate([hi, hi, lo], axis=1)


def _dot_split(k_cat, q):
    hi = q.astype(BF16)
    lo = (q - hi.astype(F32)).astype(BF16)
    return jnp.dot(k_cat, jnp.concatenate([hi, lo, hi], axis=0), preferred_element_type=F32)


def _pack_rows(x):
    return pltpu.bitcast(x.astype(BF16), jnp.uint32)


def _unpack_rows(x):
    return pltpu.bitcast(x, BF16)


def _top16_exact(sets):
    row = lax.broadcasted_iota(jnp.int32, (PEER_NKEYS, LANE), 0).astype(F32)
    for _, rank_scr, _ in sets:
        rank_scr[...] = jnp.full(rank_scr.shape, float(PEER_TOPK), F32)

    def body(r, _):
        rf = lax.convert_element_type(r, F32)
        for half in range(TM_PP // LANE):
            ls = slice(half * LANE, (half + 1) * LANE)
            for w_scr, rank_scr, v_scr in sets:
                s = w_scr[:, ls]
                m = jnp.max(s, axis=0, keepdims=True)
                hit = row == jnp.min(jnp.where(s == m, row, float(PEER_NKEYS)), axis=0, keepdims=True)
                w_scr[:, ls] = jnp.where(hit, -jnp.inf, s)
                v_scr[half, pl.ds(r, 1), :] = m
                rank_scr[:, ls] = jnp.where(hit, rf, rank_scr[:, ls])
        return 0

    lax.fori_loop(0, PEER_TOPK, body, 0)


def _top16_distinct(sets):
    chains = [(w_scr, rank_scr, v_scr, half) for half in range(TM_PP // LANE) for w_scr, rank_scr, v_scr in sets]

    def body(r, prev):
        out = []
        for (w_scr, _, v_scr, half), m_prev in zip(chains, prev):
            s = w_scr[:, half * LANE:(half + 1) * LANE]
            m = jnp.max(jnp.where(s < m_prev, s, -jnp.inf), axis=0, keepdims=True)
            v_scr[half, pl.ds(r, 1), :] = m
            out.append(m)
        return tuple(out)

    lax.fori_loop(0, PEER_TOPK, body, tuple(jnp.full((1, LANE), jnp.inf, F32) for _ in chains))
    ranked = []
    for half in range(TM_PP // LANE):
        ls = slice(half * LANE, (half + 1) * LANE)
        total = jnp.zeros((1, LANE), F32)
        for w_scr, rank_scr, v_scr in sets:
            s = w_scr[:, ls]
            v = v_scr[half]
            if rank_scr is None:
                hit = s >= v[PEER_TOPK - 1:PEER_TOPK, :]
            else:
                rank = jnp.full(s.shape, float(PEER_TOPK), F32)
                for r in range(PEER_TOPK):
                    rank = jnp.where(s == v[r:r + 1, :], float(r), rank)
                rank_scr[:, ls] = rank
                hit = rank < float(PEER_TOPK)
            total = total + jnp.sum(jnp.where(hit, 1.0, 0.0), axis=0, keepdims=True)
        ranked.append(total)
    return jnp.concatenate(ranked, axis=1)


def _peer_prep_body(x_ref, ys_ref, ym_ref, yx_ref, wos_ref, wom_ref, wox_ref, g_ref, wqt_ref,
                    k1_ref, k2_ref, pos_ref, cbias_ref,
                    h1_ref, hnt_ref, r2_ref, f_ref, n_ref, c_ref,
                    q_scr, w1_scr, w2_scr, rank1_scr, rank2_scr, v1_scr, v2_scr, cnt_scr, z_scr):
    hres = x_ref[...] + jnp.dot(ys_ref[...], wos_ref[...], preferred_element_type=F32)
    for h in range(ATT_HEADS):
        hres = hres + jnp.dot(ym_ref[0, h].astype(BF16), wom_ref[h], preferred_element_type=F32)
        hres = hres + jnp.dot(yx_ref[0, h].astype(BF16), wox_ref[h], preferred_element_type=F32)
    h1_ref[...] = hres
    ms = jnp.mean(hres * hres, axis=-1, keepdims=True)
    hn = (hres * lax.rsqrt(ms + NORM_EPS)) * g_ref[...]
    hn_t = hn.T.astype(BF16)
    hnt_ref[...] = hn_t
    q_scr[...] = jnp.dot(wqt_ref[...], hn_t, preferred_element_type=F32)

    pos = pos_ref[...]
    cbias = cbias_ref[...]
    r16 = lax.broadcasted_iota(jnp.int32, (PEER_TOPK, TM_PP), 0).astype(F32)
    halves = lambda scr: jnp.concatenate([scr[j] for j in range(TM_PP // LANE)], axis=1)

    def head(h, _):
        base = pl.multiple_of(h * (2 * PEER_HALF), 2 * PEER_HALF)
        s1 = _dot_split(k1_ref[...], q_scr[pl.ds(base, PEER_HALF), :])
        s2 = _dot_split(k2_ref[...], q_scr[pl.ds(base + PEER_HALF, PEER_HALF), :])
        sets = [(w1_scr, rank1_scr, v1_scr), (w2_scr, rank2_scr, v2_scr)]
        w1_scr[...] = s1
        w2_scr[...] = s2
        ranked = _top16_distinct([(w1_scr, None, v1_scr), (w2_scr, rank2_scr, v2_scr)])
        tied = jnp.max(jnp.abs(ranked - 2.0 * PEER_TOPK)) > 0.5

        @pl.when(tied)
        def _():
            _top16_exact(sets)

        rank2 = rank2_scr[...]
        v1 = halves(v1_scr)
        v2 = halves(v2_scr)
        parts = []
        for kind, fixed, n in _CAND_SLABS:
            if kind == "r1":
                parts.append(v1[fixed:fixed + 1, :] + v2[0:n, :])
            else:
                parts.append(v1[0:n, :] + v2[fixed:fixed + 1, :])
        cand = jnp.concatenate(parts, axis=0) + cbias
        top = v1[0:1, :] + v2[0:1, :]

        def descend(_, carry):
            m_prev, zsum = carry
            m = jnp.max(jnp.where(cand < m_prev, cand, -jnp.inf), axis=0, keepdims=True)
            return m, zsum + jnp.exp(m - top)

        tau, zsum = lax.fori_loop(0, PEER_TOPK, descend,
                                  (jnp.full((1, TM_PP), jnp.inf, F32), jnp.zeros((1, TM_PP), F32)))
        taken = jnp.where(cand >= tau, 1.0, 0.0)
        cnt = jnp.zeros((PEER_TOPK, TM_PP), F32)
        row0 = 0
        for kind, fixed, n_rows in _CAND_SLABS:
            blk = taken[row0:row0 + n_rows, :]
            if kind == "r1":
                cnt = cnt + jnp.where(r16 == float(fixed), jnp.sum(blk, axis=0, keepdims=True), 0.0)
            elif n_rows == PEER_TOPK:
                cnt = cnt + blk
            else:
                cnt = cnt + jnp.concatenate([blk, jnp.zeros((PEER_TOPK - n_rows, TM_PP), F32)], axis=0)
            row0 += n_rows
        cnt_scr[...] = cnt
        z_scr[...] = zsum
        tied2 = jnp.max(jnp.abs(jnp.sum(cnt, axis=0, keepdims=True) - float(PEER_TOPK))) > 0.5

        @pl.when(tied2)
        def _():
            def pick(_, carry):
                cand, cnt, zsum = carry
                m = jnp.max(cand, axis=0, keepdims=True)
                p = jnp.min(jnp.where(cand == m, pos, float(4 * PEER_TOPK * PEER_TOPK)), axis=0, keepdims=True)
                cand = jnp.where(pos == p, -jnp.inf, cand)
                cnt = cnt + jnp.where(r16 == jnp.floor(p * (1.0 / PEER_TOPK)), 1.0, 0.0)
                return cand, cnt, zsum + jnp.exp(m - top)

            _, cnt, zsum = lax.fori_loop(
                0, PEER_TOPK, pick, (cand, jnp.zeros((PEER_TOPK, TM_PP), F32), jnp.zeros((1, TM_PP), F32)))
            cnt_scr[...] = cnt
            z_scr[...] = zsum

        cnt = cnt_scr[...]
        zsum = z_scr[...]
        def write_nsel(is_rank_r):
            nsel = jnp.zeros((PEER_NKEYS, TM_PP), F32)
            for r in range(PEER_TOPK):
                nsel = jnp.where(is_rank_r(r), cnt[r:r + 1, :], nsel)
            for lc in range(TM_PP // LANE):
                n_ref[h, lc] = nsel[:, lc * LANE:(lc + 1) * LANE]

        @pl.when(tied)
        def _():
            rank1 = rank1_scr[...]
            write_nsel(lambda r: rank1 == float(r))

        @pl.when(jnp.logical_not(tied))
        def _():
            write_nsel(lambda r: s1 == v1[r:r + 1, :])

        r2_ref[h] = _pack_rows(rank2)
        f_ref[h] = _pack_rows(jnp.exp(s2 - v2[0:1, :]))
        cw = jnp.exp(s1 - v1[0:1, :]) * (0.5 / zsum)
        for lc in range(TM_PP // LANE):
            c_ref[h, lc] = cw[:, lc * LANE:(lc + 1) * LANE]
        return 0

    lax.fori_loop(0, PEER_HEADS, head, 0)


def _peer_prep(x2d, y_ssd, y_moba, y_mem, wo_ssd, wo_moba, wo_mem, gain, wq_t, k1, k2):
    t = x2d.shape[0]
    b, _, l, _ = y_moba.shape
    per_b = l // TM_PP
    pos = jnp.asarray(np.broadcast_to(_CAND_POS.astype(np.float32)[:, None], (_NCAND, TM_PP)))
    cbias = jnp.asarray(np.broadcast_to(np.where(_CAND_VALID, 0.0, -np.inf).astype(np.float32)[:, None],
                                        (_NCAND, TM_PP)))
    tok = lambda w: pl.BlockSpec((TM_PP, w), lambda i: (i, 0))
    att = pl.BlockSpec((1, ATT_HEADS, TM_PP, ATT_DIM), lambda i: (i // per_b, 0, i % per_b, 0))
    meta = pl.BlockSpec((PEER_HEADS, TM_PP // LANE, PEER_NKEYS, LANE), lambda i: (0, i, 0, 0))
    meta_f32 = jax.ShapeDtypeStruct((PEER_HEADS, t // LANE, PEER_NKEYS, LANE), F32)
    packed = pl.BlockSpec((PEER_HEADS, PEER_NKEYS // 2, TM_PP), lambda i: (0, 0, i))
    meta_pk = jax.ShapeDtypeStruct((PEER_HEADS, PEER_NKEYS // 2, t), jnp.uint32)
    return pl.pallas_call(
        _peer_prep_body, grid=(t // TM_PP,),
        in_specs=[tok(D_MODEL), tok(SSD_WIDTH), att, att, _full(wo_ssd.shape), _full(wo_moba.shape),
                  _full(wo_mem.shape), _full((1, D_MODEL)), _full(wq_t.shape), _full(k1.shape), _full(k2.shape),
                  _full(pos.shape), _full(cbias.shape)],
        out_specs=[tok(D_MODEL), pl.BlockSpec((D_MODEL, TM_PP), lambda i: (0, i)), packed, packed, meta, meta],
        out_shape=[jax.ShapeDtypeStruct((t, D_MODEL), F32), jax.ShapeDtypeStruct((D_MODEL, t), BF16),
                   meta_pk, meta_pk, meta_f32, meta_f32],
        scratch_shapes=[pltpu.VMEM((PEER_HEADS * 2 * PEER_HALF, TM_PP), F32)]
        + [pltpu.VMEM((PEER_NKEYS, TM_PP), F32)] * 4
        + [pltpu.VMEM((TM_PP // LANE, PEER_TOPK, LANE), F32)] * 2
        + [pltpu.VMEM((PEER_TOPK, TM_PP), F32), pltpu.VMEM((1, TM_PP), F32)],
        compiler_params=_cparams(("parallel",)), name="peer_prep",
    )(x2d, y_ssd, y_moba, y_mem, wo_ssd, wo_moba, wo_mem, gain, wq_t, k1, k2, pos, cbias)


_A_PER_SLAB = ES_PEER // PEER_NKEYS
_PEER_STEPS = PEER_EXPERTS // (PEER_SLABS * ES_PEER)


def _peer_gate_chunk(s_ref, a_ref, a_base, k, lc, r2_ref, f_ref, n_ref, c_ref):
    zero = jnp.zeros((PEER_NKEYS, LANE), BF16)

    def bcast(row):
        tile = jnp.broadcast_to(row, (16, LANE)).astype(BF16)
        return jnp.concatenate([tile] * (PEER_NKEYS // 16), axis=0)

    a = a_base + k
    rows = slice(k * PEER_NKEYS, (k + 1) * PEER_NKEYS)
    ls = slice(lc * LANE, (lc + 1) * LANE)
    s = s_ref[rows, ls].astype(BF16)
    act = s * (1.0 + lax.erf(s * math.sqrt(0.5)))
    g = zero
    for h in range(PEER_HEADS):
        nrow = bcast(n_ref[h, lc, pl.ds(a, 1), :])
        crow = bcast(c_ref[h, lc, pl.ds(a, 1), :])
        g = g + jnp.where(_unpack_rows(r2_ref[h, :, ls]) < nrow, _unpack_rows(f_ref[h, :, ls]) * crow, zero)
    a_ref[k * (PEER_NKEYS // 2):(k + 1) * (PEER_NKEYS // 2), ls] = pltpu.bitcast(act * g, jnp.uint32)


def _peer_body(*refs):
    s_n = PEER_SLABS
    wd0_ref = refs[0]
    wd_next = refs[1:1 + s_n]
    wut_prev = refs[1 + s_n:1 + 2 * s_n]
    (wut_last_ref, hnt_ref, hnt_nx_ref, r2_ref, f_ref, n_ref, c_ref, h1_ref, o_ref,
     acc_scr, s_scr, a_scr) = refs[1 + 2 * s_n:]
    n = pl.program_id(1)
    meta = (r2_ref, f_ref, n_ref, c_ref)

    @pl.when(n == 0)
    def _():
        acc_scr[...] = jnp.zeros_like(acc_scr)
        a_scr[s_n - 1] = jnp.zeros(a_scr.shape[1:], a_scr.dtype)

    @pl.when((n == 0) & (pl.program_id(0) == 0))
    def _():
        s_scr[0] = jnp.dot(wd0_ref[...], hnt_ref[...], preferred_element_type=F32)

    for j in range(s_n):
        hnt_for_next = hnt_nx_ref if j == s_n - 1 else hnt_ref
        for half in range(2):
            ls = slice(half * (TT_PEER // 2), (half + 1) * (TT_PEER // 2))
            acc_scr[:, ls] += jnp.dot(wut_prev[j][...], _unpack_rows(a_scr[(j - 1) % s_n, :, ls]),
                                      preferred_element_type=F32)
            s_scr[(j + 1) % s_n, :, ls] = jnp.dot(wd_next[j][...], hnt_for_next[:, ls],
                                                  preferred_element_type=F32)
        a_base = (n * s_n + j) * _A_PER_SLAB
        for lc in range(TT_PEER // LANE):
            for k in range(_A_PER_SLAB):
                _peer_gate_chunk(s_scr.at[j], a_scr.at[j], a_base, k, lc, *meta)

    @pl.when(n == pl.num_programs(1) - 1)
    def _():
        tail = jnp.dot(wut_last_ref[...], _unpack_rows(a_scr[s_n - 1]), preferred_element_type=F32)
        o_ref[...] = h1_ref[...] + (acc_scr[...] + tail).T


def _peer(wd, wu_t, hn_t, rank2, f, nsel, c, h1):
    t = h1.shape[0]
    s_n = PEER_SLABS
    nblk = s_n * _PEER_STEPS
    meta = pl.BlockSpec((PEER_HEADS, TT_PEER // LANE, PEER_NKEYS, LANE), lambda i, n: (0, i, 0, 0))
    packed = pl.BlockSpec((PEER_HEADS, PEER_NKEYS // 2, TT_PEER), lambda i, n: (0, 0, i))

    def wd_blk(off):
        return pl.BlockSpec((ES_PEER, D_MODEL), lambda i, n: ((s_n * n + off) % nblk, 0))

    n_tiles = t // TT_PEER
    hnt_next = pl.BlockSpec((D_MODEL, TT_PEER),
                            lambda i, n: (0, jnp.minimum(i + (n + 1) // _PEER_STEPS, n_tiles - 1)))

    def wut_blk(off):
        return pl.BlockSpec((None, D_MODEL, ES_PEER), lambda i, n: (jnp.clip(s_n * n + off, 0, nblk - 1), 0, 0))

    return pl.pallas_call(
        _peer_body, grid=(t // TT_PEER, _PEER_STEPS),
        in_specs=[pl.BlockSpec((ES_PEER, D_MODEL), lambda i, n: (0, 0))]
        + [wd_blk(j + 1) for j in range(s_n)] + [wut_blk(j - 1) for j in range(s_n)]
        + [pl.BlockSpec((None, D_MODEL, ES_PEER), lambda i, n: (nblk - 1, 0, 0)),
           pl.BlockSpec((D_MODEL, TT_PEER), lambda i, n: (0, i)), hnt_next, packed, packed, meta, meta,
           pl.BlockSpec((TT_PEER, D_MODEL), lambda i, n: (i, 0))],
        out_specs=pl.BlockSpec((TT_PEER, D_MODEL), lambda i, n: (i, 0)),
        out_shape=jax.ShapeDtypeStruct((t, D_MODEL), F32),
        scratch_shapes=[pltpu.VMEM((D_MODEL, TT_PEER), F32),
                        pltpu.VMEM((s_n, ES_PEER, TT_PEER), F32),
                        pltpu.VMEM((s_n, ES_PEER // 2, TT_PEER), jnp.uint32)],
        compiler_params=_cparams(("arbitrary", "arbitrary")), name="peer",
    )(*([wd] * (s_n + 1) + [wu_t] * (s_n + 1) + [hn_t, hn_t, rank2, f, nsel, c, h1]))


def _pad_heads(w):
    r = w.shape[0]
    w = w.reshape(r, -1, ATT_DIM)
    return jnp.pad(w, ((0, 0), (0, 0), (0, LANE - ATT_DIM))).reshape(r, -1)


def _pad_row(w):
    return jnp.pad(w, (0, LANE - w.shape[0])).reshape(1, LANE)


def kernel(x, mem, mix_norm_w, w_in, ssd_conv_w, ssd_conv_b, ssd_dt_bias, ssd_a_log, ssd_d, ssd_norm_w,
           moba_q_norm_w, moba_k_norm_w, mem_norm_w, w_mem_kv, xattn_q_norm_w, xattn_k_norm_w, w_out,
           ffn_norm_w, peer_w_query, peer_sub_keys_1, peer_sub_keys_2, peer_expert_down, peer_expert_up):
    b, l, d = x.shape
    depth = w_in.shape[0]
    h = x.reshape(b * l, d)
    for li in range(depth):
        wi = w_in[li]
        o = 0
        cols = {}
        for name, width in (("z", SSD_WIDTH), ("xbc", SSD_CONV_CH), ("dt", SSD_HEADS), ("mq", ATT_WIDTH),
                            ("mk", ATT_WIDTH), ("mv", ATT_WIDTH), ("xq", ATT_WIDTH)):
            cols[name] = wi[:, o:o + width]
            o += width
        w_list = [cols["z"], cols["xbc"], jnp.pad(cols["dt"], ((0, 0), (0, LANE - SSD_HEADS))),
                  cols["mq"], _pad_heads(cols["mk"]), cols["mv"], cols["xq"]]
        w_list = [w.astype(BF16) for w in w_list]
        z, xbc, dt, mq, mk, mv, xq = _in_proj(h, mix_norm_w[li].reshape(1, d), w_list)

        y_ssd = _ssd(z.reshape(b, l, -1), xbc.reshape(b, l, -1), dt.reshape(b, l, -1), ssd_conv_w[li],
                     ssd_conv_b[li], ssd_dt_bias[li], ssd_a_log[li], ssd_d[li], ssd_norm_w[li])

        wkv = w_mem_kv[li]
        mem_k, mem_vt = _mem_kv(mem, mem_norm_w[li].reshape(1, d), _pad_heads(wkv[:, :ATT_WIDTH]).astype(BF16),
                                _pad_heads(wkv[:, ATT_WIDTH:]).astype(BF16), _pad_row(xattn_k_norm_w[li]))
        r3 = lambda a: a.reshape(b, l, -1)
        per_head = lambda w: jnp.tile(w, ATT_HEADS).reshape(1, ATT_WIDTH)
        qat, ka, kad, vt, xqt = _attn_prep(r3(mq), r3(mk), r3(mv), r3(xq), per_head(moba_q_norm_w[li]),
                                           _pad_row(moba_k_norm_w[li]), per_head(xattn_q_norm_w[li]))
        y_moba, y_mem = _attn(qat, ka, kad, vt, xqt, mem_k, mem_vt)

        wo = w_out[li].astype(BF16)
        wo_moba = wo[SSD_WIDTH:SSD_WIDTH + ATT_WIDTH].reshape(ATT_HEADS, ATT_DIM, d)
        wo_mem = wo[SSD_WIDTH + ATT_WIDTH:].reshape(ATT_HEADS, ATT_DIM, d)
        h1, hn_t, rank2, f, nsel, c = _peer_prep(
            h, y_ssd.reshape(b * l, -1), y_moba, y_mem, wo[:SSD_WIDTH], wo_moba, wo_mem,
            ffn_norm_w[li].reshape(1, d), peer_w_query[li].T.astype(BF16), _split_cols(peer_sub_keys_1[li]),
            _split_cols(peer_sub_keys_2[li]))
        wu_t = peer_expert_up[li].reshape(-1, ES_PEER, d).transpose(0, 2, 1).astype(BF16)
        h = _peer(peer_expert_down[li].astype(BF16), wu_t, hn_t, rank2, f, nsel, c, h1)
    return h.reshape(b, l, d)
```

```python
import functools
import math

import numpy as np
import jax
import jax.numpy as jnp
from jax import lax
from jax.experimental import pallas as pl
from jax.experimental.pallas import tpu as pltpu

F32 = jnp.float32
BF16 = jnp.bfloat16
HIGHEST = lax.Precision.HIGHEST

NORM_EPS = 1e-6
D_MODEL = 1024
SSD_HEADS = 8
SSD_HEAD_DIM = 64
SSD_WIDTH = 512
SSD_GROUPS = 2
SSD_STATE = 128
SSD_CONV = 4
SSD_CONV_CH = 1024
ATT_HEADS = 4
ATT_DIM = 64
ATT_WIDTH = 256
MOBA_BLOCK = 256
MOBA_TOPK = 3
MEM_LEN = 256
PEER_HEADS = 8
PEER_NKEYS = 128
PEER_TOPK = 16
PEER_HALF = 64
PEER_EXPERTS = PEER_NKEYS * PEER_NKEYS

LANE = 128
NEG = -1e30
VMEM_LIMIT = 56 * 1024 * 1024

TM_IN = 512
SSD_CHUNK = 256
TQ = MOBA_BLOCK
ATT_GROUP = 4
TQ_MEM = 1024
TM_PP = 512
TT_PEER = 512
ES_PEER = 512
PEER_SLABS = 4

_SLOPES = [2.0 ** (-8.0 * (i + 1) / ATT_HEADS) for i in range(ATT_HEADS)]


def _bf16_split(v):
    hi = float(np.float32(v).astype(BF16).astype(np.float32))
    lo = float(np.float32(v - hi).astype(BF16).astype(np.float32))
    return hi, lo


def _cparams(sem):
    return pltpu.CompilerParams(dimension_semantics=sem, vmem_limit_bytes=VMEM_LIMIT)


def _sigmoid(x):
    return 1.0 / (1.0 + jnp.exp(-x))


def _full(shape):
    n = len(shape)
    return pl.BlockSpec(shape, lambda *_: (0,) * n)


def _in_proj_body(x_ref, g_ref, *refs):
    n = len(refs) // 2
    w_refs, o_refs = refs[:n], refs[n:]
    x = x_ref[...]
    ms = jnp.mean(x * x, axis=-1, keepdims=True)
    xn = ((x * lax.rsqrt(ms + NORM_EPS)) * g_ref[...]).astype(BF16)
    for w_ref, o_ref in zip(w_refs, o_refs):
        o_ref[...] = jnp.dot(xn, w_ref[...], preferred_element_type=F32).astype(o_ref.dtype)


def _in_proj(x2d, gain, weights):
    t = x2d.shape[0]
    in_specs = [pl.BlockSpec((TM_IN, D_MODEL), lambda i: (i, 0)), _full((1, D_MODEL))]
    in_specs += [_full(w.shape) for w in weights]
    out_specs = [pl.BlockSpec((TM_IN, w.shape[1]), lambda i: (i, 0)) for w in weights]
    out_shape = [jax.ShapeDtypeStruct((t, w.shape[1]), F32) for w in weights]
    return pl.pallas_call(
        _in_proj_body, grid=(t // TM_IN,), in_specs=in_specs, out_specs=out_specs,
        out_shape=out_shape, compiler_params=_cparams(("parallel",)), name="in_proj",
    )(x2d, gain, *weights)


def _ssd_body(z_ref, xbc_ref, dt_ref, cw_ref, cb_ref, dtb_ref, alog_ref, dskip_ref, nw_ref, e_ref,
              y_ref, ext_scr, state_scr):
    q = SSD_CHUNK
    c = pl.program_id(1)

    @pl.when(c == 0)
    def _():
        ext_scr[0:8, :] = jnp.zeros((8, SSD_CONV_CH), F32)
        state_scr[...] = jnp.zeros_like(state_scr)

    u = xbc_ref[0]
    ext_scr[8:8 + q, :] = u
    acc = cb_ref[...] + cw_ref[3:4, :] * u
    acc = acc + cw_ref[2:3, :] * ext_scr[7:7 + q, :]
    acc = acc + cw_ref[1:2, :] * ext_scr[6:6 + q, :]
    acc = acc + cw_ref[0:1, :] * ext_scr[5:5 + q, :]
    ext_scr[0:8, :] = u[q - 8:q, :]
    act = acc * _sigmoid(acc)
    xs = act[:, 0:SSD_WIDTH]
    bm = act[:, SSD_WIDTH:SSD_WIDTH + SSD_GROUPS * SSD_STATE]
    cm = act[:, SSD_WIDTH + SSD_GROUPS * SSD_STATE:]

    dtr = dt_ref[0] + dtb_ref[...]
    dt = jnp.maximum(dtr, 0.0) + jnp.log(1.0 + jnp.exp(-jnp.abs(dtr)))
    a = dt * (-jnp.exp(alog_ref[...]))
    row = lax.broadcasted_iota(jnp.int32, (q, q), 0)
    col = lax.broadcasted_iota(jnp.int32, (q, q), 1)
    causal = row >= col
    a_cs = jnp.dot(causal.astype(F32), a, precision=HIGHEST, preferred_element_type=F32)
    a_cs_t = a_cs.T
    a_cs_w = jnp.dot(a_cs, e_ref[...], precision=HIGHEST, preferred_element_type=F32)
    dt_w = jnp.dot(dt, e_ref[...], precision=HIGHEST, preferred_element_type=F32)
    total_w = a_cs_w[q - 1:q, :]
    exp_cs_w = jnp.exp(a_cs_w)
    dte_w = jnp.exp(total_w - a_cs_w)
    cd_w = jnp.exp(total_w)
    xdt_w = xs * dt_w
    lane = lax.broadcasted_iota(jnp.int32, (1, LANE), 1)
    first = lane < SSD_HEAD_DIM

    z = z_ref[0]
    gated = []
    for g in range(SSD_GROUPS):
        bg = bm[:, g * SSD_STATE:(g + 1) * SSD_STATE]
        cg = cm[:, g * SSD_STATE:(g + 1) * SSD_STATE].astype(BF16)
        cb = lax.dot_general(cg, bg.astype(BF16), (((1,), (1,)), ((), ())), preferred_element_type=F32)
        bg_t = bg.T.astype(BF16)
        for kk in range(2):
            k = 2 * g + kk
            sl = slice(k * LANE, (k + 1) * LANE)
            xdt = xdt_w[:, sl]
            xdt_b = xdt.astype(BF16)
            yd = []
            for hh in range(2):
                h = 2 * k + hh
                seg = a_cs[:, h:h + 1] - a_cs_t[h:h + 1, :]
                lm = jnp.exp(jnp.where(causal, seg, -jnp.inf))
                yd.append(jnp.dot((cb * lm).astype(BF16), xdt_b, preferred_element_type=F32))
            y = jnp.where(first, yd[0], yd[1])
            s_old = state_scr[k]
            y = y + jnp.dot(cg, s_old.astype(BF16), preferred_element_type=F32) * exp_cs_w[:, sl]
            y = y + xs[:, sl] * dskip_ref[:, sl]
            state_scr[k] = s_old * cd_w[:, sl] + jnp.dot(
                bg_t, (xdt * dte_w[:, sl]).astype(BF16), preferred_element_type=F32)
            zz = z[:, sl]
            gated.append(y * (zz * _sigmoid(zz)))
    for g in range(SSD_GROUPS):
        y0, y1 = gated[2 * g], gated[2 * g + 1]
        ms = (jnp.sum(y0 * y0, axis=-1, keepdims=True)
              + jnp.sum(y1 * y1, axis=-1, keepdims=True)) * (1.0 / (SSD_WIDTH // SSD_GROUPS))
        r = lax.rsqrt(ms + NORM_EPS)
        lo = 2 * g * LANE
        y_ref[0, :, lo:lo + LANE] = (y0 * r * nw_ref[:, lo:lo + LANE]).astype(y_ref.dtype)
        y_ref[0, :, lo + LANE:lo + 2 * LANE] = (y1 * r * nw_ref[:, lo + LANE:lo + 2 * LANE]).astype(y_ref.dtype)


def _ssd(z, xbc, dt, conv_w, conv_b, dt_bias, a_log, d_skip, norm_w):
    b, l, _ = z.shape
    q = SSD_CHUNK
    pad = LANE - SSD_HEADS
    dtb = jnp.pad(dt_bias, (0, pad)).reshape(1, LANE)
    alog = jnp.pad(a_log, (0, pad)).reshape(1, LANE)
    dsk = jnp.repeat(d_skip, SSD_HEAD_DIM).reshape(1, SSD_WIDTH)
    expand = (jnp.arange(LANE)[:, None] == (jnp.arange(SSD_WIDTH)[None, :] // SSD_HEAD_DIM)).astype(F32)
    tok = lambda w: pl.BlockSpec((1, q, w), lambda bi, ci: (bi, ci, 0))
    return pl.pallas_call(
        _ssd_body, grid=(b, l // q),
        in_specs=[tok(SSD_WIDTH), tok(SSD_CONV_CH), tok(LANE), _full((SSD_CONV, SSD_CONV_CH)),
                  _full((1, SSD_CONV_CH)), _full((1, LANE)), _full((1, LANE)), _full((1, SSD_WIDTH)),
                  _full((1, SSD_WIDTH)), _full((LANE, SSD_WIDTH))],
        out_specs=tok(SSD_WIDTH),
        out_shape=jax.ShapeDtypeStruct((b, l, SSD_WIDTH), BF16),
        scratch_shapes=[pltpu.VMEM((q + 8, SSD_CONV_CH), F32),
                        pltpu.VMEM((SSD_HEADS // 2, SSD_STATE, LANE), F32)],
        compiler_params=_cparams(("parallel", "arbitrary")), name="ssd",
    )(z, xbc, dt, conv_w, conv_b.reshape(1, -1), dtb, alog, dsk, norm_w.reshape(1, -1), expand)


V_ROWS = ATT_DIM + 16


def _with_ones_row(v_t):
    n = v_t.shape[1]
    r = lax.broadcasted_iota(jnp.int32, (V_ROWS - ATT_DIM, n), 0)
    return jnp.concatenate([v_t, jnp.where(r == 0, 1.0, 0.0)], axis=0)


def _head_rms(xh, w_row):
    ms = jnp.sum(xh * xh, axis=-1, keepdims=True) * (1.0 / ATT_DIM)
    return xh * lax.rsqrt(ms + NORM_EPS) * w_row


def _mem_kv_body(mem_ref, g_ref, wk_ref, wv_ref, kw_ref, k_ref, vt_ref):
    x = mem_ref[0]
    ms = jnp.mean(x * x, axis=-1, keepdims=True)
    xn = ((x * lax.rsqrt(ms + NORM_EPS)) * g_ref[...]).astype(BF16)
    kp = jnp.dot(xn, wk_ref[...], preferred_element_type=F32)
    vp = jnp.dot(xn, wv_ref[...], preferred_element_type=F32)
    for h in range(ATT_HEADS):
        sl = slice(h * LANE, (h + 1) * LANE)
        k_ref[0, h] = _head_rms(kp[:, sl], kw_ref[...]).astype(BF16)
        vt_ref[0, h] = _with_ones_row(vp[:, sl].T[0:ATT_DIM, :]).astype(BF16)


def _mem_kv(mem, gain, wk, wv, k_norm_w):
    b = mem.shape[0]
    return pl.pallas_call(
        _mem_kv_body, grid=(b,),
        in_specs=[pl.BlockSpec((1, MEM_LEN, D_MODEL), lambda i: (i, 0, 0)), _full((1, D_MODEL)),
                  _full(wk.shape), _full(wv.shape), _full((1, LANE))],
        out_specs=[pl.BlockSpec((1, ATT_HEADS, MEM_LEN, LANE), lambda i: (i, 0, 0, 0)),
                   pl.BlockSpec((1, ATT_HEADS, V_ROWS, MEM_LEN), lambda i: (i, 0, 0, 0))],
        out_shape=[jax.ShapeDtypeStruct((b, ATT_HEADS, MEM_LEN, LANE), BF16),
                   jax.ShapeDtypeStruct((b, ATT_HEADS, V_ROWS, MEM_LEN), BF16)],
        compiler_params=_cparams(("parallel",)), name="mem_kv",
    )(mem, gain, wk, wv, k_norm_w)


MAX_BLOCKS = 32
_AUG_ONEHOT = ATT_DIM
_AUG_EXTRA = ATT_DIM + MAX_BLOCKS


def _attn_prep_body(slopes, q_ref, k_ref, v_ref, xq_ref, qw_ref, kw_ref, xqw_ref, seg_ref,
                    qat_ref, ka_ref, kad_ref, vt_ref, xqt_ref, kmean_scr):
    i = pl.program_id(1)
    nb = kmean_scr.shape[1]

    @pl.when(i == 0)
    def _():
        kmean_scr[...] = jnp.zeros_like(kmean_scr)

    def heads_rms_t(x, w_row):
        ms = jnp.dot(x * x, seg_ref[...], precision=HIGHEST, preferred_element_type=F32)
        return (x * lax.rsqrt(ms + NORM_EPS) * w_row).T

    qn_t_all = heads_rms_t(q_ref[0], qw_ref[...])
    xqn_t_all = heads_rms_t(xq_ref[0], xqw_ref[...])
    v_t_all = v_ref[0].T

    n_iota = lax.broadcasted_iota(jnp.int32, (nb, TQ), 0)
    past = n_iota < i
    t_loc = lax.broadcasted_iota(jnp.int32, (32, TQ), 1).astype(F32)
    r32 = lax.broadcasted_iota(jnp.int32, (32, TQ), 0)
    k_lane = lax.broadcasted_iota(jnp.int32, (TQ, LANE), 1)
    s_loc = lax.broadcasted_iota(jnp.int32, (TQ, LANE), 0).astype(F32)
    blk = lax.convert_element_type(i, F32)
    for h in range(ATT_HEADS):
        hi, lo = slopes[h]
        sl = slice(h * LANE, (h + 1) * LANE)
        hs = slice(h * ATT_DIM, (h + 1) * ATT_DIM)
        kn = _head_rms(k_ref[0, :, sl], kw_ref[...])
        qn_t = qn_t_all[hs, :]
        gate = jnp.dot(kmean_scr[h][:, 0:ATT_DIM], qn_t, precision=HIGHEST,
                       preferred_element_type=F32)
        gate = jnp.where(past, gate, -jnp.inf)
        cnt = jnp.zeros((nb, TQ), F32)
        for n2 in range(nb):
            gn = gate[n2:n2 + 1, :]
            ahead = jnp.where(gn > gate, 1.0, jnp.where(gn == gate, jnp.where(n_iota > n2, 1.0, 0.0), 0.0))
            cnt = cnt + ahead
        allowed = jnp.where(past, jnp.where(cnt < float(MOBA_TOPK), 1.0, 0.0), 0.0)
        bias = jnp.where(allowed > 0.5, 0.0, NEG)
        extra = jnp.where(r32 < 2, t_loc, 0.0)
        for r, val in ((2, hi), (3, lo), (4, MOBA_BLOCK * hi), (5, MOBA_BLOCK * lo)):
            extra = jnp.where(r32 == r, val, extra)
        extra = jnp.where((r32 == 6) | (r32 == 7), blk, extra)
        qat = jnp.concatenate([qn_t * (1.0 / math.sqrt(ATT_DIM)), bias, extra], axis=0)
        qat_ref[0, h] = qat.astype(BF16)
        kx = jnp.where((k_lane == _AUG_EXTRA + 2) | (k_lane == _AUG_EXTRA + 3), s_loc, 0.0)
        kx = jnp.where((k_lane == _AUG_EXTRA + 4) | (k_lane == _AUG_EXTRA + 5), blk, kx)
        for c, val in ((0, -hi), (1, -lo), (6, -MOBA_BLOCK * hi), (7, -MOBA_BLOCK * lo)):
            kx = jnp.where(k_lane == _AUG_EXTRA + c, val, kx)
        kad = kn + kx
        kad_ref[0, h] = kad.astype(BF16)
        ka_ref[0, h] = (kad + jnp.where(k_lane - _AUG_ONEHOT == i, 1.0, 0.0)).astype(BF16)
        vt_ref[0, h] = _with_ones_row(v_t_all[hs, :]).astype(BF16)
        xqt_ref[0, h] = jnp.concatenate([xqn_t_all[hs, :] * (1.0 / math.sqrt(ATT_DIM)),
                                         jnp.zeros((LANE - ATT_DIM, TQ), F32)], axis=0).astype(BF16)
        kmean_scr[h, pl.ds(i, 1), :] = jnp.sum(kn, axis=0, keepdims=True) * (1.0 / MOBA_BLOCK)


def _attn_prep(q, k, v, xq, qw, kw, xqw):
    b, l, _ = q.shape
    nb = l // MOBA_BLOCK
    assert nb <= MAX_BLOCKS
    slopes = tuple(_bf16_split(s) for s in _SLOPES)
    tok = pl.BlockSpec((1, TQ, ATT_HEADS * LANE), lambda bi, i: (bi, i, 0))
    flat = pl.BlockSpec((1, TQ, ATT_WIDTH), lambda bi, i: (bi, i, 0))
    head_of = jnp.arange(ATT_WIDTH) // ATT_DIM
    seg = (head_of[:, None] == head_of[None, :]).astype(F32) * (1.0 / ATT_DIM)
    return pl.pallas_call(
        functools.partial(_attn_prep_body, slopes), grid=(b, nb),
        in_specs=[flat, tok, flat, flat, _full((1, ATT_WIDTH)), _full((1, LANE)), _full((1, ATT_WIDTH)),
                  _full((ATT_WIDTH, ATT_WIDTH))],
        out_specs=[pl.BlockSpec((1, ATT_HEADS, LANE, TQ), lambda bi, i: (bi, 0, 0, i)),
                   pl.BlockSpec((1, ATT_HEADS, TQ, LANE), lambda bi, i: (bi, 0, i, 0)),
                   pl.BlockSpec((1, ATT_HEADS, TQ, LANE), lambda bi, i: (bi, 0, i, 0)),
                   pl.BlockSpec((1, ATT_HEADS, V_ROWS, TQ), lambda bi, i: (bi, 0, 0, i)),
                   pl.BlockSpec((1, ATT_HEADS, LANE, TQ), lambda bi, i: (bi, 0, 0, i))],
        out_shape=[jax.ShapeDtypeStruct((b, ATT_HEADS, LANE, l), BF16),
                   jax.ShapeDtypeStruct((b, ATT_HEADS, l, LANE), BF16),
                   jax.ShapeDtypeStruct((b, ATT_HEADS, l, LANE), BF16),
                   jax.ShapeDtypeStruct((b, ATT_HEADS, V_ROWS, l), BF16),
                   jax.ShapeDtypeStruct((b, ATT_HEADS, LANE, l), BF16)],
        scratch_shapes=[pltpu.VMEM((ATT_HEADS, MAX_BLOCKS, LANE), F32)],
        compiler_params=_cparams(("parallel", "arbitrary")), name="attn_prep",
    )(q, k, v, xq, qw, kw, xqw, seg)


def _normalized_t(acc):
    return (acc[0:ATT_DIM, :] * (1.0 / acc[ATT_DIM:ATT_DIM + 1, :])).T


def _attn_body(qat_ref, ka_ref, kad_ref, vt_ref, o_ref, s_scr, p_scr):
    i = pl.program_id(1)
    base = pl.multiple_of(i * MOBA_BLOCK, MOBA_BLOCK)
    key = lax.broadcasted_iota(jnp.int32, (MOBA_BLOCK, TQ), 0)
    qry = lax.broadcasted_iota(jnp.int32, (MOBA_BLOCK, TQ), 1)

    span = ATT_GROUP * MOBA_BLOCK
    n_groups = (i + ATT_GROUP - 1) // ATT_GROUP
    last_group = ka_ref.shape[2] // span - 1

    def scores(h, g):
        off = pl.multiple_of(jnp.minimum(g, last_group) * span, span)
        return jnp.dot(ka_ref[0, h, pl.ds(off, span), :], qat_ref[0, h], preferred_element_type=F32)

    def values(h, g, p):
        off = pl.multiple_of(jnp.maximum(g, 0) * span, span)
        return jnp.dot(vt_ref[0, h, :, pl.ds(off, span)], p, preferred_element_type=F32)

    init = []
    for h in range(ATT_HEADS):
        s = jnp.dot(kad_ref[0, h], qat_ref[0, h], preferred_element_type=F32)
        s = jnp.where(key <= qry, s, NEG)
        m = jnp.max(s, axis=0, keepdims=True)
        p = jnp.exp(s - m).astype(BF16)
        acc = jnp.dot(vt_ref[0, h, :, pl.ds(base, MOBA_BLOCK)], p, preferred_element_type=F32)
        s_scr[h] = scores(h, 0)
        p_scr[h] = jnp.zeros((span, TQ), BF16)
        init += [jnp.ones((1, TQ), F32), m, acc]

    def body(g, carry):
        out = []
        for h in range(ATT_HEADS):
            alpha_prev, m, acc = carry[3 * h:3 * h + 3]
            acc = acc * alpha_prev + values(h, g - 1, p_scr[h])
            s = s_scr[h]
            m_new = jnp.maximum(m, jnp.max(s, axis=0, keepdims=True))
            p_scr[h] = jnp.exp(s - m_new).astype(BF16)
            s_scr[h] = scores(h, g + 1)
            out += [jnp.exp(m - m_new), m_new, acc]
        return tuple(out)

    fin = lax.fori_loop(0, n_groups, body, tuple(init))
    for h in range(ATT_HEADS):
        alpha_prev, _, acc = fin[3 * h:3 * h + 3]
        o_ref[0, h] = _normalized_t(acc * alpha_prev + values(h, n_groups - 1, p_scr[h]))


def _mem_attn_body(xqt_ref, mk_ref, mvt_ref, om_ref):
    for h in range(ATT_HEADS):
        sm = jnp.dot(mk_ref[0, h], xqt_ref[0, h], preferred_element_type=F32)
        pm = jnp.exp(sm - jnp.max(sm, axis=0, keepdims=True)).astype(BF16)
        om_ref[0, h] = _normalized_t(jnp.dot(mvt_ref[0, h], pm, preferred_element_type=F32))


def _mem_attn(xqt, mem_k, mem_vt):
    b, nh, _, l = xqt.shape
    return pl.pallas_call(
        _mem_attn_body, grid=(b, l // TQ_MEM),
        in_specs=[pl.BlockSpec((1, nh, LANE, TQ_MEM), lambda bi, i: (bi, 0, 0, i)),
                  pl.BlockSpec((1, nh, MEM_LEN, LANE), lambda bi, i: (bi, 0, 0, 0)),
                  pl.BlockSpec((1, nh, V_ROWS, MEM_LEN), lambda bi, i: (bi, 0, 0, 0))],
        out_specs=pl.BlockSpec((1, nh, TQ_MEM, ATT_DIM), lambda bi, i: (bi, 0, i, 0)),
        out_shape=jax.ShapeDtypeStruct((b, nh, l, ATT_DIM), F32),
        compiler_params=_cparams(("parallel", "parallel")), name="mem_attn",
    )(xqt, mem_k, mem_vt)


def _attn(qat, ka, kad, vt):
    b, _, _, l = qat.shape
    assert l % (ATT_GROUP * MOBA_BLOCK) == 0
    nh = ATT_HEADS
    per_q = lambda r: pl.BlockSpec((1, nh, r, TQ), lambda bi, i: (bi, 0, 0, i))
    out = pl.BlockSpec((1, nh, TQ, ATT_DIM), lambda bi, i: (bi, 0, i, 0))
    return pl.pallas_call(
        _attn_body, grid=(b, l // TQ),
        in_specs=[per_q(LANE),
                  pl.BlockSpec((1, nh, l, LANE), lambda bi, i: (bi, 0, 0, 0)),
                  pl.BlockSpec((1, nh, TQ, LANE), lambda bi, i: (bi, 0, i, 0)),
                  pl.BlockSpec((1, nh, V_ROWS, l), lambda bi, i: (bi, 0, 0, 0))],
        out_specs=out,
        out_shape=jax.ShapeDtypeStruct((b, nh, l, ATT_DIM), F32),
        scratch_shapes=[pltpu.VMEM((nh, ATT_GROUP * MOBA_BLOCK, TQ), F32),
                        pltpu.VMEM((nh, ATT_GROUP * MOBA_BLOCK, TQ), BF16)],
        compiler_params=_cparams(("parallel", "arbitrary")), name="attn",
    )(qat, ka, kad, vt)


def _cand_layout():
    slabs = [("r1", 0, 16), ("r1", 1, 8), ("r1", 2, 8), ("r1", 3, 8), ("r2", 0, 16), ("r2", 1, 8), ("r2", 2, 8)]
    pos, valid = [], []
    for kind, fixed, n in slabs:
        for j in range(n):
            r1, r2 = (fixed, j) if kind == "r1" else (j, fixed)
            ok = (r1 + 1) * (r2 + 1) <= PEER_TOPK and ((kind == "r1") or r1 >= 4)
            pos.append(r1 * PEER_TOPK + r2)
            valid.append(ok)
    assert sum(valid) == 50
    return slabs, np.asarray(pos, np.int32), np.asarray(valid)


_CAND_SLABS, _CAND_POS, _CAND_VALID = _cand_layout()
_NCAND = len(_CAND_POS)


def _split_cols(k):
    hi = k.astype(BF16)
    lo = (k - hi.astype(F32)).astype(BF16)
    return jnp.concatenate([hi, hi, lo], axis=1)


def _dot_split(k_cat, q):
    hi = q.astype(BF16)
    lo = (q - hi.astype(F32)).astype(BF16)
    return jnp.dot(k_cat, jnp.concatenate([hi, lo, hi], axis=0), preferred_element_type=F32)


def _pack_rows(x):
    return pltpu.bitcast(x.astype(BF16), jnp.uint32)


def _unpack_rows(x):
    return pltpu.bitcast(x, BF16)


def _top16_exact(sets):
    row = lax.broadcasted_iota(jnp.int32, (PEER_NKEYS, LANE), 0).astype(F32)
    for _, rank_scr, _ in sets:
        rank_scr[...] = jnp.full(rank_scr.shape, float(PEER_TOPK), F32)

    def body(r, _):
        rf = lax.convert_element_type(r, F32)
        for half in range(TM_PP // LANE):
            ls = slice(half * LANE, (half + 1) * LANE)
            for w_scr, rank_scr, v_scr in sets:
                s = w_scr[:, ls]
                m = jnp.max(s, axis=0, keepdims=True)
                hit = row == jnp.min(jnp.where(s == m, row, float(PEER_NKEYS)), axis=0, keepdims=True)
                w_scr[:, ls] = jnp.where(hit, -jnp.inf, s)
                v_scr[half, pl.ds(r, 1), :] = m
                rank_scr[:, ls] = jnp.where(hit, rf, rank_scr[:, ls])
        return 0

    lax.fori_loop(0, PEER_TOPK, body, 0)


def _top16_distinct(sets):
    chains = [(w_scr, rank_scr, v_scr, half) for half in range(TM_PP // LANE) for w_scr, rank_scr, v_scr in sets]

    def body(r, prev):
        out = []
        for (w_scr, _, v_scr, half), m_prev in zip(chains, prev):
            s = w_scr[:, half * LANE:(half + 1) * LANE]
            m = jnp.max(jnp.where(s < m_prev, s, -jnp.inf), axis=0, keepdims=True)
            v_scr[half, pl.ds(r, 1), :] = m
            out.append(m)
        return tuple(out)

    lax.fori_loop(0, PEER_TOPK, body, tuple(jnp.full((1, LANE), jnp.inf, F32) for _ in chains))
    ranked = []
    for half in range(TM_PP // LANE):
        ls = slice(half * LANE, (half + 1) * LANE)
        total = jnp.zeros((1, LANE), F32)
        for w_scr, rank_scr, v_scr in sets:
            s = w_scr[:, ls]
            v = v_scr[half]
            if rank_scr is None:
                hit = s >= v[PEER_TOPK - 1:PEER_TOPK, :]
            else:
                rank = jnp.full(s.shape, float(PEER_TOPK), F32)
                for r in range(PEER_TOPK):
                    rank = jnp.where(s == v[r:r + 1, :], float(r), rank)
                rank_scr[:, ls] = rank
                hit = rank < float(PEER_TOPK)
            total = total + jnp.sum(jnp.where(hit, 1.0, 0.0), axis=0, keepdims=True)
        ranked.append(total)
    return jnp.concatenate(ranked, axis=1)


def _peer_prep_body(x_ref, ys_ref, ym_ref, yx_ref, wos_ref, wom_ref, wox_ref, g_ref, wqt_ref,
                    k1_ref, k2_ref, pos_ref, cbias_ref,
                    h1_ref, hnt_ref, r2_ref, f_ref, n_ref, c_ref,
                    q_scr, w1_scr, w2_scr, rank1_scr, rank2_scr, v1_scr, v2_scr, cnt_scr, z_scr):
    hres = x_ref[...] + jnp.dot(ys_ref[...], wos_ref[...], preferred_element_type=F32)
    for h in range(ATT_HEADS):
        hres = hres + jnp.dot(ym_ref[0, h].astype(BF16), wom_ref[h], preferred_element_type=F32)
        hres = hres + jnp.dot(yx_ref[0, h].astype(BF16), wox_ref[h], preferred_element_type=F32)
    h1_ref[...] = hres
    ms = jnp.mean(hres * hres, axis=-1, keepdims=True)
    hn = (hres * lax.rsqrt(ms + NORM_EPS)) * g_ref[...]
    hn_t = hn.T.astype(BF16)
    hnt_ref[...] = hn_t
    q_scr[...] = jnp.dot(wqt_ref[...], hn_t, preferred_element_type=F32)

    pos = pos_ref[...]
    cbias = cbias_ref[...]
    r16 = lax.broadcasted_iota(jnp.int32, (PEER_TOPK, TM_PP), 0).astype(F32)
    halves = lambda scr: jnp.concatenate([scr[j] for j in range(TM_PP // LANE)], axis=1)

    def head(h, _):
        base = pl.multiple_of(h * (2 * PEER_HALF), 2 * PEER_HALF)
        s1 = _dot_split(k1_ref[...], q_scr[pl.ds(base, PEER_HALF), :])
        s2 = _dot_split(k2_ref[...], q_scr[pl.ds(base + PEER_HALF, PEER_HALF), :])
        sets = [(w1_scr, rank1_scr, v1_scr), (w2_scr, rank2_scr, v2_scr)]
        w1_scr[...] = s1
        w2_scr[...] = s2
        ranked = _top16_distinct([(w1_scr, None, v1_scr), (w2_scr, rank2_scr, v2_scr)])
        tied = jnp.max(jnp.abs(ranked - 2.0 * PEER_TOPK)) > 0.5

        @pl.when(tied)
        def _():
            _top16_exact(sets)

        rank2 = rank2_scr[...]
        v1 = halves(v1_scr)
        v2 = halves(v2_scr)
        parts = []
        for kind, fixed, n in _CAND_SLABS:
            if kind == "r1":
                parts.append(v1[fixed:fixed + 1, :] + v2[0:n, :])
            else:
                parts.append(v1[0:n, :] + v2[fixed:fixed + 1, :])
        cand = jnp.concatenate(parts, axis=0) + cbias
        top = v1[0:1, :] + v2[0:1, :]

        def descend(_, carry):
            m_prev, zsum = carry
            m = jnp.max(jnp.where(cand < m_prev, cand, -jnp.inf), axis=0, keepdims=True)
            return m, zsum + jnp.exp(m - top)

        tau, zsum = lax.fori_loop(0, PEER_TOPK, descend,
                                  (jnp.full((1, TM_PP), jnp.inf, F32), jnp.zeros((1, TM_PP), F32)))
        taken = jnp.where(cand >= tau, 1.0, 0.0)
        cnt = jnp.zeros((PEER_TOPK, TM_PP), F32)
        row0 = 0
        for kind, fixed, n_rows in _CAND_SLABS:
            blk = taken[row0:row0 + n_rows, :]
            if kind == "r1":
                cnt = cnt + jnp.where(r16 == float(fixed), jnp.sum(blk, axis=0, keepdims=True), 0.0)
            elif n_rows == PEER_TOPK:
                cnt = cnt + blk
            else:
                cnt = cnt + jnp.concatenate([blk, jnp.zeros((PEER_TOPK - n_rows, TM_PP), F32)], axis=0)
            row0 += n_rows
        cnt_scr[...] = cnt
        z_scr[...] = zsum
        tied2 = jnp.max(jnp.abs(jnp.sum(cnt, axis=0, keepdims=True) - float(PEER_TOPK))) > 0.5

        @pl.when(tied2)
        def _():
            def pick(_, carry):
                cand, cnt, zsum = carry
                m = jnp.max(cand, axis=0, keepdims=True)
                p = jnp.min(jnp.where(cand == m, pos, float(4 * PEER_TOPK * PEER_TOPK)), axis=0, keepdims=True)
                cand = jnp.where(pos == p, -jnp.inf, cand)
                cnt = cnt + jnp.where(r16 == jnp.floor(p * (1.0 / PEER_TOPK)), 1.0, 0.0)
                return cand, cnt, zsum + jnp.exp(m - top)

            _, cnt, zsum = lax.fori_loop(
                0, PEER_TOPK, pick, (cand, jnp.zeros((PEER_TOPK, TM_PP), F32), jnp.zeros((1, TM_PP), F32)))
            cnt_scr[...] = cnt
            z_scr[...] = zsum

        cnt = cnt_scr[...]
        zsum = z_scr[...]
        def write_nsel(is_rank_r):
            nsel = jnp.zeros((PEER_NKEYS, TM_PP), F32)
            for r in range(PEER_TOPK):
                nsel = jnp.where(is_rank_r(r), cnt[r:r + 1, :], nsel)
            for lc in range(TM_PP // LANE):
                n_ref[h, lc] = nsel[:, lc * LANE:(lc + 1) * LANE]

        @pl.when(tied)
        def _():
            rank1 = rank1_scr[...]
            write_nsel(lambda r: rank1 == float(r))

        @pl.when(jnp.logical_not(tied))
        def _():
            write_nsel(lambda r: s1 == v1[r:r + 1, :])

        r2_ref[h] = _pack_rows(rank2)
        f_ref[h] = _pack_rows(jnp.exp(s2 - v2[0:1, :]))
        cw = jnp.exp(s1 - v1[0:1, :]) * (0.5 / zsum)
        for lc in range(TM_PP // LANE):
            c_ref[h, lc] = cw[:, lc * LANE:(lc + 1) * LANE]
        return 0

    lax.fori_loop(0, PEER_HEADS, head, 0)


def _peer_prep(x2d, y_ssd, y_moba, y_mem, wo_ssd, wo_moba, wo_mem, gain, wq_t, k1, k2):
    t = x2d.shape[0]
    b, _, l, _ = y_moba.shape
    per_b = l // TM_PP
    pos = jnp.asarray(np.broadcast_to(_CAND_POS.astype(np.float32)[:, None], (_NCAND, TM_PP)))
    cbias = jnp.asarray(np.broadcast_to(np.where(_CAND_VALID, 0.0, -np.inf).astype(np.float32)[:, None],
                                        (_NCAND, TM_PP)))
    tok = lambda w: pl.BlockSpec((TM_PP, w), lambda i: (i, 0))
    att = pl.BlockSpec((1, ATT_HEADS, TM_PP, ATT_DIM), lambda i: (i // per_b, 0, i % per_b, 0))
    meta = pl.BlockSpec((PEER_HEADS, TM_PP // LANE, PEER_NKEYS, LANE), lambda i: (0, i, 0, 0))
    meta_f32 = jax.ShapeDtypeStruct((PEER_HEADS, t // LANE, PEER_NKEYS, LANE), F32)
    packed = pl.BlockSpec((PEER_HEADS, PEER_NKEYS // 2, TM_PP), lambda i: (0, 0, i))
    meta_pk = jax.ShapeDtypeStruct((PEER_HEADS, PEER_NKEYS // 2, t), jnp.uint32)
    return pl.pallas_call(
        _peer_prep_body, grid=(t // TM_PP,),
        in_specs=[tok(D_MODEL), tok(SSD_WIDTH), att, att, _full(wo_ssd.shape), _full(wo_moba.shape),
                  _full(wo_mem.shape), _full((1, D_MODEL)), _full(wq_t.shape), _full(k1.shape), _full(k2.shape),
                  _full(pos.shape), _full(cbias.shape)],
        out_specs=[tok(D_MODEL), pl.BlockSpec((D_MODEL, TM_PP), lambda i: (0, i)), packed, packed, meta, meta],
        out_shape=[jax.ShapeDtypeStruct((t, D_MODEL), F32), jax.ShapeDtypeStruct((D_MODEL, t), BF16),
                   meta_pk, meta_pk, meta_f32, meta_f32],
        scratch_shapes=[pltpu.VMEM((PEER_HEADS * 2 * PEER_HALF, TM_PP), F32)]
        + [pltpu.VMEM((PEER_NKEYS, TM_PP), F32)] * 4
        + [pltpu.VMEM((TM_PP // LANE, PEER_TOPK, LANE), F32)] * 2
        + [pltpu.VMEM((PEER_TOPK, TM_PP), F32), pltpu.VMEM((1, TM_PP), F32)],
        compiler_params=_cparams(("parallel",)), name="peer_prep",
    )(x2d, y_ssd, y_moba, y_mem, wo_ssd, wo_moba, wo_mem, gain, wq_t, k1, k2, pos, cbias)


_A_PER_SLAB = ES_PEER // PEER_NKEYS
_PEER_STEPS = PEER_EXPERTS // (PEER_SLABS * ES_PEER)


def _peer_gate_chunk(s_ref, a_ref, a_base, k, lc, r2_ref, f_ref, n_ref, c_ref):
    zero = jnp.zeros((PEER_NKEYS, LANE), BF16)

    def bcast(row):
        tile = jnp.broadcast_to(row, (16, LANE)).astype(BF16)
        return jnp.concatenate([tile] * (PEER_NKEYS // 16), axis=0)

    a = a_base + k
    rows = slice(k * PEER_NKEYS, (k + 1) * PEER_NKEYS)
    ls = slice(lc * LANE, (lc + 1) * LANE)
    s = s_ref[rows, ls].astype(BF16)
    act = s * (1.0 + lax.erf(s * math.sqrt(0.5)))
    g = zero
    for h in range(PEER_HEADS):
        nrow = bcast(n_ref[h, lc, pl.ds(a, 1), :])
        crow = bcast(c_ref[h, lc, pl.ds(a, 1), :])
        g = g + jnp.where(_unpack_rows(r2_ref[h, :, ls]) < nrow, _unpack_rows(f_ref[h, :, ls]) * crow, zero)
    a_ref[k * (PEER_NKEYS // 2):(k + 1) * (PEER_NKEYS // 2), ls] = pltpu.bitcast(act * g, jnp.uint32)


def _peer_body(*refs):
    s_n = PEER_SLABS
    wd0_ref = refs[0]
    wd_next = refs[1:1 + s_n]
    wut_prev = refs[1 + s_n:1 + 2 * s_n]
    (wut_last_ref, hnt_ref, r2_ref, f_ref, n_ref, c_ref, h1_ref, o_ref,
     acc_scr, s_scr, a_scr) = refs[1 + 2 * s_n:]
    n = pl.program_id(1)
    meta = (r2_ref, f_ref, n_ref, c_ref)

    @pl.when(n == 0)
    def _():
        acc_scr[...] = jnp.zeros_like(acc_scr)
        a_scr[s_n - 1] = jnp.zeros(a_scr.shape[1:], a_scr.dtype)
        s_scr[0] = jnp.dot(wd0_ref[...], hnt_ref[...], preferred_element_type=F32)

    for j in range(s_n):
        for half in range(2):
            ls = slice(half * (TT_PEER // 2), (half + 1) * (TT_PEER // 2))
            acc_scr[:, ls] += jnp.dot(wut_prev[j][...], _unpack_rows(a_scr[(j - 1) % s_n, :, ls]),
                                      preferred_element_type=F32)
            s_scr[(j + 1) % s_n, :, ls] = jnp.dot(wd_next[j][...], hnt_ref[:, ls],
                                                  preferred_element_type=F32)
        a_base = (n * s_n + j) * _A_PER_SLAB
        for lc in range(TT_PEER // LANE):
            for k in range(_A_PER_SLAB):
                _peer_gate_chunk(s_scr.at[j], a_scr.at[j], a_base, k, lc, *meta)

    @pl.when(n == pl.num_programs(1) - 1)
    def _():
        tail = jnp.dot(wut_last_ref[...], _unpack_rows(a_scr[s_n - 1]), preferred_element_type=F32)
        o_ref[...] = h1_ref[...] + (acc_scr[...] + tail).T


def _peer(wd, wu_t, hn_t, rank2, f, nsel, c, h1):
    t = h1.shape[0]
    s_n = PEER_SLABS
    nblk = s_n * _PEER_STEPS
    meta = pl.BlockSpec((PEER_HEADS, TT_PEER // LANE, PEER_NKEYS, LANE), lambda i, n: (0, i, 0, 0))
    packed = pl.BlockSpec((PEER_HEADS, PEER_NKEYS // 2, TT_PEER), lambda i, n: (0, 0, i))

    def wd_blk(off):
        return pl.BlockSpec((ES_PEER, D_MODEL), lambda i, n: (jnp.clip(s_n * n + off, 0, nblk - 1), 0))

    def wut_blk(off):
        return pl.BlockSpec((None, D_MODEL, ES_PEER), lambda i, n: (jnp.clip(s_n * n + off, 0, nblk - 1), 0, 0))

    return pl.pallas_call(
        _peer_body, grid=(t // TT_PEER, _PEER_STEPS),
        in_specs=[pl.BlockSpec((ES_PEER, D_MODEL), lambda i, n: (0, 0))]
        + [wd_blk(j + 1) for j in range(s_n)] + [wut_blk(j - 1) for j in range(s_n)]
        + [pl.BlockSpec((None, D_MODEL, ES_PEER), lambda i, n: (nblk - 1, 0, 0)),
           pl.BlockSpec((D_MODEL, TT_PEER), lambda i, n: (0, i)), packed, packed, meta, meta,
           pl.BlockSpec((TT_PEER, D_MODEL), lambda i, n: (i, 0))],
        out_specs=pl.BlockSpec((TT_PEER, D_MODEL), lambda i, n: (i, 0)),
        out_shape=jax.ShapeDtypeStruct((t, D_MODEL), F32),
        scratch_shapes=[pltpu.VMEM((D_MODEL, TT_PEER), F32),
                        pltpu.VMEM((s_n, ES_PEER, TT_PEER), F32),
                        pltpu.VMEM((s_n, ES_PEER // 2, TT_PEER), jnp.uint32)],
        compiler_params=_cparams(("parallel", "arbitrary")), name="peer",
    )(*([wd] * (s_n + 1) + [wu_t] * (s_n + 1) + [hn_t, rank2, f, nsel, c, h1]))


def _pad_heads(w):
    r = w.shape[0]
    w = w.reshape(r, -1, ATT_DIM)
    return jnp.pad(w, ((0, 0), (0, 0), (0, LANE - ATT_DIM))).reshape(r, -1)


def _pad_row(w):
    return jnp.pad(w, (0, LANE - w.shape[0])).reshape(1, LANE)


def kernel(x, mem, mix_norm_w, w_in, ssd_conv_w, ssd_conv_b, ssd_dt_bias, ssd_a_log, ssd_d, ssd_norm_w,
           moba_q_norm_w, moba_k_norm_w, mem_norm_w, w_mem_kv, xattn_q_norm_w, xattn_k_norm_w, w_out,
           ffn_norm_w, peer_w_query, peer_sub_keys_1, peer_sub_keys_2, peer_expert_down, peer_expert_up):
    b, l, d = x.shape
    depth = w_in.shape[0]
    h = x.reshape(b * l, d)
    for li in range(depth):
        wi = w_in[li]
        o = 0
        cols = {}
        for name, width in (("z", SSD_WIDTH), ("xbc", SSD_CONV_CH), ("dt", SSD_HEADS), ("mq", ATT_WIDTH),
                            ("mk", ATT_WIDTH), ("mv", ATT_WIDTH), ("xq", ATT_WIDTH)):
            cols[name] = wi[:, o:o + width]
            o += width
        w_list = [cols["z"], cols["xbc"], jnp.pad(cols["dt"], ((0, 0), (0, LANE - SSD_HEADS))),
                  cols["mq"], _pad_heads(cols["mk"]), cols["mv"], cols["xq"]]
        w_list = [w.astype(BF16) for w in w_list]
        z, xbc, dt, mq, mk, mv, xq = _in_proj(h, mix_norm_w[li].reshape(1, d), w_list)

        y_ssd = _ssd(z.reshape(b, l, -1), xbc.reshape(b, l, -1), dt.reshape(b, l, -1), ssd_conv_w[li],
                     ssd_conv_b[li], ssd_dt_bias[li], ssd_a_log[li], ssd_d[li], ssd_norm_w[li])

        wkv = w_mem_kv[li]
        mem_k, mem_vt = _mem_kv(mem, mem_norm_w[li].reshape(1, d), _pad_heads(wkv[:, :ATT_WIDTH]).astype(BF16),
                                _pad_heads(wkv[:, ATT_WIDTH:]).astype(BF16), _pad_row(xattn_k_norm_w[li]))
        r3 = lambda a: a.reshape(b, l, -1)
        per_head = lambda w: jnp.tile(w, ATT_HEADS).reshape(1, ATT_WIDTH)
        qat, ka, kad, vt, xqt = _attn_prep(r3(mq), r3(mk), r3(mv), r3(xq), per_head(moba_q_norm_w[li]),
                                           _pad_row(moba_k_norm_w[li]), per_head(xattn_q_norm_w[li]))
        y_moba = _attn(qat, ka, kad, vt)
        y_mem = _mem_attn(xqt, mem_k, mem_vt)

        wo = w_out[li].astype(BF16)
        wo_moba = wo[SSD_WIDTH:SSD_WIDTH + ATT_WIDTH].reshape(ATT_HEADS, ATT_DIM, d)
        wo_mem = wo[SSD_WIDTH + ATT_WIDTH:].reshape(ATT_HEADS, ATT_DIM, d)
        h1, hn_t, rank2, f, nsel, c = _peer_prep(
            h, y_ssd.reshape(b * l, -1), y_moba, y_mem, wo[:SSD_WIDTH], wo_moba, wo_mem,
            ffn_norm_w[li].reshape(1, d), peer_w_query[li].T.astype(BF16), _split_cols(peer_sub_keys_1[li]),
            _split_cols(peer_sub_keys_2[li]))
        wu_t = peer_expert_up[li].reshape(-1, ES_PEER, d).transpose(0, 2, 1).astype(BF16)
        h = _peer(peer_expert_down[li].astype(BF16), wu_t, hn_t, rank2, f, nsel, c, h1)
    return h.reshape(b, l, d)
```
